```python
import jax, jax.numpy as jnp
from jax import lax
import numpy as np

D_MODEL = 1024
BATCH = 8
SEQ = 4096
DEPTH = 1

CHUNK = 64
Q_BLOCK = 128
LRU_WIDTH = 512
LRU_BLOCKS = 8
LRU_BLOCK_DIM = LRU_WIDTH // LRU_BLOCKS
CONV_WIDTH = 4
LRU_C = 8.0
MLA_HEADS = 8
QK_NOPE_DIM = 64
QK_ROPE_DIM = 32
V_HEAD_DIM = 64
Q_LORA_RANK = 384
KV_LORA_RANK = 256
ROPE_THETA = 10000.0
MLA_WIDTH = MLA_HEADS * V_HEAD_DIM
D_MIX = LRU_WIDTH + MLA_WIDTH
IN_COLS = 2 * LRU_WIDTH + Q_LORA_RANK + KV_LORA_RANK + QK_ROPE_DIM
D_FF = 2816
EPS = 1e-6

kernel_name = "hybrid_rglru_mla_macaron_sandwich"


def rms_norm(x, g):
    xf = x.astype(jnp.float32)
    y = xf * lax.rsqrt(jnp.mean(xf * xf, axis=-1, keepdims=True) + EPS)
    return (y * g.astype(jnp.float32)).astype(x.dtype)


def swiglu(x, w_gate, w_up, w_down):
    return (jax.nn.silu(x @ w_gate) * (x @ w_up)) @ w_down


def causal_depthwise_conv(x, w, b):
    s = x.shape[1]
    xp = jnp.pad(x, ((0, 0), (CONV_WIDTH - 1, 0), (0, 0)))
    y = b
    for k in range(CONV_WIDTH):
        y = y + xp[:, k:k + s, :] * w[k]
    return y


def rg_lru(x, w_a, b_a, w_x, b_x, lam):
    bsz, s, _ = x.shape
    xb = x.reshape(bsz, s, LRU_BLOCKS, LRU_BLOCK_DIM)
    r = jax.nn.sigmoid((jnp.einsum('bsnd,nde->bsne', xb, w_a).reshape(bsz, s, LRU_WIDTH) + b_a).astype(jnp.float32))
    i = jax.nn.sigmoid((jnp.einsum('bsnd,nde->bsne', xb, w_x).reshape(bsz, s, LRU_WIDTH) + b_x).astype(jnp.float32))
    log_a = -LRU_C * r * jax.nn.softplus(-lam.astype(jnp.float32))
    a = jnp.exp(log_a)
    mult = jnp.sqrt(-jnp.expm1(2.0 * log_a))
    u = mult * (i * x.astype(jnp.float32))

    def combine(left, right):
        a1, b1 = left
        a2, b2 = right
        return a1 * a2, a2 * b1 + b2

    _, h = lax.associative_scan(combine, (a, u), axis=1)
    return h.astype(x.dtype)


def rope(x, cos, sin):
    half = x.shape[-1] // 2
    x1, x2 = x[..., :half], x[..., half:]
    return jnp.concatenate([x1 * cos - x2 * sin, x2 * cos + x1 * sin], axis=-1).astype(x.dtype)


def mla(q_lat, kv_lat, k_rope_raw, positions, q_a_norm, w_q_b, kv_a_norm, w_kv_b):
    bsz, s, _ = q_lat.shape
    q = (rms_norm(q_lat, q_a_norm) @ w_q_b).reshape(bsz, s, MLA_HEADS, QK_NOPE_DIM + QK_ROPE_DIM)
    q_nope, q_pe = q[..., :QK_NOPE_DIM], q[..., QK_NOPE_DIM:]
    kv = (rms_norm(kv_lat, kv_a_norm) @ w_kv_b).reshape(bsz, s, MLA_HEADS, QK_NOPE_DIM + V_HEAD_DIM)
    k_nope, v = kv[..., :QK_NOPE_DIM], kv[..., QK_NOPE_DIM:]

    inv_freq = 1.0 / (ROPE_THETA ** (jnp.arange(0, QK_ROPE_DIM, 2, dtype=jnp.float32) / QK_ROPE_DIM))
    ang = positions.astype(jnp.float32)[..., None] * inv_freq
    cos, sin = jnp.cos(ang), jnp.sin(ang)
    q_pe = rope(q_pe, cos[:, :, None, :], sin[:, :, None, :])
    k_pe = rope(k_rope_raw, cos, sin)

    scale = (QK_NOPE_DIM + QK_ROPE_DIM) ** -0.5
    n_blocks = s // Q_BLOCK
    qn_blocks = q_nope.reshape(bsz, n_blocks, Q_BLOCK, MLA_HEADS, QK_NOPE_DIM).transpose(1, 0, 2, 3, 4)
    qp_blocks = q_pe.reshape(bsz, n_blocks, Q_BLOCK, MLA_HEADS, QK_ROPE_DIM).transpose(1, 0, 2, 3, 4)
    key_chunk = jnp.arange(s) // CHUNK

    def attend(args):
        qn, qp, blk = args
        sc = (jnp.einsum('bqhd,bkhd->bhqk', qn, k_nope)
              + jnp.einsum('bqhd,bkd->bhqk', qp, k_pe)).astype(jnp.float32) * scale
        q_chunk = (blk * Q_BLOCK + jnp.arange(Q_BLOCK)) // CHUNK
        mask = key_chunk[None, :] <= q_chunk[:, None]
        sc = jnp.where(mask[None, None], sc, -1e30)
        p = jax.nn.softmax(sc, axis=-1).astype(v.dtype)
        return jnp.einsum('bhqk,bkhd->bqhd', p, v)

    out = lax.map(attend, (qn_blocks, qp_blocks, jnp.arange(n_blocks)))
    return out.transpose(1, 0, 2, 3, 4).reshape(bsz, s, MLA_WIDTH)


def hybrid_mixer(h, positions, w_in, conv_w, conv_b, w_lru_a, b_lru_a, w_lru_x, b_lru_x,
                 lru_lambda, q_a_norm, w_q_b, kv_a_norm, w_kv_b, w_out):
    proj = h @ w_in
    cuts = [LRU_WIDTH, 2 * LRU_WIDTH, 2 * LRU_WIDTH + Q_LORA_RANK,
            2 * LRU_WIDTH + Q_LORA_RANK + KV_LORA_RANK]
    x_lru, gate, q_lat, kv_lat, k_rope_raw = jnp.split(proj, cuts, axis=-1)
    y_lru = rg_lru(causal_depthwise_conv(x_lru, conv_w, conv_b),
                   w_lru_a, b_lru_a, w_lru_x, b_lru_x, lru_lambda) * jax.nn.gelu(gate)
    y_mla = mla(q_lat, kv_lat, k_rope_raw, positions, q_a_norm, w_q_b, kv_a_norm, w_kv_b)
    return jnp.concatenate([y_lru, y_mla], axis=-1) @ w_out


def _fwd_setup_inputs(seed: int = 0) -> dict:
    key = jax.random.key(seed)
    ks = iter(jax.random.split(key, 40))
    f32 = jnp.float32

    def nrm(shape, fan_in):
        return jax.random.normal(next(ks), shape, f32) * (fan_in ** -0.5)

    def gain(shape):
        return 1.0 + 0.01 * jax.random.normal(next(ks), shape, f32)

    def bias(shape):
        return 0.01 * jax.random.normal(next(ks), shape, f32)

    L = DEPTH
    x = jax.random.normal(next(ks), (BATCH, SEQ, D_MODEL), f32)
    offset = jax.random.randint(next(ks), (BATCH, 1), 0, 65536, dtype=jnp.int32)
    positions = (offset + jnp.arange(SEQ, dtype=jnp.int32)[None, :]).astype(jnp.int32)

    a0 = jax.random.uniform(next(ks), (L, LRU_WIDTH), f32, 0.9, 0.999)
    sa = a0 ** (1.0 / LRU_C)
    lru_lambda = jnp.log(sa) - jnp.log1p(-sa)

    return {
        "x": x,
        "positions": positions,
        "g_ffn1_pre": gain((L, D_MODEL)),
        "g_ffn1_post": gain((L, D_MODEL)),
        "w_ffn1_gate": nrm((L, D_MODEL, D_FF), D_MODEL),
        "w_ffn1_up": nrm((L, D_MODEL, D_FF), D_MODEL),
        "w_ffn1_down": nrm((L, D_FF, D_MODEL), D_FF),
        "g_mix_pre": gain((L, D_MODEL)),
        "g_mix_post": gain((L, D_MODEL)),
        "w_in": nrm((L, D_MODEL, IN_COLS), D_MODEL),
        "conv_w": nrm((L, CONV_WIDTH, LRU_WIDTH), CONV_WIDTH),
        "conv_b": bias((L, LRU_WIDTH)),
        "w_lru_a": nrm((L, LRU_BLOCKS, LRU_BLOCK_DIM, LRU_BLOCK_DIM), LRU_BLOCK_DIM),
        "b_lru_a": bias((L, LRU_WIDTH)),
        "w_lru_x": nrm((L, LRU_BLOCKS, LRU_BLOCK_DIM, LRU_BLOCK_DIM), LRU_BLOCK_DIM),
        "b_lru_x": bias((L, LRU_WIDTH)),
        "lru_lambda": lru_lambda,
        "q_a_norm": gain((L, Q_LORA_RANK)),
        "w_q_b": nrm((L, Q_LORA_RANK, MLA_HEADS * (QK_NOPE_DIM + QK_ROPE_DIM)), Q_LORA_RANK),
        "kv_a_norm": gain((L, KV_LORA_RANK)),
        "w_kv_b": nrm((L, KV_LORA_RANK, MLA_HEADS * (QK_NOPE_DIM + V_HEAD_DIM)), KV_LORA_RANK),
        "w_out": nrm((L, D_MIX, D_MODEL), D_MIX),
        "g_ffn2_pre": gain((L, D_MODEL)),
        "g_ffn2_post": gain((L, D_MODEL)),
        "w_ffn2_gate": nrm((L, D_MODEL, D_FF), D_MODEL),
        "w_ffn2_up": nrm((L, D_MODEL, D_FF), D_MODEL),
        "w_ffn2_down": nrm((L, D_FF, D_MODEL), D_FF),
    }


def _fwd_reference(x, positions, g_ffn1_pre, g_ffn1_post, w_ffn1_gate, w_ffn1_up, w_ffn1_down,
              g_mix_pre, g_mix_post, w_in, conv_w, conv_b, w_lru_a, b_lru_a, w_lru_x, b_lru_x,
              lru_lambda, q_a_norm, w_q_b, kv_a_norm, w_kv_b, w_out,
              g_ffn2_pre, g_ffn2_post, w_ffn2_gate, w_ffn2_up, w_ffn2_down):
    h = x
    for l in range(DEPTH):
        f = swiglu(rms_norm(h, g_ffn1_pre[l]), w_ffn1_gate[l], w_ffn1_up[l], w_ffn1_down[l])
        h = h + 0.5 * rms_norm(f, g_ffn1_post[l])
        m = hybrid_mixer(rms_norm(h, g_mix_pre[l]), positions, w_in[l], conv_w[l], conv_b[l],
                         w_lru_a[l], b_lru_a[l], w_lru_x[l], b_lru_x[l], lru_lambda[l],
                         q_a_norm[l], w_q_b[l], kv_a_norm[l], w_kv_b[l], w_out[l])
        h = h + rms_norm(m, g_mix_post[l])
        f = swiglu(rms_norm(h, g_ffn2_pre[l]), w_ffn2_gate[l], w_ffn2_up[l], w_ffn2_down[l])
        h = h + 0.5 * rms_norm(f, g_ffn2_post[l])
    return h


import jax as _jax
import jax.numpy as _jnp

TWIN_FORMAT = 'train_step'
FWD_PARAMS = ['x', 'positions', 'g_ffn1_pre', 'g_ffn1_post', 'w_ffn1_gate', 'w_ffn1_up', 'w_ffn1_down', 'g_mix_pre', 'g_mix_post', 'w_in', 'conv_w', 'conv_b', 'w_lru_a', 'b_lru_a', 'w_lru_x', 'b_lru_x', 'lru_lambda', 'q_a_norm', 'w_q_b', 'kv_a_norm', 'w_kv_b', 'w_out', 'g_ffn2_pre', 'g_ffn2_post', 'w_ffn2_gate', 'w_ffn2_up', 'w_ffn2_down']
TWIN_WEIGHTS = ['g_ffn1_pre', 'g_ffn1_post', 'w_ffn1_gate', 'w_ffn1_up', 'w_ffn1_down', 'g_mix_pre', 'g_mix_post', 'w_in', 'conv_w', 'conv_b', 'w_lru_a', 'b_lru_a', 'w_lru_x', 'b_lru_x', 'lru_lambda', 'q_a_norm', 'w_q_b', 'kv_a_norm', 'w_kv_b', 'w_out', 'g_ffn2_pre', 'g_ffn2_post', 'w_ffn2_gate', 'w_ffn2_up', 'w_ffn2_down']
TWIN_DIFF_INPUT = 'x'
TWIN_INPUTS = ['x', 'positions', 'g_ffn1_pre', 'g_ffn1_post', 'w_ffn1_gate', 'w_ffn1_up', 'w_ffn1_down', 'g_mix_pre', 'g_mix_post', 'w_in', 'conv_w', 'conv_b', 'w_lru_a', 'b_lru_a', 'w_lru_x', 'b_lru_x', 'lru_lambda', 'q_a_norm', 'w_q_b', 'kv_a_norm', 'w_kv_b', 'w_out', 'g_ffn2_pre', 'g_ffn2_post', 'w_ffn2_gate', 'w_ffn2_up', 'w_ffn2_down', 'loss_target', 'm_g_ffn1_pre', 'm_g_ffn1_post', 'm_w_ffn1_gate', 'm_w_ffn1_up', 'm_w_ffn1_down', 'm_g_mix_pre', 'm_g_mix_post', 'm_w_in', 'm_conv_w', 'm_conv_b', 'm_w_lru_a', 'm_b_lru_a', 'm_w_lru_x', 'm_b_lru_x', 'm_lru_lambda', 'm_q_a_norm', 'm_w_q_b', 'm_kv_a_norm', 'm_w_kv_b', 'm_w_out', 'm_g_ffn2_pre', 'm_g_ffn2_post', 'm_w_ffn2_gate', 'm_w_ffn2_up', 'm_w_ffn2_down', 'v_g_ffn1_pre', 'v_g_ffn1_post', 'v_w_ffn1_gate', 'v_w_ffn1_up', 'v_w_ffn1_down', 'v_g_mix_pre', 'v_g_mix_post', 'v_w_in', 'v_conv_w', 'v_conv_b', 'v_w_lru_a', 'v_b_lru_a', 'v_w_lru_x', 'v_b_lru_x', 'v_lru_lambda', 'v_q_a_norm', 'v_w_q_b', 'v_kv_a_norm', 'v_w_kv_b', 'v_w_out', 'v_g_ffn2_pre', 'v_g_ffn2_post', 'v_w_ffn2_gate', 'v_w_ffn2_up', 'v_w_ffn2_down']
TWIN_OUTPUTS = ['loss', 'grad_x', 'grad_g_ffn1_pre', 'grad_g_ffn1_post', 'grad_w_ffn1_gate', 'grad_w_ffn1_up', 'grad_w_ffn1_down', 'grad_g_mix_pre', 'grad_g_mix_post', 'grad_w_in', 'grad_conv_w', 'grad_conv_b', 'grad_w_lru_a', 'grad_b_lru_a', 'grad_w_lru_x', 'grad_b_lru_x', 'grad_lru_lambda', 'grad_q_a_norm', 'grad_w_q_b', 'grad_kv_a_norm', 'grad_w_kv_b', 'grad_w_out', 'grad_g_ffn2_pre', 'grad_g_ffn2_post', 'grad_w_ffn2_gate', 'grad_w_ffn2_up', 'grad_w_ffn2_down', 'delta_g_ffn1_pre', 'delta_g_ffn1_post', 'delta_w_ffn1_gate', 'delta_w_ffn1_up', 'delta_w_ffn1_down', 'delta_g_mix_pre', 'delta_g_mix_post', 'delta_w_in', 'delta_conv_w', 'delta_conv_b', 'delta_w_lru_a', 'delta_b_lru_a', 'delta_w_lru_x', 'delta_b_lru_x', 'delta_lru_lambda', 'delta_q_a_norm', 'delta_w_q_b', 'delta_kv_a_norm', 'delta_w_kv_b', 'delta_w_out', 'delta_g_ffn2_pre', 'delta_g_ffn2_post', 'delta_w_ffn2_gate', 'delta_w_ffn2_up', 'delta_w_ffn2_down', 'new_m_g_ffn1_pre', 'new_m_g_ffn1_post', 'new_m_w_ffn1_gate', 'new_m_w_ffn1_up', 'new_m_w_ffn1_down', 'new_m_g_mix_pre', 'new_m_g_mix_post', 'new_m_w_in', 'new_m_conv_w', 'new_m_conv_b', 'new_m_w_lru_a', 'new_m_b_lru_a', 'new_m_w_lru_x', 'new_m_b_lru_x', 'new_m_lru_lambda', 'new_m_q_a_norm', 'new_m_w_q_b', 'new_m_kv_a_norm', 'new_m_w_kv_b', 'new_m_w_out', 'new_m_g_ffn2_pre', 'new_m_g_ffn2_post', 'new_m_w_ffn2_gate', 'new_m_w_ffn2_up', 'new_m_w_ffn2_down', 'new_v_g_ffn1_pre', 'new_v_g_ffn1_post', 'new_v_w_ffn1_gate', 'new_v_w_ffn1_up', 'new_v_w_ffn1_down', 'new_v_g_mix_pre', 'new_v_g_mix_post', 'new_v_w_in', 'new_v_conv_w', 'new_v_conv_b', 'new_v_w_lru_a', 'new_v_b_lru_a', 'new_v_w_lru_x', 'new_v_b_lru_x', 'new_v_lru_lambda', 'new_v_q_a_norm', 'new_v_w_q_b', 'new_v_kv_a_norm', 'new_v_w_kv_b', 'new_v_w_out', 'new_v_g_ffn2_pre', 'new_v_g_ffn2_post', 'new_v_w_ffn2_gate', 'new_v_w_ffn2_up', 'new_v_w_ffn2_down']
TWIN_LEAF_KINDS = {'loss': 'loss', 'grad_x': 'grad_x', 'grad_g_ffn1_pre': 'grad_w', 'grad_g_ffn1_post': 'grad_w', 'grad_w_ffn1_gate': 'grad_w', 'grad_w_ffn1_up': 'grad_w', 'grad_w_ffn1_down': 'grad_w', 'grad_g_mix_pre': 'grad_w', 'grad_g_mix_post': 'grad_w', 'grad_w_in': 'grad_w', 'grad_conv_w': 'grad_w', 'grad_conv_b': 'grad_w', 'grad_w_lru_a': 'grad_w', 'grad_b_lru_a': 'grad_w', 'grad_w_lru_x': 'grad_w', 'grad_b_lru_x': 'grad_w', 'grad_lru_lambda': 'grad_w', 'grad_q_a_norm': 'grad_w', 'grad_w_q_b': 'grad_w', 'grad_kv_a_norm': 'grad_w', 'grad_w_kv_b': 'grad_w', 'grad_w_out': 'grad_w', 'grad_g_ffn2_pre': 'grad_w', 'grad_g_ffn2_post': 'grad_w', 'grad_w_ffn2_gate': 'grad_w', 'grad_w_ffn2_up': 'grad_w', 'grad_w_ffn2_down': 'grad_w', 'delta_g_ffn1_pre': 'delta_w', 'delta_g_ffn1_post': 'delta_w', 'delta_w_ffn1_gate': 'delta_w', 'delta_w_ffn1_up': 'delta_w', 'delta_w_ffn1_down': 'delta_w', 'delta_g_mix_pre': 'delta_w', 'delta_g_mix_post': 'delta_w', 'delta_w_in': 'delta_w', 'delta_conv_w': 'delta_w', 'delta_conv_b': 'delta_w', 'delta_w_lru_a': 'delta_w', 'delta_b_lru_a': 'delta_w', 'delta_w_lru_x': 'delta_w', 'delta_b_lru_x': 'delta_w', 'delta_lru_lambda': 'delta_w', 'delta_q_a_norm': 'delta_w', 'delta_w_q_b': 'delta_w', 'delta_kv_a_norm': 'delta_w', 'delta_w_kv_b': 'delta_w', 'delta_w_out': 'delta_w', 'delta_g_ffn2_pre': 'delta_w', 'delta_g_ffn2_post': 'delta_w', 'delta_w_ffn2_gate': 'delta_w', 'delta_w_ffn2_up': 'delta_w', 'delta_w_ffn2_down': 'delta_w', 'new_m_g_ffn1_pre': 'new_m', 'new_m_g_ffn1_post': 'new_m', 'new_m_w_ffn1_gate': 'new_m', 'new_m_w_ffn1_up': 'new_m', 'new_m_w_ffn1_down': 'new_m', 'new_m_g_mix_pre': 'new_m', 'new_m_g_mix_post': 'new_m', 'new_m_w_in': 'new_m', 'new_m_conv_w': 'new_m', 'new_m_conv_b': 'new_m', 'new_m_w_lru_a': 'new_m', 'new_m_b_lru_a': 'new_m', 'new_m_w_lru_x': 'new_m', 'new_m_b_lru_x': 'new_m', 'new_m_lru_lambda': 'new_m', 'new_m_q_a_norm': 'new_m', 'new_m_w_q_b': 'new_m', 'new_m_kv_a_norm': 'new_m', 'new_m_w_kv_b': 'new_m', 'new_m_w_out': 'new_m', 'new_m_g_ffn2_pre': 'new_m', 'new_m_g_ffn2_post': 'new_m', 'new_m_w_ffn2_gate': 'new_m', 'new_m_w_ffn2_up': 'new_m', 'new_m_w_ffn2_down': 'new_m', 'new_v_g_ffn1_pre': 'new_v', 'new_v_g_ffn1_post': 'new_v', 'new_v_w_ffn1_gate': 'new_v', 'new_v_w_ffn1_up': 'new_v', 'new_v_w_ffn1_down': 'new_v', 'new_v_g_mix_pre': 'new_v', 'new_v_g_mix_post': 'new_v', 'new_v_w_in': 'new_v', 'new_v_conv_w': 'new_v', 'new_v_conv_b': 'new_v', 'new_v_w_lru_a': 'new_v', 'new_v_b_lru_a': 'new_v', 'new_v_w_lru_x': 'new_v', 'new_v_b_lru_x': 'new_v', 'new_v_lru_lambda': 'new_v', 'new_v_q_a_norm': 'new_v', 'new_v_w_q_b': 'new_v', 'new_v_kv_a_norm': 'new_v', 'new_v_w_kv_b': 'new_v', 'new_v_w_out': 'new_v', 'new_v_g_ffn2_pre': 'new_v', 'new_v_g_ffn2_post': 'new_v', 'new_v_w_ffn2_gate': 'new_v', 'new_v_w_ffn2_up': 'new_v', 'new_v_w_ffn2_down': 'new_v'}


def _forward(args):
    return _fwd_reference(*[args[k] for k in FWD_PARAMS])


def _output_shape():
    out = _jax.eval_shape(lambda: _forward(_fwd_setup_inputs(0)))
    return out.shape, out.dtype

N_MICROBATCH = 1
ADAM_LR = 0.001
ADAM_B1 = 0.9
ADAM_B2 = 0.999
ADAM_EPS = 1e-08
ADAM_WD = 0.01
ADAM_STEP = 10
PER_EXAMPLE_BATCH_AXIS = {'x': 0, 'positions': 0, 'loss_target': 0}
SHARED_INPUTS = []
_WEIGHT_DTYPES = {'g_ffn1_pre': _jnp.float32, 'g_ffn1_post': _jnp.float32, 'w_ffn1_gate': _jnp.float32, 'w_ffn1_up': _jnp.float32, 'w_ffn1_down': _jnp.float32, 'g_mix_pre': _jnp.float32, 'g_mix_post': _jnp.float32, 'w_in': _jnp.float32, 'conv_w': _jnp.float32, 'conv_b': _jnp.float32, 'w_lru_a': _jnp.float32, 'b_lru_a': _jnp.float32, 'w_lru_x': _jnp.float32, 'b_lru_x': _jnp.float32, 'lru_lambda': _jnp.float32, 'q_a_norm': _jnp.float32, 'w_q_b': _jnp.float32, 'kv_a_norm': _jnp.float32, 'w_kv_b': _jnp.float32, 'w_out': _jnp.float32, 'g_ffn2_pre': _jnp.float32, 'g_ffn2_post': _jnp.float32, 'w_ffn2_gate': _jnp.float32, 'w_ffn2_up': _jnp.float32, 'w_ffn2_down': _jnp.float32}
MOMENT_SCALE = {'g_ffn1_pre': 4.186887e-01, 'g_ffn1_post': 7.957203e+00, 'w_ffn1_gate': 1.735831e-01, 'w_ffn1_up': 1.801527e-01, 'w_ffn1_down': 3.037061e-01, 'g_mix_pre': 5.073802e-01, 'g_mix_post': 3.239580e+01, 'w_in': 4.134169e-01, 'conv_w': 8.884431e-01, 'conv_b': 1.571563e+01, 'w_lru_a': 4.985687e-01, 'b_lru_a': 4.002093e-01, 'w_lru_x': 9.354827e-01, 'b_lru_x': 2.219318e-01, 'lru_lambda': 6.256794e-01, 'q_a_norm': 2.222135e-01, 'w_q_b': 1.559175e-01, 'kv_a_norm': 4.330495e-01, 'w_kv_b': 1.894832e-01, 'w_out': 6.919236e-01, 'g_ffn2_pre': 4.389218e-01, 'g_ffn2_post': 8.036583e+00, 'w_ffn2_gate': 1.648125e-01, 'w_ffn2_up': 2.695990e-01, 'w_ffn2_down': 4.400367e-01}


def _to_microbatches(a, axis):
    t = _jnp.moveaxis(a, axis, 0)
    t = t.reshape((N_MICROBATCH, t.shape[0] // N_MICROBATCH) + t.shape[1:])
    return _jnp.moveaxis(t, 1, axis + 1)


def setup_inputs(seed: int = 0) -> dict:
    inp = _fwd_setup_inputs(seed)
    key = _jax.random.fold_in(_jax.random.key(seed), 7919)
    shape, _ = _output_shape()
    out = dict(inp)
    out["loss_target"] = _jax.random.normal(_jax.random.fold_in(key, 0), shape, _jnp.float32)
    for i, name in enumerate(TWIN_WEIGHTS):
        w = inp[name].astype(_jnp.float32)
        if MOMENT_SCALE is None:
            s = _jnp.sqrt(_jnp.mean(_jnp.square(w)) + 1e-30)
        else:
            s = MOMENT_SCALE[name]
        km, kv = _jax.random.split(_jax.random.fold_in(key, i + 1))
        out[name] = w
        out["m_" + name] = s * _jax.random.normal(km, w.shape, _jnp.float32)
        out["v_" + name] = (s * s) * _jax.random.uniform(kv, w.shape, _jnp.float32, 0.5, 1.5)
    if N_MICROBATCH > 1:
        for name, axis in PER_EXAMPLE_BATCH_AXIS.items():
            out[name] = _to_microbatches(out[name], axis)
    return {'x': out['x'], 'positions': out['positions'], 'g_ffn1_pre': out['g_ffn1_pre'], 'g_ffn1_post': out['g_ffn1_post'], 'w_ffn1_gate': out['w_ffn1_gate'], 'w_ffn1_up': out['w_ffn1_up'], 'w_ffn1_down': out['w_ffn1_down'], 'g_mix_pre': out['g_mix_pre'], 'g_mix_post': out['g_mix_post'], 'w_in': out['w_in'], 'conv_w': out['conv_w'], 'conv_b': out['conv_b'], 'w_lru_a': out['w_lru_a'], 'b_lru_a': out['b_lru_a'], 'w_lru_x': out['w_lru_x'], 'b_lru_x': out['b_lru_x'], 'lru_lambda': out['lru_lambda'], 'q_a_norm': out['q_a_norm'], 'w_q_b': out['w_q_b'], 'kv_a_norm': out['kv_a_norm'], 'w_kv_b': out['w_kv_b'], 'w_out': out['w_out'], 'g_ffn2_pre': out['g_ffn2_pre'], 'g_ffn2_post': out['g_ffn2_post'], 'w_ffn2_gate': out['w_ffn2_gate'], 'w_ffn2_up': out['w_ffn2_up'], 'w_ffn2_down': out['w_ffn2_down'], 'loss_target': out['loss_target'], 'm_g_ffn1_pre': out['m_g_ffn1_pre'], 'm_g_ffn1_post': out['m_g_ffn1_post'], 'm_w_ffn1_gate': out['m_w_ffn1_gate'], 'm_w_ffn1_up': out['m_w_ffn1_up'], 'm_w_ffn1_down': out['m_w_ffn1_down'], 'm_g_mix_pre': out['m_g_mix_pre'], 'm_g_mix_post': out['m_g_mix_post'], 'm_w_in': out['m_w_in'], 'm_conv_w': out['m_conv_w'], 'm_conv_b': out['m_conv_b'], 'm_w_lru_a': out['m_w_lru_a'], 'm_b_lru_a': out['m_b_lru_a'], 'm_w_lru_x': out['m_w_lru_x'], 'm_b_lru_x': out['m_b_lru_x'], 'm_lru_lambda': out['m_lru_lambda'], 'm_q_a_norm': out['m_q_a_norm'], 'm_w_q_b': out['m_w_q_b'], 'm_kv_a_norm': out['m_kv_a_norm'], 'm_w_kv_b': out['m_w_kv_b'], 'm_w_out': out['m_w_out'], 'm_g_ffn2_pre': out['m_g_ffn2_pre'], 'm_g_ffn2_post': out['m_g_ffn2_post'], 'm_w_ffn2_gate': out['m_w_ffn2_gate'], 'm_w_ffn2_up': out['m_w_ffn2_up'], 'm_w_ffn2_down': out['m_w_ffn2_down'], 'v_g_ffn1_pre': out['v_g_ffn1_pre'], 'v_g_ffn1_post': out['v_g_ffn1_post'], 'v_w_ffn1_gate': out['v_w_ffn1_gate'], 'v_w_ffn1_up': out['v_w_ffn1_up'], 'v_w_ffn1_down': out['v_w_ffn1_down'], 'v_g_mix_pre': out['v_g_mix_pre'], 'v_g_mix_post': out['v_g_mix_post'], 'v_w_in': out['v_w_in'], 'v_conv_w': out['v_conv_w'], 'v_conv_b': out['v_conv_b'], 'v_w_lru_a': out['v_w_lru_a'], 'v_b_lru_a': out['v_b_lru_a'], 'v_w_lru_x': out['v_w_lru_x'], 'v_b_lru_x': out['v_b_lru_x'], 'v_lru_lambda': out['v_lru_lambda'], 'v_q_a_norm': out['v_q_a_norm'], 'v_w_q_b': out['v_w_q_b'], 'v_kv_a_norm': out['v_kv_a_norm'], 'v_w_kv_b': out['v_w_kv_b'], 'v_w_out': out['v_w_out'], 'v_g_ffn2_pre': out['v_g_ffn2_pre'], 'v_g_ffn2_post': out['v_g_ffn2_post'], 'v_w_ffn2_gate': out['v_w_ffn2_gate'], 'v_w_ffn2_up': out['v_w_ffn2_up'], 'v_w_ffn2_down': out['v_w_ffn2_down']}


def _loss(weights, diff, rest, loss_target):
    with _jax.named_scope("forward"):
        args = {**rest, TWIN_DIFF_INPUT: diff, **{k: w.astype(_WEIGHT_DTYPES[k]) for k, w in weights.items()}}
        y = _forward(args)
    with _jax.named_scope("loss_head"):
        err = _jnp.square(y.astype(_jnp.float32) - loss_target)
        return 0.5 * _jnp.sum(_jnp.mean(err, axis=-1)) if err.ndim else 0.5 * err


def _adamw(w, g, m, v):
    m = ADAM_B1 * m + (1.0 - ADAM_B1) * g
    v = ADAM_B2 * v + (1.0 - ADAM_B2) * _jnp.square(g)
    m_hat = m / (1.0 - ADAM_B1 ** ADAM_STEP)
    v_hat = v / (1.0 - ADAM_B2 ** ADAM_STEP)
    delta = -ADAM_LR * (m_hat / (_jnp.sqrt(v_hat) + ADAM_EPS) + ADAM_WD * w)
    return delta, m, v


def reference(x, positions, g_ffn1_pre, g_ffn1_post, w_ffn1_gate, w_ffn1_up, w_ffn1_down, g_mix_pre, g_mix_post, w_in, conv_w, conv_b, w_lru_a, b_lru_a, w_lru_x, b_lru_x, lru_lambda, q_a_norm, w_q_b, kv_a_norm, w_kv_b, w_out, g_ffn2_pre, g_ffn2_post, w_ffn2_gate, w_ffn2_up, w_ffn2_down, loss_target, m_g_ffn1_pre, m_g_ffn1_post, m_w_ffn1_gate, m_w_ffn1_up, m_w_ffn1_down, m_g_mix_pre, m_g_mix_post, m_w_in, m_conv_w, m_conv_b, m_w_lru_a, m_b_lru_a, m_w_lru_x, m_b_lru_x, m_lru_lambda, m_q_a_norm, m_w_q_b, m_kv_a_norm, m_w_kv_b, m_w_out, m_g_ffn2_pre, m_g_ffn2_post, m_w_ffn2_gate, m_w_ffn2_up, m_w_ffn2_down, v_g_ffn1_pre, v_g_ffn1_post, v_w_ffn1_gate, v_w_ffn1_up, v_w_ffn1_down, v_g_mix_pre, v_g_mix_post, v_w_in, v_conv_w, v_conv_b, v_w_lru_a, v_b_lru_a, v_w_lru_x, v_b_lru_x, v_lru_lambda, v_q_a_norm, v_w_q_b, v_kv_a_norm, v_w_kv_b, v_w_out, v_g_ffn2_pre, v_g_ffn2_post, v_w_ffn2_gate, v_w_ffn2_up, v_w_ffn2_down):
    given = dict(x=x, positions=positions, g_ffn1_pre=g_ffn1_pre, g_ffn1_post=g_ffn1_post, w_ffn1_gate=w_ffn1_gate, w_ffn1_up=w_ffn1_up, w_ffn1_down=w_ffn1_down, g_mix_pre=g_mix_pre, g_mix_post=g_mix_post, w_in=w_in, conv_w=conv_w, conv_b=conv_b, w_lru_a=w_lru_a, b_lru_a=b_lru_a, w_lru_x=w_lru_x, b_lru_x=b_lru_x, lru_lambda=lru_lambda, q_a_norm=q_a_norm, w_q_b=w_q_b, kv_a_norm=kv_a_norm, w_kv_b=w_kv_b, w_out=w_out, g_ffn2_pre=g_ffn2_pre, g_ffn2_post=g_ffn2_post, w_ffn2_gate=w_ffn2_gate, w_ffn2_up=w_ffn2_up, w_ffn2_down=w_ffn2_down, loss_target=loss_target, m_g_ffn1_pre=m_g_ffn1_pre, m_g_ffn1_post=m_g_ffn1_post, m_w_ffn1_gate=m_w_ffn1_gate, m_w_ffn1_up=m_w_ffn1_up, m_w_ffn1_down=m_w_ffn1_down, m_g_mix_pre=m_g_mix_pre, m_g_mix_post=m_g_mix_post, m_w_in=m_w_in, m_conv_w=m_conv_w, m_conv_b=m_conv_b, m_w_lru_a=m_w_lru_a, m_b_lru_a=m_b_lru_a, m_w_lru_x=m_w_lru_x, m_b_lru_x=m_b_lru_x, m_lru_lambda=m_lru_lambda, m_q_a_norm=m_q_a_norm, m_w_q_b=m_w_q_b, m_kv_a_norm=m_kv_a_norm, m_w_kv_b=m_w_kv_b, m_w_out=m_w_out, m_g_ffn2_pre=m_g_ffn2_pre, m_g_ffn2_post=m_g_ffn2_post, m_w_ffn2_gate=m_w_ffn2_gate, m_w_ffn2_up=m_w_ffn2_up, m_w_ffn2_down=m_w_ffn2_down, v_g_ffn1_pre=v_g_ffn1_pre, v_g_ffn1_post=v_g_ffn1_post, v_w_ffn1_gate=v_w_ffn1_gate, v_w_ffn1_up=v_w_ffn1_up, v_w_ffn1_down=v_w_ffn1_down, v_g_mix_pre=v_g_mix_pre, v_g_mix_post=v_g_mix_post, v_w_in=v_w_in, v_conv_w=v_conv_w, v_conv_b=v_conv_b, v_w_lru_a=v_w_lru_a, v_b_lru_a=v_b_lru_a, v_w_lru_x=v_w_lru_x, v_b_lru_x=v_b_lru_x, v_lru_lambda=v_lru_lambda, v_q_a_norm=v_q_a_norm, v_w_q_b=v_w_q_b, v_kv_a_norm=v_kv_a_norm, v_w_kv_b=v_w_kv_b, v_w_out=v_w_out, v_g_ffn2_pre=v_g_ffn2_pre, v_g_ffn2_post=v_g_ffn2_post, v_w_ffn2_gate=v_w_ffn2_gate, v_w_ffn2_up=v_w_ffn2_up, v_w_ffn2_down=v_w_ffn2_down)
    weights = {n: given[n] for n in TWIN_WEIGHTS}
    shared = {n: given[n] for n in SHARED_INPUTS}
    per_example = {n: given[n] for n in ['x', 'positions']}
    grad_fn = _jax.value_and_grad(_loss, argnums=(0, 1))

    def one_microbatch(ex, loss_target):
        ex = dict(ex)
        diff = ex.pop(TWIN_DIFF_INPUT)
        return grad_fn(weights, diff, {**shared, **ex}, loss_target)

    if N_MICROBATCH == 1:
        loss, (grad_w, grad_x) = one_microbatch(per_example, given["loss_target"])
    else:
        def body(carry, xs):
            loss_sum, grad_sum = carry
            l_k, (gw_k, gx_k) = one_microbatch(xs[0], xs[1])
            with _jax.named_scope("update"):
                return (loss_sum + l_k, _jax.tree.map(_jnp.add, grad_sum, gw_k)), gx_k

        init = (_jnp.zeros((), _jnp.float32), _jax.tree.map(_jnp.zeros_like, weights))
        (loss, grad_w), grad_x = _jax.lax.scan(body, init, (per_example, given["loss_target"]))
    with _jax.named_scope("update"):
        delta_w, new_m, new_v = {}, {}, {}
        for n in TWIN_WEIGHTS:
            delta_w[n], new_m[n], new_v[n] = _adamw(weights[n], grad_w[n], given["m_" + n], given["v_" + n])
    return (loss, grad_x, *[grad_w[n] for n in TWIN_WEIGHTS], *[delta_w[n] for n in TWIN_WEIGHTS],
            *[new_m[n] for n in TWIN_WEIGHTS], *[new_v[n] for n in TWIN_WEIGHTS])
```

```python
import math

import jax
import jax.numpy as jnp
from jax import lax
from jax.experimental import pallas as pl
from jax.experimental.pallas import tpu as pltpu

F32 = jnp.float32
BF16 = jnp.bfloat16
MESH = pl.DeviceIdType.MESH

N_DEV = 8
D_MODEL = 1024
D_FF = 2816
FF_SHARD = D_FF // N_DEV
FF_PAD = 384
FF_VIRT = N_DEV * FF_PAD
LRU_W = 512
N_HEADS = 8
HEAD_PAD = 128
QK_NOPE = 64
QK_ROPE = 32
V_DIM = 64
Q_LORA = 384
KV_LORA = 256
IN_COLS = 2 * LRU_W + Q_LORA + KV_LORA + QK_ROPE
IN_SHARD = IN_COLS // N_DEV
IN_SHARD_PAD = 256
IN_PAD = 1792
QB_SHARD = 96
CONV_W = 4
CHUNK = 64
EPS = 1e-6
LRU_C = 8.0
ROPE_THETA = 10000.0
ATT_SCALE = (QK_NOPE + QK_ROPE) ** -0.5

ADAM_LR = 0.001
ADAM_B1 = 0.9
ADAM_B2 = 0.999
ADAM_EPS = 1e-08
ADAM_WD = 0.01
ADAM_STEP = 10

VMEM_LIMIT = 56 * 1024 * 1024


def _params(**kw):
    return pltpu.CompilerParams(vmem_limit_bytes=VMEM_LIMIT, **kw)


def _full(shape):
    return pl.BlockSpec(shape, lambda *_: (0,) * len(shape))


def _dot(a, b):
    return jnp.dot(a, b, preferred_element_type=F32)


def _dot_nt(a, b):
    return lax.dot_general(a, b, (((1,), (1,)), ((), ())), preferred_element_type=F32)


def _dot_tn(a, b):
    return lax.dot_general(a, b, (((0,), (0,)), ((), ())), preferred_element_type=F32)


def _rms(x, g):
    r = lax.rsqrt(jnp.mean(x * x, axis=-1, keepdims=True) + EPS)
    return x * r * g


def _rms_bwd(x, g, dy):
    r = lax.rsqrt(jnp.mean(x * x, axis=-1, keepdims=True) + EPS)
    xh = x * r
    dg = jnp.sum(dy * xh, axis=0, keepdims=True)
    dxh = dy * g
    dx = r * (dxh - xh * jnp.mean(dxh * xh, axis=-1, keepdims=True))
    return dx, dg


def _sigmoid(x):
    return 1.0 / (1.0 + jnp.exp(-x))


_GELU_C = math.sqrt(2.0 / math.pi)


def _gelu(x):
    t = jnp.tanh(_GELU_C * (x + 0.044715 * x * x * x))
    return 0.5 * x * (1.0 + t)


def _gelu_grad(x):
    t = jnp.tanh(_GELU_C * (x + 0.044715 * x * x * x))
    return 0.5 * (1.0 + t) + 0.5 * x * (1.0 - t * t) * _GELU_C * (1.0 + 3.0 * 0.044715 * x * x)


def _tile_rows(t):
    return 512 if t >= 2048 else t // 2


def _dev_index(p):
    return 4 * p[0] + 2 * p[1] + p[2]


def _all_gather(shards, pads, dtypes):
    n = len(shards)

    def body(*refs):
        ins, outs, stages = refs[:n], refs[n:2 * n], refs[2 * n:3 * n]
        send_sems, recv_sems, local_sems = refs[3 * n:]
        x, y, c = lax.axis_index("x"), lax.axis_index("y"), lax.axis_index("c")
        me, sibling = (x, y, c), (x, y, 1 - c)
        chips = [(1 - x, y), (x, 1 - y), (1 - x, 1 - y)]

        for k in range(n):
            r, cc = ins[k].shape
            if (r, cc) != tuple(stages[k].shape):
                stages[k][...] = jnp.zeros(stages[k].shape, stages[k].dtype)
            stages[k][:r, :cc] = ins[k][...].astype(stages[k].dtype)

        def copy(k, s, block, to, src=None):
            rows = outs[k].at[_dev_index(block)]
            return pltpu.make_async_remote_copy(
                src_ref=rows if src is None else src, dst_ref=rows,
                send_sem=send_sems.at[7 * k + s], recv_sem=recv_sems.at[7 * k + s],
                device_id=to, device_id_type=MESH)

        mine, first, passed = [], [], []
        for k in range(n):
            cp = pltpu.make_async_copy(stages[k], outs[k].at[_dev_index(me)], local_sems.at[k])
            cp.start()
            mine.append(cp)
            sends = [copy(k, 0, me, sibling, src=stages[k])]
            sends += [copy(k, 1 + j, me, (*chip, c), src=stages[k]) for j, chip in enumerate(chips)]
            for cp in sends:
                cp.start()
            first += sends
        for j, chip in enumerate(chips):
            for k in range(n):
                copy(k, 1 + j, (*chip, c), me).wait_recv()
                cp = copy(k, 4 + j, (*chip, c), sibling)
                cp.start()
                passed.append(cp)
        for k in range(n):
            copy(k, 0, sibling, me).wait_recv()
        for j, chip in enumerate(chips):
            for k in range(n):
                copy(k, 4 + j, (*chip, 1 - c), me).wait_recv()
        for cp in first + passed:
            cp.wait_send()
        for cp in mine:
            cp.wait()

    return pl.pallas_call(
        body, name="weights_all_gather",
        out_shape=[jax.ShapeDtypeStruct((N_DEV,) + tuple(p), d) for p, d in zip(pads, dtypes)],
        in_specs=[pl.BlockSpec(memory_space=pltpu.VMEM)] * n,
        out_specs=[pl.BlockSpec(memory_space=pl.ANY)] * n,
        scratch_shapes=[pltpu.VMEM(tuple(p), d) for p, d in zip(pads, dtypes)] + [
            pltpu.SemaphoreType.DMA((7 * n,)), pltpu.SemaphoreType.DMA((7 * n,)), pltpu.SemaphoreType.DMA((n,))],
        compiler_params=_params(),
    )(*shards)


def _exchange(blocks):
    n = len(blocks)

    def body(*refs):
        ins, outs = refs[:n], refs[n:2 * n]
        send_sems, recv_sems, local_sems = refs[2 * n:]
        x, y, c = lax.axis_index("x"), lax.axis_index("y"), lax.axis_index("c")
        me = _dev_index((x, y, c))
        copies = []
        for k in range(n):
            cp = pltpu.make_async_copy(ins[k].at[me], outs[k].at[me], local_sems.at[k])
            cp.start()
            copies.append(cp)
        remote = []
        for s in range(1, N_DEV):
            fx, fy, fc = (s >> 2) & 1, (s >> 1) & 1, s & 1
            peer = (1 - x if fx else x, 1 - y if fy else y, 1 - c if fc else c)
            for k in range(n):
                cp = pltpu.make_async_remote_copy(
                    src_ref=ins[k].at[_dev_index(peer)], dst_ref=outs[k].at[me],
                    send_sem=send_sems.at[7 * k + s - 1], recv_sem=recv_sems.at[7 * k + s - 1],
                    device_id=peer, device_id_type=MESH)
                cp.start()
                remote.append(cp)
        for cp in remote:
            cp.wait_recv()
        for cp in remote:
            cp.wait_send()
        for cp in copies:
            cp.wait()

    return pl.pallas_call(
        body, name="grad_exchange",
        out_shape=[jax.ShapeDtypeStruct(b.shape, b.dtype) for b in blocks],
        in_specs=[pl.BlockSpec(memory_space=pl.ANY)] * n,
        out_specs=[pl.BlockSpec(memory_space=pl.ANY)] * n,
        scratch_shapes=[pltpu.SemaphoreType.DMA((7 * n,)), pltpu.SemaphoreType.DMA((7 * n,)),
                        pltpu.SemaphoreType.DMA((n,))],
        compiler_params=_params(),
    )(*blocks)


FF_BLOCKS_PER_STEP = 2
FF_CHUNK = FF_BLOCKS_PER_STEP * FF_PAD


def _ffn_fwd(h0, g_pre, g_post, wg, wu, wd):
    t = h0.shape[0]
    tt = _tile_rows(t)
    nb = FF_BLOCKS_PER_STEP
    n_chunks = N_DEV // nb

    def body(h0_ref, gpre_ref, gpost_ref, wg_ref, wu_ref, wd_ref,
             h1_ref, f_ref, n1_ref, gt_ref, up_ref, acc_ref, n1s_ref):
        j = pl.program_id(1)

        @pl.when(j == 0)
        def _():
            n1 = _rms(h0_ref[...], gpre_ref[...]).astype(BF16)
            n1s_ref[...] = n1
            n1_ref[...] = n1
            acc_ref[...] = jnp.zeros_like(acc_ref)

        n1 = n1s_ref[...]
        for b in range(nb):
            gt = _dot(n1, wg_ref[b])
            up = _dot(n1, wu_ref[b])
            gt_ref[:, b * FF_PAD:(b + 1) * FF_PAD] = gt.astype(BF16)
            up_ref[:, b * FF_PAD:(b + 1) * FF_PAD] = up.astype(BF16)
            act = (gt * _sigmoid(gt) * up).astype(BF16)
            acc_ref[...] += _dot(act, wd_ref[b])

        @pl.when(j == n_chunks - 1)
        def _():
            f = acc_ref[...]
            f_ref[...] = f
            h1_ref[...] = h0_ref[...] + 0.5 * _rms(f, gpost_ref[...])

    row = lambda i, j: (i, 0)
    return pl.pallas_call(
        body, name="ffn_fwd",
        grid=(t // tt, n_chunks),
        in_specs=[pl.BlockSpec((tt, D_MODEL), row), _full((1, D_MODEL)), _full((1, D_MODEL)),
                  pl.BlockSpec((nb, D_MODEL, FF_PAD), lambda i, j: (j, 0, 0)),
                  pl.BlockSpec((nb, D_MODEL, FF_PAD), lambda i, j: (j, 0, 0)),
                  pl.BlockSpec((nb, FF_PAD, D_MODEL), lambda i, j: (j, 0, 0))],
        out_specs=[pl.BlockSpec((tt, D_MODEL), row), pl.BlockSpec((tt, D_MODEL), row),
                   pl.BlockSpec((tt, D_MODEL), row),
                   pl.BlockSpec((tt, FF_CHUNK), lambda i, j: (i, j)),
                   pl.BlockSpec((tt, FF_CHUNK), lambda i, j: (i, j))],
        out_shape=[jax.ShapeDtypeStruct((t, D_MODEL), F32), jax.ShapeDtypeStruct((t, D_MODEL), F32),
                   jax.ShapeDtypeStruct((t, D_MODEL), BF16),
                   jax.ShapeDtypeStruct((t, FF_VIRT), BF16), jax.ShapeDtypeStruct((t, FF_VIRT), BF16)],
        scratch_shapes=[pltpu.VMEM((tt, D_MODEL), F32), pltpu.VMEM((tt, D_MODEL), BF16)],
        compiler_params=_params(),
    )(h0, g_pre, g_post, wg, wu, wd)


def _ffn_bwd(dh1, h0, f, gt, up, g_pre, g_post, wg, wu, wd):
    t = h0.shape[0]
    tt = _tile_rows(t)
    nb = FF_BLOCKS_PER_STEP
    n_chunks = N_DEV // nb

    def body(dh1_ref, h0_ref, f_ref, gt_ref, up_ref, gpre_ref, gpost_ref, wg_ref, wu_ref, wd_ref,
             dh0_ref, df_ref, act_ref, dgt_ref, dup_ref, dgpre_ref, dgpost_ref, acc_ref, dfs_ref):
        i, j = pl.program_id(0), pl.program_id(1)

        @pl.when((i == 0) & (j == 0))
        def _():
            dgpre_ref[...] = jnp.zeros_like(dgpre_ref)
            dgpost_ref[...] = jnp.zeros_like(dgpost_ref)

        @pl.when(j == 0)
        def _():
            df, dg = _rms_bwd(f_ref[...], gpost_ref[...], 0.5 * dh1_ref[...])
            dgpost_ref[...] += dg
            dfb = df.astype(BF16)
            dfs_ref[...] = dfb
            df_ref[...] = dfb
            acc_ref[...] = jnp.zeros_like(acc_ref)

        dfb = dfs_ref[...]
        for b in range(nb):
            cols = slice(b * FF_PAD, (b + 1) * FF_PAD)
            da = _dot_nt(dfb, wd_ref[b])
            g = gt_ref[:, cols].astype(F32)
            u = up_ref[:, cols].astype(F32)
            s = _sigmoid(g)
            sl = g * s
            act_ref[:, cols] = (sl * u).astype(BF16)
            dgt = (da * u * (s * (1.0 + g * (1.0 - s)))).astype(BF16)
            dup = (da * sl).astype(BF16)
            dgt_ref[:, cols] = dgt
            dup_ref[:, cols] = dup
            acc_ref[...] += _dot_nt(dgt, wg_ref[b]) + _dot_nt(dup, wu_ref[b])

        @pl.when(j == n_chunks - 1)
        def _():
            dx, dg = _rms_bwd(h0_ref[...], gpre_ref[...], acc_ref[...])
            dgpre_ref[...] += dg
            dh0_ref[...] = dh1_ref[...] + dx

    row = lambda i, j: (i, 0)
    chunk = lambda i, j: (i, j)
    return pl.pallas_call(
        body, name="ffn_bwd",
        grid=(t // tt, n_chunks),
        in_specs=[pl.BlockSpec((tt, D_MODEL), row), pl.BlockSpec((tt, D_MODEL), row),
                  pl.BlockSpec((tt, D_MODEL), row),
                  pl.BlockSpec((tt, FF_CHUNK), chunk), pl.BlockSpec((tt, FF_CHUNK), chunk),
                  _full((1, D_MODEL)), _full((1, D_MODEL)),
                  pl.BlockSpec((nb, D_MODEL, FF_PAD), lambda i, j: (j, 0, 0)),
                  pl.BlockSpec((nb, D_MODEL, FF_PAD), lambda i, j: (j, 0, 0)),
                  pl.BlockSpec((nb, FF_PAD, D_MODEL), lambda i, j: (j, 0, 0))],
        out_specs=[pl.BlockSpec((tt, D_MODEL), row), pl.BlockSpec((tt, D_MODEL), row),
                   pl.BlockSpec((tt, FF_CHUNK), chunk), pl.BlockSpec((tt, FF_CHUNK), chunk),
                   pl.BlockSpec((tt, FF_CHUNK), chunk), _full((1, D_MODEL)), _full((1, D_MODEL))],
        out_shape=[jax.ShapeDtypeStruct((t, D_MODEL), F32), jax.ShapeDtypeStruct((t, D_MODEL), BF16),
                   jax.ShapeDtypeStruct((t, FF_VIRT), BF16), jax.ShapeDtypeStruct((t, FF_VIRT), BF16),
                   jax.ShapeDtypeStruct((t, FF_VIRT), BF16),
                   jax.ShapeDtypeStruct((1, D_MODEL), F32), jax.ShapeDtypeStruct((1, D_MODEL), F32)],
        scratch_shapes=[pltpu.VMEM((tt, D_MODEL), F32), pltpu.VMEM((tt, D_MODEL), BF16)],
        compiler_params=_params(),
    )(dh1, h0, f, gt, up, g_pre, g_post, wg, wu, wd)


def _mm_tn(a, b, m, tm, tn, blocked, name):
    t, n = b.shape
    tk = min(t, 2048)
    nk = t // tk

    def body(a_ref, b_ref, out_ref, acc_ref):
        k = pl.program_id(2)

        @pl.when(k == 0)
        def _():
            acc_ref[...] = jnp.zeros_like(acc_ref)

        acc_ref[...] += _dot_tn(a_ref[...], b_ref[...])

        @pl.when(k == nk - 1)
        def _():
            out_ref[...] = acc_ref[...].astype(out_ref.dtype)

    if blocked:
        out_shape = jax.ShapeDtypeStruct((n // tn, m, tn), BF16)
        out_spec = pl.BlockSpec((None, tm, tn), lambda i, j, k: (j, i, 0))
    else:
        out_shape = jax.ShapeDtypeStruct((m, n), BF16)
        out_spec = pl.BlockSpec((tm, tn), lambda i, j, k: (i, j))
    return pl.pallas_call(
        body, name=name,
        grid=(m // tm, n // tn, nk),
        in_specs=[pl.BlockSpec((tk, tm), lambda i, j, k: (k, i)),
                  pl.BlockSpec((tk, tn), lambda i, j, k: (k, j))],
        out_specs=out_spec, out_shape=out_shape,
        scratch_shapes=[pltpu.VMEM((tm, tn), F32)],
        compiler_params=_params(),
    )(a, b)


_SPLITS = (0, LRU_W, 2 * LRU_W, 2 * LRU_W + Q_LORA, 2 * LRU_W + Q_LORA + KV_LORA, IN_PAD)


def _mixer_in(h1, g_pre, w_in):
    t = h1.shape[0]
    tt = _tile_rows(t)
    widths = [_SPLITS[k + 1] - _SPLITS[k] for k in range(5)]

    def body(h_ref, g_ref, w_ref, n_ref, *outs):
        n = _rms(h_ref[...], g_ref[...]).astype(BF16)
        n_ref[...] = n
        proj = _dot(n, w_ref[...])
        for k in range(5):
            outs[k][...] = proj[:, _SPLITS[k]:_SPLITS[k + 1]]

    row = lambda i: (i, 0)
    return pl.pallas_call(
        body, name="mixer_in",
        grid=(t // tt,),
        in_specs=[pl.BlockSpec((tt, D_MODEL), row), _full((1, D_MODEL)), _full((D_MODEL, IN_PAD))],
        out_specs=[pl.BlockSpec((tt, D_MODEL), row)] + [pl.BlockSpec((tt, w), row) for w in widths],
        out_shape=[jax.ShapeDtypeStruct((t, D_MODEL), BF16)] + [jax.ShapeDtypeStruct((t, w), F32) for w in widths],
        compiler_params=_params(),
    )(h1, g_pre, w_in)


def _mixer_in_bwd(dh2, h1, g_pre, w_in, dparts):
    t = h1.shape[0]
    tt = _tile_rows(t)
    widths = [_SPLITS[k + 1] - _SPLITS[k] for k in range(5)]

    def body(dh2_ref, h_ref, g_ref, w_ref, p0, p1, p2, p3, p4, dh1_ref, dproj_ref, dg_ref):
        i = pl.program_id(0)

        @pl.when(i == 0)
        def _():
            dg_ref[...] = jnp.zeros_like(dg_ref)

        dproj = jnp.concatenate([p[...] for p in (p0, p1, p2, p3, p4)], axis=1)
        dproj_ref[...] = dproj
        dn = _dot_nt(dproj, w_ref[...])
        dx, dg = _rms_bwd(h_ref[...], g_ref[...], dn)
        dg_ref[...] += dg
        dh1_ref[...] = dh2_ref[...] + dx

    row = lambda i: (i, 0)
    return pl.pallas_call(
        body, name="mixer_in_bwd",
        grid=(t // tt,),
        in_specs=[pl.BlockSpec((tt, D_MODEL), row), pl.BlockSpec((tt, D_MODEL), row), _full((1, D_MODEL)),
                  _full((D_MODEL, IN_PAD))] + [pl.BlockSpec((tt, w), row) for w in widths],
        out_specs=[pl.BlockSpec((tt, D_MODEL), row), pl.BlockSpec((tt, IN_PAD), row), _full((1, D_MODEL))],
        out_shape=[jax.ShapeDtypeStruct((t, D_MODEL), F32), jax.ShapeDtypeStruct((t, IN_PAD), BF16),
                   jax.ShapeDtypeStruct((1, D_MODEL), F32)],
        compiler_params=_params(),
    )(dh2, h1, g_pre, w_in, *dparts)


def _conv(x_prev8, x_tile, w_ref, b_ref):
    tt = x_tile.shape[0]
    xx = jnp.concatenate([x_prev8, x_tile], axis=0)
    y = b_ref[...] + w_ref[CONV_W - 1:CONV_W, :] * x_tile
    for k in range(CONV_W - 1):
        y = y + w_ref[k:k + 1, :] * pltpu.roll(xx, CONV_W - 1 - k, 0)[8:8 + tt]
    return y


def _neg_expm1(z):
    series = -z * (1.0 + z * 0.5 * (1.0 + z / 3.0 * (1.0 + z * 0.25 * (1.0 + z * 0.2 * (1.0 + z / 6.0)))))
    return jnp.where(z > -0.3, series, 1.0 - jnp.exp(z))


def _lru_elementwise(xc, pre_a, pre_i, lam):
    ra = _sigmoid(pre_a)
    ri = _sigmoid(pre_i)
    neg = -lam
    softplus = jnp.maximum(neg, 0.0) + jnp.log(1.0 + jnp.exp(-jnp.abs(neg)))
    log_a = -LRU_C * ra * softplus
    a = jnp.exp(log_a)
    mult = jnp.sqrt(_neg_expm1(2.0 * log_a))
    return a, mult * (ri * xc)


def _lru_gates(xl, conv_w, conv_b, wa, wx, ba, bx, lam):
    t = xl.shape[0]
    tt = _tile_rows(t)
    r8 = tt // 8

    def body(prev_ref, x_ref, cw_ref, cb_ref, wa_ref, wx_ref, ba_ref, bx_ref, lam_ref, xc_ref, a_ref, u_ref):
        i = pl.program_id(0)
        prev = jnp.where(i == 0, 0.0, prev_ref[...])
        xc = _conv(prev, x_ref[...], cw_ref, cb_ref)
        xc_ref[...] = xc
        xb = xc.astype(BF16)
        a, u = _lru_elementwise(xc, _dot(xb, wa_ref[...]) + ba_ref[...], _dot(xb, wx_ref[...]) + bx_ref[...],
                                lam_ref[...])
        a_ref[...] = a
        u_ref[...] = u

    row = lambda i: (i, 0)
    vec = _full((1, LRU_W))
    return pl.pallas_call(
        body, name="lru_gates",
        grid=(t // tt,),
        in_specs=[pl.BlockSpec((8, LRU_W), lambda i: (jnp.maximum(i * r8 - 1, 0), 0)),
                  pl.BlockSpec((tt, LRU_W), row), _full((8, LRU_W)), vec,
                  _full((LRU_W, LRU_W)), _full((LRU_W, LRU_W)), vec, vec, vec],
        out_specs=[pl.BlockSpec((tt, LRU_W), row)] * 3,
        out_shape=[jax.ShapeDtypeStruct((t, LRU_W), F32)] * 3,
        compiler_params=_params(),
    )(xl, xl, conv_w, conv_b, wa, wx, ba, bx, lam)


def _lru_scan(a, u, reverse):
    t = a.shape[0]
    nblk = t // 8

    def body(a_ref, u_ref, h_ref):
        def fwd(blk, h):
            base = pl.multiple_of(blk * 8, 8)
            for k in range(8):
                h = a_ref[pl.ds(base + k, 1), :] * h + u_ref[pl.ds(base + k, 1), :]
                h_ref[pl.ds(base + k, 1), :] = h
            return h

        def bwd(n, carry):
            base = pl.multiple_of((nblk - 1 - n) * 8, 8)
            for k in range(7, -1, -1):
                g = u_ref[pl.ds(base + k, 1), :] + carry
                h_ref[pl.ds(base + k, 1), :] = g
                carry = a_ref[pl.ds(base + k, 1), :] * g
            return carry

        lax.fori_loop(0, nblk, bwd if reverse else fwd, jnp.zeros((1, LRU_W), F32))

    return pl.pallas_call(
        body, name="lru_scan_rev" if reverse else "lru_scan",
        in_specs=[pl.BlockSpec(memory_space=pltpu.VMEM)] * 2,
        out_specs=pl.BlockSpec(memory_space=pltpu.VMEM),
        out_shape=jax.ShapeDtypeStruct((t, LRU_W), F32),
        compiler_params=_params(),
    )(a, u)


def _lru_gates_bwd(dh, h, xc, wa, wx, ba, bx, lam):
    t = dh.shape[0]
    tt = _tile_rows(t)
    r8 = tt // 8

    def body(dh_ref, hprev_ref, h_ref, xc_ref, wa_ref, wx_ref, ba_ref, bx_ref, lam_ref,
             dxc_ref, dwa_ref, dwx_ref, dba_ref, dbx_ref, dlam_ref):
        i = pl.program_id(0)

        @pl.when(i == 0)
        def _():
            for r in (dwa_ref, dwx_ref, dba_ref, dbx_ref, dlam_ref):
                r[...] = jnp.zeros_like(r)

        prev = jnp.where(i == 0, 0.0, hprev_ref[...])
        h_before = pltpu.roll(jnp.concatenate([prev, h_ref[...]], axis=0), 1, 0)[8:8 + tt]
        dh_t = dh_ref[...]
        xc = xc_ref[...]
        xb = xc.astype(BF16)
        pre_a = _dot(xb, wa_ref[...]) + ba_ref[...]
        pre_i = _dot(xb, wx_ref[...]) + bx_ref[...]
        _, vjp = jax.vjp(_lru_elementwise, xc, pre_a, pre_i, lam_ref[...])
        dxc, dpre_a, dpre_i, dlam = vjp((dh_t * h_before, dh_t))
        da_b = dpre_a.astype(BF16)
        di_b = dpre_i.astype(BF16)
        dxc_ref[...] = dxc + _dot_nt(da_b, wa_ref[...]) + _dot_nt(di_b, wx_ref[...])
        dwa_ref[...] += _dot_tn(xb, da_b)
        dwx_ref[...] += _dot_tn(xb, di_b)
        dba_ref[...] += jnp.sum(dpre_a, axis=0, keepdims=True)
        dbx_ref[...] += jnp.sum(dpre_i, axis=0, keepdims=True)
        dlam_ref[...] += dlam

    row = lambda i: (i, 0)
    vec = _full((1, LRU_W))
    sq = _full((LRU_W, LRU_W))
    return pl.pallas_call(
        body, name="lru_gates_bwd",
        grid=(t // tt,),
        in_specs=[pl.BlockSpec((tt, LRU_W), row),
                  pl.BlockSpec((8, LRU_W), lambda i: (jnp.maximum(i * r8 - 1, 0), 0)),
                  pl.BlockSpec((tt, LRU_W), row), pl.BlockSpec((tt, LRU_W), row), sq, sq, vec, vec, vec],
        out_specs=[pl.BlockSpec((tt, LRU_W), row), sq, sq, vec, vec, vec],
        out_shape=[jax.ShapeDtypeStruct((t, LRU_W), F32), jax.ShapeDtypeStruct((LRU_W, LRU_W), F32),
                   jax.ShapeDtypeStruct((LRU_W, LRU_W), F32)] + [jax.ShapeDtypeStruct((1, LRU_W), F32)] * 3,
        compiler_params=_params(),
    )(dh, h, h, xc, wa, wx, ba, bx, lam)


def _conv_bwd(dxc, xl, conv_w):
    t = dxc.shape[0]
    tt = _tile_rows(t)
    r8 = tt // 8
    last = t // 8 - 1

    def body(d_ref, dnext_ref, xprev_ref, x_ref, cw_ref, dxl_ref, dw_ref, db_ref):
        i = pl.program_id(0)

        @pl.when(i == 0)
        def _():
            dw_ref[...] = jnp.zeros_like(dw_ref)
            db_ref[...] = jnp.zeros_like(db_ref)

        d = d_ref[...]
        nxt = jnp.where(i == pl.num_programs(0) - 1, 0.0, dnext_ref[...])
        dd = jnp.concatenate([d, nxt], axis=0)
        dx = cw_ref[CONV_W - 1:CONV_W, :] * d
        for k in range(CONV_W - 1):
            shift = CONV_W - 1 - k
            dx = dx + cw_ref[k:k + 1, :] * pltpu.roll(dd, tt + 8 - shift, 0)[:tt]
        dxl_ref[...] = dx.astype(BF16)

        prev = jnp.where(i == 0, 0.0, xprev_ref[...])
        xx = jnp.concatenate([prev, x_ref[...]], axis=0)
        rows = []
        for k in range(CONV_W):
            shifted = x_ref[...] if k == CONV_W - 1 else pltpu.roll(xx, CONV_W - 1 - k, 0)[8:8 + tt]
            rows.append(jnp.sum(d * shifted, axis=0, keepdims=True))
        rows.append(jnp.zeros((8 - CONV_W, LRU_W), F32))
        dw_ref[...] += jnp.concatenate(rows, axis=0)
        db_ref[...] += jnp.sum(d, axis=0, keepdims=True)

    row = lambda i: (i, 0)
    return pl.pallas_call(
        body, name="conv_bwd",
        grid=(t // tt,),
        in_specs=[pl.BlockSpec((tt, LRU_W), row),
                  pl.BlockSpec((8, LRU_W), lambda i: (jnp.minimum((i + 1) * r8, last), 0)),
                  pl.BlockSpec((8, LRU_W), lambda i: (jnp.maximum(i * r8 - 1, 0), 0)),
                  pl.BlockSpec((tt, LRU_W), row), _full((8, LRU_W))],
        out_specs=[pl.BlockSpec((tt, LRU_W), row), _full((8, LRU_W)), _full((1, LRU_W))],
        out_shape=[jax.ShapeDtypeStruct((t, LRU_W), BF16), jax.ShapeDtypeStruct((8, LRU_W), F32),
                   jax.ShapeDtypeStruct((1, LRU_W), F32)],
        compiler_params=_params(),
    )(dxc, dxc, xl, xl, conv_w)


def _rope(x, c, s1, s2):
    n = x.shape[-1]
    return x * c + pltpu.roll(x, n - 16, 1) * s1 + pltpu.roll(x, 16, 1) * s2


def _rope_t(d, c, s1, s2):
    n = d.shape[-1]
    return d * c + pltpu.roll(d * s1, 16, 1) + pltpu.roll(d * s2, n - 16, 1)


def _mla_proj(qlat, kvlat, krope, tables, gq, gkv, wq, wk, wv):
    t = qlat.shape[0]
    tt = _tile_rows(t)
    hw = N_HEADS * HEAD_PAD

    def body(q_ref, kv_ref, kr_ref, cq, s1q, s2q, ck, s1k, s2k, gq_ref, gkv_ref, wq_ref, wk_ref, wv_ref,
             qo_ref, ko_ref, vo_ref, qn_ref, kva_ref):
        qn = _rms(q_ref[...], gq_ref[...]).astype(BF16)
        kvn = _rms(kv_ref[...], gkv_ref[...]).astype(BF16)
        kr = _rope(kr_ref[...], ck[...], s1k[...], s2k[...]).astype(BF16)
        kva = jnp.concatenate([kvn, kr], axis=1)
        qn_ref[...] = qn
        kva_ref[...] = kva
        q = _dot(qn, wq_ref[...])
        tile = lambda r: jnp.tile(r[...], (1, N_HEADS))
        qo_ref[...] = _rope(q, tile(cq), tile(s1q), tile(s2q)).astype(BF16)
        ko_ref[...] = _dot(kva, wk_ref[...]).astype(BF16)
        vo_ref[...] = _dot(kvn, wv_ref[...]).astype(BF16)

    row = lambda i: (i, 0)
    tab = pl.BlockSpec((tt, HEAD_PAD), row)
    return pl.pallas_call(
        body, name="mla_proj",
        grid=(t // tt,),
        in_specs=[pl.BlockSpec((tt, Q_LORA), row), pl.BlockSpec((tt, KV_LORA), row), tab] + [tab] * 6 + [
            _full((1, Q_LORA)), _full((1, KV_LORA)), _full((Q_LORA, hw)), _full((KV_LORA + HEAD_PAD, hw)),
            _full((KV_LORA, hw))],
        out_specs=[pl.BlockSpec((tt, hw), row)] * 3 + [pl.BlockSpec((tt, Q_LORA), row),
                                                        pl.BlockSpec((tt, KV_LORA + HEAD_PAD), row)],
        out_shape=[jax.ShapeDtypeStruct((t, hw), BF16)] * 3 + [
            jax.ShapeDtypeStruct((t, Q_LORA), BF16), jax.ShapeDtypeStruct((t, KV_LORA + HEAD_PAD), BF16)],
        compiler_params=_params(),
    )(qlat, kvlat, krope, *tables, gq, gkv, wq, wk, wv)


def _mla_proj_bwd(dq, dk, dv, qlat, kvlat, tables, gq, gkv, wq, wk, wv):
    t = qlat.shape[0]
    tt = _tile_rows(t)
    hw = N_HEADS * HEAD_PAD

    def body(dq_ref, dk_ref, dv_ref, q_ref, kv_ref, cq, s1q, s2q, ck, s1k, s2k, gq_ref, gkv_ref,
             wq_ref, wk_ref, wv_ref, dqr_ref, dql_ref, dkvl_ref, dkr_ref, dgq_ref, dgkv_ref):
        i = pl.program_id(0)

        @pl.when(i == 0)
        def _():
            dgq_ref[...] = jnp.zeros_like(dgq_ref)
            dgkv_ref[...] = jnp.zeros_like(dgkv_ref)

        tile = lambda r: jnp.tile(r[...], (1, N_HEADS))
        dqr = _rope_t(dq_ref[...], tile(cq), tile(s1q), tile(s2q)).astype(BF16)
        dqr_ref[...] = dqr
        dql, dg = _rms_bwd(q_ref[...], gq_ref[...], _dot_nt(dqr, wq_ref[...]))
        dgq_ref[...] += dg
        dql_ref[...] = dql.astype(BF16)
        dkva = _dot_nt(dk_ref[...], wk_ref[...])
        dkvn = dkva[:, :KV_LORA] + _dot_nt(dv_ref[...], wv_ref[...])
        dkvl, dg = _rms_bwd(kv_ref[...], gkv_ref[...], dkvn)
        dgkv_ref[...] += dg
        dkvl_ref[...] = dkvl.astype(BF16)
        dkr_ref[...] = _rope_t(dkva[:, KV_LORA:], ck[...], s1k[...], s2k[...]).astype(BF16)

    row = lambda i: (i, 0)
    tab = pl.BlockSpec((tt, HEAD_PAD), row)
    wide = pl.BlockSpec((tt, hw), row)
    return pl.pallas_call(
        body, name="mla_proj_bwd",
        grid=(t // tt,),
        in_specs=[wide, wide, wide, pl.BlockSpec((tt, Q_LORA), row), pl.BlockSpec((tt, KV_LORA), row)] + [tab] * 6 + [
            _full((1, Q_LORA)), _full((1, KV_LORA)), _full((Q_LORA, hw)), _full((KV_LORA + HEAD_PAD, hw)),
            _full((KV_LORA, hw))],
        out_specs=[wide, pl.BlockSpec((tt, Q_LORA), row), pl.BlockSpec((tt, KV_LORA), row), tab,
                   _full((1, Q_LORA)), _full((1, KV_LORA))],
        out_shape=[jax.ShapeDtypeStruct((t, hw), BF16), jax.ShapeDtypeStruct((t, Q_LORA), BF16),
                   jax.ShapeDtypeStruct((t, KV_LORA), BF16), jax.ShapeDtypeStruct((t, HEAD_PAD), BF16),
                   jax.ShapeDtypeStruct((1, Q_LORA), F32), jax.ShapeDtypeStruct((1, KV_LORA), F32)],
        compiler_params=_params(),
    )(dq, dk, dv, qlat, kvlat, *tables, gq, gkv, wq, wk, wv)


NEG = -1e30


def _chunk_mask(bq):
    r = lax.broadcasted_iota(jnp.int32, (bq, bq), 0) // CHUNK
    c = lax.broadcasted_iota(jnp.int32, (bq, bq), 1) // CHUNK
    return c <= r


def _attn_fwd(q, k, v):
    t = q.shape[0]
    bq = _tile_rows(t)
    nq = t // bq

    def body(q_ref, k_ref, v_ref, o_ref, lse_ref, m_ref, l_ref, acc_ref):
        i, j = pl.program_id(1), pl.program_id(2)

        @pl.when(j == 0)
        def _():
            m_ref[...] = jnp.full_like(m_ref, NEG)
            l_ref[...] = jnp.zeros_like(l_ref)
            acc_ref[...] = jnp.zeros_like(acc_ref)

        @pl.when(j <= i)
        def _():
            s = _dot_nt(q_ref[...], k_ref[...])
            s = jnp.where((j < i) | _chunk_mask(bq), s, NEG)
            m_old = m_ref[...]
            m_new = jnp.maximum(m_old, jnp.max(s, axis=-1, keepdims=True))
            alpha = jnp.exp(m_old - m_new)
            p = jnp.exp(s - m_new)
            l_ref[...] = alpha * l_ref[...] + jnp.sum(p, axis=-1, keepdims=True)
            acc_ref[...] = alpha * acc_ref[...] + _dot(p.astype(BF16), v_ref[...])
            m_ref[...] = m_new

        @pl.when(j == i)
        def _():
            l = l_ref[...]
            o_ref[...] = (acc_ref[...] / l).astype(BF16)
            lse_ref[...] = jnp.broadcast_to(m_ref[...] + jnp.log(l), (bq, HEAD_PAD))

    qmap = lambda h, i, j: (i, h)
    kmap = lambda h, i, j: (jnp.minimum(j, i), h)
    return pl.pallas_call(
        body, name="attn_fwd",
        grid=(N_HEADS, nq, nq),
        in_specs=[pl.BlockSpec((bq, HEAD_PAD), qmap), pl.BlockSpec((bq, HEAD_PAD), kmap),
                  pl.BlockSpec((bq, HEAD_PAD), kmap)],
        out_specs=[pl.BlockSpec((bq, HEAD_PAD), qmap), pl.BlockSpec((bq, HEAD_PAD), qmap)],
        out_shape=[jax.ShapeDtypeStruct(q.shape, BF16), jax.ShapeDtypeStruct(q.shape, F32)],
        scratch_shapes=[pltpu.VMEM((bq, 1), F32), pltpu.VMEM((bq, 1), F32), pltpu.VMEM((bq, HEAD_PAD), F32)],
        compiler_params=_params(),
    )(q, k, v)


def _attn_bwd(q, k, v, o, do, lse):
    t = q.shape[0]
    bq = _tile_rows(t)
    nq = t // bq

    def body(q_ref, k_ref, v_ref, o_ref, do_ref, lse_ref, dq_ref, dk_ref, dv_ref, dk_acc, dv_acc):
        j, i = pl.program_id(1), pl.program_id(2)

        @pl.when((j == 0) & (i == 0))
        def _():
            dq_ref[...] = jnp.zeros_like(dq_ref)

        @pl.when(i == 0)
        def _():
            dk_acc[...] = jnp.zeros_like(dk_acc)
            dv_acc[...] = jnp.zeros_like(dv_acc)

        @pl.when(i >= j)
        def _():
            qb, kb, vb, dob = q_ref[...], k_ref[...], v_ref[...], do_ref[...]
            s = _dot_nt(qb, kb)
            s = jnp.where((j < i) | _chunk_mask(bq), s, NEG)
            p = jnp.exp(s - lse_ref[:, :1])
            pb = p.astype(BF16)
            dv_acc[...] += _dot_tn(pb, dob)
            dp = _dot_nt(dob, vb)
            delta = jnp.sum(dob.astype(F32) * o_ref[...].astype(F32), axis=-1, keepdims=True)
            ds = (p * (dp - delta)).astype(BF16)
            dk_acc[...] += _dot_tn(ds, qb)
            rows = pl.ds(pl.multiple_of(i * bq, bq), bq)
            dq_ref[rows, :] += _dot(ds, kb)

        @pl.when(i == nq - 1)
        def _():
            dk_ref[...] = dk_acc[...].astype(BF16)
            dv_ref[...] = dv_acc[...].astype(BF16)

    qmap = lambda h, j, i: (jnp.maximum(i, j), h)
    kmap = lambda h, j, i: (j, h)
    return pl.pallas_call(
        body, name="attn_bwd",
        grid=(N_HEADS, nq, nq),
        in_specs=[pl.BlockSpec((bq, HEAD_PAD), qmap), pl.BlockSpec((bq, HEAD_PAD), kmap),
                  pl.BlockSpec((bq, HEAD_PAD), kmap), pl.BlockSpec((bq, HEAD_PAD), qmap),
                  pl.BlockSpec((bq, HEAD_PAD), qmap), pl.BlockSpec((bq, HEAD_PAD), qmap)],
        out_specs=[pl.BlockSpec((t, HEAD_PAD), lambda h, j, i: (0, h)),
                   pl.BlockSpec((bq, HEAD_PAD), kmap), pl.BlockSpec((bq, HEAD_PAD), kmap)],
        out_shape=[jax.ShapeDtypeStruct(q.shape, F32), jax.ShapeDtypeStruct(q.shape, BF16),
                   jax.ShapeDtypeStruct(q.shape, BF16)],
        scratch_shapes=[pltpu.VMEM((bq, HEAD_PAD), F32), pltpu.VMEM((bq, HEAD_PAD), F32)],
        compiler_params=_params(),
    )(q, k, v, o, do, lse)


def _mixer_out(h1, hl, gate, o, w_lru, w_mla, g_post):
    t = h1.shape[0]
    tt = _tile_rows(t)
    hw = N_HEADS * HEAD_PAD

    def body(h1_ref, hl_ref, gate_ref, o_ref, wl_ref, wm_ref, g_ref, h2_ref, m_ref, y_ref):
        y = (hl_ref[...] * _gelu(gate_ref[...])).astype(BF16)
        y_ref[...] = y
        m = _dot(y, wl_ref[...]) + _dot(o_ref[...], wm_ref[...])
        m_ref[...] = m
        h2_ref[...] = h1_ref[...] + _rms(m, g_ref[...])

    row = lambda i: (i, 0)
    return pl.pallas_call(
        body, name="mixer_out",
        grid=(t // tt,),
        in_specs=[pl.BlockSpec((tt, D_MODEL), row), pl.BlockSpec((tt, LRU_W), row), pl.BlockSpec((tt, LRU_W), row),
                  pl.BlockSpec((tt, hw), row), _full((LRU_W, D_MODEL)), _full((hw, D_MODEL)), _full((1, D_MODEL))],
        out_specs=[pl.BlockSpec((tt, D_MODEL), row), pl.BlockSpec((tt, D_MODEL), row),
                   pl.BlockSpec((tt, LRU_W), row)],
        out_shape=[jax.ShapeDtypeStruct((t, D_MODEL), F32), jax.ShapeDtypeStruct((t, D_MODEL), F32),
                   jax.ShapeDtypeStruct((t, LRU_W), BF16)],
        compiler_params=_params(),
    )(h1, hl, gate, o, w_lru, w_mla, g_post)


def _mixer_out_bwd(dh2, m, hl, gate, w_lru, w_mla, g_post):
    t = m.shape[0]
    tt = _tile_rows(t)
    hw = N_HEADS * HEAD_PAD

    def body(dh2_ref, m_ref, hl_ref, gate_ref, wl_ref, wm_ref, g_ref, dm_ref, dgate_ref, dhl_ref, do_ref, dg_ref):
        i = pl.program_id(0)

        @pl.when(i == 0)
        def _():
            dg_ref[...] = jnp.zeros_like(dg_ref)

        dm, dg = _rms_bwd(m_ref[...], g_ref[...], dh2_ref[...])
        dg_ref[...] += dg
        dmb = dm.astype(BF16)
        dm_ref[...] = dmb
        dy = _dot_nt(dmb, wl_ref[...])
        gate = gate_ref[...]
        dgate_ref[...] = (dy * hl_ref[...] * _gelu_grad(gate)).astype(BF16)
        dhl_ref[...] = dy * _gelu(gate)
        do_ref[...] = _dot_nt(dmb, wm_ref[...]).astype(BF16)

    row = lambda i: (i, 0)
    return pl.pallas_call(
        body, name="mixer_out_bwd",
        grid=(t // tt,),
        in_specs=[pl.BlockSpec((tt, D_MODEL), row), pl.BlockSpec((tt, D_MODEL), row), pl.BlockSpec((tt, LRU_W), row),
                  pl.BlockSpec((tt, LRU_W), row), _full((LRU_W, D_MODEL)), _full((hw, D_MODEL)), _full((1, D_MODEL))],
        out_specs=[pl.BlockSpec((tt, D_MODEL), row), pl.BlockSpec((tt, LRU_W), row), pl.BlockSpec((tt, LRU_W), row),
                   pl.BlockSpec((tt, hw), row), _full((1, D_MODEL))],
        out_shape=[jax.ShapeDtypeStruct((t, D_MODEL), BF16), jax.ShapeDtypeStruct((t, LRU_W), BF16),
                   jax.ShapeDtypeStruct((t, LRU_W), F32), jax.ShapeDtypeStruct((t, hw), BF16),
                   jax.ShapeDtypeStruct((1, D_MODEL), F32)],
        compiler_params=_params(),
    )(dh2, m, hl, gate, w_lru, w_mla, g_post)


def _loss_head(y, target):
    t = y.shape[0]
    tt = _tile_rows(t)

    def body(y_ref, t_ref, dy_ref, loss_ref):
        i = pl.program_id(0)

        @pl.when(i == 0)
        def _():
            loss_ref[...] = jnp.zeros_like(loss_ref)

        e = y_ref[...] - t_ref[...]
        dy_ref[...] = e * (1.0 / D_MODEL)
        per_token = jnp.mean(e * e, axis=-1, keepdims=True)
        loss_ref[...] += 0.5 * jnp.sum(per_token, axis=0, keepdims=True)

    row = lambda i: (i, 0)
    return pl.pallas_call(
        body, name="loss_head",
        grid=(t // tt,),
        in_specs=[pl.BlockSpec((tt, D_MODEL), row), pl.BlockSpec((tt, D_MODEL), row)],
        out_specs=[pl.BlockSpec((tt, D_MODEL), row), _full((8, 128))],
        out_shape=[jax.ShapeDtypeStruct((t, D_MODEL), F32), jax.ShapeDtypeStruct((8, 128), F32)],
        compiler_params=_params(),
    )(y, target)


def _adamw(w, g, m, v):
    m = ADAM_B1 * m + (1.0 - ADAM_B1) * g
    v = ADAM_B2 * v + (1.0 - ADAM_B2) * (g * g)
    m_hat = m / (1.0 - ADAM_B1 ** ADAM_STEP)
    v_hat = v / (1.0 - ADAM_B2 ** ADAM_STEP)
    delta = -ADAM_LR * (m_hat / (jnp.sqrt(v_hat) + ADAM_EPS) + ADAM_WD * w)
    return delta, m, v


def _sum_adamw(terms, w, m, v, name):
    r, c = w.shape

    def body(t_ref, w_ref, m_ref, v_ref, g_out, d_out, m_out, v_out):
        g = t_ref[0, :r, :c].astype(F32)
        for q in range(1, N_DEV):
            g = g + t_ref[q, :r, :c].astype(F32)
        g_out[...] = g
        d_out[...], m_out[...], v_out[...] = _adamw(w_ref[...], g, m_ref[...], v_ref[...])

    return pl.pallas_call(
        body, name=name,
        in_specs=[pl.BlockSpec(memory_space=pltpu.VMEM)] * 4,
        out_specs=[pl.BlockSpec(memory_space=pltpu.VMEM)] * 4,
        out_shape=[jax.ShapeDtypeStruct((r, c), F32)] * 4,
        compiler_params=_params(),
    )(terms, w, m, v)


def _sum_terms(terms):
    def body(t_ref, out_ref):
        g = t_ref[0]
        for q in range(1, N_DEV):
            g = g + t_ref[q]
        out_ref[...] = g

    return pl.pallas_call(
        body, name="small_grad_sum",
        in_specs=[pl.BlockSpec(memory_space=pltpu.VMEM)],
        out_specs=pl.BlockSpec(memory_space=pltpu.VMEM),
        out_shape=jax.ShapeDtypeStruct(terms.shape[1:], F32),
        compiler_params=_params(),
    )(terms)


def _adamw_flat(w, g, m, v):
    def body(w_ref, g_ref, m_ref, v_ref, d_out, m_out, v_out):
        d_out[...], m_out[...], v_out[...] = _adamw(w_ref[...], g_ref[...], m_ref[...], v_ref[...])

    return pl.pallas_call(
        body, name="small_adamw",
        in_specs=[pl.BlockSpec(memory_space=pltpu.VMEM)] * 4,
        out_specs=[pl.BlockSpec(memory_space=pltpu.VMEM)] * 3,
        out_shape=[jax.ShapeDtypeStruct(w.shape, F32)] * 3,
        compiler_params=_params(),
    )(w, g, m, v)


def _pack(arrays):
    flat = jnp.concatenate([a.reshape(-1).astype(F32) for a in arrays])
    rows = -(-flat.shape[0] // 1024) * 8
    return jnp.pad(flat, (0, rows * 128 - flat.shape[0])).reshape(rows, 128)


def _unpack(packed, shapes):
    flat = packed.reshape(-1)
    out, pos = [], 0
    for s in shapes:
        n = math.prod(s)
        out.append(flat[pos:pos + n].reshape(s))
        pos += n
    return out


def _rope_tables(positions):
    t = positions.shape[-1]
    inv_freq = 1.0 / (ROPE_THETA ** (jnp.arange(0, QK_ROPE, 2, dtype=F32) / QK_ROPE))
    ang = positions.reshape(t, 1).astype(F32) * inv_freq
    cos, sin = jnp.cos(ang), jnp.sin(ang)
    z16 = jnp.zeros((t, 16), F32)

    def place(first, second, lead, lead_value):
        parts = [jnp.full((t, lead), lead_value, F32)] if lead else []
        parts += [first, second]
        width = lead + 32
        parts.append(jnp.zeros((t, HEAD_PAD - width), F32))
        return jnp.concatenate(parts, axis=1)

    q_tabs = [place(cos, cos, QK_NOPE, 1.0) * ATT_SCALE, place(-sin, z16, QK_NOPE, 0.0) * ATT_SCALE,
              place(z16, sin, QK_NOPE, 0.0) * ATT_SCALE]
    k_tabs = [place(cos, cos, 0, 0.0), place(-sin, z16, 0, 0.0), place(z16, sin, 0, 0.0)]
    return q_tabs + k_tabs


def _block_diag(w):
    eye = jnp.eye(w.shape[0], dtype=w.dtype)
    return jnp.einsum("nde,nm->ndme", w, eye).reshape(LRU_W, LRU_W)


def _diag_blocks(g):
    g4 = g.reshape(8, 64, 8, 64)
    return jnp.moveaxis(jnp.diagonal(g4, axis1=0, axis2=2), -1, 0)


def _by_destination(g, shard, pad):
    r = g.shape[0]
    blocks = jnp.transpose(g.reshape(r, N_DEV, shard), (1, 0, 2))
    return jnp.pad(blocks, ((0, 0), (0, 0), (0, pad - shard)))


_SMALL = ["g_ffn1_pre", "g_ffn1_post", "g_mix_pre", "g_mix_post", "conv_b", "w_lru_a", "b_lru_a", "w_lru_x",
          "b_lru_x", "lru_lambda", "q_a_norm", "kv_a_norm", "g_ffn2_pre", "g_ffn2_post", "conv_w"]
_SHARDED = ["w_ffn1_gate", "w_ffn1_up", "w_ffn1_down", "w_in", "w_q_b", "w_kv_b", "w_out",
            "w_ffn2_gate", "w_ffn2_up", "w_ffn2_down"]
_ORDER = ["g_ffn1_pre", "g_ffn1_post", "w_ffn1_gate", "w_ffn1_up", "w_ffn1_down", "g_mix_pre", "g_mix_post", "w_in",
          "conv_w", "conv_b", "w_lru_a", "b_lru_a", "w_lru_x", "b_lru_x", "lru_lambda", "q_a_norm", "w_q_b",
          "kv_a_norm", "w_kv_b", "w_out", "g_ffn2_pre", "g_ffn2_post", "w_ffn2_gate", "w_ffn2_up", "w_ffn2_down"]
_PADS = {"w_ffn1_gate": (D_MODEL, FF_PAD), "w_ffn1_up": (D_MODEL, FF_PAD), "w_ffn1_down": (FF_PAD, D_MODEL),
         "w_in": (D_MODEL, IN_SHARD_PAD), "w_q_b": (Q_LORA, HEAD_PAD), "w_kv_b": (KV_LORA, HEAD_PAD),
         "w_out": (HEAD_PAD, D_MODEL), "w_ffn2_gate": (D_MODEL, FF_PAD), "w_ffn2_up": (D_MODEL, FF_PAD),
         "w_ffn2_down": (FF_PAD, D_MODEL)}


def _local_grads(x, positions, loss_target, p, full):
    t = x.shape[0]
    tables = _rope_tables(positions)
    wa = _block_diag(p["w_lru_a"]).astype(BF16)
    wx = _block_diag(p["w_lru_x"]).astype(BF16)
    conv_w8 = jnp.pad(full["conv_w"], ((0, 8 - CONV_W), (0, 0)))
    vec = lambda name: p[name].reshape(1, -1)

    h1, f1, n1, gt1, up1 = _ffn_fwd(x, vec("g_ffn1_pre"), vec("g_ffn1_post"), full["wg1"], full["wu1"], full["wd1"])
    nmix, xl, gate, qlat, kvlat, krope = _mixer_in(h1, vec("g_mix_pre"), full["w_in"])
    xc, a, u = _lru_gates(xl, conv_w8, vec("conv_b"), wa, wx, vec("b_lru_a"), vec("b_lru_x"), vec("lru_lambda"))
    hl = _lru_scan(a, u, reverse=False)
    q, k, v, qn, kva = _mla_proj(qlat, kvlat, krope, tables, vec("q_a_norm"), vec("kv_a_norm"),
                                 full["wq"], full["wk"], full["wv"])
    o, lse = _attn_fwd(q, k, v)
    h2, m, ylru = _mixer_out(h1, hl, gate, o, full["w_out_lru"], full["w_out_mla"], vec("g_mix_post"))
    h3, f2, n2, gt2, up2 = _ffn_fwd(h2, vec("g_ffn2_pre"), vec("g_ffn2_post"), full["wg2"], full["wu2"], full["wd2"])
    dy, loss_tile = _loss_head(h3, loss_target)

    dh2, df2, act2, dgt2, dup2, dg2pre, dg2post = _ffn_bwd(
        dy, h2, f2, gt2, up2, vec("g_ffn2_pre"), vec("g_ffn2_post"), full["wg2"], full["wu2"], full["wd2"])
    big = {}
    big["w_ffn2_gate"] = _mm_tn(n2, dgt2, D_MODEL, D_MODEL, FF_PAD, True, "dw_ffn_gate")
    big["w_ffn2_up"] = _mm_tn(n2, dup2, D_MODEL, D_MODEL, FF_PAD, True, "dw_ffn_up")
    big["w_ffn2_down"] = _mm_tn(act2, df2, FF_VIRT, 2 * FF_PAD, D_MODEL, False, "dw_ffn_down").reshape(
        N_DEV, FF_PAD, D_MODEL)

    dm, dgate, dhl, do, dgmixpost = _mixer_out_bwd(dh2, m, hl, gate, full["w_out_lru"], full["w_out_mla"],
                                                   vec("g_mix_post"))
    dw_out_lru = _mm_tn(ylru, dm, LRU_W, LRU_W, D_MODEL, False, "dw_out_lru")
    dw_out_mla = _mm_tn(o, dm, N_HEADS * HEAD_PAD, N_HEADS * HEAD_PAD, D_MODEL, False, "dw_out_mla")
    dw_out = jnp.concatenate(
        [dw_out_lru, dw_out_mla.reshape(N_HEADS, HEAD_PAD, D_MODEL)[:, :V_DIM].reshape(N_HEADS * V_DIM, D_MODEL)], axis=0)
    big["w_out"] = dw_out.reshape(N_DEV, HEAD_PAD, D_MODEL)

    dq, dk, dv = _attn_bwd(q, k, v, o, do, lse)
    dqr, dqlat, dkvlat, dkrope, dgq, dgkv = _mla_proj_bwd(
        dq, dk, dv, qlat, kvlat, tables, vec("q_a_norm"), vec("kv_a_norm"), full["wq"], full["wk"], full["wv"])
    hw = N_HEADS * HEAD_PAD
    dwq = _mm_tn(qn, dqr, Q_LORA, Q_LORA, hw, False, "dw_q_b")
    dwk = _mm_tn(kva, dk, KV_LORA, KV_LORA, hw, False, "dw_kv_b_k")
    dwv = _mm_tn(kva, dv, KV_LORA, KV_LORA, hw, False, "dw_kv_b_v")
    big["w_q_b"] = jnp.transpose(dwq.reshape(Q_LORA, N_HEADS, HEAD_PAD), (1, 0, 2))
    big["w_kv_b"] = jnp.transpose(jnp.concatenate(
        [dwk.reshape(KV_LORA, N_HEADS, HEAD_PAD)[:, :, :QK_NOPE], dwv.reshape(KV_LORA, N_HEADS, HEAD_PAD)[:, :, :V_DIM]],
        axis=2), (1, 0, 2))

    dh = _lru_scan(a, dhl, reverse=True)
    dxc, dwa, dwx, dba, dbx, dlam = _lru_gates_bwd(dh, hl, xc, wa, wx, vec("b_lru_a"), vec("b_lru_x"),
                                                   vec("lru_lambda"))
    dxl, dconv_w8, dconv_b = _conv_bwd(dxc, xl, conv_w8)
    dh1, dproj, dgmixpre = _mixer_in_bwd(dh2, h1, vec("g_mix_pre"), full["w_in"], [dxl, dgate, dqlat, dkvlat, dkrope])
    dw_in = _mm_tn(nmix, dproj, D_MODEL, D_MODEL, IN_PAD // 2, False, "dw_in")
    big["w_in"] = _by_destination(dw_in[:, :IN_COLS], IN_SHARD, IN_SHARD_PAD)

    dx, df1, act1, dgt1, dup1, dg1pre, dg1post = _ffn_bwd(
        dh1, x, f1, gt1, up1, vec("g_ffn1_pre"), vec("g_ffn1_post"), full["wg1"], full["wu1"], full["wd1"])
    big["w_ffn1_gate"] = _mm_tn(n1, dgt1, D_MODEL, D_MODEL, FF_PAD, True, "dw_ffn_gate")
    big["w_ffn1_up"] = _mm_tn(n1, dup1, D_MODEL, D_MODEL, FF_PAD, True, "dw_ffn_up")
    big["w_ffn1_down"] = _mm_tn(act1, df1, FF_VIRT, 2 * FF_PAD, D_MODEL, False, "dw_ffn_down").reshape(
        N_DEV, FF_PAD, D_MODEL)

    small = {"g_ffn1_pre": dg1pre, "g_ffn1_post": dg1post, "g_mix_pre": dgmixpre, "g_mix_post": dgmixpost,
             "conv_b": dconv_b, "w_lru_a": _diag_blocks(dwa), "b_lru_a": dba, "w_lru_x": _diag_blocks(dwx),
             "b_lru_x": dbx, "lru_lambda": dlam, "q_a_norm": dgq, "kv_a_norm": dgkv, "g_ffn2_pre": dg2pre,
             "g_ffn2_post": dg2post, "conv_w": dconv_w8[:CONV_W]}
    return loss_tile, dx, small, big


def _kernel_layouts(gathered):
    full = {}
    for n in ("1", "2"):
        full["wg" + n] = gathered["w_ffn%s_gate" % n]
        full["wu" + n] = gathered["w_ffn%s_up" % n]
        full["wd" + n] = gathered["w_ffn%s_down" % n]
    w_in = jnp.transpose(gathered["w_in"][:, :, :IN_SHARD], (1, 0, 2)).reshape(D_MODEL, IN_COLS)
    full["w_in"] = jnp.pad(w_in, ((0, 0), (0, IN_PAD - IN_COLS)))
    hw = N_HEADS * HEAD_PAD
    full["wq"] = jnp.transpose(gathered["w_q_b"], (1, 0, 2)).reshape(Q_LORA, hw)
    kvb = gathered["w_kv_b"]
    zeros = jnp.zeros((N_HEADS, KV_LORA, HEAD_PAD - QK_NOPE), BF16)
    k_part = jnp.transpose(jnp.concatenate([kvb[:, :, :QK_NOPE], zeros], axis=2), (1, 0, 2)).reshape(KV_LORA, hw)
    rope_rows = jnp.pad(jnp.eye(QK_ROPE, dtype=BF16), ((0, HEAD_PAD - QK_ROPE), (QK_NOPE, HEAD_PAD - QK_NOPE - QK_ROPE)))
    full["wk"] = jnp.concatenate([k_part, jnp.tile(rope_rows, (1, N_HEADS))], axis=0)
    full["wv"] = jnp.transpose(jnp.concatenate([kvb[:, :, QK_NOPE:], zeros], axis=2), (1, 0, 2)).reshape(KV_LORA, hw)
    w_out = gathered["w_out"].reshape(D_MODEL, D_MODEL)
    full["w_out_lru"] = w_out[:LRU_W]
    full["w_out_mla"] = jnp.pad(w_out[LRU_W:].reshape(N_HEADS, V_DIM, D_MODEL),
                                ((0, 0), (0, HEAD_PAD - V_DIM), (0, 0))).reshape(hw, D_MODEL)
    full["conv_w"] = jnp.transpose(gathered["conv_w"][:, :CONV_W, :LRU_W // N_DEV], (1, 0, 2)).reshape(CONV_W, LRU_W)
    return full


def _step(args):
    x = args["x"][0]
    target = args["loss_target"][0]
    positions = args["positions"][0]
    w = {n: args[n][0] for n in _ORDER}
    mom = {n: args["m_" + n][0] for n in _ORDER}
    var = {n: args["v_" + n][0] for n in _ORDER}

    shards = [w[n] for n in _SHARDED] + [w["conv_w"]]
    pads = [_PADS[n] for n in _SHARDED] + [(8, 128)]
    dtypes = [BF16] * len(_SHARDED) + [F32]
    got = _all_gather(shards, pads, dtypes)
    gathered = dict(zip(_SHARDED + ["conv_w"], got))
    full = _kernel_layouts(gathered)

    loss_tile, dx, small, big = _local_grads(x, positions, target, w, full)

    small_names = _SMALL
    packed = _pack([small[n] for n in small_names])
    received = _exchange([big[n] for n in _SHARDED] + [jnp.broadcast_to(packed[None], (N_DEV,) + packed.shape)])
    terms = dict(zip(_SHARDED, received[:-1]))

    grads, delta, new_m, new_v = {}, {}, {}, {}
    for n in _SHARDED:
        grads[n], delta[n], new_m[n], new_v[n] = _sum_adamw(terms[n], w[n], mom[n], var[n], "adamw_" + n)

    summed = _sum_terms(received[-1])
    full_shapes = [small[n].shape for n in small_names]
    small_sum = dict(zip(small_names, _unpack(summed, full_shapes)))
    me = 4 * lax.axis_index("x") + 2 * lax.axis_index("y") + lax.axis_index("c")
    cshard = LRU_W // N_DEV
    small_sum["conv_w"] = lax.dynamic_slice(small_sum["conv_w"], (0, me * cshard), (CONV_W, cshard))
    for n in small_names:
        grads[n] = small_sum[n].reshape(w[n].shape)
    shapes = [w[n].shape for n in small_names]
    d_p, m_p, v_p = _adamw_flat(_pack([w[n] for n in small_names]), _pack([grads[n] for n in small_names]),
                                _pack([mom[n] for n in small_names]), _pack([var[n] for n in small_names]))
    for n, d_, m_, v_ in zip(small_names, _unpack(d_p, shapes), _unpack(m_p, shapes), _unpack(v_p, shapes)):
        delta[n], new_m[n], new_v[n] = d_, m_, v_

    loss = lax.psum(loss_tile[0, 0], ("x", "y", "c"))
    lead = lambda d: [d[n][None] for n in _ORDER]
    return (loss, dx[None], *lead(grads), *lead(delta), *lead(new_m), *lead(new_v))


def kernel(x, positions, g_ffn1_pre, g_ffn1_post, w_ffn1_gate, w_ffn1_up, w_ffn1_down, g_mix_pre, g_mix_post, w_in, conv_w, conv_b, w_lru_a, b_lru_a, w_lru_x, b_lru_x, lru_lambda, q_a_norm, w_q_b, kv_a_norm, w_kv_b, w_out, g_ffn2_pre, g_ffn2_post, w_ffn2_gate, w_ffn2_up, w_ffn2_down, loss_target, m_g_ffn1_pre, m_g_ffn1_post, m_w_ffn1_gate, m_w_ffn1_up, m_w_ffn1_down, m_g_mix_pre, m_g_mix_post, m_w_in, m_conv_w, m_conv_b, m_w_lru_a, m_b_lru_a, m_w_lru_x, m_b_lru_x, m_lru_lambda, m_q_a_norm, m_w_q_b, m_kv_a_norm, m_w_kv_b, m_w_out, m_g_ffn2_pre, m_g_ffn2_post, m_w_ffn2_gate, m_w_ffn2_up, m_w_ffn2_down, v_g_ffn1_pre, v_g_ffn1_post, v_w_ffn1_gate, v_w_ffn1_up, v_w_ffn1_down, v_g_mix_pre, v_g_mix_post, v_w_in, v_conv_w, v_conv_b, v_w_lru_a, v_b_lru_a, v_w_lru_x, v_b_lru_x, v_lru_lambda, v_q_a_norm, v_w_q_b, v_kv_a_norm, v_w_kv_b, v_w_out, v_g_ffn2_pre, v_g_ffn2_post, v_w_ffn2_gate, v_w_ffn2_up, v_w_ffn2_down):
    return _step(dict(locals()))
```

```python
import functools
import math
import operator

import jax
import jax.numpy as jnp
from jax import lax
from jax.experimental import pallas as pl
from jax.experimental.pallas import tpu as pltpu

F32 = jnp.float32
BF16 = jnp.bfloat16
MESH = pl.DeviceIdType.MESH

N_DEV = 8
D_MODEL = 1024
D_FF = 2816
FF_SHARD = D_FF // N_DEV
FF_PAD = 384
FF_VIRT = N_DEV * FF_PAD
FF_CHUNK = 2 * FF_PAD
LRU_W = 512
N_HEADS = 8
HEAD_PAD = 128
HEADS_W = N_HEADS * HEAD_PAD
QK_NOPE = 64
QK_ROPE = 32
V_DIM = 64
Q_LORA = 384
KV_LORA = 256
IN_COLS = 2 * LRU_W + Q_LORA + KV_LORA + QK_ROPE
IN_SHARD = IN_COLS // N_DEV
IN_SHARD_PAD = 256
IN_PAD = 1792
QB_SHARD = 96
CONV_W = 4
CHUNK = 64
EPS = 1e-6
LRU_C = 8.0
ROPE_THETA = 10000.0
ATT_SCALE = (QK_NOPE + QK_ROPE) ** -0.5

ADAM_LR = 0.001
ADAM_B1 = 0.9
ADAM_B2 = 0.999
ADAM_EPS = 1e-08
ADAM_WD = 0.01
ADAM_STEP = 10

VMEM_LIMIT = 56 * 1024 * 1024
ANY = pl.BlockSpec(memory_space=pl.ANY)
WHOLE = pl.BlockSpec(memory_space=pltpu.VMEM)


def _params(**kw):
    return pltpu.CompilerParams(vmem_limit_bytes=VMEM_LIMIT, **kw)


def _full(shape):
    return pl.BlockSpec(shape, lambda *_: (0,) * len(shape))


def _dot(a, b):
    return jnp.dot(a, b, preferred_element_type=F32)


def _dot_nt(a, b):
    return lax.dot_general(a, b, (((1,), (1,)), ((), ())), preferred_element_type=F32)


def _dot_tn(a, b):
    return lax.dot_general(a, b, (((0,), (0,)), ((), ())), preferred_element_type=F32)


def _rms(x, g):
    r = lax.rsqrt(jnp.mean(x * x, axis=-1, keepdims=True) + EPS)
    return x * r * g


def _rms_bwd(x, g, dy):
    r = lax.rsqrt(jnp.mean(x * x, axis=-1, keepdims=True) + EPS)
    xh = x * r
    dg = jnp.sum(dy * xh, axis=0, keepdims=True)
    dxh = dy * g
    dx = r * (dxh - xh * jnp.mean(dxh * xh, axis=-1, keepdims=True))
    return dx, dg


def _sigmoid(x):
    return 1.0 / (1.0 + jnp.exp(-x))


_GELU_C = math.sqrt(2.0 / math.pi)


def _gelu(x):
    t = jnp.tanh(_GELU_C * (x + 0.044715 * x * x * x))
    return 0.5 * x * (1.0 + t)


def _gelu_grad(x):
    t = jnp.tanh(_GELU_C * (x + 0.044715 * x * x * x))
    return 0.5 * (1.0 + t) + 0.5 * x * (1.0 - t * t) * _GELU_C * (1.0 + 3.0 * 0.044715 * x * x)


def _tile_rows(t):
    return 512 if t >= 2048 else t // 2


def _dev_index(p):
    return 4 * p[0] + 2 * p[1] + p[2]


def _place():
    x, y, c = lax.axis_index("x"), lax.axis_index("y"), lax.axis_index("c")
    return (x, y, c), (x, y, 1 - c), [(1 - x, y), (x, 1 - y), (1 - x, 1 - y)]


def _dma_sems(n):
    return [pltpu.SemaphoreType.DMA((7 * n,)), pltpu.SemaphoreType.DMA((7 * n,)), pltpu.SemaphoreType.DMA((n,))]


class _Gather:
    has_middle = True

    def __init__(self, arrays):
        self.inputs = list(arrays)
        self.n = len(arrays)
        self.out_shape = [jax.ShapeDtypeStruct((N_DEV,) + a.shape, a.dtype) for a in arrays]
        self.scratch = _dma_sems(self.n)

    @staticmethod
    def _copy(outs, sems, k, s, block, to, src=None):
        rows = outs[k].at[_dev_index(block)]
        return pltpu.make_async_remote_copy(
            src_ref=rows if src is None else src, dst_ref=rows,
            send_sem=sems[0].at[7 * k + s], recv_sem=sems[1].at[7 * k + s], device_id=to, device_id_type=MESH)

    def _first(self, srcs, outs, sems):
        me, sibling, chips = _place()
        mine = [pltpu.make_async_copy(srcs[k], outs[k].at[_dev_index(me)], sems[2].at[k]) for k in range(self.n)]
        sends = []
        for k in range(self.n):
            sends.append(self._copy(outs, sems, k, 0, me, sibling, src=srcs[k]))
            sends += [self._copy(outs, sems, k, 1 + j, me, (*chip, me[2]), src=srcs[k])
                      for j, chip in enumerate(chips)]
        return mine, sends

    def _passed(self, outs, sems):
        me, sibling, chips = _place()
        return [[self._copy(outs, sems, k, 4 + j, (*chip, me[2]), sibling) for k in range(self.n)]
                for j, chip in enumerate(chips)]

    def start(self, srcs, outs, sems):
        mine, sends = self._first(srcs, outs, sems)
        for cp in mine + sends:
            cp.start()

    def middle(self, srcs, outs, sems):
        me, sibling, chips = _place()
        passed = self._passed(outs, sems)
        for j, chip in enumerate(chips):
            for k in range(self.n):
                self._copy(outs, sems, k, 1 + j, (*chip, me[2]), me).wait_recv()
                passed[j][k].start()

    def finish(self, srcs, outs, sems):
        me, sibling, chips = _place()
        for k in range(self.n):
            self._copy(outs, sems, k, 0, sibling, me).wait_recv()
        for j, chip in enumerate(chips):
            for k in range(self.n):
                self._copy(outs, sems, k, 4 + j, (*chip, 1 - me[2]), me).wait_recv()
        mine, sends = self._first(srcs, outs, sems)
        for cp in sends + [cp for row in self._passed(outs, sems) for cp in row]:
            cp.wait_send()
        for cp in mine:
            cp.wait()


class _Exchange:
    has_middle = False

    def __init__(self, arrays):
        self.inputs = list(arrays)
        self.n = len(arrays)
        self.out_shape = [jax.ShapeDtypeStruct(a.shape, a.dtype) for a in arrays]
        self.scratch = _dma_sems(self.n)

    def _copies(self, srcs, outs, sems):
        (x, y, c), _, _ = _place()
        me = _dev_index((x, y, c))
        local = [pltpu.make_async_copy(srcs[k].at[me], outs[k].at[me], sems[2].at[k]) for k in range(self.n)]
        remote = []
        for s in range(1, N_DEV):
            peer = (1 - x if s & 4 else x, 1 - y if s & 2 else y, 1 - c if s & 1 else c)
            for k in range(self.n):
                remote.append(pltpu.make_async_remote_copy(
                    src_ref=srcs[k].at[_dev_index(peer)], dst_ref=outs[k].at[me],
                    send_sem=sems[0].at[7 * k + s - 1], recv_sem=sems[1].at[7 * k + s - 1],
                    device_id=peer, device_id_type=MESH))
        return local, remote

    def start(self, srcs, outs, sems):
        local, remote = self._copies(srcs, outs, sems)
        for cp in local + remote:
            cp.start()

    def finish(self, srcs, outs, sems):
        local, remote = self._copies(srcs, outs, sems)
        for cp in remote:
            cp.wait_recv()
        for cp in remote:
            cp.wait_send()
        for cp in local:
            cp.wait()


def _call(body, comm, *, name, grid, in_specs, out_specs, out_shape, scratch_shapes=(), args):
    in_specs, out_specs, out_shape = list(in_specs), list(out_specs), list(out_shape)
    scratch_shapes = list(scratch_shapes)
    if comm is None:
        outs = pl.pallas_call(body, name=name, grid=grid, in_specs=in_specs, out_specs=out_specs, out_shape=out_shape,
                              scratch_shapes=scratch_shapes, compiler_params=_params())(*args)
        return list(outs), []
    n_in, n_out, n_scr, c_n = len(in_specs), len(out_specs), len(scratch_shapes), comm.n
    middle = tuple(g // 2 if d == 0 else 0 for d, g in enumerate(grid))

    def hosted(*refs):
        pos = 0
        parts = []
        for width in (n_in, c_n, n_out, c_n, n_scr, 3):
            parts.append(refs[pos:pos + width])
            pos += width
        ins, c_in, outs, c_out, scr, sems = parts
        ids = [pl.program_id(d) for d in range(len(grid))]
        at = lambda where: functools.reduce(operator.and_, [i == w for i, w in zip(ids, where)])

        @pl.when(at([0] * len(grid)))
        def _():
            comm.start(c_in, c_out, sems)

        body(*ins, *outs, *scr)

        if comm.has_middle:
            @pl.when(at(middle))
            def _():
                comm.middle(c_in, c_out, sems)

        @pl.when(at([g - 1 for g in grid]))
        def _():
            comm.finish(c_in, c_out, sems)

    outs = pl.pallas_call(
        hosted, name=name, grid=grid,
        in_specs=in_specs + [ANY] * c_n, out_specs=out_specs + [ANY] * c_n,
        out_shape=out_shape + comm.out_shape, scratch_shapes=scratch_shapes + comm.scratch,
        compiler_params=_params())(*args, *comm.inputs)
    return list(outs[:n_out]), list(outs[n_out:])


def _first_gather(shards, pads, dtypes, n_gathered):
    n = len(shards)
    gather = _Gather([jax.ShapeDtypeStruct(tuple(pads[k]), dtypes[k]) for k in range(n_gathered)])

    def body(*refs):
        ins, outs = refs[:n], refs[n:2 * n]
        stages = list(refs[2 * n:2 * n + n_gathered]) + list(outs[n_gathered:])
        sems = refs[2 * n + n_gathered:]
        for k in range(n):
            r, cc = ins[k].shape
            if (r, cc) != tuple(stages[k].shape):
                stages[k][...] = jnp.zeros(stages[k].shape, stages[k].dtype)
            stages[k][:r, :cc] = ins[k][...].astype(stages[k].dtype)
        gather.start(stages[:n_gathered], outs[:n_gathered], sems)
        gather.middle(stages[:n_gathered], outs[:n_gathered], sems)
        gather.finish(stages[:n_gathered], outs[:n_gathered], sems)

    staged_shape = [jax.ShapeDtypeStruct(tuple(pads[k]), dtypes[k]) for k in range(n_gathered, n)]
    outs = pl.pallas_call(
        body, name="first_gather",
        out_shape=gather.out_shape + staged_shape,
        in_specs=[WHOLE] * n,
        out_specs=[ANY] * n_gathered + [WHOLE] * (n - n_gathered),
        scratch_shapes=[pltpu.VMEM(tuple(pads[k]), dtypes[k]) for k in range(n_gathered)] + gather.scratch,
        compiler_params=_params(),
    )(*shards)
    return list(outs[:n_gathered]), list(outs[n_gathered:])


def _last_exchange(arrays):
    comm = _Exchange(arrays)

    def body(*refs):
        srcs, outs, sems = refs[:comm.n], refs[comm.n:2 * comm.n], refs[2 * comm.n:]
        comm.start(srcs, outs, sems)
        comm.finish(srcs, outs, sems)

    return pl.pallas_call(
        body, name="last_exchange", out_shape=comm.out_shape, in_specs=[ANY] * comm.n, out_specs=[ANY] * comm.n,
        scratch_shapes=comm.scratch, compiler_params=_params())(*arrays)


def _ffn_fwd(h0, g_pre, g_post, wg, wu, wd, comm, name):
    t = h0.shape[0]
    tt = _tile_rows(t)
    n_chunks = FF_VIRT // FF_CHUNK

    def body(h0_ref, gpre_ref, gpost_ref, wg_ref, wu_ref, wd_ref,
             h1_ref, f_ref, n1_ref, gt_ref, up_ref, acc_ref, n1s_ref):
        j = pl.program_id(1)

        @pl.when(j == 0)
        def _():
            n1 = _rms(h0_ref[...], gpre_ref[...]).astype(BF16)
            n1s_ref[...] = n1
            n1_ref[...] = n1
            acc_ref[...] = jnp.zeros_like(acc_ref)

        n1 = n1s_ref[...]
        gt = _dot_nt(n1, wg_ref[...])
        up = _dot_nt(n1, wu_ref[...])
        gt_ref[...] = gt.astype(BF16)
        up_ref[...] = up.astype(BF16)
        act = (gt * _sigmoid(gt) * up).astype(BF16)
        acc_ref[...] += _dot(act, wd_ref[...])

        @pl.when(j == n_chunks - 1)
        def _():
            f = acc_ref[...]
            f_ref[...] = f
            h1_ref[...] = h0_ref[...] + 0.5 * _rms(f, gpost_ref[...])

    row = lambda i, j: (i, 0)
    wspec = pl.BlockSpec((FF_CHUNK, D_MODEL), lambda i, j: (j, 0))
    return _call(
        body, comm, name=name, grid=(t // tt, n_chunks),
        in_specs=[pl.BlockSpec((tt, D_MODEL), row), _full((1, D_MODEL)), _full((1, D_MODEL)), wspec, wspec, wspec],
        out_specs=[pl.BlockSpec((tt, D_MODEL), row), pl.BlockSpec((tt, D_MODEL), row),
                   pl.BlockSpec((tt, D_MODEL), row),
                   pl.BlockSpec((tt, FF_CHUNK), lambda i, j: (i, j)),
                   pl.BlockSpec((tt, FF_CHUNK), lambda i, j: (i, j))],
        out_shape=[jax.ShapeDtypeStruct((t, D_MODEL), F32), jax.ShapeDtypeStruct((t, D_MODEL), F32),
                   jax.ShapeDtypeStruct((t, D_MODEL), BF16),
                   jax.ShapeDtypeStruct((t, FF_VIRT), BF16), jax.ShapeDtypeStruct((t, FF_VIRT), BF16)],
        scratch_shapes=[pltpu.VMEM((tt, D_MODEL), F32), pltpu.VMEM((tt, D_MODEL), BF16)],
        args=(h0, g_pre, g_post, wg, wu, wd))


def _ffn_bwd(dh1, h0, f, gt, up, g_pre, g_post, wg, wu, wd, comm, name):
    t = h0.shape[0]
    tt = _tile_rows(t)
    n_chunks = FF_VIRT // FF_CHUNK

    def body(dh1_ref, h0_ref, f_ref, gt_ref, up_ref, gpre_ref, gpost_ref, wg_ref, wu_ref, wd_ref,
             dh0_ref, df_ref, act_ref, dgt_ref, dup_ref, dgpre_ref, dgpost_ref, acc_ref, dfs_ref):
        i, j = pl.program_id(0), pl.program_id(1)

        @pl.when((i == 0) & (j == 0))
        def _():
            dgpre_ref[...] = jnp.zeros_like(dgpre_ref)
            dgpost_ref[...] = jnp.zeros_like(dgpost_ref)

        @pl.when(j == 0)
        def _():
            df, dg = _rms_bwd(f_ref[...], gpost_ref[...], 0.5 * dh1_ref[...])
            dgpost_ref[...] += dg
            dfb = df.astype(BF16)
            dfs_ref[...] = dfb
            df_ref[...] = dfb
            acc_ref[...] = jnp.zeros_like(acc_ref)

        da = _dot_nt(dfs_ref[...], wd_ref[...])
        g = gt_ref[...].astype(F32)
        u = up_ref[...].astype(F32)
        s = _sigmoid(g)
        sl = g * s
        act_ref[...] = (sl * u).astype(BF16)
        dgt = (da * u * (s * (1.0 + g * (1.0 - s)))).astype(BF16)
        dup = (da * sl).astype(BF16)
        dgt_ref[...] = dgt
        dup_ref[...] = dup
        acc_ref[...] += _dot(dgt, wg_ref[...]) + _dot(dup, wu_ref[...])

        @pl.when(j == n_chunks - 1)
        def _():
            dx, dg = _rms_bwd(h0_ref[...], gpre_ref[...], acc_ref[...])
            dgpre_ref[...] += dg
            dh0_ref[...] = dh1_ref[...] + dx

    row = lambda i, j: (i, 0)
    chunk = lambda i, j: (i, j)
    wspec = pl.BlockSpec((FF_CHUNK, D_MODEL), lambda i, j: (j, 0))
    return _call(
        body, comm, name=name, grid=(t // tt, n_chunks),
        in_specs=[pl.BlockSpec((tt, D_MODEL), row), pl.BlockSpec((tt, D_MODEL), row),
                  pl.BlockSpec((tt, D_MODEL), row),
                  pl.BlockSpec((tt, FF_CHUNK), chunk), pl.BlockSpec((tt, FF_CHUNK), chunk),
                  _full((1, D_MODEL)), _full((1, D_MODEL)), wspec, wspec, wspec],
        out_specs=[pl.BlockSpec((tt, D_MODEL), row), pl.BlockSpec((tt, D_MODEL), row),
                   pl.BlockSpec((tt, FF_CHUNK), chunk), pl.BlockSpec((tt, FF_CHUNK), chunk),
                   pl.BlockSpec((tt, FF_CHUNK), chunk), _full((1, D_MODEL)), _full((1, D_MODEL))],
        out_shape=[jax.ShapeDtypeStruct((t, D_MODEL), F32), jax.ShapeDtypeStruct((t, D_MODEL), BF16),
                   jax.ShapeDtypeStruct((t, FF_VIRT), BF16), jax.ShapeDtypeStruct((t, FF_VIRT), BF16),
                   jax.ShapeDtypeStruct((t, FF_VIRT), BF16),
                   jax.ShapeDtypeStruct((1, D_MODEL), F32), jax.ShapeDtypeStruct((1, D_MODEL), F32)],
        scratch_shapes=[pltpu.VMEM((tt, D_MODEL), F32), pltpu.VMEM((tt, D_MODEL), BF16)],
        args=(dh1, h0, f, gt, up, g_pre, g_post, wg, wu, wd))


def _mm_tn(a, b, m, tm, tn, name, comm=None):
    t, n = b.shape
    tk = min(t, 2048)
    nk = t // tk

    def body(a_ref, b_ref, out_ref, acc_ref):
        k = pl.program_id(2)

        @pl.when(k == 0)
        def _():
            acc_ref[...] = jnp.zeros_like(acc_ref)

        acc_ref[...] += _dot_tn(a_ref[...], b_ref[...])

        @pl.when(k == nk - 1)
        def _():
            out_ref[...] = acc_ref[...].astype(out_ref.dtype)

    outs, got = _call(
        body, comm, name=name, grid=(m // tm, n // tn, nk),
        in_specs=[pl.BlockSpec((tk, tm), lambda i, j, k: (k, i)), pl.BlockSpec((tk, tn), lambda i, j, k: (k, j))],
        out_specs=[pl.BlockSpec((tm, tn), lambda i, j, k: (i, j))],
        out_shape=[jax.ShapeDtypeStruct((m, n), BF16)],
        scratch_shapes=[pltpu.VMEM((tm, tn), F32)], args=(a, b))
    return outs[0], got


_SPLITS = (0, LRU_W, 2 * LRU_W, 2 * LRU_W + Q_LORA, 2 * LRU_W + Q_LORA + KV_LORA, IN_PAD)
_WIDTHS = tuple(_SPLITS[k + 1] - _SPLITS[k] for k in range(5))


def _mixer_in(h1, g_pre, w_in):
    t = h1.shape[0]
    tt = _tile_rows(t)

    def body(h_ref, g_ref, w_ref, n_ref, *outs):
        n = _rms(h_ref[...], g_ref[...]).astype(BF16)
        n_ref[...] = n
        proj = _dot_nt(n, w_ref[...])
        for k in range(5):
            outs[k][...] = proj[:, _SPLITS[k]:_SPLITS[k + 1]]

    row = lambda i: (i, 0)
    return pl.pallas_call(
        body, name="mixer_in",
        grid=(t // tt,),
        in_specs=[pl.BlockSpec((tt, D_MODEL), row), _full((1, D_MODEL)), _full((IN_PAD, D_MODEL))],
        out_specs=[pl.BlockSpec((tt, D_MODEL), row)] + [pl.BlockSpec((tt, w), row) for w in _WIDTHS],
        out_shape=[jax.ShapeDtypeStruct((t, D_MODEL), BF16)] + [jax.ShapeDtypeStruct((t, w), F32) for w in _WIDTHS],
        compiler_params=_params(),
    )(h1, g_pre, w_in)


def _mixer_in_bwd(dh2, h1, g_pre, w_in, dparts):
    t = h1.shape[0]
    tt = _tile_rows(t)

    def body(dh2_ref, h_ref, g_ref, w_ref, p0, p1, p2, p3, p4, dh1_ref, dproj_ref, dg_ref):
        i = pl.program_id(0)

        @pl.when(i == 0)
        def _():
            dg_ref[...] = jnp.zeros_like(dg_ref)

        dproj = jnp.concatenate([p[...] for p in (p0, p1, p2, p3, p4)], axis=1)
        dproj_ref[...] = dproj
        dx, dg = _rms_bwd(h_ref[...], g_ref[...], _dot(dproj, w_ref[...]))
        dg_ref[...] += dg
        dh1_ref[...] = dh2_ref[...] + dx

    row = lambda i: (i, 0)
    return pl.pallas_call(
        body, name="mixer_in_bwd",
        grid=(t // tt,),
        in_specs=[pl.BlockSpec((tt, D_MODEL), row), pl.BlockSpec((tt, D_MODEL), row), _full((1, D_MODEL)),
                  _full((IN_PAD, D_MODEL))] + [pl.BlockSpec((tt, w), row) for w in _WIDTHS],
        out_specs=[pl.BlockSpec((tt, D_MODEL), row), pl.BlockSpec((tt, IN_PAD), row), _full((1, D_MODEL))],
        out_shape=[jax.ShapeDtypeStruct((t, D_MODEL), F32), jax.ShapeDtypeStruct((t, IN_PAD), BF16),
                   jax.ShapeDtypeStruct((1, D_MODEL), F32)],
        compiler_params=_params(),
    )(dh2, h1, g_pre, w_in, *dparts)


def _conv(x_prev8, x_tile, w_ref, b_ref):
    tt = x_tile.shape[0]
    xx = jnp.concatenate([x_prev8, x_tile], axis=0)
    y = b_ref[...] + w_ref[CONV_W - 1:CONV_W, :] * x_tile
    for k in range(CONV_W - 1):
        y = y + w_ref[k:k + 1, :] * pltpu.roll(xx, CONV_W - 1 - k, 0)[8:8 + tt]
    return y


def _neg_expm1(z):
    series = -z * (1.0 + z * 0.5 * (1.0 + z / 3.0 * (1.0 + z * 0.25 * (1.0 + z * 0.2 * (1.0 + z / 6.0)))))
    return jnp.where(z > -0.3, series, 1.0 - jnp.exp(z))


def _lru_elementwise(xc, pre_a, pre_i, lam):
    ra = _sigmoid(pre_a)
    ri = _sigmoid(pre_i)
    neg = -lam
    softplus = jnp.maximum(neg, 0.0) + jnp.log(1.0 + jnp.exp(-jnp.abs(neg)))
    log_a = -LRU_C * ra * softplus
    a = jnp.exp(log_a)
    mult = jnp.sqrt(_neg_expm1(2.0 * log_a))
    return a, mult * (ri * xc)


def _lru_gates(xl, conv_w, conv_b, wa, wx, ba, bx, lam):
    t = xl.shape[0]
    tt = _tile_rows(t)
    r8 = tt // 8

    def body(prev_ref, x_ref, cw_ref, cb_ref, wa_ref, wx_ref, ba_ref, bx_ref, lam_ref, xc_ref, a_ref, u_ref):
        i = pl.program_id(0)
        prev = jnp.where(i == 0, 0.0, prev_ref[...])
        xc = _conv(prev, x_ref[...], cw_ref, cb_ref)
        xc_ref[...] = xc
        xb = xc.astype(BF16)
        a, u = _lru_elementwise(xc, _dot(xb, wa_ref[...]) + ba_ref[...], _dot(xb, wx_ref[...]) + bx_ref[...],
                                lam_ref[...])
        a_ref[...] = a
        u_ref[...] = u

    row = lambda i: (i, 0)
    vec = _full((1, LRU_W))
    return pl.pallas_call(
        body, name="lru_gates",
        grid=(t // tt,),
        in_specs=[pl.BlockSpec((8, LRU_W), lambda i: (jnp.maximum(i * r8 - 1, 0), 0)),
                  pl.BlockSpec((tt, LRU_W), row), _full((8, LRU_W)), vec,
                  _full((LRU_W, LRU_W)), _full((LRU_W, LRU_W)), vec, vec, vec],
        out_specs=[pl.BlockSpec((tt, LRU_W), row)] * 3,
        out_shape=[jax.ShapeDtypeStruct((t, LRU_W), F32)] * 3,
        compiler_params=_params(),
    )(xl, xl, conv_w, conv_b, wa, wx, ba, bx, lam)


def _lru_scan(a, u, reverse):
    t = a.shape[0]
    nblk = t // 8

    def body(a_ref, u_ref, h_ref):
        def fwd(blk, h):
            base = pl.multiple_of(blk * 8, 8)
            for k in range(8):
                h = a_ref[pl.ds(base + k, 1), :] * h + u_ref[pl.ds(base + k, 1), :]
                h_ref[pl.ds(base + k, 1), :] = h
            return h

        def bwd(n, carry):
            base = pl.multiple_of((nblk - 1 - n) * 8, 8)
            for k in range(7, -1, -1):
                g = u_ref[pl.ds(base + k, 1), :] + carry
                h_ref[pl.ds(base + k, 1), :] = g
                carry = a_ref[pl.ds(base + k, 1), :] * g
            return carry

        lax.fori_loop(0, nblk, bwd if reverse else fwd, jnp.zeros((1, LRU_W), F32))

    return pl.pallas_call(
        body, name="lru_scan_rev" if reverse else "lru_scan",
        in_specs=[WHOLE] * 2, out_specs=WHOLE,
        out_shape=jax.ShapeDtypeStruct((t, LRU_W), F32),
        compiler_params=_params(),
    )(a, u)


def _lru_gates_bwd(dh, h, xc, wa, wx, ba, bx, lam):
    t = dh.shape[0]
    tt = _tile_rows(t)
    r8 = tt // 8

    def body(dh_ref, hprev_ref, h_ref, xc_ref, wa_ref, wx_ref, ba_ref, bx_ref, lam_ref,
             dxc_ref, dwa_ref, dwx_ref, dba_ref, dbx_ref, dlam_ref):
        i = pl.program_id(0)

        @pl.when(i == 0)
        def _():
            for r in (dwa_ref, dwx_ref, dba_ref, dbx_ref, dlam_ref):
                r[...] = jnp.zeros_like(r)

        prev = jnp.where(i == 0, 0.0, hprev_ref[...])
        h_before = pltpu.roll(jnp.concatenate([prev, h_ref[...]], axis=0), 1, 0)[8:8 + tt]
        dh_t = dh_ref[...]
        xc = xc_ref[...]
        xb = xc.astype(BF16)
        pre_a = _dot(xb, wa_ref[...]) + ba_ref[...]
        pre_i = _dot(xb, wx_ref[...]) + bx_ref[...]
        _, vjp = jax.vjp(_lru_elementwise, xc, pre_a, pre_i, lam_ref[...])
        dxc, dpre_a, dpre_i, dlam = vjp((dh_t * h_before, dh_t))
        da_b = dpre_a.astype(BF16)
        di_b = dpre_i.astype(BF16)
        dxc_ref[...] = dxc + _dot_nt(da_b, wa_ref[...]) + _dot_nt(di_b, wx_ref[...])
        dwa_ref[...] += _dot_tn(xb, da_b)
        dwx_ref[...] += _dot_tn(xb, di_b)
        dba_ref[...] += jnp.sum(dpre_a, axis=0, keepdims=True)
        dbx_ref[...] += jnp.sum(dpre_i, axis=0, keepdims=True)
        dlam_ref[...] += dlam

    row = lambda i: (i, 0)
    vec = _full((1, LRU_W))
    sq = _full((LRU_W, LRU_W))
    return pl.pallas_call(
        body, name="lru_gates_bwd",
        grid=(t // tt,),
        in_specs=[pl.BlockSpec((tt, LRU_W), row),
                  pl.BlockSpec((8, LRU_W), lambda i: (jnp.maximum(i * r8 - 1, 0), 0)),
                  pl.BlockSpec((tt, LRU_W), row), pl.BlockSpec((tt, LRU_W), row), sq, sq, vec, vec, vec],
        out_specs=[pl.BlockSpec((tt, LRU_W), row), sq, sq, vec, vec, vec],
        out_shape=[jax.ShapeDtypeStruct((t, LRU_W), F32), jax.ShapeDtypeStruct((LRU_W, LRU_W), F32),
                   jax.ShapeDtypeStruct((LRU_W, LRU_W), F32)] + [jax.ShapeDtypeStruct((1, LRU_W), F32)] * 3,
        compiler_params=_params(),
    )(dh, h, h, xc, wa, wx, ba, bx, lam)


def _conv_bwd(dxc, xl, conv_w):
    t = dxc.shape[0]
    tt = _tile_rows(t)
    r8 = tt // 8
    n_tiles = t // tt
    last = t // 8 - 1

    def body(d_ref, dnext_ref, xprev_ref, x_ref, cw_ref, dxl_ref, dw_ref, db_ref):
        i = pl.program_id(0)

        @pl.when(i == 0)
        def _():
            dw_ref[...] = jnp.zeros_like(dw_ref)
            db_ref[...] = jnp.zeros_like(db_ref)

        d = d_ref[...]
        nxt = jnp.where(i == n_tiles - 1, 0.0, dnext_ref[...])
        dd = jnp.concatenate([d, nxt], axis=0)
        dx = cw_ref[CONV_W - 1:CONV_W, :] * d
        for k in range(CONV_W - 1):
            shift = CONV_W - 1 - k
            dx = dx + cw_ref[k:k + 1, :] * pltpu.roll(dd, tt + 8 - shift, 0)[:tt]
        dxl_ref[...] = dx.astype(BF16)

        prev = jnp.where(i == 0, 0.0, xprev_ref[...])
        xx = jnp.concatenate([prev, x_ref[...]], axis=0)
        rows = []
        for k in range(CONV_W):
            shifted = x_ref[...] if k == CONV_W - 1 else pltpu.roll(xx, CONV_W - 1 - k, 0)[8:8 + tt]
            rows.append(jnp.sum(d * shifted, axis=0, keepdims=True))
        rows.append(jnp.zeros((8 - CONV_W, LRU_W), F32))
        dw_ref[...] += jnp.concatenate(rows, axis=0)
        db_ref[...] += jnp.sum(d, axis=0, keepdims=True)

    row = lambda i: (i, 0)
    return pl.pallas_call(
        body, name="conv_bwd",
        grid=(n_tiles,),
        in_specs=[pl.BlockSpec((tt, LRU_W), row),
                  pl.BlockSpec((8, LRU_W), lambda i: (jnp.minimum((i + 1) * r8, last), 0)),
                  pl.BlockSpec((8, LRU_W), lambda i: (jnp.maximum(i * r8 - 1, 0), 0)),
                  pl.BlockSpec((tt, LRU_W), row), _full((8, LRU_W))],
        out_specs=[pl.BlockSpec((tt, LRU_W), row), _full((8, LRU_W)), _full((1, LRU_W))],
        out_shape=[jax.ShapeDtypeStruct((t, LRU_W), BF16), jax.ShapeDtypeStruct((8, LRU_W), F32),
                   jax.ShapeDtypeStruct((1, LRU_W), F32)],
        compiler_params=_params(),
    )(dxc, dxc, xl, xl, conv_w)


def _rope(x, c, s1, s2):
    n = x.shape[-1]
    return x * c + pltpu.roll(x, n - 16, 1) * s1 + pltpu.roll(x, 16, 1) * s2


def _rope_t(d, c, s1, s2):
    n = d.shape[-1]
    return d * c + pltpu.roll(d * s1, 16, 1) + pltpu.roll(d * s2, n - 16, 1)


def _mla_proj(qlat, kvlat, krope, tables, gq, gkv, wq, wk, wv):
    t = qlat.shape[0]
    tt = _tile_rows(t)

    def body(q_ref, kv_ref, kr_ref, cq, s1q, s2q, ck, s1k, s2k, gq_ref, gkv_ref, wq_ref, wk_ref, wv_ref,
             qo_ref, ko_ref, vo_ref, qn_ref, kva_ref):
        qn = _rms(q_ref[...], gq_ref[...]).astype(BF16)
        kvn = _rms(kv_ref[...], gkv_ref[...]).astype(BF16)
        kr = _rope(kr_ref[...], ck[...], s1k[...], s2k[...]).astype(BF16)
        kva = jnp.concatenate([kvn, kr], axis=1)
        qn_ref[...] = qn
        kva_ref[...] = kva
        q = _dot_nt(qn, wq_ref[...])
        tile = lambda r: jnp.tile(r[...], (1, N_HEADS))
        qo_ref[...] = _rope(q, tile(cq), tile(s1q), tile(s2q)).astype(BF16)
        ko_ref[...] = _dot(kva, wk_ref[...]).astype(BF16)
        vo_ref[...] = _dot(kvn, wv_ref[...]).astype(BF16)

    row = lambda i: (i, 0)
    tab = pl.BlockSpec((tt, HEAD_PAD), row)
    return pl.pallas_call(
        body, name="mla_proj",
        grid=(t // tt,),
        in_specs=[pl.BlockSpec((tt, Q_LORA), row), pl.BlockSpec((tt, KV_LORA), row), tab] + [tab] * 6 + [
            _full((1, Q_LORA)), _full((1, KV_LORA)), _full((HEADS_W, Q_LORA)), _full((KV_LORA + HEAD_PAD, HEADS_W)),
            _full((KV_LORA, HEADS_W))],
        out_specs=[pl.BlockSpec((tt, HEADS_W), row)] * 3 + [pl.BlockSpec((tt, Q_LORA), row),
                                                             pl.BlockSpec((tt, KV_LORA + HEAD_PAD), row)],
        out_shape=[jax.ShapeDtypeStruct((t, HEADS_W), BF16)] * 3 + [
            jax.ShapeDtypeStruct((t, Q_LORA), BF16), jax.ShapeDtypeStruct((t, KV_LORA + HEAD_PAD), BF16)],
        compiler_params=_params(),
    )(qlat, kvlat, krope, *tables, gq, gkv, wq, wk, wv)


def _mla_proj_bwd(dq, dk, dv, qlat, kvlat, tables, gq, gkv, wq, wk, wv):
    t = qlat.shape[0]
    tt = _tile_rows(t)

    def body(dq_ref, dk_ref, dv_ref, q_ref, kv_ref, cq, s1q, s2q, ck, s1k, s2k, gq_ref, gkv_ref,
             wq_ref, wk_ref, wv_ref, dqr_ref, dql_ref, dkvl_ref, dkr_ref, dgq_ref, dgkv_ref):
        i = pl.program_id(0)

        @pl.when(i == 0)
        def _():
            dgq_ref[...] = jnp.zeros_like(dgq_ref)
            dgkv_ref[...] = jnp.zeros_like(dgkv_ref)

        tile = lambda r: jnp.tile(r[...], (1, N_HEADS))
        dqr = _rope_t(dq_ref[...], tile(cq), tile(s1q), tile(s2q)).astype(BF16)
        dqr_ref[...] = dqr
        dql, dg = _rms_bwd(q_ref[...], gq_ref[...], _dot(dqr, wq_ref[...]))
        dgq_ref[...] += dg
        dql_ref[...] = dql.astype(BF16)
        dkva = _dot_nt(dk_ref[...], wk_ref[...])
        dkvn = dkva[:, :KV_LORA] + _dot_nt(dv_ref[...], wv_ref[...])
        dkvl, dg = _rms_bwd(kv_ref[...], gkv_ref[...], dkvn)
        dgkv_ref[...] += dg
        dkvl_ref[...] = dkvl.astype(BF16)
        dkr_ref[...] = _rope_t(dkva[:, KV_LORA:], ck[...], s1k[...], s2k[...]).astype(BF16)

    row = lambda i: (i, 0)
    tab = pl.BlockSpec((tt, HEAD_PAD), row)
    wide = pl.BlockSpec((tt, HEADS_W), row)
    return pl.pallas_call(
        body, name="mla_proj_bwd",
        grid=(t // tt,),
        in_specs=[wide, wide, wide, pl.BlockSpec((tt, Q_LORA), row), pl.BlockSpec((tt, KV_LORA), row)] + [tab] * 6 + [
            _full((1, Q_LORA)), _full((1, KV_LORA)), _full((HEADS_W, Q_LORA)), _full((KV_LORA + HEAD_PAD, HEADS_W)),
            _full((KV_LORA, HEADS_W))],
        out_specs=[wide, pl.BlockSpec((tt, Q_LORA), row), pl.BlockSpec((tt, KV_LORA), row), tab,
                   _full((1, Q_LORA)), _full((1, KV_LORA))],
        out_shape=[jax.ShapeDtypeStruct((t, HEADS_W), BF16), jax.ShapeDtypeStruct((t, Q_LORA), BF16),
                   jax.ShapeDtypeStruct((t, KV_LORA), BF16), jax.ShapeDtypeStruct((t, HEAD_PAD), BF16),
                   jax.ShapeDtypeStruct((1, Q_LORA), F32), jax.ShapeDtypeStruct((1, KV_LORA), F32)],
        compiler_params=_params(),
    )(dq, dk, dv, qlat, kvlat, *tables, gq, gkv, wq, wk, wv)


NEG = -1e30


def _chunk_mask(rows, cols, row0):
    r = (lax.broadcasted_iota(jnp.int32, (rows, cols), 0) + row0) // CHUNK
    c = lax.broadcasted_iota(jnp.int32, (rows, cols), 1) // CHUNK
    return c <= r


def _attn_fwd(q, k, v, comm):
    t = q.shape[0]
    bq = _tile_rows(t)
    nq = t // bq
    hb = bq // 2

    def body(q_ref, k_ref, v_ref, o_ref, lse_ref, m_ref, l_ref, acc_ref):
        i = pl.program_id(1)
        m_ref[...] = jnp.full_like(m_ref, NEG)
        l_ref[...] = jnp.zeros_like(l_ref)
        acc_ref[...] = jnp.zeros_like(acc_ref)

        def block(j, diagonal):
            cols = pl.ds(pl.multiple_of(j * bq, bq), bq)
            kb, vb = k_ref[cols, :], v_ref[cols, :]
            for half in range(2):
                rows = slice(half * hb, (half + 1) * hb)
                s = _dot_nt(q_ref[rows, :], kb)
                if diagonal:
                    s = jnp.where(_chunk_mask(hb, bq, half * hb), s, NEG)
                m_old = m_ref[rows, :]
                m_new = jnp.maximum(m_old, jnp.max(s, axis=-1, keepdims=True))
                alpha = jnp.exp(m_old - m_new)
                p = jnp.exp(s - m_new)
                l_ref[rows, :] = alpha * l_ref[rows, :] + jnp.sum(p, axis=-1, keepdims=True)
                acc_ref[rows, :] = alpha * acc_ref[rows, :] + _dot(p.astype(BF16), vb)
                m_ref[rows, :] = m_new

        def below_diagonal(j, carry):
            block(j, False)
            return carry

        lax.fori_loop(0, i, below_diagonal, 0)
        block(i, True)
        l = l_ref[...]
        o_ref[...] = (acc_ref[...] / l).astype(BF16)
        lse_ref[...] = jnp.broadcast_to(m_ref[...] + jnp.log(l), (bq, HEAD_PAD))

    qmap = lambda h, i: (i, h)
    head = lambda h, i: (0, h)
    return _call(
        body, comm, name="attn_fwd" if comm is None else "attn_fwd_gather", grid=(N_HEADS, nq),
        in_specs=[pl.BlockSpec((bq, HEAD_PAD), qmap), pl.BlockSpec((t, HEAD_PAD), head),
                  pl.BlockSpec((t, HEAD_PAD), head)],
        out_specs=[pl.BlockSpec((bq, HEAD_PAD), qmap), pl.BlockSpec((bq, HEAD_PAD), qmap)],
        out_shape=[jax.ShapeDtypeStruct(q.shape, BF16), jax.ShapeDtypeStruct(q.shape, F32)],
        scratch_shapes=[pltpu.VMEM((bq, 1), F32), pltpu.VMEM((bq, 1), F32), pltpu.VMEM((bq, HEAD_PAD), F32)],
        args=(q, k, v))


def _attn_bwd(q, k, v, o, do, lse, comm):
    t = q.shape[0]
    bq = _tile_rows(t)
    nq = t // bq

    def body(q_ref, k_ref, v_ref, o_ref, do_ref, lse_ref, dq_ref, dk_ref, dv_ref, dk_acc, dv_acc, delta_ref):
        j = pl.program_id(1)

        @pl.when(j == 0)
        def _():
            dq_ref[...] = jnp.zeros_like(dq_ref)
            for blk in range(nq):
                rows = slice(blk * bq, (blk + 1) * bq)
                delta_ref[rows, :] = jnp.sum(do_ref[rows, :].astype(F32) * o_ref[rows, :].astype(F32),
                                             axis=-1, keepdims=True)

        dk_acc[...] = jnp.zeros_like(dk_acc)
        dv_acc[...] = jnp.zeros_like(dv_acc)
        kb, vb = k_ref[...], v_ref[...]

        def block(i, diagonal):
            rows = pl.ds(pl.multiple_of(i * bq, bq), bq)
            qb, dob = q_ref[rows, :], do_ref[rows, :]
            s = _dot_nt(qb, kb)
            if diagonal:
                s = jnp.where(_chunk_mask(bq, bq, 0), s, NEG)
            p = jnp.exp(s - lse_ref[rows, :1])
            dv_acc[...] += _dot_tn(p.astype(BF16), dob)
            dp = _dot_nt(dob, vb)
            ds = (p * (dp - delta_ref[rows, :])).astype(BF16)
            dk_acc[...] += _dot_tn(ds, qb)
            dq_ref[rows, :] += _dot(ds, kb)

        def above_diagonal(i, carry):
            block(i, False)
            return carry

        block(j, True)
        lax.fori_loop(j + 1, nq, above_diagonal, 0)
        dk_ref[...] = dk_acc[...].astype(BF16)
        dv_ref[...] = dv_acc[...].astype(BF16)

    kmap = lambda h, j: (j, h)
    head = lambda h, j: (0, h)
    whole = pl.BlockSpec((t, HEAD_PAD), head)
    return _call(
        body, comm, name="attn_bwd" if comm is None else "attn_bwd_exchange", grid=(N_HEADS, nq),
        in_specs=[whole, pl.BlockSpec((bq, HEAD_PAD), kmap), pl.BlockSpec((bq, HEAD_PAD), kmap), whole, whole, whole],
        out_specs=[whole, pl.BlockSpec((bq, HEAD_PAD), kmap), pl.BlockSpec((bq, HEAD_PAD), kmap)],
        out_shape=[jax.ShapeDtypeStruct(q.shape, F32), jax.ShapeDtypeStruct(q.shape, BF16),
                   jax.ShapeDtypeStruct(q.shape, BF16)],
        scratch_shapes=[pltpu.VMEM((bq, HEAD_PAD), F32), pltpu.VMEM((bq, HEAD_PAD), F32), pltpu.VMEM((t, 1), F32)],
        args=(q, k, v, o, do, lse))


def _mixer_out(h1, hl, gate, o, w_lru, w_mla, g_post):
    t = h1.shape[0]
    tt = _tile_rows(t)

    def body(h1_ref, hl_ref, gate_ref, o_ref, wl_ref, wm_ref, g_ref, h2_ref, m_ref, y_ref):
        y = (hl_ref[...] * _gelu(gate_ref[...])).astype(BF16)
        y_ref[...] = y
        m = _dot(y, wl_ref[...]) + _dot(o_ref[...], wm_ref[...])
        m_ref[...] = m
        h2_ref[...] = h1_ref[...] + _rms(m, g_ref[...])

    row = lambda i: (i, 0)
    return pl.pallas_call(
        body, name="mixer_out",
        grid=(t // tt,),
        in_specs=[pl.BlockSpec((tt, D_MODEL), row), pl.BlockSpec((tt, LRU_W), row), pl.BlockSpec((tt, LRU_W), row),
                  pl.BlockSpec((tt, HEADS_W), row), _full((LRU_W, D_MODEL)), _full((HEADS_W, D_MODEL)),
                  _full((1, D_MODEL))],
        out_specs=[pl.BlockSpec((tt, D_MODEL), row), pl.BlockSpec((tt, D_MODEL), row),
                   pl.BlockSpec((tt, LRU_W), row)],
        out_shape=[jax.ShapeDtypeStruct((t, D_MODEL), F32), jax.ShapeDtypeStruct((t, D_MODEL), F32),
                   jax.ShapeDtypeStruct((t, LRU_W), BF16)],
        compiler_params=_params(),
    )(h1, hl, gate, o, w_lru, w_mla, g_post)


def _mixer_out_bwd(dh2, m, hl, gate, w_lru, w_mla, g_post):
    t = m.shape[0]
    tt = _tile_rows(t)

    def body(dh2_ref, m_ref, hl_ref, gate_ref, wl_ref, wm_ref, g_ref, dm_ref, dgate_ref, dhl_ref, do_ref, dg_ref):
        i = pl.program_id(0)

        @pl.when(i == 0)
        def _():
            dg_ref[...] = jnp.zeros_like(dg_ref)

        dm, dg = _rms_bwd(m_ref[...], g_ref[...], dh2_ref[...])
        dg_ref[...] += dg
        dmb = dm.astype(BF16)
        dm_ref[...] = dmb
        dy = _dot_nt(dmb, wl_ref[...])
        gate = gate_ref[...]
        dgate_ref[...] = (dy * hl_ref[...] * _gelu_grad(gate)).astype(BF16)
        dhl_ref[...] = dy * _gelu(gate)
        do_ref[...] = _dot_nt(dmb, wm_ref[...]).astype(BF16)

    row = lambda i: (i, 0)
    return pl.pallas_call(
        body, name="mixer_out_bwd",
        grid=(t // tt,),
        in_specs=[pl.BlockSpec((tt, D_MODEL), row), pl.BlockSpec((tt, D_MODEL), row), pl.BlockSpec((tt, LRU_W), row),
                  pl.BlockSpec((tt, LRU_W), row), _full((LRU_W, D_MODEL)), _full((HEADS_W, D_MODEL)),
                  _full((1, D_MODEL))],
        out_specs=[pl.BlockSpec((tt, D_MODEL), row), pl.BlockSpec((tt, LRU_W), row), pl.BlockSpec((tt, LRU_W), row),
                   pl.BlockSpec((tt, HEADS_W), row), _full((1, D_MODEL))],
        out_shape=[jax.ShapeDtypeStruct((t, D_MODEL), BF16), jax.ShapeDtypeStruct((t, LRU_W), BF16),
                   jax.ShapeDtypeStruct((t, LRU_W), F32), jax.ShapeDtypeStruct((t, HEADS_W), BF16),
                   jax.ShapeDtypeStruct((1, D_MODEL), F32)],
        compiler_params=_params(),
    )(dh2, m, hl, gate, w_lru, w_mla, g_post)


def _loss_head(y, target):
    t = y.shape[0]
    tt = _tile_rows(t)

    def body(y_ref, t_ref, dy_ref, loss_ref):
        i = pl.program_id(0)

        @pl.when(i == 0)
        def _():
            loss_ref[...] = jnp.zeros_like(loss_ref)

        e = y_ref[...] - t_ref[...]
        dy_ref[...] = e * (1.0 / D_MODEL)
        per_token = jnp.mean(e * e, axis=-1, keepdims=True)
        loss_ref[...] += 0.5 * jnp.sum(per_token, axis=0, keepdims=True)

    row = lambda i: (i, 0)
    return pl.pallas_call(
        body, name="loss_head",
        grid=(t // tt,),
        in_specs=[pl.BlockSpec((tt, D_MODEL), row), pl.BlockSpec((tt, D_MODEL), row)],
        out_specs=[pl.BlockSpec((tt, D_MODEL), row), _full((8, 128))],
        out_shape=[jax.ShapeDtypeStruct((t, D_MODEL), F32), jax.ShapeDtypeStruct((8, 128), F32)],
        compiler_params=_params(),
    )(y, target)


def _adamw(w, g, m, v):
    m = ADAM_B1 * m + (1.0 - ADAM_B1) * g
    v = ADAM_B2 * v + (1.0 - ADAM_B2) * (g * g)
    m_hat = m / (1.0 - ADAM_B1 ** ADAM_STEP)
    v_hat = v / (1.0 - ADAM_B2 ** ADAM_STEP)
    delta = -ADAM_LR * (m_hat / (jnp.sqrt(v_hat) + ADAM_EPS) + ADAM_WD * w)
    return delta, m, v


def _sum_adamw(terms, w, m, v, name):
    r, c = w.shape

    def body(t_ref, w_ref, m_ref, v_ref, g_out, d_out, m_out, v_out):
        g = t_ref[0, :r, :c].astype(F32)
        for q in range(1, N_DEV):
            g = g + t_ref[q, :r, :c].astype(F32)
        g_out[...] = g
        d_out[...], m_out[...], v_out[...] = _adamw(w_ref[...], g, m_ref[...], v_ref[...])

    return pl.pallas_call(
        body, name=name, in_specs=[WHOLE] * 4, out_specs=[WHOLE] * 4,
        out_shape=[jax.ShapeDtypeStruct((r, c), F32)] * 4,
        compiler_params=_params(),
    )(terms, w, m, v)


def _sum_terms(terms):
    def body(t_ref, out_ref):
        g = t_ref[0]
        for q in range(1, N_DEV):
            g = g + t_ref[q]
        out_ref[...] = g

    return pl.pallas_call(
        body, name="small_grad_sum", in_specs=[WHOLE], out_specs=WHOLE,
        out_shape=jax.ShapeDtypeStruct(terms.shape[1:], F32),
        compiler_params=_params(),
    )(terms)


def _adamw_flat(w, g, m, v):
    def body(w_ref, g_ref, m_ref, v_ref, d_out, m_out, v_out):
        d_out[...], m_out[...], v_out[...] = _adamw(w_ref[...], g_ref[...], m_ref[...], v_ref[...])

    return pl.pallas_call(
        body, name="small_adamw", in_specs=[WHOLE] * 4, out_specs=[WHOLE] * 3,
        out_shape=[jax.ShapeDtypeStruct(w.shape, F32)] * 3,
        compiler_params=_params(),
    )(w, g, m, v)


def _pack(arrays):
    flat = jnp.concatenate([a.reshape(-1).astype(F32) for a in arrays])
    rows = -(-flat.shape[0] // 1024) * 8
    return jnp.pad(flat, (0, rows * 128 - flat.shape[0])).reshape(rows, 128)


def _unpack(packed, shapes):
    flat = packed.reshape(-1)
    out, pos = [], 0
    for s in shapes:
        n = math.prod(s)
        out.append(flat[pos:pos + n].reshape(s))
        pos += n
    return out


def _rope_tables(positions):
    t = positions.shape[-1]
    inv_freq = 1.0 / (ROPE_THETA ** (jnp.arange(0, QK_ROPE, 2, dtype=F32) / QK_ROPE))
    ang = positions.reshape(t, 1).astype(F32) * inv_freq
    cos, sin = jnp.cos(ang), jnp.sin(ang)
    z16 = jnp.zeros((t, 16), F32)

    def place(first, second, lead, lead_value):
        parts = [jnp.full((t, lead), lead_value, F32)] if lead else []
        parts += [first, second, jnp.zeros((t, HEAD_PAD - lead - 32), F32)]
        return jnp.concatenate(parts, axis=1)

    q_tabs = [place(cos, cos, QK_NOPE, 1.0) * ATT_SCALE, place(-sin, z16, QK_NOPE, 0.0) * ATT_SCALE,
              place(z16, sin, QK_NOPE, 0.0) * ATT_SCALE]
    k_tabs = [place(cos, cos, 0, 0.0), place(-sin, z16, 0, 0.0), place(z16, sin, 0, 0.0)]
    return q_tabs + k_tabs


def _block_diag(w):
    eye = jnp.eye(w.shape[0], dtype=w.dtype)
    return jnp.einsum("nde,nm->ndme", w, eye).reshape(LRU_W, LRU_W)


def _diag_blocks(g):
    g4 = g.reshape(8, 64, 8, 64)
    return jnp.moveaxis(jnp.diagonal(g4, axis1=0, axis2=2), -1, 0)


def _mixer_layouts(got):
    full = {}
    w_in = got["w_in"][:, :IN_SHARD].reshape(IN_COLS, D_MODEL)
    full["w_in"] = jnp.pad(w_in, ((0, IN_PAD - IN_COLS), (0, 0)))
    full["wq"] = got["w_q_b"].reshape(HEADS_W, Q_LORA)
    kvb = got["w_kv_b"]
    zeros = jnp.zeros((N_HEADS, KV_LORA, HEAD_PAD - QK_NOPE), BF16)
    k_part = jnp.transpose(jnp.concatenate([kvb[:, :, :QK_NOPE], zeros], axis=2), (1, 0, 2)).reshape(KV_LORA, HEADS_W)
    rope_rows = jnp.pad(jnp.eye(QK_ROPE, dtype=BF16), ((0, HEAD_PAD - QK_ROPE), (QK_NOPE, HEAD_PAD - QK_NOPE - QK_ROPE)))
    full["wk"] = jnp.concatenate([k_part, jnp.tile(rope_rows, (1, N_HEADS))], axis=0)
    full["wv"] = jnp.transpose(jnp.concatenate([kvb[:, :, QK_NOPE:], zeros], axis=2), (1, 0, 2)).reshape(KV_LORA, HEADS_W)
    w_out = got["w_out"].reshape(D_MODEL, D_MODEL)
    full["w_out_lru"] = w_out[:LRU_W]
    full["w_out_mla"] = jnp.pad(w_out[LRU_W:].reshape(N_HEADS, V_DIM, D_MODEL),
                                ((0, 0), (0, HEAD_PAD - V_DIM), (0, 0))).reshape(HEADS_W, D_MODEL)
    conv_w = jnp.transpose(got["conv_w"][:, :CONV_W, :LRU_W // N_DEV], (1, 0, 2)).reshape(CONV_W, LRU_W)
    full["conv_w"] = jnp.pad(conv_w, ((0, 8 - CONV_W), (0, 0)))
    return full


_FFN1 = ["w_ffn1_gate", "w_ffn1_up", "w_ffn1_down"]
_MIX = ["w_in", "w_q_b", "w_kv_b", "w_out"]
_FFN2 = ["w_ffn2_gate", "w_ffn2_up", "w_ffn2_down"]
_SHARDED = _FFN1 + _MIX + _FFN2
_TRANSPOSED = ("w_ffn1_gate", "w_ffn1_up", "w_in", "w_q_b", "w_ffn2_gate", "w_ffn2_up")
_SMALL = ["g_ffn1_pre", "g_ffn1_post", "g_mix_pre", "g_mix_post", "conv_b", "w_lru_a", "b_lru_a", "w_lru_x",
          "b_lru_x", "lru_lambda", "q_a_norm", "kv_a_norm", "g_ffn2_pre", "g_ffn2_post", "conv_w"]
_ORDER = ["g_ffn1_pre", "g_ffn1_post", "w_ffn1_gate", "w_ffn1_up", "w_ffn1_down", "g_mix_pre", "g_mix_post", "w_in",
          "conv_w", "conv_b", "w_lru_a", "b_lru_a", "w_lru_x", "b_lru_x", "lru_lambda", "q_a_norm", "w_q_b",
          "kv_a_norm", "w_kv_b", "w_out", "g_ffn2_pre", "g_ffn2_post", "w_ffn2_gate", "w_ffn2_up", "w_ffn2_down"]
_FF_BLOCK = (FF_PAD, D_MODEL)
_PADS = {"w_ffn1_gate": _FF_BLOCK, "w_ffn1_up": _FF_BLOCK, "w_ffn1_down": _FF_BLOCK,
         "w_in": (IN_SHARD_PAD, D_MODEL), "w_q_b": (HEAD_PAD, Q_LORA), "w_kv_b": (KV_LORA, HEAD_PAD),
         "w_out": (HEAD_PAD, D_MODEL), "conv_w": (8, 128),
         "w_ffn2_gate": _FF_BLOCK, "w_ffn2_up": _FF_BLOCK, "w_ffn2_down": _FF_BLOCK}


def _rows(name, a):
    return a.T if name in _TRANSPOSED else a


def _step(args):
    x = args["x"][0]
    target = args["loss_target"][0]
    positions = args["positions"][0]
    w = {n: args[n][0] for n in _ORDER}
    mom = {n: args["m_" + n][0] for n in _ORDER}
    var = {n: args["v_" + n][0] for n in _ORDER}
    vec = lambda name: w[name].reshape(1, -1)
    t = x.shape[0]
    ff_blocks = lambda a: a.reshape(N_DEV, FF_PAD, D_MODEL)
    ff_rows = lambda a: a.reshape(FF_VIRT, D_MODEL)

    names = _FFN1 + _MIX + ["conv_w"] + _FFN2
    ffn1, staged = _first_gather([_rows(n, w[n]) for n in names], [_PADS[n] for n in names],
                                 [F32 if n == "conv_w" else BF16 for n in names], len(_FFN1))
    wg1, wu1, wd1 = [ff_rows(a) for a in ffn1]
    n_mix = len(_MIX) + 1

    tables = _rope_tables(positions)
    wa = _block_diag(w["w_lru_a"]).astype(BF16)
    wx = _block_diag(w["w_lru_x"]).astype(BF16)

    (h1, f1, n1, gt1, up1), got = _ffn_fwd(x, vec("g_ffn1_pre"), vec("g_ffn1_post"), wg1, wu1, wd1,
                                           _Gather(staged[:n_mix]), "ffn_fwd_gather")
    mix = _mixer_layouts(dict(zip(_MIX + ["conv_w"], got)))
    nmix, xl, gate, qlat, kvlat, krope = _mixer_in(h1, vec("g_mix_pre"), mix["w_in"])
    xc, a, u = _lru_gates(xl, mix["conv_w"], vec("conv_b"), wa, wx, vec("b_lru_a"), vec("b_lru_x"), vec("lru_lambda"))
    hl = _lru_scan(a, u, reverse=False)
    q, k, v, qn, kva = _mla_proj(qlat, kvlat, krope, tables, vec("q_a_norm"), vec("kv_a_norm"),
                                 mix["wq"], mix["wk"], mix["wv"])
    (o, lse), ffn2 = _attn_fwd(q, k, v, _Gather(staged[n_mix:]))
    wg2, wu2, wd2 = [ff_rows(a_) for a_ in ffn2]
    h2, m, ylru = _mixer_out(h1, hl, gate, o, mix["w_out_lru"], mix["w_out_mla"], vec("g_mix_post"))
    (h3, f2, n2, gt2, up2), _ = _ffn_fwd(h2, vec("g_ffn2_pre"), vec("g_ffn2_post"), wg2, wu2, wd2, None, "ffn_fwd")
    dy, loss_tile = _loss_head(h3, target)

    (dh2, df2, act2, dgt2, dup2, dg2pre, dg2post), _ = _ffn_bwd(
        dy, h2, f2, gt2, up2, vec("g_ffn2_pre"), vec("g_ffn2_post"), wg2, wu2, wd2, None, "ffn_bwd")
    dw_ffn = lambda a_, b_, name, comm=None: _mm_tn(a_, b_, FF_VIRT, FF_CHUNK, D_MODEL, name, comm)
    big2 = [ff_blocks(dw_ffn(dgt2, n2, "dw_ffn_gate")[0]), ff_blocks(dw_ffn(dup2, n2, "dw_ffn_up")[0]),
            ff_blocks(dw_ffn(act2, df2, "dw_ffn_down")[0])]

    dm, dgate, dhl, do, dgmixpost = _mixer_out_bwd(dh2, m, hl, gate, mix["w_out_lru"], mix["w_out_mla"],
                                                   vec("g_mix_post"))
    dw_out_lru, _ = _mm_tn(ylru, dm, LRU_W, LRU_W, D_MODEL, "dw_out_lru")
    dw_out_mla, _ = _mm_tn(o, dm, HEADS_W, HEADS_W, D_MODEL, "dw_out_mla")
    dw_out = jnp.concatenate(
        [dw_out_lru, dw_out_mla.reshape(N_HEADS, HEAD_PAD, D_MODEL)[:, :V_DIM].reshape(N_HEADS * V_DIM, D_MODEL)], axis=0)

    (dq, dk, dv), terms2 = _attn_bwd(q, k, v, o, do, lse, _Exchange(big2))
    dqr, dqlat, dkvlat, dkrope, dgq, dgkv = _mla_proj_bwd(
        dq, dk, dv, qlat, kvlat, tables, vec("q_a_norm"), vec("kv_a_norm"), mix["wq"], mix["wk"], mix["wv"])
    dwq, _ = _mm_tn(dqr, qn, HEADS_W, HEADS_W, Q_LORA, "dw_q_b")
    dwk, _ = _mm_tn(kva, dk, KV_LORA, KV_LORA, HEADS_W, "dw_kv_b_k")
    dwv, _ = _mm_tn(kva, dv, KV_LORA, KV_LORA, HEADS_W, "dw_kv_b_v")
    dw_kvb = jnp.transpose(jnp.concatenate(
        [dwk.reshape(KV_LORA, N_HEADS, HEAD_PAD)[:, :, :QK_NOPE], dwv.reshape(KV_LORA, N_HEADS, HEAD_PAD)[:, :, :V_DIM]],
        axis=2), (1, 0, 2))

    dh = _lru_scan(a, dhl, reverse=True)
    dxc, dwa, dwx, dba, dbx, dlam = _lru_gates_bwd(dh, hl, xc, wa, wx, vec("b_lru_a"), vec("b_lru_x"),
                                                   vec("lru_lambda"))
    dxl, dconv_w8, dconv_b = _conv_bwd(dxc, xl, mix["conv_w"])
    dh1, dproj, dgmixpre = _mixer_in_bwd(dh2, h1, vec("g_mix_pre"), mix["w_in"], [dxl, dgate, dqlat, dkvlat, dkrope])
    dw_in, _ = _mm_tn(dproj, nmix, IN_PAD, IN_PAD // 2, D_MODEL, "dw_in")
    dw_in = jnp.pad(dw_in[:IN_COLS].reshape(N_DEV, IN_SHARD, D_MODEL), ((0, 0), (0, IN_SHARD_PAD - IN_SHARD), (0, 0)))
    big_mix = [dw_in, dwq.reshape(N_DEV, HEAD_PAD, Q_LORA), dw_kvb, dw_out.reshape(N_DEV, HEAD_PAD, D_MODEL)]

    (dx, df1, act1, dgt1, dup1, dg1pre, dg1post), terms_mix = _ffn_bwd(
        dh1, x, f1, gt1, up1, vec("g_ffn1_pre"), vec("g_ffn1_post"), wg1, wu1, wd1, _Exchange(big_mix),
        "ffn_bwd_exchange")
    dwd1, _ = dw_ffn(act1, df1, "dw_ffn_down")
    dwg1, terms_d1 = dw_ffn(dgt1, n1, "dw_ffn_gate_exchange", _Exchange([ff_blocks(dwd1)]))
    dwu1, terms_g1 = dw_ffn(dup1, n1, "dw_ffn_up_exchange", _Exchange([ff_blocks(dwg1)]))

    small = {"g_ffn1_pre": dg1pre, "g_ffn1_post": dg1post, "g_mix_pre": dgmixpre, "g_mix_post": dgmixpost,
             "conv_b": dconv_b, "w_lru_a": _diag_blocks(dwa), "b_lru_a": dba, "w_lru_x": _diag_blocks(dwx),
             "b_lru_x": dbx, "lru_lambda": dlam, "q_a_norm": dgq, "kv_a_norm": dgkv, "g_ffn2_pre": dg2pre,
             "g_ffn2_post": dg2post, "conv_w": dconv_w8[:CONV_W]}
    packed = _pack([small[n] for n in _SMALL] + [loss_tile[:1, :1]])
    terms_u1, small_terms = _last_exchange([ff_blocks(dwu1), jnp.broadcast_to(packed[None], (N_DEV,) + packed.shape)])

    terms = dict(zip(_FFN2, terms2))
    terms.update(zip(_MIX, terms_mix))
    terms.update({"w_ffn1_down": terms_d1[0], "w_ffn1_gate": terms_g1[0], "w_ffn1_up": terms_u1})

    grads, delta, new_m, new_v = {}, {}, {}, {}
    for n in _SHARDED:
        res = _sum_adamw(terms[n], _rows(n, w[n]), _rows(n, mom[n]), _rows(n, var[n]), "adamw_" + n)
        grads[n], delta[n], new_m[n], new_v[n] = [_rows(n, r) for r in res]

    summed = _sum_terms(small_terms)
    small_sum = dict(zip(_SMALL + ["loss"], _unpack(summed, [small[n].shape for n in _SMALL] + [(1, 1)])))
    loss = small_sum.pop("loss").reshape(())
    me = 4 * lax.axis_index("x") + 2 * lax.axis_index("y") + lax.axis_index("c")
    cshard = LRU_W // N_DEV
    small_sum["conv_w"] = lax.dynamic_slice(small_sum["conv_w"], (0, me * cshard), (CONV_W, cshard))
    for n in _SMALL:
        grads[n] = small_sum[n].reshape(w[n].shape)
    shapes = [w[n].shape for n in _SMALL]
    d_p, m_p, v_p = _adamw_flat(_pack([w[n] for n in _SMALL]), _pack([grads[n] for n in _SMALL]),
                                _pack([mom[n] for n in _SMALL]), _pack([var[n] for n in _SMALL]))
    for n, d_, m_, v_ in zip(_SMALL, _unpack(d_p, shapes), _unpack(m_p, shapes), _unpack(v_p, shapes)):
        delta[n], new_m[n], new_v[n] = d_, m_, v_

    lead = lambda d: [d[n][None] for n in _ORDER]
    return (loss, dx[None], *lead(grads), *lead(delta), *lead(new_m), *lead(new_v))


def kernel(x, positions, g_ffn1_pre, g_ffn1_post, w_ffn1_gate, w_ffn1_up, w_ffn1_down, g_mix_pre, g_mix_post, w_in, conv_w, conv_b, w_lru_a, b_lru_a, w_lru_x, b_lru_x, lru_lambda, q_a_norm, w_q_b, kv_a_norm, w_kv_b, w_out, g_ffn2_pre, g_ffn2_post, w_ffn2_gate, w_ffn2_up, w_ffn2_down, loss_target, m_g_ffn1_pre, m_g_ffn1_post, m_w_ffn1_gate, m_w_ffn1_up, m_w_ffn1_down, m_g_mix_pre, m_g_mix_post, m_w_in, m_conv_w, m_conv_b, m_w_lru_a, m_b_lru_a, m_w_lru_x, m_b_lru_x, m_lru_lambda, m_q_a_norm, m_w_q_b, m_kv_a_norm, m_w_kv_b, m_w_out, m_g_ffn2_pre, m_g_ffn2_post, m_w_ffn2_gate, m_w_ffn2_up, m_w_ffn2_down, v_g_ffn1_pre, v_g_ffn1_post, v_w_ffn1_gate, v_w_ffn1_up, v_w_ffn1_down, v_g_mix_pre, v_g_mix_post, v_w_in, v_conv_w, v_conv_b, v_w_lru_a, v_b_lru_a, v_w_lru_x, v_b_lru_x, v_lru_lambda, v_q_a_norm, v_w_q_b, v_kv_a_norm, v_w_kv_b, v_w_out, v_g_ffn2_pre, v_g_ffn2_post, v_w_ffn2_gate, v_w_ffn2_up, v_w_ffn2_down):
    return _step(dict(locals()))
```

```python
import functools
import math
import operator

import jax
import jax.numpy as jnp
from jax import lax
from jax.experimental import pallas as pl
from jax.experimental.pallas import tpu as pltpu

F32 = jnp.float32
BF16 = jnp.bfloat16
MESH = pl.DeviceIdType.MESH

N_DEV = 8
D_MODEL = 1024
D_FF = 2816
FF_SHARD = D_FF // N_DEV
FF_PAD = 384
FF_VIRT = N_DEV * FF_PAD
FF_CHUNK = 2 * FF_PAD
LRU_W = 512
N_HEADS = 8
HEAD_PAD = 128
HEADS_W = N_HEADS * HEAD_PAD
QK_NOPE = 64
QK_ROPE = 32
V_DIM = 64
Q_LORA = 384
KV_LORA = 256
IN_COLS = 2 * LRU_W + Q_LORA + KV_LORA + QK_ROPE
IN_SHARD = IN_COLS // N_DEV
IN_SHARD_PAD = 256
IN_PAD = 1792
QB_SHARD = 96
CONV_W = 4
CHUNK = 64
EPS = 1e-6
LRU_C = 8.0
ROPE_THETA = 10000.0
ATT_SCALE = (QK_NOPE + QK_ROPE) ** -0.5

ADAM_LR = 0.001
ADAM_B1 = 0.9
ADAM_B2 = 0.999
ADAM_EPS = 1e-08
ADAM_WD = 0.01
ADAM_STEP = 10

VMEM_LIMIT = 56 * 1024 * 1024
ANY = pl.BlockSpec(memory_space=pl.ANY)
WHOLE = pl.BlockSpec(memory_space=pltpu.VMEM)


def _params(**kw):
    return pltpu.CompilerParams(vmem_limit_bytes=VMEM_LIMIT, **kw)


def _full(shape):
    return pl.BlockSpec(shape, lambda *_: (0,) * len(shape))


def _dot(a, b):
    return jnp.dot(a, b, preferred_element_type=F32)


def _dot_nt(a, b):
    return lax.dot_general(a, b, (((1,), (1,)), ((), ())), preferred_element_type=F32)


def _dot_tn(a, b):
    return lax.dot_general(a, b, (((0,), (0,)), ((), ())), preferred_element_type=F32)


def _rms(x, g):
    r = lax.rsqrt(jnp.mean(x * x, axis=-1, keepdims=True) + EPS)
    return x * r * g


def _rms_bwd(x, g, dy):
    r = lax.rsqrt(jnp.mean(x * x, axis=-1, keepdims=True) + EPS)
    xh = x * r
    dg = jnp.sum(dy * xh, axis=0, keepdims=True)
    dxh = dy * g
    dx = r * (dxh - xh * jnp.mean(dxh * xh, axis=-1, keepdims=True))
    return dx, dg


def _sigmoid(x):
    return 0.5 * jnp.tanh(0.5 * x) + 0.5


_GELU_C = math.sqrt(2.0 / math.pi)


def _gelu(x):
    t = jnp.tanh(_GELU_C * (x + 0.044715 * x * x * x))
    return 0.5 * x * (1.0 + t)


def _gelu_grad(x):
    t = jnp.tanh(_GELU_C * (x + 0.044715 * x * x * x))
    return 0.5 * (1.0 + t) + 0.5 * x * (1.0 - t * t) * _GELU_C * (1.0 + 3.0 * 0.044715 * x * x)


def _tile_rows(t):
    return 512 if t >= 2048 else t // 2


def _dev_index(p):
    return 4 * p[0] + 2 * p[1] + p[2]


def _place():
    x, y, c = lax.axis_index("x"), lax.axis_index("y"), lax.axis_index("c")
    return (x, y, c), (x, y, 1 - c), [(1 - x, y), (x, 1 - y), (1 - x, 1 - y)]


def _dma_sems(n):
    return [pltpu.SemaphoreType.DMA((7 * n,)), pltpu.SemaphoreType.DMA((7 * n,)), pltpu.SemaphoreType.DMA((n,))]


class _Gather:
    has_middle = True

    def __init__(self, arrays):
        self.inputs = list(arrays)
        self.n = len(arrays)
        self.out_shape = [jax.ShapeDtypeStruct((N_DEV,) + a.shape, a.dtype) for a in arrays]
        self.scratch = _dma_sems(self.n)

    @staticmethod
    def _copy(outs, sems, k, s, block, to, src=None):
        rows = outs[k].at[_dev_index(block)]
        return pltpu.make_async_remote_copy(
            src_ref=rows if src is None else src, dst_ref=rows,
            send_sem=sems[0].at[7 * k + s], recv_sem=sems[1].at[7 * k + s], device_id=to, device_id_type=MESH)

    def _first(self, srcs, outs, sems):
        me, sibling, chips = _place()
        mine = [pltpu.make_async_copy(srcs[k], outs[k].at[_dev_index(me)], sems[2].at[k]) for k in range(self.n)]
        sends = []
        for k in range(self.n):
            sends.append(self._copy(outs, sems, k, 0, me, sibling, src=srcs[k]))
            sends += [self._copy(outs, sems, k, 1 + j, me, (*chip, me[2]), src=srcs[k])
                      for j, chip in enumerate(chips)]
        return mine, sends

    def _passed(self, outs, sems):
        me, sibling, chips = _place()
        return [[self._copy(outs, sems, k, 4 + j, (*chip, me[2]), sibling) for k in range(self.n)]
                for j, chip in enumerate(chips)]

    def start(self, srcs, outs, sems):
        mine, sends = self._first(srcs, outs, sems)
        for cp in mine + sends:
            cp.start()

    def middle(self, srcs, outs, sems):
        me, sibling, chips = _place()
        passed = self._passed(outs, sems)
        for j, chip in enumerate(chips):
            for k in range(self.n):
                self._copy(outs, sems, k, 1 + j, (*chip, me[2]), me).wait_recv()
                passed[j][k].start()

    def finish(self, srcs, outs, sems):
        me, sibling, chips = _place()
        for k in range(self.n):
            self._copy(outs, sems, k, 0, sibling, me).wait_recv()
        for j, chip in enumerate(chips):
            for k in range(self.n):
                self._copy(outs, sems, k, 4 + j, (*chip, 1 - me[2]), me).wait_recv()
        mine, sends = self._first(srcs, outs, sems)
        for cp in sends + [cp for row in self._passed(outs, sems) for cp in row]:
            cp.wait_send()
        for cp in mine:
            cp.wait()


class _Exchange:
    has_middle = False

    def __init__(self, arrays):
        self.inputs = list(arrays)
        self.n = len(arrays)
        self.out_shape = [jax.ShapeDtypeStruct(a.shape, a.dtype) for a in arrays]
        self.scratch = _dma_sems(self.n)

    def _copies(self, srcs, outs, sems):
        (x, y, c), _, _ = _place()
        me = _dev_index((x, y, c))
        local = [pltpu.make_async_copy(srcs[k].at[me], outs[k].at[me], sems[2].at[k]) for k in range(self.n)]
        remote = []
        for s in range(1, N_DEV):
            peer = (1 - x if s & 4 else x, 1 - y if s & 2 else y, 1 - c if s & 1 else c)
            for k in range(self.n):
                remote.append(pltpu.make_async_remote_copy(
                    src_ref=srcs[k].at[_dev_index(peer)], dst_ref=outs[k].at[me],
                    send_sem=sems[0].at[7 * k + s - 1], recv_sem=sems[1].at[7 * k + s - 1],
                    device_id=peer, device_id_type=MESH))
        return local, remote

    def start(self, srcs, outs, sems):
        local, remote = self._copies(srcs, outs, sems)
        for cp in local + remote:
            cp.start()

    def finish(self, srcs, outs, sems):
        local, remote = self._copies(srcs, outs, sems)
        for cp in remote:
            cp.wait_recv()
        for cp in remote:
            cp.wait_send()
        for cp in local:
            cp.wait()


def _call(body, comm, *, name, grid, in_specs, out_specs, out_shape, scratch_shapes=(), args):
    in_specs, out_specs, out_shape = list(in_specs), list(out_specs), list(out_shape)
    scratch_shapes = list(scratch_shapes)
    if comm is None:
        outs = pl.pallas_call(body, name=name, grid=grid, in_specs=in_specs, out_specs=out_specs, out_shape=out_shape,
                              scratch_shapes=scratch_shapes, compiler_params=_params())(*args)
        return list(outs), []
    n_in, n_out, n_scr, c_n = len(in_specs), len(out_specs), len(scratch_shapes), comm.n
    middle = tuple(g // 2 if d == 0 else 0 for d, g in enumerate(grid))

    def hosted(*refs):
        pos = 0
        parts = []
        for width in (n_in, c_n, n_out, c_n, n_scr, 3):
            parts.append(refs[pos:pos + width])
            pos += width
        ins, c_in, outs, c_out, scr, sems = parts
        ids = [pl.program_id(d) for d in range(len(grid))]
        at = lambda where: functools.reduce(operator.and_, [i == w for i, w in zip(ids, where)])

        @pl.when(at([0] * len(grid)))
        def _():
            comm.start(c_in, c_out, sems)

        body(*ins, *outs, *scr)

        if comm.has_middle:
            @pl.when(at(middle))
            def _():
                comm.middle(c_in, c_out, sems)

        @pl.when(at([g - 1 for g in grid]))
        def _():
            comm.finish(c_in, c_out, sems)

    outs = pl.pallas_call(
        hosted, name=name, grid=grid,
        in_specs=in_specs + [ANY] * c_n, out_specs=out_specs + [ANY] * c_n,
        out_shape=out_shape + comm.out_shape, scratch_shapes=scratch_shapes + comm.scratch,
        compiler_params=_params())(*args, *comm.inputs)
    return list(outs[:n_out]), list(outs[n_out:])


def _first_gather(shards, pads, dtypes, n_gathered):
    n = len(shards)
    gather = _Gather([jax.ShapeDtypeStruct(tuple(pads[k]), dtypes[k]) for k in range(n_gathered)])

    def body(*refs):
        ins, outs = refs[:n], refs[n:2 * n]
        stages = list(refs[2 * n:2 * n + n_gathered]) + list(outs[n_gathered:])
        sems = refs[2 * n + n_gathered:]
        for k in range(n):
            r, cc = ins[k].shape
            if (r, cc) != tuple(stages[k].shape):
                stages[k][...] = jnp.zeros(stages[k].shape, stages[k].dtype)
            stages[k][:r, :cc] = ins[k][...].astype(stages[k].dtype)
        gather.start(stages[:n_gathered], outs[:n_gathered], sems)
        gather.middle(stages[:n_gathered], outs[:n_gathered], sems)
        gather.finish(stages[:n_gathered], outs[:n_gathered], sems)

    staged_shape = [jax.ShapeDtypeStruct(tuple(pads[k]), dtypes[k]) for k in range(n_gathered, n)]
    outs = pl.pallas_call(
        body, name="first_gather",
        out_shape=gather.out_shape + staged_shape,
        in_specs=[WHOLE] * n,
        out_specs=[ANY] * n_gathered + [WHOLE] * (n - n_gathered),
        scratch_shapes=[pltpu.VMEM(tuple(pads[k]), dtypes[k]) for k in range(n_gathered)] + gather.scratch,
        compiler_params=_params(),
    )(*shards)
    return list(outs[:n_gathered]), list(outs[n_gathered:])


def _last_exchange(arrays):
    comm = _Exchange(arrays)

    def body(*refs):
        srcs, outs, sems = refs[:comm.n], refs[comm.n:2 * comm.n], refs[2 * comm.n:]
        comm.start(srcs, outs, sems)
        comm.finish(srcs, outs, sems)

    return pl.pallas_call(
        body, name="last_exchange", out_shape=comm.out_shape, in_specs=[ANY] * comm.n, out_specs=[ANY] * comm.n,
        scratch_shapes=comm.scratch, compiler_params=_params())(*arrays)


def _ffn_fwd(h0, g_pre, g_post, wg, wu, wd, comm, name):
    t = h0.shape[0]
    tt = _tile_rows(t)
    n_chunks = FF_VIRT // FF_CHUNK

    def body(h0_ref, gpre_ref, gpost_ref, wg_ref, wu_ref, wd_ref,
             h1_ref, f_ref, n1_ref, gt_ref, up_ref, acc_ref, n1s_ref):
        j = pl.program_id(1)

        @pl.when(j == 0)
        def _():
            n1 = _rms(h0_ref[...], gpre_ref[...]).astype(BF16)
            n1s_ref[...] = n1
            n1_ref[...] = n1
            acc_ref[...] = jnp.zeros_like(acc_ref)

        n1 = n1s_ref[...]
        gt = _dot_nt(n1, wg_ref[...])
        up = _dot_nt(n1, wu_ref[...])
        gt_ref[...] = gt.astype(BF16)
        up_ref[...] = up.astype(BF16)
        act = (gt * _sigmoid(gt) * up).astype(BF16)
        acc_ref[...] += _dot(act, wd_ref[...])

        @pl.when(j == n_chunks - 1)
        def _():
            f = acc_ref[...]
            f_ref[...] = f
            h1_ref[...] = h0_ref[...] + 0.5 * _rms(f, gpost_ref[...])

    row = lambda i, j: (i, 0)
    wspec = pl.BlockSpec((FF_CHUNK, D_MODEL), lambda i, j: (j, 0))
    return _call(
        body, comm, name=name, grid=(t // tt, n_chunks),
        in_specs=[pl.BlockSpec((tt, D_MODEL), row), _full((1, D_MODEL)), _full((1, D_MODEL)), wspec, wspec, wspec],
        out_specs=[pl.BlockSpec((tt, D_MODEL), row), pl.BlockSpec((tt, D_MODEL), row),
                   pl.BlockSpec((tt, D_MODEL), row),
                   pl.BlockSpec((tt, FF_CHUNK), lambda i, j: (i, j)),
                   pl.BlockSpec((tt, FF_CHUNK), lambda i, j: (i, j))],
        out_shape=[jax.ShapeDtypeStruct((t, D_MODEL), F32), jax.ShapeDtypeStruct((t, D_MODEL), F32),
                   jax.ShapeDtypeStruct((t, D_MODEL), BF16),
                   jax.ShapeDtypeStruct((t, FF_VIRT), BF16), jax.ShapeDtypeStruct((t, FF_VIRT), BF16)],
        scratch_shapes=[pltpu.VMEM((tt, D_MODEL), F32), pltpu.VMEM((tt, D_MODEL), BF16)],
        args=(h0, g_pre, g_post, wg, wu, wd))


def _ffn_bwd_hidden(dh1, f, gt, up, g_post, wd, comm, name):
    t = f.shape[0]
    tt = _tile_rows(t)
    n_chunks = FF_VIRT // FF_CHUNK

    def body(dh1_ref, f_ref, gt_ref, up_ref, gpost_ref, wd_ref,
             df_ref, act_ref, dgt_ref, dup_ref, dgpost_ref, dfs_ref):
        i, j = pl.program_id(0), pl.program_id(1)

        @pl.when((i == 0) & (j == 0))
        def _():
            dgpost_ref[...] = jnp.zeros_like(dgpost_ref)

        @pl.when(j == 0)
        def _():
            df, dg = _rms_bwd(f_ref[...], gpost_ref[...], 0.5 * dh1_ref[...])
            dgpost_ref[...] += dg
            dfb = df.astype(BF16)
            dfs_ref[...] = dfb
            df_ref[...] = dfb

        da = _dot_nt(dfs_ref[...], wd_ref[...])
        g = gt_ref[...].astype(F32)
        u = up_ref[...].astype(F32)
        s = _sigmoid(g)
        sl = g * s
        act_ref[...] = (sl * u).astype(BF16)
        dgt_ref[...] = (da * u * (s * (1.0 + g * (1.0 - s)))).astype(BF16)
        dup_ref[...] = (da * sl).astype(BF16)

    row = lambda i, j: (i, 0)
    chunk = pl.BlockSpec((tt, FF_CHUNK), lambda i, j: (i, j))
    return _call(
        body, comm, name=name, grid=(t // tt, n_chunks),
        in_specs=[pl.BlockSpec((tt, D_MODEL), row), pl.BlockSpec((tt, D_MODEL), row), chunk, chunk,
                  _full((1, D_MODEL)), pl.BlockSpec((FF_CHUNK, D_MODEL), lambda i, j: (j, 0))],
        out_specs=[pl.BlockSpec((tt, D_MODEL), row), chunk, chunk, chunk, _full((1, D_MODEL))],
        out_shape=[jax.ShapeDtypeStruct((t, D_MODEL), BF16)] + [jax.ShapeDtypeStruct((t, FF_VIRT), BF16)] * 3 + [
            jax.ShapeDtypeStruct((1, D_MODEL), F32)],
        scratch_shapes=[pltpu.VMEM((tt, D_MODEL), BF16)],
        args=(dh1, f, gt, up, g_post, wd))


def _ffn_bwd_input(dh1, h0, dgt, dup, g_pre, wg, wu, comm, name):
    t = h0.shape[0]
    tt = _tile_rows(t)
    n_chunks = FF_VIRT // FF_CHUNK

    def body(dh1_ref, h0_ref, dgt_ref, dup_ref, gpre_ref, wg_ref, wu_ref, dh0_ref, dgpre_ref, acc_ref):
        i, j = pl.program_id(0), pl.program_id(1)

        @pl.when((i == 0) & (j == 0))
        def _():
            dgpre_ref[...] = jnp.zeros_like(dgpre_ref)

        @pl.when(j == 0)
        def _():
            acc_ref[...] = jnp.zeros_like(acc_ref)

        acc_ref[...] += _dot(dgt_ref[...], wg_ref[...]) + _dot(dup_ref[...], wu_ref[...])

        @pl.when(j == n_chunks - 1)
        def _():
            dx, dg = _rms_bwd(h0_ref[...], gpre_ref[...], acc_ref[...])
            dgpre_ref[...] += dg
            dh0_ref[...] = dh1_ref[...] + dx

    row = lambda i, j: (i, 0)
    chunk = pl.BlockSpec((tt, FF_CHUNK), lambda i, j: (i, j))
    wspec = pl.BlockSpec((FF_CHUNK, D_MODEL), lambda i, j: (j, 0))
    return _call(
        body, comm, name=name, grid=(t // tt, n_chunks),
        in_specs=[pl.BlockSpec((tt, D_MODEL), row), pl.BlockSpec((tt, D_MODEL), row), chunk, chunk,
                  _full((1, D_MODEL)), wspec, wspec],
        out_specs=[pl.BlockSpec((tt, D_MODEL), row), _full((1, D_MODEL))],
        out_shape=[jax.ShapeDtypeStruct((t, D_MODEL), F32), jax.ShapeDtypeStruct((1, D_MODEL), F32)],
        scratch_shapes=[pltpu.VMEM((tt, D_MODEL), F32)],
        args=(dh1, h0, dgt, dup, g_pre, wg, wu))


def _mm_tn(a, b, m, tm, tn, name, comm=None):
    t, n = b.shape
    tk = min(t, 2048)
    nk = t // tk

    def body(a_ref, b_ref, out_ref, acc_ref):
        k = pl.program_id(2)

        @pl.when(k == 0)
        def _():
            acc_ref[...] = jnp.zeros_like(acc_ref)

        acc_ref[...] += _dot_tn(a_ref[...], b_ref[...])

        @pl.when(k == nk - 1)
        def _():
            out_ref[...] = acc_ref[...].astype(out_ref.dtype)

    outs, got = _call(
        body, comm, name=name, grid=(m // tm, n // tn, nk),
        in_specs=[pl.BlockSpec((tk, tm), lambda i, j, k: (k, i)), pl.BlockSpec((tk, tn), lambda i, j, k: (k, j))],
        out_specs=[pl.BlockSpec((tm, tn), lambda i, j, k: (i, j))],
        out_shape=[jax.ShapeDtypeStruct((m, n), BF16)],
        scratch_shapes=[pltpu.VMEM((tm, tn), F32)], args=(a, b))
    return outs[0], got


_SPLITS = (0, LRU_W, 2 * LRU_W, 2 * LRU_W + Q_LORA, 2 * LRU_W + Q_LORA + KV_LORA, IN_PAD)
_WIDTHS = tuple(_SPLITS[k + 1] - _SPLITS[k] for k in range(5))


def _mixer_in(h1, g_pre, w_in):
    t = h1.shape[0]
    tt = _tile_rows(t)

    def body(h_ref, g_ref, w_ref, n_ref, *outs):
        n = _rms(h_ref[...], g_ref[...]).astype(BF16)
        n_ref[...] = n
        proj = _dot_nt(n, w_ref[...])
        for k in range(5):
            outs[k][...] = proj[:, _SPLITS[k]:_SPLITS[k + 1]]

    row = lambda i: (i, 0)
    return pl.pallas_call(
        body, name="mixer_in",
        grid=(t // tt,),
        in_specs=[pl.BlockSpec((tt, D_MODEL), row), _full((1, D_MODEL)), _full((IN_PAD, D_MODEL))],
        out_specs=[pl.BlockSpec((tt, D_MODEL), row)] + [pl.BlockSpec((tt, w), row) for w in _WIDTHS],
        out_shape=[jax.ShapeDtypeStruct((t, D_MODEL), BF16)] + [jax.ShapeDtypeStruct((t, w), F32) for w in _WIDTHS],
        compiler_params=_params(),
    )(h1, g_pre, w_in)


def _mixer_in_bwd(dh2, h1, g_pre, w_in, dparts):
    t = h1.shape[0]
    tt = _tile_rows(t)

    def body(dh2_ref, h_ref, g_ref, w_ref, p0, p1, p2, p3, p4, dh1_ref, dproj_ref, dg_ref):
        i = pl.program_id(0)

        @pl.when(i == 0)
        def _():
            dg_ref[...] = jnp.zeros_like(dg_ref)

        dproj = jnp.concatenate([p[...] for p in (p0, p1, p2, p3, p4)], axis=1)
        dproj_ref[...] = dproj
        dx, dg = _rms_bwd(h_ref[...], g_ref[...], _dot(dproj, w_ref[...]))
        dg_ref[...] += dg
        dh1_ref[...] = dh2_ref[...] + dx

    row = lambda i: (i, 0)
    return pl.pallas_call(
        body, name="mixer_in_bwd",
        grid=(t // tt,),
        in_specs=[pl.BlockSpec((tt, D_MODEL), row), pl.BlockSpec((tt, D_MODEL), row), _full((1, D_MODEL)),
                  _full((IN_PAD, D_MODEL))] + [pl.BlockSpec((tt, w), row) for w in _WIDTHS],
        out_specs=[pl.BlockSpec((tt, D_MODEL), row), pl.BlockSpec((tt, IN_PAD), row), _full((1, D_MODEL))],
        out_shape=[jax.ShapeDtypeStruct((t, D_MODEL), F32), jax.ShapeDtypeStruct((t, IN_PAD), BF16),
                   jax.ShapeDtypeStruct((1, D_MODEL), F32)],
        compiler_params=_params(),
    )(dh2, h1, g_pre, w_in, *dparts)


def _conv(x_prev8, x_tile, w_ref, b_ref):
    tt = x_tile.shape[0]
    xx = jnp.concatenate([x_prev8, x_tile], axis=0)
    y = b_ref[...] + w_ref[CONV_W - 1:CONV_W, :] * x_tile
    for k in range(CONV_W - 1):
        y = y + w_ref[k:k + 1, :] * pltpu.roll(xx, CONV_W - 1 - k, 0)[8:8 + tt]
    return y


def _neg_expm1(z):
    series = -z * (1.0 + z * 0.5 * (1.0 + z / 3.0 * (1.0 + z * 0.25 * (1.0 + z * 0.2 * (1.0 + z / 6.0)))))
    return jnp.where(z > -0.3, series, 1.0 - jnp.exp(z))


def _lru_elementwise(xc, pre_a, pre_i, lam):
    ra = _sigmoid(pre_a)
    ri = _sigmoid(pre_i)
    neg = -lam
    softplus = jnp.maximum(neg, 0.0) + jnp.log(1.0 + jnp.exp(-jnp.abs(neg)))
    log_a = -LRU_C * ra * softplus
    a = jnp.exp(log_a)
    mult = jnp.sqrt(_neg_expm1(2.0 * log_a))
    return a, mult * (ri * xc)


def _lru_gates(xl, conv_w, conv_b, wa, wx, ba, bx, lam):
    t = xl.shape[0]
    tt = _tile_rows(t)
    r8 = tt // 8

    def body(prev_ref, x_ref, cw_ref, cb_ref, wa_ref, wx_ref, ba_ref, bx_ref, lam_ref, xc_ref, a_ref, u_ref):
        i = pl.program_id(0)
        prev = jnp.where(i == 0, 0.0, prev_ref[...])
        xc = _conv(prev, x_ref[...], cw_ref, cb_ref)
        xc_ref[...] = xc
        xb = xc.astype(BF16)
        a, u = _lru_elementwise(xc, _dot(xb, wa_ref[...]) + ba_ref[...], _dot(xb, wx_ref[...]) + bx_ref[...],
                                lam_ref[...])
        a_ref[...] = a
        u_ref[...] = u

    row = lambda i: (i, 0)
    vec = _full((1, LRU_W))
    return pl.pallas_call(
        body, name="lru_gates",
        grid=(t // tt,),
        in_specs=[pl.BlockSpec((8, LRU_W), lambda i: (jnp.maximum(i * r8 - 1, 0), 0)),
                  pl.BlockSpec((tt, LRU_W), row), _full((8, LRU_W)), vec,
                  _full((LRU_W, LRU_W)), _full((LRU_W, LRU_W)), vec, vec, vec],
        out_specs=[pl.BlockSpec((tt, LRU_W), row)] * 3,
        out_shape=[jax.ShapeDtypeStruct((t, LRU_W), F32)] * 3,
        compiler_params=_params(),
    )(xl, xl, conv_w, conv_b, wa, wx, ba, bx, lam)


def _lru_scan(a, u, reverse):
    t = a.shape[0]
    nblk = t // 8

    def body(a_ref, u_ref, h_ref):
        def fwd(blk, h):
            base = pl.multiple_of(blk * 8, 8)
            for k in range(8):
                h = a_ref[pl.ds(base + k, 1), :] * h + u_ref[pl.ds(base + k, 1), :]
                h_ref[pl.ds(base + k, 1), :] = h
            return h

        def bwd(n, carry):
            base = pl.multiple_of((nblk - 1 - n) * 8, 8)
            for k in range(7, -1, -1):
                g = u_ref[pl.ds(base + k, 1), :] + carry
                h_ref[pl.ds(base + k, 1), :] = g
                carry = a_ref[pl.ds(base + k, 1), :] * g
            return carry

        lax.fori_loop(0, nblk, bwd if reverse else fwd, jnp.zeros((1, LRU_W), F32))

    return pl.pallas_call(
        body, name="lru_scan_rev" if reverse else "lru_scan",
        in_specs=[WHOLE] * 2, out_specs=WHOLE,
        out_shape=jax.ShapeDtypeStruct((t, LRU_W), F32),
        compiler_params=_params(),
    )(a, u)


def _lru_gates_bwd(dh, h, xc, wa, wx, ba, bx, lam):
    t = dh.shape[0]
    tt = _tile_rows(t)
    r8 = tt // 8

    def body(dh_ref, hprev_ref, h_ref, xc_ref, wa_ref, wx_ref, ba_ref, bx_ref, lam_ref,
             dxc_ref, dwa_ref, dwx_ref, dba_ref, dbx_ref, dlam_ref):
        i = pl.program_id(0)

        @pl.when(i == 0)
        def _():
            for r in (dwa_ref, dwx_ref, dba_ref, dbx_ref, dlam_ref):
                r[...] = jnp.zeros_like(r)

        prev = jnp.where(i == 0, 0.0, hprev_ref[...])
        h_before = pltpu.roll(jnp.concatenate([prev, h_ref[...]], axis=0), 1, 0)[8:8 + tt]
        dh_t = dh_ref[...]
        xc = xc_ref[...]
        xb = xc.astype(BF16)
        pre_a = _dot(xb, wa_ref[...]) + ba_ref[...]
        pre_i = _dot(xb, wx_ref[...]) + bx_ref[...]
        _, vjp = jax.vjp(_lru_elementwise, xc, pre_a, pre_i, lam_ref[...])
        dxc, dpre_a, dpre_i, dlam = vjp((dh_t * h_before, dh_t))
        da_b = dpre_a.astype(BF16)
        di_b = dpre_i.astype(BF16)
        dxc_ref[...] = dxc + _dot_nt(da_b, wa_ref[...]) + _dot_nt(di_b, wx_ref[...])
        dwa_ref[...] += _dot_tn(xb, da_b)
        dwx_ref[...] += _dot_tn(xb, di_b)
        dba_ref[...] += jnp.sum(dpre_a, axis=0, keepdims=True)
        dbx_ref[...] += jnp.sum(dpre_i, axis=0, keepdims=True)
        dlam_ref[...] += dlam

    row = lambda i: (i, 0)
    vec = _full((1, LRU_W))
    sq = _full((LRU_W, LRU_W))
    return pl.pallas_call(
        body, name="lru_gates_bwd",
        grid=(t // tt,),
        in_specs=[pl.BlockSpec((tt, LRU_W), row),
                  pl.BlockSpec((8, LRU_W), lambda i: (jnp.maximum(i * r8 - 1, 0), 0)),
                  pl.BlockSpec((tt, LRU_W), row), pl.BlockSpec((tt, LRU_W), row), sq, sq, vec, vec, vec],
        out_specs=[pl.BlockSpec((tt, LRU_W), row), sq, sq, vec, vec, vec],
        out_shape=[jax.ShapeDtypeStruct((t, LRU_W), F32), jax.ShapeDtypeStruct((LRU_W, LRU_W), F32),
                   jax.ShapeDtypeStruct((LRU_W, LRU_W), F32)] + [jax.ShapeDtypeStruct((1, LRU_W), F32)] * 3,
        compiler_params=_params(),
    )(dh, h, h, xc, wa, wx, ba, bx, lam)


def _conv_bwd(dxc, xl, conv_w):
    t = dxc.shape[0]
    tt = _tile_rows(t)
    r8 = tt // 8
    n_tiles = t // tt
    last = t // 8 - 1

    def body(d_ref, dnext_ref, xprev_ref, x_ref, cw_ref, dxl_ref, dw_ref, db_ref):
        i = pl.program_id(0)

        @pl.when(i == 0)
        def _():
            dw_ref[...] = jnp.zeros_like(dw_ref)
            db_ref[...] = jnp.zeros_like(db_ref)

        d = d_ref[...]
        nxt = jnp.where(i == n_tiles - 1, 0.0, dnext_ref[...])
        dd = jnp.concatenate([d, nxt], axis=0)
        dx = cw_ref[CONV_W - 1:CONV_W, :] * d
        for k in range(CONV_W - 1):
            shift = CONV_W - 1 - k
            dx = dx + cw_ref[k:k + 1, :] * pltpu.roll(dd, tt + 8 - shift, 0)[:tt]
        dxl_ref[...] = dx.astype(BF16)

        prev = jnp.where(i == 0, 0.0, xprev_ref[...])
        xx = jnp.concatenate([prev, x_ref[...]], axis=0)
        rows = []
        for k in range(CONV_W):
            shifted = x_ref[...] if k == CONV_W - 1 else pltpu.roll(xx, CONV_W - 1 - k, 0)[8:8 + tt]
            rows.append(jnp.sum(d * shifted, axis=0, keepdims=True))
        rows.append(jnp.zeros((8 - CONV_W, LRU_W), F32))
        dw_ref[...] += jnp.concatenate(rows, axis=0)
        db_ref[...] += jnp.sum(d, axis=0, keepdims=True)

    row = lambda i: (i, 0)
    return pl.pallas_call(
        body, name="conv_bwd",
        grid=(n_tiles,),
        in_specs=[pl.BlockSpec((tt, LRU_W), row),
                  pl.BlockSpec((8, LRU_W), lambda i: (jnp.minimum((i + 1) * r8, last), 0)),
                  pl.BlockSpec((8, LRU_W), lambda i: (jnp.maximum(i * r8 - 1, 0), 0)),
                  pl.BlockSpec((tt, LRU_W), row), _full((8, LRU_W))],
        out_specs=[pl.BlockSpec((tt, LRU_W), row), _full((8, LRU_W)), _full((1, LRU_W))],
        out_shape=[jax.ShapeDtypeStruct((t, LRU_W), BF16), jax.ShapeDtypeStruct((8, LRU_W), F32),
                   jax.ShapeDtypeStruct((1, LRU_W), F32)],
        compiler_params=_params(),
    )(dxc, dxc, xl, xl, conv_w)


def _rope(x, c, s1, s2):
    n = x.shape[-1]
    return x * c + pltpu.roll(x, n - 16, 1) * s1 + pltpu.roll(x, 16, 1) * s2


def _rope_t(d, c, s1, s2):
    n = d.shape[-1]
    return d * c + pltpu.roll(d * s1, 16, 1) + pltpu.roll(d * s2, n - 16, 1)


def _mla_proj(qlat, kvlat, krope, tables, gq, gkv, wq, wk, wv):
    t = qlat.shape[0]
    tt = _tile_rows(t)

    def body(q_ref, kv_ref, kr_ref, cq, s1q, s2q, ck, s1k, s2k, gq_ref, gkv_ref, wq_ref, wk_ref, wv_ref,
             qo_ref, ko_ref, vo_ref, qn_ref, kva_ref, vt_ref):
        qn = _rms(q_ref[...], gq_ref[...]).astype(BF16)
        kvn = _rms(kv_ref[...], gkv_ref[...]).astype(BF16)
        kr = _rope(kr_ref[...], ck[...], s1k[...], s2k[...]).astype(BF16)
        kva = jnp.concatenate([kvn, kr], axis=1)
        qn_ref[...] = qn
        kva_ref[...] = kva
        q = _dot_nt(qn, wq_ref[...])
        tile = lambda r: jnp.tile(r[...], (1, N_HEADS))
        qo_ref[...] = _rope(q, tile(cq), tile(s1q), tile(s2q)).astype(BF16)
        ko_ref[...] = _dot(kva, wk_ref[...]).astype(BF16)
        v = _dot(kvn, wv_ref[...])
        vo_ref[...] = v.astype(BF16)
        vt_ref[...] = jnp.transpose(v).astype(BF16)

    row = lambda i: (i, 0)
    tab = pl.BlockSpec((tt, HEAD_PAD), row)
    return pl.pallas_call(
        body, name="mla_proj",
        grid=(t // tt,),
        in_specs=[pl.BlockSpec((tt, Q_LORA), row), pl.BlockSpec((tt, KV_LORA), row), tab] + [tab] * 6 + [
            _full((1, Q_LORA)), _full((1, KV_LORA)), _full((HEADS_W, Q_LORA)), _full((KV_LORA + HEAD_PAD, HEADS_W)),
            _full((KV_LORA, HEADS_W))],
        out_specs=[pl.BlockSpec((tt, HEADS_W), row)] * 3 + [pl.BlockSpec((tt, Q_LORA), row),
                                                             pl.BlockSpec((tt, KV_LORA + HEAD_PAD), row),
                                                             pl.BlockSpec((HEADS_W, tt), lambda i: (0, i))],
        out_shape=[jax.ShapeDtypeStruct((t, HEADS_W), BF16)] * 3 + [
            jax.ShapeDtypeStruct((t, Q_LORA), BF16), jax.ShapeDtypeStruct((t, KV_LORA + HEAD_PAD), BF16),
            jax.ShapeDtypeStruct((HEADS_W, t), BF16)],
        compiler_params=_params(),
    )(qlat, kvlat, krope, *tables, gq, gkv, wq, wk, wv)


def _mla_proj_bwd(dq, dk, dv, qlat, kvlat, tables, gq, gkv, wq, wk, wv):
    t = qlat.shape[0]
    tt = _tile_rows(t)

    def body(dq_ref, dk_ref, dv_ref, q_ref, kv_ref, cq, s1q, s2q, ck, s1k, s2k, gq_ref, gkv_ref,
             wq_ref, wk_ref, wv_ref, dqr_ref, dql_ref, dkvl_ref, dkr_ref, dgq_ref, dgkv_ref):
        i = pl.program_id(0)

        @pl.when(i == 0)
        def _():
            dgq_ref[...] = jnp.zeros_like(dgq_ref)
            dgkv_ref[...] = jnp.zeros_like(dgkv_ref)

        tile = lambda r: jnp.tile(r[...], (1, N_HEADS))
        dqr = _rope_t(dq_ref[...], tile(cq), tile(s1q), tile(s2q)).astype(BF16)
        dqr_ref[...] = dqr
        dql, dg = _rms_bwd(q_ref[...], gq_ref[...], _dot(dqr, wq_ref[...]))
        dgq_ref[...] += dg
        dql_ref[...] = dql.astype(BF16)
        dkva = _dot_nt(dk_ref[...], wk_ref[...])
        dkvn = dkva[:, :KV_LORA] + _dot_nt(dv_ref[...], wv_ref[...])
        dkvl, dg = _rms_bwd(kv_ref[...], gkv_ref[...], dkvn)
        dgkv_ref[...] += dg
        dkvl_ref[...] = dkvl.astype(BF16)
        dkr_ref[...] = _rope_t(dkva[:, KV_LORA:], ck[...], s1k[...], s2k[...]).astype(BF16)

    row = lambda i: (i, 0)
    tab = pl.BlockSpec((tt, HEAD_PAD), row)
    wide = pl.BlockSpec((tt, HEADS_W), row)
    return pl.pallas_call(
        body, name="mla_proj_bwd",
        grid=(t // tt,),
        in_specs=[wide, wide, wide, pl.BlockSpec((tt, Q_LORA), row), pl.BlockSpec((tt, KV_LORA), row)] + [tab] * 6 + [
            _full((1, Q_LORA)), _full((1, KV_LORA)), _full((HEADS_W, Q_LORA)), _full((KV_LORA + HEAD_PAD, HEADS_W)),
            _full((KV_LORA, HEADS_W))],
        out_specs=[wide, pl.BlockSpec((tt, Q_LORA), row), pl.BlockSpec((tt, KV_LORA), row), tab,
                   _full((1, Q_LORA)), _full((1, KV_LORA))],
        out_shape=[jax.ShapeDtypeStruct((t, HEADS_W), BF16), jax.ShapeDtypeStruct((t, Q_LORA), BF16),
                   jax.ShapeDtypeStruct((t, KV_LORA), BF16), jax.ShapeDtypeStruct((t, HEAD_PAD), BF16),
                   jax.ShapeDtypeStruct((1, Q_LORA), F32), jax.ShapeDtypeStruct((1, KV_LORA), F32)],
        compiler_params=_params(),
    )(dq, dk, dv, qlat, kvlat, *tables, gq, gkv, wq, wk, wv)


NEG = -1e30


def _chunk_mask(rows, cols, row0):
    r = (lax.broadcasted_iota(jnp.int32, (rows, cols), 0) + row0) // CHUNK
    c = lax.broadcasted_iota(jnp.int32, (rows, cols), 1) // CHUNK
    return c <= r


def _chunk_mask_t(n):
    kc = lax.broadcasted_iota(jnp.int32, (n, n), 0) // CHUNK
    qc = lax.broadcasted_iota(jnp.int32, (n, n), 1) // CHUNK
    return kc <= qc


def _attn_fwd(q, k, vt, comm):
    t = q.shape[0]
    bq = _tile_rows(t)
    nq = t // bq

    def body(q_ref, k_ref, vt_ref, o_ref, lse_ref, m_ref, l_ref, acc_ref):
        i = pl.program_id(1)
        m_ref[...] = jnp.full_like(m_ref, NEG)
        l_ref[...] = jnp.zeros_like(l_ref)
        acc_ref[...] = jnp.zeros_like(acc_ref)
        qb = q_ref[...]

        def block(j, diagonal):
            cols = pl.ds(pl.multiple_of(j * bq, bq), bq)
            s = _dot_nt(k_ref[cols, :], qb)
            if diagonal:
                s = jnp.where(_chunk_mask_t(bq), s, NEG)
            m_old = m_ref[...]
            m_new = jnp.maximum(m_old, jnp.max(s, axis=0, keepdims=True))
            alpha = jnp.exp(m_old - m_new)
            p = jnp.exp(s - m_new)
            l_ref[...] = alpha * l_ref[...] + jnp.sum(p, axis=0, keepdims=True)
            acc_ref[...] = alpha * acc_ref[...] + _dot(vt_ref[:, cols], p.astype(BF16))
            m_ref[...] = m_new

        def below_diagonal(j, carry):
            block(j, False)
            return carry

        lax.fori_loop(0, i, below_diagonal, 0)
        block(i, True)
        l = l_ref[...]
        o_ref[...] = jnp.transpose(acc_ref[...] / l).astype(BF16)
        lse_ref[...] = m_ref[...] + jnp.log(l)

    qmap = lambda h, i: (i, h)
    return _call(
        body, comm, name="attn_fwd" if comm is None else "attn_fwd_gather", grid=(N_HEADS, nq),
        in_specs=[pl.BlockSpec((bq, HEAD_PAD), qmap), pl.BlockSpec((t, HEAD_PAD), lambda h, i: (0, h)),
                  pl.BlockSpec((HEAD_PAD, t), lambda h, i: (h, 0))],
        out_specs=[pl.BlockSpec((bq, HEAD_PAD), qmap), pl.BlockSpec((None, 1, bq), lambda h, i: (h, 0, i))],
        out_shape=[jax.ShapeDtypeStruct(q.shape, BF16), jax.ShapeDtypeStruct((N_HEADS, 1, t), F32)],
        scratch_shapes=[pltpu.VMEM((1, bq), F32), pltpu.VMEM((1, bq), F32), pltpu.VMEM((HEAD_PAD, bq), F32)],
        args=(q, k, vt))


def _attn_bwd(q, k, v, o, do, lse, comm):
    t = q.shape[0]
    bq = _tile_rows(t)
    nq = t // bq

    def body(q_ref, k_ref, v_ref, o_ref, do_ref, lse_ref, dq_ref, dk_ref, dv_ref, dk_acc, dv_acc, delta_ref):
        j = pl.program_id(1)

        @pl.when(j == 0)
        def _():
            dq_ref[...] = jnp.zeros_like(dq_ref)
            for blk in range(nq):
                rows = slice(blk * bq, (blk + 1) * bq)
                delta = jnp.sum(do_ref[rows, :].astype(F32) * o_ref[rows, :].astype(F32), axis=-1, keepdims=True)
                delta_ref[:, rows] = jnp.transpose(jnp.broadcast_to(delta, (bq, HEAD_PAD)))[:1, :]

        dk_acc[...] = jnp.zeros_like(dk_acc)
        dv_acc[...] = jnp.zeros_like(dv_acc)
        kb, vb = k_ref[...], v_ref[...]

        def block(i, diagonal):
            rows = pl.ds(pl.multiple_of(i * bq, bq), bq)
            qb, dob = q_ref[rows, :], do_ref[rows, :]
            s = _dot_nt(kb, qb)
            if diagonal:
                s = jnp.where(_chunk_mask_t(bq), s, NEG)
            p = jnp.exp(s - lse_ref[:, rows])
            dv_acc[...] += _dot(p.astype(BF16), dob)
            dp = _dot_nt(vb, dob)
            ds = (p * (dp - delta_ref[:, rows])).astype(BF16)
            dk_acc[...] += _dot(ds, qb)
            dq_ref[rows, :] += _dot_tn(ds, kb)

        def above_diagonal(i, carry):
            block(i, False)
            return carry

        block(j, True)
        lax.fori_loop(j + 1, nq, above_diagonal, 0)
        dk_ref[...] = dk_acc[...].astype(BF16)
        dv_ref[...] = dv_acc[...].astype(BF16)

    kmap = lambda h, j: (j, h)
    head = lambda h, j: (0, h)
    whole = pl.BlockSpec((t, HEAD_PAD), head)
    return _call(
        body, comm, name="attn_bwd" if comm is None else "attn_bwd_exchange", grid=(N_HEADS, nq),
        in_specs=[whole, pl.BlockSpec((bq, HEAD_PAD), kmap), pl.BlockSpec((bq, HEAD_PAD), kmap), whole, whole,
                  pl.BlockSpec((None, 1, t), lambda h, j: (h, 0, 0))],
        out_specs=[whole, pl.BlockSpec((bq, HEAD_PAD), kmap), pl.BlockSpec((bq, HEAD_PAD), kmap)],
        out_shape=[jax.ShapeDtypeStruct(q.shape, F32), jax.ShapeDtypeStruct(q.shape, BF16),
                   jax.ShapeDtypeStruct(q.shape, BF16)],
        scratch_shapes=[pltpu.VMEM((bq, HEAD_PAD), F32), pltpu.VMEM((bq, HEAD_PAD), F32), pltpu.VMEM((1, t), F32)],
        args=(q, k, v, o, do, lse))


def _mixer_out(h1, hl, gate, o, w_lru, w_mla, g_post):
    t = h1.shape[0]
    tt = _tile_rows(t)

    def body(h1_ref, hl_ref, gate_ref, o_ref, wl_ref, wm_ref, g_ref, h2_ref, m_ref, y_ref):
        y = (hl_ref[...] * _gelu(gate_ref[...])).astype(BF16)
        y_ref[...] = y
        m = _dot(y, wl_ref[...]) + _dot(o_ref[...], wm_ref[...])
        m_ref[...] = m
        h2_ref[...] = h1_ref[...] + _rms(m, g_ref[...])

    row = lambda i: (i, 0)
    return pl.pallas_call(
        body, name="mixer_out",
        grid=(t // tt,),
        in_specs=[pl.BlockSpec((tt, D_MODEL), row), pl.BlockSpec((tt, LRU_W), row), pl.BlockSpec((tt, LRU_W), row),
                  pl.BlockSpec((tt, HEADS_W), row), _full((LRU_W, D_MODEL)), _full((HEADS_W, D_MODEL)),
                  _full((1, D_MODEL))],
        out_specs=[pl.BlockSpec((tt, D_MODEL), row), pl.BlockSpec((tt, D_MODEL), row),
                   pl.BlockSpec((tt, LRU_W), row)],
        out_shape=[jax.ShapeDtypeStruct((t, D_MODEL), F32), jax.ShapeDtypeStruct((t, D_MODEL), F32),
                   jax.ShapeDtypeStruct((t, LRU_W), BF16)],
        compiler_params=_params(),
    )(h1, hl, gate, o, w_lru, w_mla, g_post)


def _mixer_out_bwd(dh2, m, hl, gate, w_lru, w_mla, g_post):
    t = m.shape[0]
    tt = _tile_rows(t)

    def body(dh2_ref, m_ref, hl_ref, gate_ref, wl_ref, wm_ref, g_ref, dm_ref, dgate_ref, dhl_ref, do_ref, dg_ref):
        i = pl.program_id(0)

        @pl.when(i == 0)
        def _():
            dg_ref[...] = jnp.zeros_like(dg_ref)

        dm, dg = _rms_bwd(m_ref[...], g_ref[...], dh2_ref[...])
        dg_ref[...] += dg
        dmb = dm.astype(BF16)
        dm_ref[...] = dmb
        dy = _dot_nt(dmb, wl_ref[...])
        gate = gate_ref[...]
        dgate_ref[...] = (dy * hl_ref[...] * _gelu_grad(gate)).astype(BF16)
        dhl_ref[...] = dy * _gelu(gate)
        do_ref[...] = _dot_nt(dmb, wm_ref[...]).astype(BF16)

    row = lambda i: (i, 0)
    return pl.pallas_call(
        body, name="mixer_out_bwd",
        grid=(t // tt,),
        in_specs=[pl.BlockSpec((tt, D_MODEL), row), pl.BlockSpec((tt, D_MODEL), row), pl.BlockSpec((tt, LRU_W), row),
                  pl.BlockSpec((tt, LRU_W), row), _full((LRU_W, D_MODEL)), _full((HEADS_W, D_MODEL)),
                  _full((1, D_MODEL))],
        out_specs=[pl.BlockSpec((tt, D_MODEL), row), pl.BlockSpec((tt, LRU_W), row), pl.BlockSpec((tt, LRU_W), row),
                   pl.BlockSpec((tt, HEADS_W), row), _full((1, D_MODEL))],
        out_shape=[jax.ShapeDtypeStruct((t, D_MODEL), BF16), jax.ShapeDtypeStruct((t, LRU_W), BF16),
                   jax.ShapeDtypeStruct((t, LRU_W), F32), jax.ShapeDtypeStruct((t, HEADS_W), BF16),
                   jax.ShapeDtypeStruct((1, D_MODEL), F32)],
        compiler_params=_params(),
    )(dh2, m, hl, gate, w_lru, w_mla, g_post)


def _loss_head(y, target):
    t = y.shape[0]
    tt = _tile_rows(t)

    def body(y_ref, t_ref, dy_ref, loss_ref):
        i = pl.program_id(0)

        @pl.when(i == 0)
        def _():
            loss_ref[...] = jnp.zeros_like(loss_ref)

        e = y_ref[...] - t_ref[...]
        dy_ref[...] = e * (1.0 / D_MODEL)
        per_token = jnp.mean(e * e, axis=-1, keepdims=True)
        loss_ref[...] += 0.5 * jnp.sum(per_token, axis=0, keepdims=True)

    row = lambda i: (i, 0)
    return pl.pallas_call(
        body, name="loss_head",
        grid=(t // tt,),
        in_specs=[pl.BlockSpec((tt, D_MODEL), row), pl.BlockSpec((tt, D_MODEL), row)],
        out_specs=[pl.BlockSpec((tt, D_MODEL), row), _full((8, 128))],
        out_shape=[jax.ShapeDtypeStruct((t, D_MODEL), F32), jax.ShapeDtypeStruct((8, 128), F32)],
        compiler_params=_params(),
    )(y, target)


def _adamw(w, g, m, v):
    m = ADAM_B1 * m + (1.0 - ADAM_B1) * g
    v = ADAM_B2 * v + (1.0 - ADAM_B2) * (g * g)
    m_hat = m / (1.0 - ADAM_B1 ** ADAM_STEP)
    v_hat = v / (1.0 - ADAM_B2 ** ADAM_STEP)
    delta = -ADAM_LR * (m_hat / (jnp.sqrt(v_hat) + ADAM_EPS) + ADAM_WD * w)
    return delta, m, v


def _sum_adamw(terms, w, m, v, name):
    r, c = w.shape
    rp = terms.shape[1]
    cb = 256 if c % 256 == 0 else 128

    def body(t_ref, w_ref, m_ref, v_ref, g_out, d_out, m_out, v_out):
        g = t_ref[0, :r, :].astype(F32)
        for q in range(1, N_DEV):
            g = g + t_ref[q, :r, :].astype(F32)
        g_out[...] = g
        d_out[...], m_out[...], v_out[...] = _adamw(w_ref[...], g, m_ref[...], v_ref[...])

    cols = pl.BlockSpec((r, cb), lambda j: (0, j))
    return pl.pallas_call(
        body, name=name, grid=(c // cb,),
        in_specs=[pl.BlockSpec((N_DEV, rp, cb), lambda j: (0, 0, j)), cols, cols, cols], out_specs=[cols] * 4,
        out_shape=[jax.ShapeDtypeStruct((r, c), F32)] * 4,
        compiler_params=_params(),
    )(terms, w, m, v)


def _sum_terms(terms):
    def body(t_ref, out_ref):
        g = t_ref[0]
        for q in range(1, N_DEV):
            g = g + t_ref[q]
        out_ref[...] = g

    return pl.pallas_call(
        body, name="small_grad_sum", in_specs=[WHOLE], out_specs=WHOLE,
        out_shape=jax.ShapeDtypeStruct(terms.shape[1:], F32),
        compiler_params=_params(),
    )(terms)


def _adamw_flat(w, g, m, v):
    def body(w_ref, g_ref, m_ref, v_ref, d_out, m_out, v_out):
        d_out[...], m_out[...], v_out[...] = _adamw(w_ref[...], g_ref[...], m_ref[...], v_ref[...])

    return pl.pallas_call(
        body, name="small_adamw", in_specs=[WHOLE] * 4, out_specs=[WHOLE] * 3,
        out_shape=[jax.ShapeDtypeStruct(w.shape, F32)] * 3,
        compiler_params=_params(),
    )(w, g, m, v)


def _pack(arrays):
    flat = jnp.concatenate([a.reshape(-1).astype(F32) for a in arrays])
    rows = -(-flat.shape[0] // 1024) * 8
    return jnp.pad(flat, (0, rows * 128 - flat.shape[0])).reshape(rows, 128)


def _unpack(packed, shapes):
    flat = packed.reshape(-1)
    out, pos = [], 0
    for s in shapes:
        n = math.prod(s)
        out.append(flat[pos:pos + n].reshape(s))
        pos += n
    return out


def _rope_tables(positions):
    t = positions.shape[-1]
    inv_freq = 1.0 / (ROPE_THETA ** (jnp.arange(0, QK_ROPE, 2, dtype=F32) / QK_ROPE))
    ang = positions.reshape(t, 1).astype(F32) * inv_freq
    cos, sin = jnp.cos(ang), jnp.sin(ang)
    z16 = jnp.zeros((t, 16), F32)

    def place(first, second, lead, lead_value):
        parts = [jnp.full((t, lead), lead_value, F32)] if lead else []
        parts += [first, second, jnp.zeros((t, HEAD_PAD - lead - 32), F32)]
        return jnp.concatenate(parts, axis=1)

    q_tabs = [place(cos, cos, QK_NOPE, 1.0) * ATT_SCALE, place(-sin, z16, QK_NOPE, 0.0) * ATT_SCALE,
              place(z16, sin, QK_NOPE, 0.0) * ATT_SCALE]
    k_tabs = [place(cos, cos, 0, 0.0), place(-sin, z16, 0, 0.0), place(z16, sin, 0, 0.0)]
    return q_tabs + k_tabs


def _block_diag(w):
    eye = jnp.eye(w.shape[0], dtype=w.dtype)
    return jnp.einsum("nde,nm->ndme", w, eye).reshape(LRU_W, LRU_W)


def _diag_blocks(g):
    g4 = g.reshape(8, 64, 8, 64)
    return jnp.moveaxis(jnp.diagonal(g4, axis1=0, axis2=2), -1, 0)


def _mixer_layouts(got):
    full = {}
    w_in = got["w_in"][:, :IN_SHARD].reshape(IN_COLS, D_MODEL)
    full["w_in"] = jnp.pad(w_in, ((0, IN_PAD - IN_COLS), (0, 0)))
    full["wq"] = got["w_q_b"].reshape(HEADS_W, Q_LORA)
    kvb = got["w_kv_b"]
    zeros = jnp.zeros((N_HEADS, KV_LORA, HEAD_PAD - QK_NOPE), BF16)
    k_part = jnp.transpose(jnp.concatenate([kvb[:, :, :QK_NOPE], zeros], axis=2), (1, 0, 2)).reshape(KV_LORA, HEADS_W)
    rope_rows = jnp.pad(jnp.eye(QK_ROPE, dtype=BF16), ((0, HEAD_PAD - QK_ROPE), (QK_NOPE, HEAD_PAD - QK_NOPE - QK_ROPE)))
    full["wk"] = jnp.concatenate([k_part, jnp.tile(rope_rows, (1, N_HEADS))], axis=0)
    full["wv"] = jnp.transpose(jnp.concatenate([kvb[:, :, QK_NOPE:], zeros], axis=2), (1, 0, 2)).reshape(KV_LORA, HEADS_W)
    w_out = got["w_out"].reshape(D_MODEL, D_MODEL)
    full["w_out_lru"] = w_out[:LRU_W]
    full["w_out_mla"] = jnp.pad(w_out[LRU_W:].reshape(N_HEADS, V_DIM, D_MODEL),
                                ((0, 0), (0, HEAD_PAD - V_DIM), (0, 0))).reshape(HEADS_W, D_MODEL)
    conv_w = jnp.transpose(got["conv_w"][:, :CONV_W, :LRU_W // N_DEV], (1, 0, 2)).reshape(CONV_W, LRU_W)
    full["conv_w"] = jnp.pad(conv_w, ((0, 8 - CONV_W), (0, 0)))
    return full


_FFN1 = ["w_ffn1_gate", "w_ffn1_up", "w_ffn1_down"]
_MIX = ["w_in", "w_q_b", "w_kv_b", "w_out"]
_FFN2 = ["w_ffn2_gate", "w_ffn2_up", "w_ffn2_down"]
_SHARDED = _FFN1 + _MIX + _FFN2
_TRANSPOSED = ("w_ffn1_gate", "w_ffn1_up", "w_in", "w_q_b", "w_ffn2_gate", "w_ffn2_up")
_SMALL = ["g_ffn1_pre", "g_ffn1_post", "g_mix_pre", "g_mix_post", "conv_b", "w_lru_a", "b_lru_a", "w_lru_x",
          "b_lru_x", "lru_lambda", "q_a_norm", "kv_a_norm", "g_ffn2_pre", "g_ffn2_post", "conv_w"]
_ORDER = ["g_ffn1_pre", "g_ffn1_post", "w_ffn1_gate", "w_ffn1_up", "w_ffn1_down", "g_mix_pre", "g_mix_post", "w_in",
          "conv_w", "conv_b", "w_lru_a", "b_lru_a", "w_lru_x", "b_lru_x", "lru_lambda", "q_a_norm", "w_q_b",
          "kv_a_norm", "w_kv_b", "w_out", "g_ffn2_pre", "g_ffn2_post", "w_ffn2_gate", "w_ffn2_up", "w_ffn2_down"]
_FF_BLOCK = (FF_PAD, D_MODEL)
_PADS = {"w_ffn1_gate": _FF_BLOCK, "w_ffn1_up": _FF_BLOCK, "w_ffn1_down": _FF_BLOCK,
         "w_in": (IN_SHARD_PAD, D_MODEL), "w_q_b": (HEAD_PAD, Q_LORA), "w_kv_b": (KV_LORA, HEAD_PAD),
         "w_out": (HEAD_PAD, D_MODEL), "conv_w": (8, 128),
         "w_ffn2_gate": _FF_BLOCK, "w_ffn2_up": _FF_BLOCK, "w_ffn2_down": _FF_BLOCK}


def _rows(name, a):
    return a.T if name in _TRANSPOSED else a


def _step(args):
    x = args["x"][0]
    target = args["loss_target"][0]
    positions = args["positions"][0]
    w = {n: args[n][0] for n in _ORDER}
    mom = {n: args["m_" + n][0] for n in _ORDER}
    var = {n: args["v_" + n][0] for n in _ORDER}
    vec = lambda name: w[name].reshape(1, -1)
    t = x.shape[0]
    ff_blocks = lambda a: a.reshape(N_DEV, FF_PAD, D_MODEL)
    ff_rows = lambda a: a.reshape(FF_VIRT, D_MODEL)

    names = _FFN1 + _MIX + ["conv_w"] + _FFN2
    ffn1, staged = _first_gather([_rows(n, w[n]) for n in names], [_PADS[n] for n in names],
                                 [F32 if n == "conv_w" else BF16 for n in names], len(_FFN1))
    wg1, wu1, wd1 = [ff_rows(a) for a in ffn1]
    n_mix = len(_MIX) + 1

    tables = _rope_tables(positions)
    wa = _block_diag(w["w_lru_a"]).astype(BF16)
    wx = _block_diag(w["w_lru_x"]).astype(BF16)

    (h1, f1, n1, gt1, up1), got = _ffn_fwd(x, vec("g_ffn1_pre"), vec("g_ffn1_post"), wg1, wu1, wd1,
                                           _Gather(staged[:n_mix]), "ffn_fwd_gather")
    mix = _mixer_layouts(dict(zip(_MIX + ["conv_w"], got)))
    nmix, xl, gate, qlat, kvlat, krope = _mixer_in(h1, vec("g_mix_pre"), mix["w_in"])
    xc, a, u = _lru_gates(xl, mix["conv_w"], vec("conv_b"), wa, wx, vec("b_lru_a"), vec("b_lru_x"), vec("lru_lambda"))
    hl = _lru_scan(a, u, reverse=False)
    q, k, v, qn, kva, vt = _mla_proj(qlat, kvlat, krope, tables, vec("q_a_norm"), vec("kv_a_norm"),
                                 mix["wq"], mix["wk"], mix["wv"])
    (o, lse), ffn2 = _attn_fwd(q, k, vt, _Gather(staged[n_mix:]))
    wg2, wu2, wd2 = [ff_rows(a_) for a_ in ffn2]
    h2, m, ylru = _mixer_out(h1, hl, gate, o, mix["w_out_lru"], mix["w_out_mla"], vec("g_mix_post"))
    (h3, f2, n2, gt2, up2), _ = _ffn_fwd(h2, vec("g_ffn2_pre"), vec("g_ffn2_post"), wg2, wu2, wd2, None, "ffn_fwd")
    dy, loss_tile = _loss_head(h3, target)

    (df2, act2, dgt2, dup2, dg2post), _ = _ffn_bwd_hidden(dy, f2, gt2, up2, vec("g_ffn2_post"), wd2, None,
                                                          "ffn_bwd_hidden")
    (dh2, dg2pre), _ = _ffn_bwd_input(dy, h2, dgt2, dup2, vec("g_ffn2_pre"), wg2, wu2, None, "ffn_bwd_input")
    dw_ffn = lambda a_, b_, name, comm=None: _mm_tn(a_, b_, FF_VIRT, FF_CHUNK, D_MODEL, name, comm)
    big2 = [ff_blocks(dw_ffn(dgt2, n2, "dw_ffn_gate")[0]), ff_blocks(dw_ffn(dup2, n2, "dw_ffn_up")[0]),
            ff_blocks(dw_ffn(act2, df2, "dw_ffn_down")[0])]

    dm, dgate, dhl, do, dgmixpost = _mixer_out_bwd(dh2, m, hl, gate, mix["w_out_lru"], mix["w_out_mla"],
                                                   vec("g_mix_post"))
    dw_out_lru, _ = _mm_tn(ylru, dm, LRU_W, LRU_W, D_MODEL, "dw_out_lru")
    dw_out_mla, _ = _mm_tn(o, dm, HEADS_W, HEADS_W, D_MODEL, "dw_out_mla")
    dw_out = jnp.concatenate(
        [dw_out_lru, dw_out_mla.reshape(N_HEADS, HEAD_PAD, D_MODEL)[:, :V_DIM].reshape(N_HEADS * V_DIM, D_MODEL)], axis=0)

    (dq, dk, dv), terms2 = _attn_bwd(q, k, v, o, do, lse, _Exchange(big2))
    dqr, dqlat, dkvlat, dkrope, dgq, dgkv = _mla_proj_bwd(
        dq, dk, dv, qlat, kvlat, tables, vec("q_a_norm"), vec("kv_a_norm"), mix["wq"], mix["wk"], mix["wv"])
    dwq, _ = _mm_tn(dqr, qn, HEADS_W, HEADS_W, Q_LORA, "dw_q_b")
    dwk, _ = _mm_tn(kva, dk, KV_LORA, KV_LORA, HEADS_W, "dw_kv_b_k")
    dwv, _ = _mm_tn(kva, dv, KV_LORA, KV_LORA, HEADS_W, "dw_kv_b_v")
    dw_kvb = jnp.transpose(jnp.concatenate(
        [dwk.reshape(KV_LORA, N_HEADS, HEAD_PAD)[:, :, :QK_NOPE], dwv.reshape(KV_LORA, N_HEADS, HEAD_PAD)[:, :, :V_DIM]],
        axis=2), (1, 0, 2))

    dh = _lru_scan(a, dhl, reverse=True)
    dxc, dwa, dwx, dba, dbx, dlam = _lru_gates_bwd(dh, hl, xc, wa, wx, vec("b_lru_a"), vec("b_lru_x"),
                                                   vec("lru_lambda"))
    dxl, dconv_w8, dconv_b = _conv_bwd(dxc, xl, mix["conv_w"])
    dh1, dproj, dgmixpre = _mixer_in_bwd(dh2, h1, vec("g_mix_pre"), mix["w_in"], [dxl, dgate, dqlat, dkvlat, dkrope])
    dw_in, _ = _mm_tn(dproj, nmix, IN_PAD, IN_PAD // 2, D_MODEL, "dw_in")
    dw_in = jnp.pad(dw_in[:IN_COLS].reshape(N_DEV, IN_SHARD, D_MODEL), ((0, 0), (0, IN_SHARD_PAD - IN_SHARD), (0, 0)))
    big_mix = [dw_in, dwq.reshape(N_DEV, HEAD_PAD, Q_LORA), dw_kvb, dw_out.reshape(N_DEV, HEAD_PAD, D_MODEL)]

    (df1, act1, dgt1, dup1, dg1post), terms_mix = _ffn_bwd_hidden(
        dh1, f1, gt1, up1, vec("g_ffn1_post"), wd1, _Exchange(big_mix), "ffn_bwd_hidden_exchange")
    dwd1, _ = dw_ffn(act1, df1, "dw_ffn_down")
    dwg1, terms_d1 = dw_ffn(dgt1, n1, "dw_ffn_gate_exchange", _Exchange([ff_blocks(dwd1)]))
    dwu1, terms_g1 = dw_ffn(dup1, n1, "dw_ffn_up_exchange", _Exchange([ff_blocks(dwg1)]))
    (dx, dg1pre), terms_u1 = _ffn_bwd_input(dh1, x, dgt1, dup1, vec("g_ffn1_pre"), wg1, wu1,
                                            _Exchange([ff_blocks(dwu1)]), "ffn_bwd_input_exchange")

    small = {"g_ffn1_pre": dg1pre, "g_ffn1_post": dg1post, "g_mix_pre": dgmixpre, "g_mix_post": dgmixpost,
             "conv_b": dconv_b, "w_lru_a": _diag_blocks(dwa), "b_lru_a": dba, "w_lru_x": _diag_blocks(dwx),
             "b_lru_x": dbx, "lru_lambda": dlam, "q_a_norm": dgq, "kv_a_norm": dgkv, "g_ffn2_pre": dg2pre,
             "g_ffn2_post": dg2post, "conv_w": dconv_w8[:CONV_W]}
    packed = _pack([small[n] for n in _SMALL] + [loss_tile[:1, :1]])
    small_terms, = _last_exchange([jnp.broadcast_to(packed[None], (N_DEV,) + packed.shape)])

    terms = dict(zip(_FFN2, terms2))
    terms.update(zip(_MIX, terms_mix))
    terms.update({"w_ffn1_down": terms_d1[0], "w_ffn1_gate": terms_g1[0], "w_ffn1_up": terms_u1[0]})

    grads, delta, new_m, new_v = {}, {}, {}, {}
    for n in _SHARDED:
        res = _sum_adamw(terms[n], _rows(n, w[n]), _rows(n, mom[n]), _rows(n, var[n]), "adamw_" + n)
        grads[n], delta[n], new_m[n], new_v[n] = [_rows(n, r) for r in res]

    summed = _sum_terms(small_terms)
    small_sum = dict(zip(_SMALL + ["loss"], _unpack(summed, [small[n].shape for n in _SMALL] + [(1, 1)])))
    loss = small_sum.pop("loss").reshape(())
    me = 4 * lax.axis_index("x") + 2 * lax.axis_index("y") + lax.axis_index("c")
    cshard = LRU_W // N_DEV
    small_sum["conv_w"] = lax.dynamic_slice(small_sum["conv_w"], (0, me * cshard), (CONV_W, cshard))
    for n in _SMALL:
        grads[n] = small_sum[n].reshape(w[n].shape)
    shapes = [w[n].shape for n in _SMALL]
    d_p, m_p, v_p = _adamw_flat(_pack([w[n] for n in _SMALL]), _pack([grads[n] for n in _SMALL]),
                                _pack([mom[n] for n in _SMALL]), _pack([var[n] for n in _SMALL]))
    for n, d_, m_, v_ in zip(_SMALL, _unpack(d_p, shapes), _unpack(m_p, shapes), _unpack(v_p, shapes)):
        delta[n], new_m[n], new_v[n] = d_, m_, v_

    lead = lambda d: [d[n][None] for n in _ORDER]
    return (loss, dx[None], *lead(grads), *lead(delta), *lead(new_m), *lead(new_v))


def kernel(x, positions, g_ffn1_pre, g_ffn1_post, w_ffn1_gate, w_ffn1_up, w_ffn1_down, g_mix_pre, g_mix_post, w_in, conv_w, conv_b, w_lru_a, b_lru_a, w_lru_x, b_lru_x, lru_lambda, q_a_norm, w_q_b, kv_a_norm, w_kv_b, w_out, g_ffn2_pre, g_ffn2_post, w_ffn2_gate, w_ffn2_up, w_ffn2_down, loss_target, m_g_ffn1_pre, m_g_ffn1_post, m_w_ffn1_gate, m_w_ffn1_up, m_w_ffn1_down, m_g_mix_pre, m_g_mix_post, m_w_in, m_conv_w, m_conv_b, m_w_lru_a, m_b_lru_a, m_w_lru_x, m_b_lru_x, m_lru_lambda, m_q_a_norm, m_w_q_b, m_kv_a_norm, m_w_kv_b, m_w_out, m_g_ffn2_pre, m_g_ffn2_post, m_w_ffn2_gate, m_w_ffn2_up, m_w_ffn2_down, v_g_ffn1_pre, v_g_ffn1_post, v_w_ffn1_gate, v_w_ffn1_up, v_w_ffn1_down, v_g_mix_pre, v_g_mix_post, v_w_in, v_conv_w, v_conv_b, v_w_lru_a, v_b_lru_a, v_w_lru_x, v_b_lru_x, v_lru_lambda, v_q_a_norm, v_w_q_b, v_kv_a_norm, v_w_kv_b, v_w_out, v_g_ffn2_pre, v_g_ffn2_post, v_w_ffn2_gate, v_w_ffn2_up, v_w_ffn2_down):
    return _step(dict(locals()))
```

```python
import functools
import math
import operator

import jax
import jax.numpy as jnp
from jax import lax
from jax.experimental import pallas as pl
from jax.experimental.pallas import tpu as pltpu

F32 = jnp.float32
BF16 = jnp.bfloat16
MESH = pl.DeviceIdType.MESH

N_DEV = 8
D_MODEL = 1024
D_FF = 2816
FF_SHARD = D_FF // N_DEV
FF_PAD = 384
FF_VIRT = N_DEV * FF_PAD
FF_CHUNK = 2 * FF_PAD
LRU_W = 512
N_HEADS = 8
HEAD_PAD = 128
HEADS_W = N_HEADS * HEAD_PAD
QK_NOPE = 64
QK_ROPE = 32
V_DIM = 64
Q_LORA = 384
KV_LORA = 256
IN_COLS = 2 * LRU_W + Q_LORA + KV_LORA + QK_ROPE
IN_SHARD = IN_COLS // N_DEV
IN_SHARD_PAD = 256
IN_PAD = 1792
QB_SHARD = 96
CONV_W = 4
CHUNK = 64
EPS = 1e-6
LRU_C = 8.0
ROPE_THETA = 10000.0
ATT_SCALE = (QK_NOPE + QK_ROPE) ** -0.5

ADAM_LR = 0.001
ADAM_B1 = 0.9
ADAM_B2 = 0.999
ADAM_EPS = 1e-08
ADAM_WD = 0.01
ADAM_STEP = 10

VMEM_LIMIT = 56 * 1024 * 1024
ANY = pl.BlockSpec(memory_space=pl.ANY)
WHOLE = pl.BlockSpec(memory_space=pltpu.VMEM)


def _params(**kw):
    return pltpu.CompilerParams(vmem_limit_bytes=VMEM_LIMIT, **kw)


def _full(shape):
    return pl.BlockSpec(shape, lambda *_: (0,) * len(shape))


def _dot(a, b):
    return jnp.dot(a, b, preferred_element_type=F32)


def _dot_nt(a, b):
    return lax.dot_general(a, b, (((1,), (1,)), ((), ())), preferred_element_type=F32)


def _dot_tn(a, b):
    return lax.dot_general(a, b, (((0,), (0,)), ((), ())), preferred_element_type=F32)


def _rms(x, g):
    r = lax.rsqrt(jnp.mean(x * x, axis=-1, keepdims=True) + EPS)
    return x * r * g


def _rms_bwd(x, g, dy):
    r = lax.rsqrt(jnp.mean(x * x, axis=-1, keepdims=True) + EPS)
    xh = x * r
    dg = jnp.sum(dy * xh, axis=0, keepdims=True)
    dxh = dy * g
    dx = r * (dxh - xh * jnp.mean(dxh * xh, axis=-1, keepdims=True))
    return dx, dg


def _sigmoid(x):
    return 0.5 * jnp.tanh(0.5 * x) + 0.5


_GELU_C = math.sqrt(2.0 / math.pi)


def _gelu(x):
    t = jnp.tanh(_GELU_C * (x + 0.044715 * x * x * x))
    return 0.5 * x * (1.0 + t)


def _gelu_grad(x):
    t = jnp.tanh(_GELU_C * (x + 0.044715 * x * x * x))
    return 0.5 * (1.0 + t) + 0.5 * x * (1.0 - t * t) * _GELU_C * (1.0 + 3.0 * 0.044715 * x * x)


def _tile_rows(t):
    return 512 if t >= 2048 else t // 2


def _dev_index(p):
    return 4 * p[0] + 2 * p[1] + p[2]


def _place():
    x, y, c = lax.axis_index("x"), lax.axis_index("y"), lax.axis_index("c")
    return (x, y, c), (x, y, 1 - c), [(1 - x, y), (x, 1 - y), (1 - x, 1 - y)]


def _dma_sems(n):
    return [pltpu.SemaphoreType.DMA((7 * n,)), pltpu.SemaphoreType.DMA((7 * n,)), pltpu.SemaphoreType.DMA((n,))]


class _Gather:
    has_middle = True

    def __init__(self, arrays):
        self.inputs = list(arrays)
        self.n = len(arrays)
        self.out_shape = [jax.ShapeDtypeStruct((N_DEV,) + a.shape, a.dtype) for a in arrays]
        self.scratch = _dma_sems(self.n)

    @staticmethod
    def _copy(outs, sems, k, s, block, to, src=None):
        rows = outs[k].at[_dev_index(block)]
        return pltpu.make_async_remote_copy(
            src_ref=rows if src is None else src, dst_ref=rows,
            send_sem=sems[0].at[7 * k + s], recv_sem=sems[1].at[7 * k + s], device_id=to, device_id_type=MESH)

    def _first(self, srcs, outs, sems):
        me, sibling, chips = _place()
        mine = [pltpu.make_async_copy(srcs[k], outs[k].at[_dev_index(me)], sems[2].at[k]) for k in range(self.n)]
        sends = []
        for k in range(self.n):
            sends.append(self._copy(outs, sems, k, 0, me, sibling, src=srcs[k]))
            sends += [self._copy(outs, sems, k, 1 + j, me, (*chip, me[2]), src=srcs[k])
                      for j, chip in enumerate(chips)]
        return mine, sends

    def _passed(self, outs, sems):
        me, sibling, chips = _place()
        return [[self._copy(outs, sems, k, 4 + j, (*chip, me[2]), sibling) for k in range(self.n)]
                for j, chip in enumerate(chips)]

    def start(self, srcs, outs, sems):
        mine, sends = self._first(srcs, outs, sems)
        for cp in mine + sends:
            cp.start()

    def middle(self, srcs, outs, sems):
        me, sibling, chips = _place()
        passed = self._passed(outs, sems)
        for j, chip in enumerate(chips):
            for k in range(self.n):
                self._copy(outs, sems, k, 1 + j, (*chip, me[2]), me).wait_recv()
                passed[j][k].start()

    def finish(self, srcs, outs, sems):
        me, sibling, chips = _place()
        for k in range(self.n):
            self._copy(outs, sems, k, 0, sibling, me).wait_recv()
        for j, chip in enumerate(chips):
            for k in range(self.n):
                self._copy(outs, sems, k, 4 + j, (*chip, 1 - me[2]), me).wait_recv()
        mine, sends = self._first(srcs, outs, sems)
        for cp in sends + [cp for row in self._passed(outs, sems) for cp in row]:
            cp.wait_send()
        for cp in mine:
            cp.wait()


class _Exchange:
    has_middle = False

    def __init__(self, arrays):
        self.inputs = list(arrays)
        self.n = len(arrays)
        self.out_shape = [jax.ShapeDtypeStruct(a.shape, a.dtype) for a in arrays]
        self.scratch = _dma_sems(self.n)

    def _copies(self, srcs, outs, sems):
        (x, y, c), _, _ = _place()
        me = _dev_index((x, y, c))
        local = [pltpu.make_async_copy(srcs[k].at[me], outs[k].at[me], sems[2].at[k]) for k in range(self.n)]
        remote = []
        for s in range(1, N_DEV):
            peer = (1 - x if s & 4 else x, 1 - y if s & 2 else y, 1 - c if s & 1 else c)
            for k in range(self.n):
                remote.append(pltpu.make_async_remote_copy(
                    src_ref=srcs[k].at[_dev_index(peer)], dst_ref=outs[k].at[me],
                    send_sem=sems[0].at[7 * k + s - 1], recv_sem=sems[1].at[7 * k + s - 1],
                    device_id=peer, device_id_type=MESH))
        return local, remote

    def start(self, srcs, outs, sems):
        local, remote = self._copies(srcs, outs, sems)
        for cp in local + remote:
            cp.start()

    def finish(self, srcs, outs, sems):
        local, remote = self._copies(srcs, outs, sems)
        for cp in remote:
            cp.wait_recv()
        for cp in remote:
            cp.wait_send()
        for cp in local:
            cp.wait()


def _call(body, comm, *, name, grid, in_specs, out_specs, out_shape, scratch_shapes=(), args):
    in_specs, out_specs, out_shape = list(in_specs), list(out_specs), list(out_shape)
    scratch_shapes = list(scratch_shapes)
    if comm is None:
        outs = pl.pallas_call(body, name=name, grid=grid, in_specs=in_specs, out_specs=out_specs, out_shape=out_shape,
                              scratch_shapes=scratch_shapes, compiler_params=_params())(*args)
        return list(outs), []
    n_in, n_out, n_scr, c_n = len(in_specs), len(out_specs), len(scratch_shapes), comm.n
    middle = tuple(g // 2 if d == 0 else 0 for d, g in enumerate(grid))

    def hosted(*refs):
        pos = 0
        parts = []
        for width in (n_in, c_n, n_out, c_n, n_scr, 3):
            parts.append(refs[pos:pos + width])
            pos += width
        ins, c_in, outs, c_out, scr, sems = parts
        ids = [pl.program_id(d) for d in range(len(grid))]
        at = lambda where: functools.reduce(operator.and_, [i == w for i, w in zip(ids, where)])

        @pl.when(at([0] * len(grid)))
        def _():
            comm.start(c_in, c_out, sems)

        body(*ins, *outs, *scr)

        if comm.has_middle:
            @pl.when(at(middle))
            def _():
                comm.middle(c_in, c_out, sems)

        @pl.when(at([g - 1 for g in grid]))
        def _():
            comm.finish(c_in, c_out, sems)

    outs = pl.pallas_call(
        hosted, name=name, grid=grid,
        in_specs=in_specs + [ANY] * c_n, out_specs=out_specs + [ANY] * c_n,
        out_shape=out_shape + comm.out_shape, scratch_shapes=scratch_shapes + comm.scratch,
        compiler_params=_params())(*args, *comm.inputs)
    return list(outs[:n_out]), list(outs[n_out:])


def _first_gather(shards, pads, dtypes, n_gathered):
    n = len(shards)
    gather = _Gather([jax.ShapeDtypeStruct(tuple(pads[k]), dtypes[k]) for k in range(n_gathered)])

    def body(*refs):
        ins, outs = refs[:n], refs[n:2 * n]
        stages = list(refs[2 * n:2 * n + n_gathered]) + list(outs[n_gathered:])
        sems = refs[2 * n + n_gathered:]
        for k in range(n):
            r, cc = ins[k].shape
            if (r, cc) != tuple(stages[k].shape):
                stages[k][...] = jnp.zeros(stages[k].shape, stages[k].dtype)
            stages[k][:r, :cc] = ins[k][...].astype(stages[k].dtype)
        gather.start(stages[:n_gathered], outs[:n_gathered], sems)
        gather.middle(stages[:n_gathered], outs[:n_gathered], sems)
        gather.finish(stages[:n_gathered], outs[:n_gathered], sems)

    staged_shape = [jax.ShapeDtypeStruct(tuple(pads[k]), dtypes[k]) for k in range(n_gathered, n)]
    outs = pl.pallas_call(
        body, name="first_gather",
        out_shape=gather.out_shape + staged_shape,
        in_specs=[WHOLE] * n,
        out_specs=[ANY] * n_gathered + [WHOLE] * (n - n_gathered),
        scratch_shapes=[pltpu.VMEM(tuple(pads[k]), dtypes[k]) for k in range(n_gathered)] + gather.scratch,
        compiler_params=_params(),
    )(*shards)
    return list(outs[:n_gathered]), list(outs[n_gathered:])


def _last_exchange(arrays):
    comm = _Exchange(arrays)

    def body(*refs):
        srcs, outs, sems = refs[:comm.n], refs[comm.n:2 * comm.n], refs[2 * comm.n:]
        comm.start(srcs, outs, sems)
        comm.finish(srcs, outs, sems)

    return pl.pallas_call(
        body, name="last_exchange", out_shape=comm.out_shape, in_specs=[ANY] * comm.n, out_specs=[ANY] * comm.n,
        scratch_shapes=comm.scratch, compiler_params=_params())(*arrays)


def _ffn_fwd(h0, g_pre, g_post, wg, wu, wd, comm, name):
    t = h0.shape[0]
    tt = _tile_rows(t)
    n_chunks = FF_VIRT // FF_CHUNK

    def body(h0_ref, gpre_ref, gpost_ref, wg_ref, wu_ref, wd_ref,
             h1_ref, f_ref, n1_ref, gt_ref, up_ref, acc_ref, n1s_ref):
        j = pl.program_id(1)

        @pl.when(j == 0)
        def _():
            n1 = _rms(h0_ref[...], gpre_ref[...]).astype(BF16)
            n1s_ref[...] = n1
            n1_ref[...] = n1
            acc_ref[...] = jnp.zeros_like(acc_ref)

        n1 = n1s_ref[...]
        gt = _dot_nt(n1, wg_ref[...])
        up = _dot_nt(n1, wu_ref[...])
        gt_ref[...] = gt.astype(BF16)
        up_ref[...] = up.astype(BF16)
        act = (gt * _sigmoid(gt) * up).astype(BF16)
        acc_ref[...] += _dot(act, wd_ref[...])

        @pl.when(j == n_chunks - 1)
        def _():
            f = acc_ref[...]
            f_ref[...] = f
            h1_ref[...] = h0_ref[...] + 0.5 * _rms(f, gpost_ref[...])

    row = lambda i, j: (i, 0)
    wspec = pl.BlockSpec((FF_CHUNK, D_MODEL), lambda i, j: (j, 0))
    return _call(
        body, comm, name=name, grid=(t // tt, n_chunks),
        in_specs=[pl.BlockSpec((tt, D_MODEL), row), _full((1, D_MODEL)), _full((1, D_MODEL)), wspec, wspec, wspec],
        out_specs=[pl.BlockSpec((tt, D_MODEL), row), pl.BlockSpec((tt, D_MODEL), row),
                   pl.BlockSpec((tt, D_MODEL), row),
                   pl.BlockSpec((tt, FF_CHUNK), lambda i, j: (i, j)),
                   pl.BlockSpec((tt, FF_CHUNK), lambda i, j: (i, j))],
        out_shape=[jax.ShapeDtypeStruct((t, D_MODEL), F32), jax.ShapeDtypeStruct((t, D_MODEL), F32),
                   jax.ShapeDtypeStruct((t, D_MODEL), BF16),
                   jax.ShapeDtypeStruct((t, FF_VIRT), BF16), jax.ShapeDtypeStruct((t, FF_VIRT), BF16)],
        scratch_shapes=[pltpu.VMEM((tt, D_MODEL), F32), pltpu.VMEM((tt, D_MODEL), BF16)],
        args=(h0, g_pre, g_post, wg, wu, wd))


def _ffn_fwd_hidden(h0, g_pre, wg, wu, comm, name):
    t = h0.shape[0]
    tt = _tile_rows(t)

    def body(h0_ref, gpre_ref, wg_ref, wu_ref, n1_ref, gt_ref, up_ref):
        @pl.when(pl.program_id(1) == 0)
        def _():
            n1_ref[...] = _rms(h0_ref[...], gpre_ref[...]).astype(BF16)

        n1 = n1_ref[...]
        gt_ref[...] = _dot_nt(n1, wg_ref[...]).astype(BF16)
        up_ref[...] = _dot_nt(n1, wu_ref[...]).astype(BF16)

    row = lambda i, j: (i, 0)
    wspec = pl.BlockSpec((FF_CHUNK, D_MODEL), lambda i, j: (j, 0))
    chunk = pl.BlockSpec((tt, FF_CHUNK), lambda i, j: (i, j))
    return _call(
        body, comm, name=name, grid=(t // tt, FF_VIRT // FF_CHUNK),
        in_specs=[pl.BlockSpec((tt, D_MODEL), row), _full((1, D_MODEL)), wspec, wspec],
        out_specs=[pl.BlockSpec((tt, D_MODEL), row), chunk, chunk],
        out_shape=[jax.ShapeDtypeStruct((t, D_MODEL), BF16), jax.ShapeDtypeStruct((t, FF_VIRT), BF16),
                   jax.ShapeDtypeStruct((t, FF_VIRT), BF16)],
        args=(h0, g_pre, wg, wu))


def _ffn_fwd_out(h0, gt, up, g_post, wd, name):
    t = h0.shape[0]
    tt = 2 * _tile_rows(t)
    n_chunks = FF_VIRT // FF_CHUNK

    def body(h0_ref, gt_ref, up_ref, gpost_ref, wd_ref, h1_ref, f_ref):
        j = pl.program_id(1)
        g = gt_ref[...].astype(F32)
        term = _dot((g * _sigmoid(g) * up_ref[...].astype(F32)).astype(BF16), wd_ref[...])

        @pl.when(j == 0)
        def _():
            f_ref[...] = term

        @pl.when(j > 0)
        def _():
            f_ref[...] += term

        @pl.when(j == n_chunks - 1)
        def _():
            h1_ref[...] = h0_ref[...] + 0.5 * _rms(f_ref[...], gpost_ref[...])

    row = lambda i, j: (i, 0)
    chunk = pl.BlockSpec((tt, FF_CHUNK), lambda i, j: (i, j))
    outs, _ = _call(
        body, None, name=name, grid=(t // tt, n_chunks),
        in_specs=[pl.BlockSpec((tt, D_MODEL), row), chunk, chunk, _full((1, D_MODEL)),
                  pl.BlockSpec((FF_CHUNK, D_MODEL), lambda i, j: (j, 0))],
        out_specs=[pl.BlockSpec((tt, D_MODEL), row), pl.BlockSpec((tt, D_MODEL), row)],
        out_shape=[jax.ShapeDtypeStruct((t, D_MODEL), F32), jax.ShapeDtypeStruct((t, D_MODEL), F32)],
        args=(h0, gt, up, g_post, wd))
    return outs


def _ffn_bwd_hidden(dh1, f, gt, up, g_post, wd, comm, name):
    t = f.shape[0]
    tt = _tile_rows(t)
    n_chunks = FF_VIRT // FF_CHUNK

    def body(dh1_ref, f_ref, gt_ref, up_ref, gpost_ref, wd_ref,
             df_ref, act_ref, dgt_ref, dup_ref, dgpost_ref, dfs_ref):
        i, j = pl.program_id(0), pl.program_id(1)

        @pl.when((i == 0) & (j == 0))
        def _():
            dgpost_ref[...] = jnp.zeros_like(dgpost_ref)

        @pl.when(j == 0)
        def _():
            df, dg = _rms_bwd(f_ref[...], gpost_ref[...], 0.5 * dh1_ref[...])
            dgpost_ref[...] += dg
            dfb = df.astype(BF16)
            dfs_ref[...] = dfb
            df_ref[...] = dfb

        da = _dot_nt(dfs_ref[...], wd_ref[...])
        g = gt_ref[...].astype(F32)
        u = up_ref[...].astype(F32)
        s = _sigmoid(g)
        sl = g * s
        act_ref[...] = (sl * u).astype(BF16)
        dgt_ref[...] = (da * u * (s * (1.0 + g * (1.0 - s)))).astype(BF16)
        dup_ref[...] = (da * sl).astype(BF16)

    row = lambda i, j: (i, 0)
    chunk = pl.BlockSpec((tt, FF_CHUNK), lambda i, j: (i, j))
    return _call(
        body, comm, name=name, grid=(t // tt, n_chunks),
        in_specs=[pl.BlockSpec((tt, D_MODEL), row), pl.BlockSpec((tt, D_MODEL), row), chunk, chunk,
                  _full((1, D_MODEL)), pl.BlockSpec((FF_CHUNK, D_MODEL), lambda i, j: (j, 0))],
        out_specs=[pl.BlockSpec((tt, D_MODEL), row), chunk, chunk, chunk, _full((1, D_MODEL))],
        out_shape=[jax.ShapeDtypeStruct((t, D_MODEL), BF16)] + [jax.ShapeDtypeStruct((t, FF_VIRT), BF16)] * 3 + [
            jax.ShapeDtypeStruct((1, D_MODEL), F32)],
        scratch_shapes=[pltpu.VMEM((tt, D_MODEL), BF16)],
        args=(dh1, f, gt, up, g_post, wd))


def _ffn_bwd_input(dh1, h0, dgt, dup, g_pre, wg, wu, comm, name):
    t = h0.shape[0]
    tt = 2 * _tile_rows(t)
    n_chunks = FF_VIRT // FF_CHUNK

    def body(dh1_ref, h0_ref, dgt_ref, dup_ref, gpre_ref, wg_ref, wu_ref, dh0_ref, dgpre_ref, acc_ref):
        i, j = pl.program_id(0), pl.program_id(1)

        @pl.when((i == 0) & (j == 0))
        def _():
            dgpre_ref[...] = jnp.zeros_like(dgpre_ref)

        @pl.when(j == 0)
        def _():
            acc_ref[...] = jnp.zeros_like(acc_ref)

        acc_ref[...] += _dot(dgt_ref[...], wg_ref[...]) + _dot(dup_ref[...], wu_ref[...])

        @pl.when(j == n_chunks - 1)
        def _():
            dx, dg = _rms_bwd(h0_ref[...], gpre_ref[...], acc_ref[...])
            dgpre_ref[...] += dg
            dh0_ref[...] = dh1_ref[...] + dx

    row = lambda i, j: (i, 0)
    chunk = pl.BlockSpec((tt, FF_CHUNK), lambda i, j: (i, j))
    wspec = pl.BlockSpec((FF_CHUNK, D_MODEL), lambda i, j: (j, 0))
    return _call(
        body, comm, name=name, grid=(t // tt, n_chunks),
        in_specs=[pl.BlockSpec((tt, D_MODEL), row), pl.BlockSpec((tt, D_MODEL), row), chunk, chunk,
                  _full((1, D_MODEL)), wspec, wspec],
        out_specs=[pl.BlockSpec((tt, D_MODEL), row), _full((1, D_MODEL))],
        out_shape=[jax.ShapeDtypeStruct((t, D_MODEL), F32), jax.ShapeDtypeStruct((1, D_MODEL), F32)],
        scratch_shapes=[pltpu.VMEM((tt, D_MODEL), F32)],
        args=(dh1, h0, dgt, dup, g_pre, wg, wu))


def _mm_tn(a, b, m, tm, tn, name, comm=None):
    t, n = b.shape
    tk = min(t, 2048)
    nk = t // tk

    def body(a_ref, b_ref, out_ref, acc_ref):
        k = pl.program_id(2)

        @pl.when(k == 0)
        def _():
            acc_ref[...] = jnp.zeros_like(acc_ref)

        acc_ref[...] += _dot_tn(a_ref[...], b_ref[...])

        @pl.when(k == nk - 1)
        def _():
            out_ref[...] = acc_ref[...].astype(out_ref.dtype)

    outs, got = _call(
        body, comm, name=name, grid=(m // tm, n // tn, nk),
        in_specs=[pl.BlockSpec((tk, tm), lambda i, j, k: (k, i)), pl.BlockSpec((tk, tn), lambda i, j, k: (k, j))],
        out_specs=[pl.BlockSpec((tm, tn), lambda i, j, k: (i, j))],
        out_shape=[jax.ShapeDtypeStruct((m, n), BF16)],
        scratch_shapes=[pltpu.VMEM((tm, tn), F32)], args=(a, b))
    return outs[0], got


_SPLITS = (0, LRU_W, 2 * LRU_W, 2 * LRU_W + Q_LORA, 2 * LRU_W + Q_LORA + KV_LORA, IN_PAD)
_WIDTHS = tuple(_SPLITS[k + 1] - _SPLITS[k] for k in range(5))


def _mixer_in(h1, g_pre, w_in):
    t = h1.shape[0]
    tt = _tile_rows(t)

    def body(h_ref, g_ref, w_ref, n_ref, *outs):
        n = _rms(h_ref[...], g_ref[...]).astype(BF16)
        n_ref[...] = n
        proj = _dot_nt(n, w_ref[...])
        for k in range(5):
            outs[k][...] = proj[:, _SPLITS[k]:_SPLITS[k + 1]]

    row = lambda i: (i, 0)
    return pl.pallas_call(
        body, name="mixer_in",
        grid=(t // tt,),
        in_specs=[pl.BlockSpec((tt, D_MODEL), row), _full((1, D_MODEL)), _full((IN_PAD, D_MODEL))],
        out_specs=[pl.BlockSpec((tt, D_MODEL), row)] + [pl.BlockSpec((tt, w), row) for w in _WIDTHS],
        out_shape=[jax.ShapeDtypeStruct((t, D_MODEL), BF16)] + [jax.ShapeDtypeStruct((t, w), F32) for w in _WIDTHS],
        compiler_params=_params(),
    )(h1, g_pre, w_in)


def _mixer_in_bwd(dh2, h1, g_pre, w_in, dparts):
    t = h1.shape[0]
    tt = _tile_rows(t)

    def body(dh2_ref, h_ref, g_ref, w_ref, p0, p1, p2, p3, p4, dh1_ref, dproj_ref, dg_ref):
        i = pl.program_id(0)

        @pl.when(i == 0)
        def _():
            dg_ref[...] = jnp.zeros_like(dg_ref)

        dproj = jnp.concatenate([p[...] for p in (p0, p1, p2, p3, p4)], axis=1)
        dproj_ref[...] = dproj
        dx, dg = _rms_bwd(h_ref[...], g_ref[...], _dot(dproj, w_ref[...]))
        dg_ref[...] += dg
        dh1_ref[...] = dh2_ref[...] + dx

    row = lambda i: (i, 0)
    return pl.pallas_call(
        body, name="mixer_in_bwd",
        grid=(t // tt,),
        in_specs=[pl.BlockSpec((tt, D_MODEL), row), pl.BlockSpec((tt, D_MODEL), row), _full((1, D_MODEL)),
                  _full((IN_PAD, D_MODEL))] + [pl.BlockSpec((tt, w), row) for w in _WIDTHS],
        out_specs=[pl.BlockSpec((tt, D_MODEL), row), pl.BlockSpec((tt, IN_PAD), row), _full((1, D_MODEL))],
        out_shape=[jax.ShapeDtypeStruct((t, D_MODEL), F32), jax.ShapeDtypeStruct((t, IN_PAD), BF16),
                   jax.ShapeDtypeStruct((1, D_MODEL), F32)],
        compiler_params=_params(),
    )(dh2, h1, g_pre, w_in, *dparts)


def _conv(x_prev8, x_tile, w_ref, b_ref):
    tt = x_tile.shape[0]
    xx = jnp.concatenate([x_prev8, x_tile], axis=0)
    y = b_ref[...] + w_ref[CONV_W - 1:CONV_W, :] * x_tile
    for k in range(CONV_W - 1):
        y = y + w_ref[k:k + 1, :] * pltpu.roll(xx, CONV_W - 1 - k, 0)[8:8 + tt]
    return y


def _neg_expm1(z):
    series = -z * (1.0 + z * 0.5 * (1.0 + z / 3.0 * (1.0 + z * 0.25 * (1.0 + z * 0.2 * (1.0 + z / 6.0)))))
    return jnp.where(z > -0.3, series, 1.0 - jnp.exp(z))


def _lru_elementwise(xc, pre_a, pre_i, lam):
    ra = _sigmoid(pre_a)
    ri = _sigmoid(pre_i)
    neg = -lam
    softplus = jnp.maximum(neg, 0.0) + jnp.log(1.0 + jnp.exp(-jnp.abs(neg)))
    log_a = -LRU_C * ra * softplus
    a = jnp.exp(log_a)
    mult = jnp.sqrt(_neg_expm1(2.0 * log_a))
    return a, mult * (ri * xc)


def _lru_gates(xl, conv_w, conv_b, wa, wx, ba, bx, lam):
    t = xl.shape[0]
    tt = _tile_rows(t)
    r8 = tt // 8

    def body(prev_ref, x_ref, cw_ref, cb_ref, wa_ref, wx_ref, ba_ref, bx_ref, lam_ref, xc_ref, a_ref, u_ref):
        i = pl.program_id(0)
        prev = jnp.where(i == 0, 0.0, prev_ref[...])
        xc = _conv(prev, x_ref[...], cw_ref, cb_ref)
        xc_ref[...] = xc
        xb = xc.astype(BF16)
        a, u = _lru_elementwise(xc, _dot(xb, wa_ref[...]) + ba_ref[...], _dot(xb, wx_ref[...]) + bx_ref[...],
                                lam_ref[...])
        a_ref[...] = a
        u_ref[...] = u

    row = lambda i: (i, 0)
    vec = _full((1, LRU_W))
    return pl.pallas_call(
        body, name="lru_gates",
        grid=(t // tt,),
        in_specs=[pl.BlockSpec((8, LRU_W), lambda i: (jnp.maximum(i * r8 - 1, 0), 0)),
                  pl.BlockSpec((tt, LRU_W), row), _full((8, LRU_W)), vec,
                  _full((LRU_W, LRU_W)), _full((LRU_W, LRU_W)), vec, vec, vec],
        out_specs=[pl.BlockSpec((tt, LRU_W), row)] * 3,
        out_shape=[jax.ShapeDtypeStruct((t, LRU_W), F32)] * 3,
        compiler_params=_params(),
    )(xl, xl, conv_w, conv_b, wa, wx, ba, bx, lam)


def _lru_scan(a, u, reverse):
    t = a.shape[0]
    nblk = t // 8

    def body(a_ref, u_ref, h_ref):
        def fwd(blk, h):
            base = pl.multiple_of(blk * 8, 8)
            for k in range(8):
                h = a_ref[pl.ds(base + k, 1), :] * h + u_ref[pl.ds(base + k, 1), :]
                h_ref[pl.ds(base + k, 1), :] = h
            return h

        def bwd(n, carry):
            base = pl.multiple_of((nblk - 1 - n) * 8, 8)
            for k in range(7, -1, -1):
                g = u_ref[pl.ds(base + k, 1), :] + carry
                h_ref[pl.ds(base + k, 1), :] = g
                carry = a_ref[pl.ds(base + k, 1), :] * g
            return carry

        lax.fori_loop(0, nblk, bwd if reverse else fwd, jnp.zeros((1, LRU_W), F32))

    return pl.pallas_call(
        body, name="lru_scan_rev" if reverse else "lru_scan",
        in_specs=[WHOLE] * 2, out_specs=WHOLE,
        out_shape=jax.ShapeDtypeStruct((t, LRU_W), F32),
        compiler_params=_params(),
    )(a, u)


def _lru_gates_bwd(dh, h, xc, wa, wx, ba, bx, lam):
    t = dh.shape[0]
    tt = _tile_rows(t)
    r8 = tt // 8

    def body(dh_ref, hprev_ref, h_ref, xc_ref, wa_ref, wx_ref, ba_ref, bx_ref, lam_ref,
             dxc_ref, dwa_ref, dwx_ref, dba_ref, dbx_ref, dlam_ref):
        i = pl.program_id(0)

        @pl.when(i == 0)
        def _():
            for r in (dwa_ref, dwx_ref, dba_ref, dbx_ref, dlam_ref):
                r[...] = jnp.zeros_like(r)

        prev = jnp.where(i == 0, 0.0, hprev_ref[...])
        h_before = pltpu.roll(jnp.concatenate([prev, h_ref[...]], axis=0), 1, 0)[8:8 + tt]
        dh_t = dh_ref[...]
        xc = xc_ref[...]
        xb = xc.astype(BF16)
        pre_a = _dot(xb, wa_ref[...]) + ba_ref[...]
        pre_i = _dot(xb, wx_ref[...]) + bx_ref[...]
        _, vjp = jax.vjp(_lru_elementwise, xc, pre_a, pre_i, lam_ref[...])
        dxc, dpre_a, dpre_i, dlam = vjp((dh_t * h_before, dh_t))
        da_b = dpre_a.astype(BF16)
        di_b = dpre_i.astype(BF16)
        dxc_ref[...] = dxc + _dot_nt(da_b, wa_ref[...]) + _dot_nt(di_b, wx_ref[...])
        dwa_ref[...] += _dot_tn(xb, da_b)
        dwx_ref[...] += _dot_tn(xb, di_b)
        dba_ref[...] += jnp.sum(dpre_a, axis=0, keepdims=True)
        dbx_ref[...] += jnp.sum(dpre_i, axis=0, keepdims=True)
        dlam_ref[...] += dlam

    row = lambda i: (i, 0)
    vec = _full((1, LRU_W))
    sq = _full((LRU_W, LRU_W))
    return pl.pallas_call(
        body, name="lru_gates_bwd",
        grid=(t // tt,),
        in_specs=[pl.BlockSpec((tt, LRU_W), row),
                  pl.BlockSpec((8, LRU_W), lambda i: (jnp.maximum(i * r8 - 1, 0), 0)),
                  pl.BlockSpec((tt, LRU_W), row), pl.BlockSpec((tt, LRU_W), row), sq, sq, vec, vec, vec],
        out_specs=[pl.BlockSpec((tt, LRU_W), row), sq, sq, vec, vec, vec],
        out_shape=[jax.ShapeDtypeStruct((t, LRU_W), F32), jax.ShapeDtypeStruct((LRU_W, LRU_W), F32),
                   jax.ShapeDtypeStruct((LRU_W, LRU_W), F32)] + [jax.ShapeDtypeStruct((1, LRU_W), F32)] * 3,
        compiler_params=_params(),
    )(dh, h, h, xc, wa, wx, ba, bx, lam)


def _conv_bwd(dxc, xl, conv_w):
    t = dxc.shape[0]
    tt = _tile_rows(t)
    r8 = tt // 8
    n_tiles = t // tt
    last = t // 8 - 1

    def body(d_ref, dnext_ref, xprev_ref, x_ref, cw_ref, dxl_ref, dw_ref, db_ref):
        i = pl.program_id(0)

        @pl.when(i == 0)
        def _():
            dw_ref[...] = jnp.zeros_like(dw_ref)
            db_ref[...] = jnp.zeros_like(db_ref)

        d = d_ref[...]
        nxt = jnp.where(i == n_tiles - 1, 0.0, dnext_ref[...])
        dd = jnp.concatenate([d, nxt], axis=0)
        dx = cw_ref[CONV_W - 1:CONV_W, :] * d
        for k in range(CONV_W - 1):
            shift = CONV_W - 1 - k
            dx = dx + cw_ref[k:k + 1, :] * pltpu.roll(dd, tt + 8 - shift, 0)[:tt]
        dxl_ref[...] = dx.astype(BF16)

        prev = jnp.where(i == 0, 0.0, xprev_ref[...])
        xx = jnp.concatenate([prev, x_ref[...]], axis=0)
        rows = []
        for k in range(CONV_W):
            shifted = x_ref[...] if k == CONV_W - 1 else pltpu.roll(xx, CONV_W - 1 - k, 0)[8:8 + tt]
            rows.append(jnp.sum(d * shifted, axis=0, keepdims=True))
        rows.append(jnp.zeros((8 - CONV_W, LRU_W), F32))
        dw_ref[...] += jnp.concatenate(rows, axis=0)
        db_ref[...] += jnp.sum(d, axis=0, keepdims=True)

    row = lambda i: (i, 0)
    return pl.pallas_call(
        body, name="conv_bwd",
        grid=(n_tiles,),
        in_specs=[pl.BlockSpec((tt, LRU_W), row),
                  pl.BlockSpec((8, LRU_W), lambda i: (jnp.minimum((i + 1) * r8, last), 0)),
                  pl.BlockSpec((8, LRU_W), lambda i: (jnp.maximum(i * r8 - 1, 0), 0)),
                  pl.BlockSpec((tt, LRU_W), row), _full((8, LRU_W))],
        out_specs=[pl.BlockSpec((tt, LRU_W), row), _full((8, LRU_W)), _full((1, LRU_W))],
        out_shape=[jax.ShapeDtypeStruct((t, LRU_W), BF16), jax.ShapeDtypeStruct((8, LRU_W), F32),
                   jax.ShapeDtypeStruct((1, LRU_W), F32)],
        compiler_params=_params(),
    )(dxc, dxc, xl, xl, conv_w)


def _rope(x, c, s1, s2):
    n = x.shape[-1]
    return x * c + pltpu.roll(x, n - 16, 1) * s1 + pltpu.roll(x, 16, 1) * s2


def _rope_t(d, c, s1, s2):
    n = d.shape[-1]
    return d * c + pltpu.roll(d * s1, 16, 1) + pltpu.roll(d * s2, n - 16, 1)


def _mla_proj(qlat, kvlat, krope, tables, gq, gkv, wq, wk, wv):
    t = qlat.shape[0]
    tt = _tile_rows(t)

    def body(q_ref, kv_ref, kr_ref, cq, s1q, s2q, ck, s1k, s2k, gq_ref, gkv_ref, wq_ref, wk_ref, wv_ref,
             qo_ref, ko_ref, vo_ref, qn_ref, kva_ref, vt_ref):
        qn = _rms(q_ref[...], gq_ref[...]).astype(BF16)
        kvn = _rms(kv_ref[...], gkv_ref[...]).astype(BF16)
        kr = _rope(kr_ref[...], ck[...], s1k[...], s2k[...]).astype(BF16)
        kva = jnp.concatenate([kvn, kr], axis=1)
        qn_ref[...] = qn
        kva_ref[...] = kva
        q = _dot_nt(qn, wq_ref[...])
        tile = lambda r: jnp.tile(r[...], (1, N_HEADS))
        qo_ref[...] = _rope(q, tile(cq), tile(s1q), tile(s2q)).astype(BF16)
        ko_ref[...] = _dot(kva, wk_ref[...]).astype(BF16)
        v = _dot(kvn, wv_ref[...])
        vo_ref[...] = v.astype(BF16)
        vt_ref[...] = jnp.transpose(v).astype(BF16)

    row = lambda i: (i, 0)
    tab = pl.BlockSpec((tt, HEAD_PAD), row)
    return pl.pallas_call(
        body, name="mla_proj",
        grid=(t // tt,),
        in_specs=[pl.BlockSpec((tt, Q_LORA), row), pl.BlockSpec((tt, KV_LORA), row), tab] + [tab] * 6 + [
            _full((1, Q_LORA)), _full((1, KV_LORA)), _full((HEADS_W, Q_LORA)), _full((KV_LORA + HEAD_PAD, HEADS_W)),
            _full((KV_LORA, HEADS_W))],
        out_specs=[pl.BlockSpec((tt, HEADS_W), row)] * 3 + [pl.BlockSpec((tt, Q_LORA), row),
                                                             pl.BlockSpec((tt, KV_LORA + HEAD_PAD), row),
                                                             pl.BlockSpec((HEADS_W, tt), lambda i: (0, i))],
        out_shape=[jax.ShapeDtypeStruct((t, HEADS_W), BF16)] * 3 + [
            jax.ShapeDtypeStruct((t, Q_LORA), BF16), jax.ShapeDtypeStruct((t, KV_LORA + HEAD_PAD), BF16),
            jax.ShapeDtypeStruct((HEADS_W, t), BF16)],
        compiler_params=_params(),
    )(qlat, kvlat, krope, *tables, gq, gkv, wq, wk, wv)


def _mla_proj_bwd(dq, dk, dv, qlat, kvlat, tables, gq, gkv, wq, wk, wv):
    t = qlat.shape[0]
    tt = _tile_rows(t)

    def body(dq_ref, dk_ref, dv_ref, q_ref, kv_ref, cq, s1q, s2q, ck, s1k, s2k, gq_ref, gkv_ref,
             wq_ref, wk_ref, wv_ref, dqr_ref, dql_ref, dkvl_ref, dkr_ref, dgq_ref, dgkv_ref):
        i = pl.program_id(0)

        @pl.when(i == 0)
        def _():
            dgq_ref[...] = jnp.zeros_like(dgq_ref)
            dgkv_ref[...] = jnp.zeros_like(dgkv_ref)

        tile = lambda r: jnp.tile(r[...], (1, N_HEADS))
        dqr = _rope_t(dq_ref[...], tile(cq), tile(s1q), tile(s2q)).astype(BF16)
        dqr_ref[...] = dqr
        dql, dg = _rms_bwd(q_ref[...], gq_ref[...], _dot(dqr, wq_ref[...]))
        dgq_ref[...] += dg
        dql_ref[...] = dql.astype(BF16)
        dkva = _dot_nt(dk_ref[...], wk_ref[...])
        dkvn = dkva[:, :KV_LORA] + _dot_nt(dv_ref[...], wv_ref[...])
        dkvl, dg = _rms_bwd(kv_ref[...], gkv_ref[...], dkvn)
        dgkv_ref[...] += dg
        dkvl_ref[...] = dkvl.astype(BF16)
        dkr_ref[...] = _rope_t(dkva[:, KV_LORA:], ck[...], s1k[...], s2k[...]).astype(BF16)

    row = lambda i: (i, 0)
    tab = pl.BlockSpec((tt, HEAD_PAD), row)
    wide = pl.BlockSpec((tt, HEADS_W), row)
    return pl.pallas_call(
        body, name="mla_proj_bwd",
        grid=(t // tt,),
        in_specs=[wide, wide, wide, pl.BlockSpec((tt, Q_LORA), row), pl.BlockSpec((tt, KV_LORA), row)] + [tab] * 6 + [
            _full((1, Q_LORA)), _full((1, KV_LORA)), _full((HEADS_W, Q_LORA)), _full((KV_LORA + HEAD_PAD, HEADS_W)),
            _full((KV_LORA, HEADS_W))],
        out_specs=[wide, pl.BlockSpec((tt, Q_LORA), row), pl.BlockSpec((tt, KV_LORA), row), tab,
                   _full((1, Q_LORA)), _full((1, KV_LORA))],
        out_shape=[jax.ShapeDtypeStruct((t, HEADS_W), BF16), jax.ShapeDtypeStruct((t, Q_LORA), BF16),
                   jax.ShapeDtypeStruct((t, KV_LORA), BF16), jax.ShapeDtypeStruct((t, HEAD_PAD), BF16),
                   jax.ShapeDtypeStruct((1, Q_LORA), F32), jax.ShapeDtypeStruct((1, KV_LORA), F32)],
        compiler_params=_params(),
    )(dq, dk, dv, qlat, kvlat, *tables, gq, gkv, wq, wk, wv)


NEG = -1e30


def _chunk_mask(rows, cols, row0):
    r = (lax.broadcasted_iota(jnp.int32, (rows, cols), 0) + row0) // CHUNK
    c = lax.broadcasted_iota(jnp.int32, (rows, cols), 1) // CHUNK
    return c <= r


def _chunk_mask_t(keys, key0, queries):
    kc = (lax.broadcasted_iota(jnp.int32, (keys, queries), 0) + key0) // CHUNK
    qc = lax.broadcasted_iota(jnp.int32, (keys, queries), 1) // CHUNK
    return kc <= qc


def _attn_fwd(q, k, vt, comm):
    t = q.shape[0]
    bq = _tile_rows(t)
    nq = t // bq

    def body(q_ref, k_ref, vt_ref, o_ref, lse_ref, m_ref, l_ref, acc_ref):
        i = pl.program_id(1)
        m_ref[...] = jnp.full_like(m_ref, NEG)
        l_ref[...] = jnp.zeros_like(l_ref)
        acc_ref[...] = jnp.zeros_like(acc_ref)
        qb = q_ref[...]

        def block(first_key, keys, diagonal):
            cols = pl.ds(pl.multiple_of(first_key, bq), keys)
            s = _dot_nt(k_ref[cols, :], qb)
            if diagonal:
                s = jnp.where(_chunk_mask_t(keys, bq - keys, bq), s, NEG)
            m_old = m_ref[...]
            m_new = jnp.maximum(m_old, jnp.max(s, axis=0, keepdims=True))
            alpha = jnp.exp(m_old - m_new)
            p = jnp.exp(s - m_new)
            l_ref[...] = alpha * l_ref[...] + jnp.sum(p, axis=0, keepdims=True)
            acc_ref[...] = alpha * acc_ref[...] + _dot(vt_ref[:, cols], p.astype(BF16))
            m_ref[...] = m_new

        def pair_below(jj, carry):
            block(jj * 2 * bq, 2 * bq, False)
            return carry

        lax.fori_loop(0, i // 2, pair_below, 0)

        @pl.when(i % 2 == 1)
        def _():
            block((i - 1) * bq, 2 * bq, True)

        @pl.when(i % 2 == 0)
        def _():
            block(i * bq, bq, True)

        l = l_ref[...]
        o_ref[...] = jnp.transpose(acc_ref[...] / l).astype(BF16)
        lse_ref[...] = m_ref[...] + jnp.log(l)

    qmap = lambda h, i: (i, h)
    return _call(
        body, comm, name="attn_fwd" if comm is None else "attn_fwd_gather", grid=(N_HEADS, nq),
        in_specs=[pl.BlockSpec((bq, HEAD_PAD), qmap), pl.BlockSpec((t, HEAD_PAD), lambda h, i: (0, h)),
                  pl.BlockSpec((HEAD_PAD, t), lambda h, i: (h, 0))],
        out_specs=[pl.BlockSpec((bq, HEAD_PAD), qmap), pl.BlockSpec((None, 1, bq), lambda h, i: (h, 0, i))],
        out_shape=[jax.ShapeDtypeStruct(q.shape, BF16), jax.ShapeDtypeStruct((N_HEADS, 1, t), F32)],
        scratch_shapes=[pltpu.VMEM((1, bq), F32), pltpu.VMEM((1, bq), F32), pltpu.VMEM((HEAD_PAD, bq), F32)],
        args=(q, k, vt))


def _attn_bwd(q, k, v, o, do, lse, comm):
    t = q.shape[0]
    bq = _tile_rows(t)
    nq = t // bq

    def body(q_ref, k_ref, v_ref, o_ref, do_ref, lse_ref, dq_ref, dk_ref, dv_ref, dk_acc, dv_acc, delta_ref):
        j = pl.program_id(1)

        @pl.when(j == 0)
        def _():
            dq_ref[...] = jnp.zeros_like(dq_ref)
            for blk in range(nq):
                rows = slice(blk * bq, (blk + 1) * bq)
                delta = jnp.sum(do_ref[rows, :].astype(F32) * o_ref[rows, :].astype(F32), axis=-1, keepdims=True)
                delta_ref[:, rows] = jnp.transpose(jnp.broadcast_to(delta, (bq, HEAD_PAD)))[:1, :]

        dk_acc[...] = jnp.zeros_like(dk_acc)
        dv_acc[...] = jnp.zeros_like(dv_acc)
        kb, vb = k_ref[...], v_ref[...]

        def block(first_query, queries, diagonal):
            rows = pl.ds(pl.multiple_of(first_query, bq), queries)
            qb, dob = q_ref[rows, :], do_ref[rows, :]
            s = _dot_nt(kb, qb)
            if diagonal:
                s = jnp.where(_chunk_mask_t(bq, 0, queries), s, NEG)
            p = jnp.exp(s - lse_ref[:, rows])
            dv_acc[...] += _dot(p.astype(BF16), dob)
            dp = _dot_nt(vb, dob)
            ds = (p * (dp - delta_ref[:, rows])).astype(BF16)
            dk_acc[...] += _dot(ds, qb)
            dq_ref[rows, :] += _dot_tn(ds, kb)

        after = nq - 1 - j
        odd = after % 2

        @pl.when(odd == 1)
        def _():
            block(j * bq, 2 * bq, True)

        @pl.when(odd == 0)
        def _():
            block(j * bq, bq, True)

        def pair_above(n, carry):
            block((j + 1 + odd + 2 * n) * bq, 2 * bq, False)
            return carry

        lax.fori_loop(0, after // 2, pair_above, 0)
        dk_ref[...] = dk_acc[...].astype(BF16)
        dv_ref[...] = dv_acc[...].astype(BF16)

    kmap = lambda h, j: (j, h)
    head = lambda h, j: (0, h)
    whole = pl.BlockSpec((t, HEAD_PAD), head)
    return _call(
        body, comm, name="attn_bwd" if comm is None else "attn_bwd_exchange", grid=(N_HEADS, nq),
        in_specs=[whole, pl.BlockSpec((bq, HEAD_PAD), kmap), pl.BlockSpec((bq, HEAD_PAD), kmap), whole, whole,
                  pl.BlockSpec((None, 1, t), lambda h, j: (h, 0, 0))],
        out_specs=[whole, pl.BlockSpec((bq, HEAD_PAD), kmap), pl.BlockSpec((bq, HEAD_PAD), kmap)],
        out_shape=[jax.ShapeDtypeStruct(q.shape, F32), jax.ShapeDtypeStruct(q.shape, BF16),
                   jax.ShapeDtypeStruct(q.shape, BF16)],
        scratch_shapes=[pltpu.VMEM((bq, HEAD_PAD), F32), pltpu.VMEM((bq, HEAD_PAD), F32), pltpu.VMEM((1, t), F32)],
        args=(q, k, v, o, do, lse))


def _mixer_out(h1, hl, gate, o, w_lru, w_mla, g_post):
    t = h1.shape[0]
    tt = _tile_rows(t)

    def body(h1_ref, hl_ref, gate_ref, o_ref, wl_ref, wm_ref, g_ref, h2_ref, m_ref, y_ref):
        y = (hl_ref[...] * _gelu(gate_ref[...])).astype(BF16)
        y_ref[...] = y
        m = _dot(y, wl_ref[...]) + _dot(o_ref[...], wm_ref[...])
        m_ref[...] = m
        h2_ref[...] = h1_ref[...] + _rms(m, g_ref[...])

    row = lambda i: (i, 0)
    return pl.pallas_call(
        body, name="mixer_out",
        grid=(t // tt,),
        in_specs=[pl.BlockSpec((tt, D_MODEL), row), pl.BlockSpec((tt, LRU_W), row), pl.BlockSpec((tt, LRU_W), row),
                  pl.BlockSpec((tt, HEADS_W), row), _full((LRU_W, D_MODEL)), _full((HEADS_W, D_MODEL)),
                  _full((1, D_MODEL))],
        out_specs=[pl.BlockSpec((tt, D_MODEL), row), pl.BlockSpec((tt, D_MODEL), row),
                   pl.BlockSpec((tt, LRU_W), row)],
        out_shape=[jax.ShapeDtypeStruct((t, D_MODEL), F32), jax.ShapeDtypeStruct((t, D_MODEL), F32),
                   jax.ShapeDtypeStruct((t, LRU_W), BF16)],
        compiler_params=_params(),
    )(h1, hl, gate, o, w_lru, w_mla, g_post)


def _mixer_out_bwd(dh2, m, hl, gate, w_lru, w_mla, g_post):
    t = m.shape[0]
    tt = _tile_rows(t)

    def body(dh2_ref, m_ref, hl_ref, gate_ref, wl_ref, wm_ref, g_ref, dm_ref, dgate_ref, dhl_ref, do_ref, dg_ref):
        i = pl.program_id(0)

        @pl.when(i == 0)
        def _():
            dg_ref[...] = jnp.zeros_like(dg_ref)

        dm, dg = _rms_bwd(m_ref[...], g_ref[...], dh2_ref[...])
        dg_ref[...] += dg
        dmb = dm.astype(BF16)
        dm_ref[...] = dmb
        dy = _dot_nt(dmb, wl_ref[...])
        gate = gate_ref[...]
        dgate_ref[...] = (dy * hl_ref[...] * _gelu_grad(gate)).astype(BF16)
        dhl_ref[...] = dy * _gelu(gate)
        do_ref[...] = _dot_nt(dmb, wm_ref[...]).astype(BF16)

    row = lambda i: (i, 0)
    return pl.pallas_call(
        body, name="mixer_out_bwd",
        grid=(t // tt,),
        in_specs=[pl.BlockSpec((tt, D_MODEL), row), pl.BlockSpec((tt, D_MODEL), row), pl.BlockSpec((tt, LRU_W), row),
                  pl.BlockSpec((tt, LRU_W), row), _full((LRU_W, D_MODEL)), _full((HEADS_W, D_MODEL)),
                  _full((1, D_MODEL))],
        out_specs=[pl.BlockSpec((tt, D_MODEL), row), pl.BlockSpec((tt, LRU_W), row), pl.BlockSpec((tt, LRU_W), row),
                   pl.BlockSpec((tt, HEADS_W), row), _full((1, D_MODEL))],
        out_shape=[jax.ShapeDtypeStruct((t, D_MODEL), BF16), jax.ShapeDtypeStruct((t, LRU_W), BF16),
                   jax.ShapeDtypeStruct((t, LRU_W), F32), jax.ShapeDtypeStruct((t, HEADS_W), BF16),
                   jax.ShapeDtypeStruct((1, D_MODEL), F32)],
        compiler_params=_params(),
    )(dh2, m, hl, gate, w_lru, w_mla, g_post)


def _loss_head(y, target):
    t = y.shape[0]
    tt = _tile_rows(t)

    def body(y_ref, t_ref, dy_ref, loss_ref):
        i = pl.program_id(0)

        @pl.when(i == 0)
        def _():
            loss_ref[...] = jnp.zeros_like(loss_ref)

        e = y_ref[...] - t_ref[...]
        dy_ref[...] = e * (1.0 / D_MODEL)
        per_token = jnp.mean(e * e, axis=-1, keepdims=True)
        loss_ref[...] += 0.5 * jnp.sum(per_token, axis=0, keepdims=True)

    row = lambda i: (i, 0)
    return pl.pallas_call(
        body, name="loss_head",
        grid=(t // tt,),
        in_specs=[pl.BlockSpec((tt, D_MODEL), row), pl.BlockSpec((tt, D_MODEL), row)],
        out_specs=[pl.BlockSpec((tt, D_MODEL), row), _full((8, 128))],
        out_shape=[jax.ShapeDtypeStruct((t, D_MODEL), F32), jax.ShapeDtypeStruct((8, 128), F32)],
        compiler_params=_params(),
    )(y, target)


def _adamw(w, g, m, v):
    m = ADAM_B1 * m + (1.0 - ADAM_B1) * g
    v = ADAM_B2 * v + (1.0 - ADAM_B2) * (g * g)
    m_hat = m / (1.0 - ADAM_B1 ** ADAM_STEP)
    v_hat = v / (1.0 - ADAM_B2 ** ADAM_STEP)
    delta = -ADAM_LR * (m_hat / (jnp.sqrt(v_hat) + ADAM_EPS) + ADAM_WD * w)
    return delta, m, v


def _sum_adamw(terms, w, m, v, name):
    r, c = w.shape
    rp = terms.shape[1]
    cb = 256 if c % 256 == 0 else 128

    def body(t_ref, w_ref, m_ref, v_ref, g_out, d_out, m_out, v_out):
        g = t_ref[0, :r, :].astype(F32)
        for q in range(1, N_DEV):
            g = g + t_ref[q, :r, :].astype(F32)
        g_out[...] = g
        d_out[...], m_out[...], v_out[...] = _adamw(w_ref[...], g, m_ref[...], v_ref[...])

    cols = pl.BlockSpec((r, cb), lambda j: (0, j))
    return pl.pallas_call(
        body, name=name, grid=(c // cb,),
        in_specs=[pl.BlockSpec((N_DEV, rp, cb), lambda j: (0, 0, j)), cols, cols, cols], out_specs=[cols] * 4,
        out_shape=[jax.ShapeDtypeStruct((r, c), F32)] * 4,
        compiler_params=_params(),
    )(terms, w, m, v)


def _sum_terms(terms):
    def body(t_ref, out_ref):
        g = t_ref[0]
        for q in range(1, N_DEV):
            g = g + t_ref[q]
        out_ref[...] = g

    return pl.pallas_call(
        body, name="small_grad_sum", in_specs=[WHOLE], out_specs=WHOLE,
        out_shape=jax.ShapeDtypeStruct(terms.shape[1:], F32),
        compiler_params=_params(),
    )(terms)


def _adamw_flat(w, g, m, v):
    def body(w_ref, g_ref, m_ref, v_ref, d_out, m_out, v_out):
        d_out[...], m_out[...], v_out[...] = _adamw(w_ref[...], g_ref[...], m_ref[...], v_ref[...])

    return pl.pallas_call(
        body, name="small_adamw", in_specs=[WHOLE] * 4, out_specs=[WHOLE] * 3,
        out_shape=[jax.ShapeDtypeStruct(w.shape, F32)] * 3,
        compiler_params=_params(),
    )(w, g, m, v)


def _pack(arrays):
    flat = jnp.concatenate([a.reshape(-1).astype(F32) for a in arrays])
    rows = -(-flat.shape[0] // 1024) * 8
    return jnp.pad(flat, (0, rows * 128 - flat.shape[0])).reshape(rows, 128)


def _unpack(packed, shapes):
    flat = packed.reshape(-1)
    out, pos = [], 0
    for s in shapes:
        n = math.prod(s)
        out.append(flat[pos:pos + n].reshape(s))
        pos += n
    return out


def _rope_tables(positions):
    t = positions.shape[-1]
    inv_freq = 1.0 / (ROPE_THETA ** (jnp.arange(0, QK_ROPE, 2, dtype=F32) / QK_ROPE))
    ang = positions.reshape(t, 1).astype(F32) * inv_freq
    cos, sin = jnp.cos(ang), jnp.sin(ang)
    z16 = jnp.zeros((t, 16), F32)

    def place(first, second, lead, lead_value):
        parts = [jnp.full((t, lead), lead_value, F32)] if lead else []
        parts += [first, second, jnp.zeros((t, HEAD_PAD - lead - 32), F32)]
        return jnp.concatenate(parts, axis=1)

    q_tabs = [place(cos, cos, QK_NOPE, 1.0) * ATT_SCALE, place(-sin, z16, QK_NOPE, 0.0) * ATT_SCALE,
              place(z16, sin, QK_NOPE, 0.0) * ATT_SCALE]
    k_tabs = [place(cos, cos, 0, 0.0), place(-sin, z16, 0, 0.0), place(z16, sin, 0, 0.0)]
    return q_tabs + k_tabs


def _block_diag(w):
    eye = jnp.eye(w.shape[0], dtype=w.dtype)
    return jnp.einsum("nde,nm->ndme", w, eye).reshape(LRU_W, LRU_W)


def _diag_blocks(g):
    g4 = g.reshape(8, 64, 8, 64)
    return jnp.moveaxis(jnp.diagonal(g4, axis1=0, axis2=2), -1, 0)


def _mixer_layouts(got):
    full = {}
    w_in = got["w_in"][:, :IN_SHARD].reshape(IN_COLS, D_MODEL)
    full["w_in"] = jnp.pad(w_in, ((0, IN_PAD - IN_COLS), (0, 0)))
    full["wq"] = got["w_q_b"].reshape(HEADS_W, Q_LORA)
    kvb = got["w_kv_b"]
    zeros = jnp.zeros((N_HEADS, KV_LORA, HEAD_PAD - QK_NOPE), BF16)
    k_part = jnp.transpose(jnp.concatenate([kvb[:, :, :QK_NOPE], zeros], axis=2), (1, 0, 2)).reshape(KV_LORA, HEADS_W)
    rope_rows = jnp.pad(jnp.eye(QK_ROPE, dtype=BF16), ((0, HEAD_PAD - QK_ROPE), (QK_NOPE, HEAD_PAD - QK_NOPE - QK_ROPE)))
    full["wk"] = jnp.concatenate([k_part, jnp.tile(rope_rows, (1, N_HEADS))], axis=0)
    full["wv"] = jnp.transpose(jnp.concatenate([kvb[:, :, QK_NOPE:], zeros], axis=2), (1, 0, 2)).reshape(KV_LORA, HEADS_W)
    w_out = got["w_out"].reshape(D_MODEL, D_MODEL)
    full["w_out_lru"] = w_out[:LRU_W]
    full["w_out_mla"] = jnp.pad(w_out[LRU_W:].reshape(N_HEADS, V_DIM, D_MODEL),
                                ((0, 0), (0, HEAD_PAD - V_DIM), (0, 0))).reshape(HEADS_W, D_MODEL)
    conv_w = jnp.transpose(got["conv_w"][:, :CONV_W, :LRU_W // N_DEV], (1, 0, 2)).reshape(CONV_W, LRU_W)
    full["conv_w"] = jnp.pad(conv_w, ((0, 8 - CONV_W), (0, 0)))
    return full


_FFN1 = ["w_ffn1_gate", "w_ffn1_up", "w_ffn1_down"]
_MIX = ["w_in", "w_q_b", "w_kv_b", "w_out"]
_FFN2 = ["w_ffn2_gate", "w_ffn2_up", "w_ffn2_down"]
_SHARDED = _FFN1 + _MIX + _FFN2
_TRANSPOSED = ("w_ffn1_gate", "w_ffn1_up", "w_in", "w_q_b", "w_ffn2_gate", "w_ffn2_up")
_SMALL = ["g_ffn1_pre", "g_ffn1_post", "g_mix_pre", "g_mix_post", "conv_b", "w_lru_a", "b_lru_a", "w_lru_x",
          "b_lru_x", "lru_lambda", "q_a_norm", "kv_a_norm", "g_ffn2_pre", "g_ffn2_post", "conv_w"]
_ORDER = ["g_ffn1_pre", "g_ffn1_post", "w_ffn1_gate", "w_ffn1_up", "w_ffn1_down", "g_mix_pre", "g_mix_post", "w_in",
          "conv_w", "conv_b", "w_lru_a", "b_lru_a", "w_lru_x", "b_lru_x", "lru_lambda", "q_a_norm", "w_q_b",
          "kv_a_norm", "w_kv_b", "w_out", "g_ffn2_pre", "g_ffn2_post", "w_ffn2_gate", "w_ffn2_up", "w_ffn2_down"]
_FF_BLOCK = (FF_PAD, D_MODEL)
_PADS = {"w_ffn1_gate": _FF_BLOCK, "w_ffn1_up": _FF_BLOCK, "w_ffn1_down": _FF_BLOCK,
         "w_in": (IN_SHARD_PAD, D_MODEL), "w_q_b": (HEAD_PAD, Q_LORA), "w_kv_b": (KV_LORA, HEAD_PAD),
         "w_out": (HEAD_PAD, D_MODEL), "conv_w": (8, 128),
         "w_ffn2_gate": _FF_BLOCK, "w_ffn2_up": _FF_BLOCK, "w_ffn2_down": _FF_BLOCK}


def _rows(name, a):
    return a.T if name in _TRANSPOSED else a


def _step(args):
    x = args["x"][0]
    target = args["loss_target"][0]
    positions = args["positions"][0]
    w = {n: args[n][0] for n in _ORDER}
    mom = {n: args["m_" + n][0] for n in _ORDER}
    var = {n: args["v_" + n][0] for n in _ORDER}
    vec = lambda name: w[name].reshape(1, -1)
    t = x.shape[0]
    ff_blocks = lambda a: a.reshape(N_DEV, FF_PAD, D_MODEL)
    ff_rows = lambda a: a.reshape(FF_VIRT, D_MODEL)

    names = _FFN1 + _MIX + ["conv_w"] + _FFN2
    (wg1, wu1), staged = _first_gather([_rows(n, w[n]) for n in names], [_PADS[n] for n in names],
                                       [F32 if n == "conv_w" else BF16 for n in names], 2)
    wg1, wu1 = ff_rows(wg1), ff_rows(wu1)
    n_mix = 1 + len(_MIX) + 1

    tables = _rope_tables(positions)
    wa = _block_diag(w["w_lru_a"]).astype(BF16)
    wx = _block_diag(w["w_lru_x"]).astype(BF16)

    (n1, gt1, up1), got = _ffn_fwd_hidden(x, vec("g_ffn1_pre"), wg1, wu1, _Gather(staged[:n_mix]),
                                          "ffn_fwd_hidden_gather")
    wd1 = ff_rows(got[0])
    h1, f1 = _ffn_fwd_out(x, gt1, up1, vec("g_ffn1_post"), wd1, "ffn_fwd_out")
    mix = _mixer_layouts(dict(zip(_MIX + ["conv_w"], got[1:])))
    nmix, xl, gate, qlat, kvlat, krope = _mixer_in(h1, vec("g_mix_pre"), mix["w_in"])
    xc, a, u = _lru_gates(xl, mix["conv_w"], vec("conv_b"), wa, wx, vec("b_lru_a"), vec("b_lru_x"), vec("lru_lambda"))
    hl = _lru_scan(a, u, reverse=False)
    q, k, v, qn, kva, vt = _mla_proj(qlat, kvlat, krope, tables, vec("q_a_norm"), vec("kv_a_norm"),
                                 mix["wq"], mix["wk"], mix["wv"])
    (o, lse), ffn2 = _attn_fwd(q, k, vt, _Gather(staged[n_mix:]))
    wg2, wu2, wd2 = [ff_rows(a_) for a_ in ffn2]
    h2, m, ylru = _mixer_out(h1, hl, gate, o, mix["w_out_lru"], mix["w_out_mla"], vec("g_mix_post"))
    (h3, f2, n2, gt2, up2), _ = _ffn_fwd(h2, vec("g_ffn2_pre"), vec("g_ffn2_post"), wg2, wu2, wd2, None, "ffn_fwd")
    dy, loss_tile = _loss_head(h3, target)

    (df2, act2, dgt2, dup2, dg2post), _ = _ffn_bwd_hidden(dy, f2, gt2, up2, vec("g_ffn2_post"), wd2, None,
                                                          "ffn_bwd_hidden")
    (dh2, dg2pre), _ = _ffn_bwd_input(dy, h2, dgt2, dup2, vec("g_ffn2_pre"), wg2, wu2, None, "ffn_bwd_input")
    dw_ffn = lambda a_, b_, name, comm=None: _mm_tn(a_, b_, FF_VIRT, FF_CHUNK, D_MODEL, name, comm)
    big2 = [ff_blocks(dw_ffn(dgt2, n2, "dw_ffn_gate")[0]), ff_blocks(dw_ffn(dup2, n2, "dw_ffn_up")[0]),
            ff_blocks(dw_ffn(act2, df2, "dw_ffn_down")[0])]

    dm, dgate, dhl, do, dgmixpost = _mixer_out_bwd(dh2, m, hl, gate, mix["w_out_lru"], mix["w_out_mla"],
                                                   vec("g_mix_post"))
    dw_out_lru, _ = _mm_tn(ylru, dm, LRU_W, LRU_W, D_MODEL, "dw_out_lru")
    dw_out_mla, _ = _mm_tn(o, dm, HEADS_W, HEADS_W, D_MODEL, "dw_out_mla")
    dw_out = jnp.concatenate(
        [dw_out_lru, dw_out_mla.reshape(N_HEADS, HEAD_PAD, D_MODEL)[:, :V_DIM].reshape(N_HEADS * V_DIM, D_MODEL)], axis=0)

    (dq, dk, dv), terms2 = _attn_bwd(q, k, v, o, do, lse, _Exchange(big2))
    dqr, dqlat, dkvlat, dkrope, dgq, dgkv = _mla_proj_bwd(
        dq, dk, dv, qlat, kvlat, tables, vec("q_a_norm"), vec("kv_a_norm"), mix["wq"], mix["wk"], mix["wv"])
    dwq, _ = _mm_tn(dqr, qn, HEADS_W, HEADS_W, Q_LORA, "dw_q_b")
    dwk, _ = _mm_tn(kva, dk, KV_LORA, KV_LORA, HEADS_W, "dw_kv_b_k")
    dwv, _ = _mm_tn(kva, dv, KV_LORA, KV_LORA, HEADS_W, "dw_kv_b_v")
    dw_kvb = jnp.transpose(jnp.concatenate(
        [dwk.reshape(KV_LORA, N_HEADS, HEAD_PAD)[:, :, :QK_NOPE], dwv.reshape(KV_LORA, N_HEADS, HEAD_PAD)[:, :, :V_DIM]],
        axis=2), (1, 0, 2))

    dh = _lru_scan(a, dhl, reverse=True)
    dxc, dwa, dwx, dba, dbx, dlam = _lru_gates_bwd(dh, hl, xc, wa, wx, vec("b_lru_a"), vec("b_lru_x"),
                                                   vec("lru_lambda"))
    dxl, dconv_w8, dconv_b = _conv_bwd(dxc, xl, mix["conv_w"])
    dh1, dproj, dgmixpre = _mixer_in_bwd(dh2, h1, vec("g_mix_pre"), mix["w_in"], [dxl, dgate, dqlat, dkvlat, dkrope])
    dw_in, _ = _mm_tn(dproj, nmix, IN_PAD, IN_PAD // 2, D_MODEL, "dw_in")
    dw_in = jnp.pad(dw_in[:IN_COLS].reshape(N_DEV, IN_SHARD, D_MODEL), ((0, 0), (0, IN_SHARD_PAD - IN_SHARD), (0, 0)))
    big_mix = [dw_in, dwq.reshape(N_DEV, HEAD_PAD, Q_LORA), dw_kvb, dw_out.reshape(N_DEV, HEAD_PAD, D_MODEL)]

    (df1, act1, dgt1, dup1, dg1post), terms_mix = _ffn_bwd_hidden(
        dh1, f1, gt1, up1, vec("g_ffn1_post"), wd1, _Exchange(big_mix), "ffn_bwd_hidden_exchange")
    dwd1, _ = dw_ffn(act1, df1, "dw_ffn_down")
    dwg1, terms_d1 = dw_ffn(dgt1, n1, "dw_ffn_gate_exchange", _Exchange([ff_blocks(dwd1)]))
    dwu1, terms_g1 = dw_ffn(dup1, n1, "dw_ffn_up_exchange", _Exchange([ff_blocks(dwg1)]))
    (dx, dg1pre), terms_u1 = _ffn_bwd_input(dh1, x, dgt1, dup1, vec("g_ffn1_pre"), wg1, wu1,
                                            _Exchange([ff_blocks(dwu1)]), "ffn_bwd_input_exchange")

    small = {"g_ffn1_pre": dg1pre, "g_ffn1_post": dg1post, "g_mix_pre": dgmixpre, "g_mix_post": dgmixpost,
             "conv_b": dconv_b, "w_lru_a": _diag_blocks(dwa), "b_lru_a": dba, "w_lru_x": _diag_blocks(dwx),
             "b_lru_x": dbx, "lru_lambda": dlam, "q_a_norm": dgq, "kv_a_norm": dgkv, "g_ffn2_pre": dg2pre,
             "g_ffn2_post": dg2post, "conv_w": dconv_w8[:CONV_W]}
    packed = _pack([small[n] for n in _SMALL] + [loss_tile[:1, :1]])
    small_terms, = _last_exchange([jnp.broadcast_to(packed[None], (N_DEV,) + packed.shape)])

    terms = dict(zip(_FFN2, terms2))
    terms.update(zip(_MIX, terms_mix))
    terms.update({"w_ffn1_down": terms_d1[0], "w_ffn1_gate": terms_g1[0], "w_ffn1_up": terms_u1[0]})

    grads, delta, new_m, new_v = {}, {}, {}, {}
    for n in _SHARDED:
        res = _sum_adamw(terms[n], _rows(n, w[n]), _rows(n, mom[n]), _rows(n, var[n]), "adamw_" + n)
        grads[n], delta[n], new_m[n], new_v[n] = [_rows(n, r) for r in res]

    summed = _sum_terms(small_terms)
    small_sum = dict(zip(_SMALL + ["loss"], _unpack(summed, [small[n].shape for n in _SMALL] + [(1, 1)])))
    loss = small_sum.pop("loss").reshape(())
    me = 4 * lax.axis_index("x") + 2 * lax.axis_index("y") + lax.axis_index("c")
    cshard = LRU_W // N_DEV
    small_sum["conv_w"] = lax.dynamic_slice(small_sum["conv_w"], (0, me * cshard), (CONV_W, cshard))
    for n in _SMALL:
        grads[n] = small_sum[n].reshape(w[n].shape)
    shapes = [w[n].shape for n in _SMALL]
    d_p, m_p, v_p = _adamw_flat(_pack([w[n] for n in _SMALL]), _pack([grads[n] for n in _SMALL]),
                                _pack([mom[n] for n in _SMALL]), _pack([var[n] for n in _SMALL]))
    for n, d_, m_, v_ in zip(_SMALL, _unpack(d_p, shapes), _unpack(m_p, shapes), _unpack(v_p, shapes)):
        delta[n], new_m[n], new_v[n] = d_, m_, v_

    lead = lambda d: [d[n][None] for n in _ORDER]
    return (loss, dx[None], *lead(grads), *lead(delta), *lead(new_m), *lead(new_v))


def kernel(x, positions, g_ffn1_pre, g_ffn1_post, w_ffn1_gate, w_ffn1_up, w_ffn1_down, g_mix_pre, g_mix_post, w_in, conv_w, conv_b, w_lru_a, b_lru_a, w_lru_x, b_lru_x, lru_lambda, q_a_norm, w_q_b, kv_a_norm, w_kv_b, w_out, g_ffn2_pre, g_ffn2_post, w_ffn2_gate, w_ffn2_up, w_ffn2_down, loss_target, m_g_ffn1_pre, m_g_ffn1_post, m_w_ffn1_gate, m_w_ffn1_up, m_w_ffn1_down, m_g_mix_pre, m_g_mix_post, m_w_in, m_conv_w, m_conv_b, m_w_lru_a, m_b_lru_a, m_w_lru_x, m_b_lru_x, m_lru_lambda, m_q_a_norm, m_w_q_b, m_kv_a_norm, m_w_kv_b, m_w_out, m_g_ffn2_pre, m_g_ffn2_post, m_w_ffn2_gate, m_w_ffn2_up, m_w_ffn2_down, v_g_ffn1_pre, v_g_ffn1_post, v_w_ffn1_gate, v_w_ffn1_up, v_w_ffn1_down, v_g_mix_pre, v_g_mix_post, v_w_in, v_conv_w, v_conv_b, v_w_lru_a, v_b_lru_a, v_w_lru_x, v_b_lru_x, v_lru_lambda, v_q_a_norm, v_w_q_b, v_kv_a_norm, v_w_kv_b, v_w_out, v_g_ffn2_pre, v_g_ffn2_post, v_w_ffn2_gate, v_w_ffn2_up, v_w_ffn2_down):
    return _step(dict(locals()))
```

```python
import functools
import math
import operator

import jax
import jax.numpy as jnp
from jax import lax
from jax.experimental import pallas as pl
from jax.experimental.pallas import tpu as pltpu

F32 = jnp.float32
BF16 = jnp.bfloat16
MESH = pl.DeviceIdType.MESH

N_DEV = 8
D_MODEL = 1024
D_FF = 2816
FF_SHARD = D_FF // N_DEV
FF_PAD = 384
FF_VIRT = N_DEV * FF_PAD
FF_CHUNK = 2 * FF_PAD
LRU_W = 512
N_HEADS = 8
HEAD_PAD = 128
HEADS_W = N_HEADS * HEAD_PAD
QK_NOPE = 64
QK_ROPE = 32
V_DIM = 64
Q_LORA = 384
KV_LORA = 256
IN_COLS = 2 * LRU_W + Q_LORA + KV_LORA + QK_ROPE
IN_SHARD = IN_COLS // N_DEV
IN_SHARD_PAD = 256
IN_PAD = 1792
QB_SHARD = 96
CONV_W = 4
CHUNK = 64
EPS = 1e-6
LRU_C = 8.0
ROPE_THETA = 10000.0
ATT_SCALE = (QK_NOPE + QK_ROPE) ** -0.5

ADAM_LR = 0.001
ADAM_B1 = 0.9
ADAM_B2 = 0.999
ADAM_EPS = 1e-08
ADAM_WD = 0.01
ADAM_STEP = 10

VMEM_LIMIT = 56 * 1024 * 1024
ANY = pl.BlockSpec(memory_space=pl.ANY)
WHOLE = pl.BlockSpec(memory_space=pltpu.VMEM)


def _params(**kw):
    return pltpu.CompilerParams(vmem_limit_bytes=VMEM_LIMIT, **kw)


def _full(shape):
    return pl.BlockSpec(shape, lambda *_: (0,) * len(shape))


def _dot(a, b):
    return jnp.dot(a, b, preferred_element_type=F32)


def _dot_nt(a, b):
    return lax.dot_general(a, b, (((1,), (1,)), ((), ())), preferred_element_type=F32)


def _dot_tn(a, b):
    return lax.dot_general(a, b, (((0,), (0,)), ((), ())), preferred_element_type=F32)


def _rms(x, g):
    r = lax.rsqrt(jnp.mean(x * x, axis=-1, keepdims=True) + EPS)
    return x * r * g


def _rms_bwd(x, g, dy):
    r = lax.rsqrt(jnp.mean(x * x, axis=-1, keepdims=True) + EPS)
    xh = x * r
    dg = jnp.sum(dy * xh, axis=0, keepdims=True)
    dxh = dy * g
    dx = r * (dxh - xh * jnp.mean(dxh * xh, axis=-1, keepdims=True))
    return dx, dg


def _sigmoid(x):
    return 0.5 * jnp.tanh(0.5 * x) + 0.5


_GELU_C = math.sqrt(2.0 / math.pi)


def _gelu(x):
    t = jnp.tanh(_GELU_C * (x + 0.044715 * x * x * x))
    return 0.5 * x * (1.0 + t)


def _gelu_grad(x):
    t = jnp.tanh(_GELU_C * (x + 0.044715 * x * x * x))
    return 0.5 * (1.0 + t) + 0.5 * x * (1.0 - t * t) * _GELU_C * (1.0 + 3.0 * 0.044715 * x * x)


def _tile_rows(t):
    return 512 if t >= 2048 else t // 2


def _dev_index(p):
    return 4 * p[0] + 2 * p[1] + p[2]


def _place():
    x, y, c = lax.axis_index("x"), lax.axis_index("y"), lax.axis_index("c")
    return (x, y, c), (x, y, 1 - c), [(1 - x, y), (x, 1 - y), (1 - x, 1 - y)]


def _dma_sems(n):
    return [pltpu.SemaphoreType.DMA((7 * n,)), pltpu.SemaphoreType.DMA((7 * n,)), pltpu.SemaphoreType.DMA((n,))]


class _Gather:
    has_middle = True

    def __init__(self, arrays):
        self.inputs = list(arrays)
        self.n = len(arrays)
        self.out_shape = [jax.ShapeDtypeStruct((N_DEV,) + a.shape, a.dtype) for a in arrays]
        self.scratch = _dma_sems(self.n)

    @staticmethod
    def _copy(outs, sems, k, s, block, to, src=None):
        rows = outs[k].at[_dev_index(block)]
        return pltpu.make_async_remote_copy(
            src_ref=rows if src is None else src, dst_ref=rows,
            send_sem=sems[0].at[7 * k + s], recv_sem=sems[1].at[7 * k + s], device_id=to, device_id_type=MESH)

    def _first(self, srcs, outs, sems):
        me, sibling, chips = _place()
        mine = [pltpu.make_async_copy(srcs[k], outs[k].at[_dev_index(me)], sems[2].at[k]) for k in range(self.n)]
        sends = []
        for k in range(self.n):
            sends.append(self._copy(outs, sems, k, 0, me, sibling, src=srcs[k]))
            sends += [self._copy(outs, sems, k, 1 + j, me, (*chip, me[2]), src=srcs[k])
                      for j, chip in enumerate(chips)]
        return mine, sends

    def _passed(self, outs, sems):
        me, sibling, chips = _place()
        return [[self._copy(outs, sems, k, 4 + j, (*chip, me[2]), sibling) for k in range(self.n)]
                for j, chip in enumerate(chips)]

    def start(self, srcs, outs, sems):
        mine, sends = self._first(srcs, outs, sems)
        for cp in mine + sends:
            cp.start()

    def middle(self, srcs, outs, sems):
        me, sibling, chips = _place()
        passed = self._passed(outs, sems)
        for j, chip in enumerate(chips):
            for k in range(self.n):
                self._copy(outs, sems, k, 1 + j, (*chip, me[2]), me).wait_recv()
                passed[j][k].start()

    def finish(self, srcs, outs, sems):
        me, sibling, chips = _place()
        for k in range(self.n):
            self._copy(outs, sems, k, 0, sibling, me).wait_recv()
        for j, chip in enumerate(chips):
            for k in range(self.n):
                self._copy(outs, sems, k, 4 + j, (*chip, 1 - me[2]), me).wait_recv()
        mine, sends = self._first(srcs, outs, sems)
        for cp in sends + [cp for row in self._passed(outs, sems) for cp in row]:
            cp.wait_send()
        for cp in mine:
            cp.wait()


class _Exchange:
    has_middle = False

    def __init__(self, arrays):
        self.inputs = list(arrays)
        self.n = len(arrays)
        self.out_shape = [jax.ShapeDtypeStruct(a.shape, a.dtype) for a in arrays]
        self.scratch = _dma_sems(self.n)

    def _copies(self, srcs, outs, sems):
        (x, y, c), _, _ = _place()
        me = _dev_index((x, y, c))
        local = [pltpu.make_async_copy(srcs[k].at[me], outs[k].at[me], sems[2].at[k]) for k in range(self.n)]
        remote = []
        for s in range(1, N_DEV):
            peer = (1 - x if s & 4 else x, 1 - y if s & 2 else y, 1 - c if s & 1 else c)
            for k in range(self.n):
                remote.append(pltpu.make_async_remote_copy(
                    src_ref=srcs[k].at[_dev_index(peer)], dst_ref=outs[k].at[me],
                    send_sem=sems[0].at[7 * k + s - 1], recv_sem=sems[1].at[7 * k + s - 1],
                    device_id=peer, device_id_type=MESH))
        return local, remote

    def start(self, srcs, outs, sems):
        local, remote = self._copies(srcs, outs, sems)
        for cp in local + remote:
            cp.start()

    def finish(self, srcs, outs, sems):
        local, remote = self._copies(srcs, outs, sems)
        for cp in remote:
            cp.wait_recv()
        for cp in remote:
            cp.wait_send()
        for cp in local:
            cp.wait()


class _PairSwap:
    has_middle = False

    def __init__(self, arrays):
        self.inputs = list(arrays)
        self.n = len(arrays)
        self.out_shape = [jax.ShapeDtypeStruct((4,) + a.shape[1:], a.dtype) for a in arrays]
        self.scratch = _dma_sems(self.n)

    def _copies(self, srcs, outs, sems):
        (x, y, c), sibling, _ = _place()
        return [pltpu.make_async_remote_copy(
            src_ref=srcs[k].at[2 * q + 1 - c], dst_ref=outs[k].at[q],
            send_sem=sems[0].at[7 * k + q], recv_sem=sems[1].at[7 * k + q], device_id=sibling, device_id_type=MESH)
            for q in range(4) for k in range(self.n)]

    def start(self, srcs, outs, sems):
        for cp in self._copies(srcs, outs, sems):
            cp.start()

    def finish(self, srcs, outs, sems):
        copies = self._copies(srcs, outs, sems)
        for cp in copies:
            cp.wait_recv()
        for cp in copies:
            cp.wait_send()


class _ChipExchange:
    has_middle = False

    def __init__(self, arrays):
        self.inputs = list(arrays)
        self.n = len(arrays)
        self.out_shape = [jax.ShapeDtypeStruct(a.shape, a.dtype) for a in arrays]
        self.scratch = _dma_sems(self.n)

    def _copies(self, srcs, outs, sems):
        (x, y, c), _, _ = _place()
        mine = 2 * x + y
        local = [pltpu.make_async_copy(srcs[k].at[mine], outs[k].at[mine], sems[2].at[k]) for k in range(self.n)]
        remote = []
        for s in range(1, 4):
            px, py = (1 - x if s & 2 else x), (1 - y if s & 1 else y)
            for k in range(self.n):
                remote.append(pltpu.make_async_remote_copy(
                    src_ref=srcs[k].at[2 * px + py], dst_ref=outs[k].at[mine],
                    send_sem=sems[0].at[7 * k + s], recv_sem=sems[1].at[7 * k + s],
                    device_id=(px, py, c), device_id_type=MESH))
        return local, remote

    start = _Exchange.start
    finish = _Exchange.finish


class _Several:
    def __init__(self, comms):
        self.comms = comms
        self.inputs = [a for cm in comms for a in cm.inputs]
        self.n = len(self.inputs)
        self.out_shape = [s for cm in comms for s in cm.out_shape]
        self.scratch = [s for cm in comms for s in cm.scratch]
        self.has_middle = any(cm.has_middle for cm in comms)

    def _each(self, method, srcs, outs, sems):
        pos = 0
        for i, cm in enumerate(self.comms):
            if hasattr(cm, method):
                getattr(cm, method)(srcs[pos:pos + cm.n], outs[pos:pos + cm.n], sems[3 * i:3 * i + 3])
            pos += cm.n

    def start(self, srcs, outs, sems):
        self._each("start", srcs, outs, sems)

    def middle(self, srcs, outs, sems):
        self._each("middle", srcs, outs, sems)

    def finish(self, srcs, outs, sems):
        self._each("finish", srcs, outs, sems)


def _call(body, comm, *, name, grid, in_specs, out_specs, out_shape, scratch_shapes=(), args):
    in_specs, out_specs, out_shape = list(in_specs), list(out_specs), list(out_shape)
    scratch_shapes = list(scratch_shapes)
    if comm is None:
        outs = pl.pallas_call(body, name=name, grid=grid, in_specs=in_specs, out_specs=out_specs, out_shape=out_shape,
                              scratch_shapes=scratch_shapes, compiler_params=_params())(*args)
        return list(outs), []
    n_in, n_out, n_scr, c_n = len(in_specs), len(out_specs), len(scratch_shapes), comm.n
    middle = tuple(g - 1 if d == 0 else 0 for d, g in enumerate(grid))

    def hosted(*refs):
        pos = 0
        parts = []
        for width in (n_in, c_n, n_out, c_n, n_scr, len(comm.scratch)):
            parts.append(refs[pos:pos + width])
            pos += width
        ins, c_in, outs, c_out, scr, sems = parts
        ids = [pl.program_id(d) for d in range(len(grid))]
        at = lambda where: functools.reduce(operator.and_, [i == w for i, w in zip(ids, where)])

        @pl.when(at([0] * len(grid)))
        def _():
            comm.start(c_in, c_out, sems)

        body(*ins, *outs, *scr)

        if comm.has_middle:
            @pl.when(at(middle))
            def _():
                comm.middle(c_in, c_out, sems)

        @pl.when(at([g - 1 for g in grid]))
        def _():
            comm.finish(c_in, c_out, sems)

    outs = pl.pallas_call(
        hosted, name=name, grid=grid,
        in_specs=in_specs + [ANY] * c_n, out_specs=out_specs + [ANY] * c_n,
        out_shape=out_shape + comm.out_shape, scratch_shapes=scratch_shapes + comm.scratch,
        compiler_params=_params())(*args, *comm.inputs)
    return list(outs[:n_out]), list(outs[n_out:])


def _first_gather(shards, pads, dtypes, n_gathered):
    n = len(shards)
    gather = _Gather([jax.ShapeDtypeStruct(tuple(pads[k]), dtypes[k]) for k in range(n_gathered)])

    def body(*refs):
        ins, outs = refs[:n], refs[n:2 * n]
        stages = list(refs[2 * n:2 * n + n_gathered]) + list(outs[n_gathered:])
        sems = refs[2 * n + n_gathered:]
        for k in range(n):
            r, cc = ins[k].shape
            if (r, cc) != tuple(stages[k].shape):
                stages[k][...] = jnp.zeros(stages[k].shape, stages[k].dtype)
            stages[k][:r, :cc] = ins[k][...].astype(stages[k].dtype)
        gather.start(stages[:n_gathered], outs[:n_gathered], sems)
        gather.middle(stages[:n_gathered], outs[:n_gathered], sems)
        gather.finish(stages[:n_gathered], outs[:n_gathered], sems)

    staged_shape = [jax.ShapeDtypeStruct(tuple(pads[k]), dtypes[k]) for k in range(n_gathered, n)]
    outs = pl.pallas_call(
        body, name="first_gather",
        out_shape=gather.out_shape + staged_shape,
        in_specs=[WHOLE] * n,
        out_specs=[ANY] * n_gathered + [WHOLE] * (n - n_gathered),
        scratch_shapes=[pltpu.VMEM(tuple(pads[k]), dtypes[k]) for k in range(n_gathered)] + gather.scratch,
        compiler_params=_params(),
    )(*shards)
    return list(outs[:n_gathered]), list(outs[n_gathered:])


def _last_exchange(comm):
    def body(*refs):
        srcs, outs, sems = refs[:comm.n], refs[comm.n:2 * comm.n], refs[2 * comm.n:]
        comm.start(srcs, outs, sems)
        comm.finish(srcs, outs, sems)

    return pl.pallas_call(
        body, name="last_exchange", out_shape=comm.out_shape, in_specs=[ANY] * comm.n, out_specs=[ANY] * comm.n,
        scratch_shapes=comm.scratch, compiler_params=_params())(*comm.inputs)


def _pair_sum(blocks, got):
    _, r, c = blocks.shape
    mine = lax.dynamic_index_in_dim(blocks.reshape(4, 2, r, c), lax.axis_index("c"), axis=1, keepdims=False)

    def body(a_ref, b_ref, out_ref):
        out_ref[...] = (a_ref[...].astype(F32) + b_ref[...].astype(F32)).astype(out_ref.dtype)

    spec = pl.BlockSpec((None, r, c), lambda q: (q, 0, 0))
    return pl.pallas_call(
        body, name="pair_sum", grid=(4,), in_specs=[spec, spec], out_specs=spec,
        out_shape=jax.ShapeDtypeStruct((4, r, c), blocks.dtype), compiler_params=_params())(mine, got)


def _ffn_fwd(h0, g_pre, g_post, wg, wu, wd, comm, name):
    t = h0.shape[0]
    tt = _tile_rows(t)
    n_chunks = FF_VIRT // FF_CHUNK

    def body(h0_ref, gpre_ref, gpost_ref, wg_ref, wu_ref, wd_ref,
             h1_ref, f_ref, n1_ref, gt_ref, up_ref, acc_ref, n1s_ref):
        j = pl.program_id(1)

        @pl.when(j == 0)
        def _():
            n1 = _rms(h0_ref[...], gpre_ref[...]).astype(BF16)
            n1s_ref[...] = n1
            n1_ref[...] = n1
            acc_ref[...] = jnp.zeros_like(acc_ref)

        n1 = n1s_ref[...]
        gt = _dot_nt(n1, wg_ref[...])
        up = _dot_nt(n1, wu_ref[...])
        gt_ref[...] = gt.astype(BF16)
        up_ref[...] = up.astype(BF16)
        act = (gt * _sigmoid(gt) * up).astype(BF16)
        acc_ref[...] += _dot(act, wd_ref[...])

        @pl.when(j == n_chunks - 1)
        def _():
            f = acc_ref[...]
            f_ref[...] = f
            h1_ref[...] = h0_ref[...] + 0.5 * _rms(f, gpost_ref[...])

    row = lambda i, j: (i, 0)
    wspec = pl.BlockSpec((FF_CHUNK, D_MODEL), lambda i, j: (j, 0))
    return _call(
        body, comm, name=name, grid=(t // tt, n_chunks),
        in_specs=[pl.BlockSpec((tt, D_MODEL), row), _full((1, D_MODEL)), _full((1, D_MODEL)), wspec, wspec, wspec],
        out_specs=[pl.BlockSpec((tt, D_MODEL), row), pl.BlockSpec((tt, D_MODEL), row),
                   pl.BlockSpec((tt, D_MODEL), row),
                   pl.BlockSpec((tt, FF_CHUNK), lambda i, j: (i, j)),
                   pl.BlockSpec((tt, FF_CHUNK), lambda i, j: (i, j))],
        out_shape=[jax.ShapeDtypeStruct((t, D_MODEL), F32), jax.ShapeDtypeStruct((t, D_MODEL), F32),
                   jax.ShapeDtypeStruct((t, D_MODEL), BF16),
                   jax.ShapeDtypeStruct((t, FF_VIRT), BF16), jax.ShapeDtypeStruct((t, FF_VIRT), BF16)],
        scratch_shapes=[pltpu.VMEM((tt, D_MODEL), F32), pltpu.VMEM((tt, D_MODEL), BF16)],
        args=(h0, g_pre, g_post, wg, wu, wd))


def _ffn_fwd_hidden(h0, g_pre, wg, wu, comm, name):
    t = h0.shape[0]
    tt = _tile_rows(t)

    def body(h0_ref, gpre_ref, wg_ref, wu_ref, n1_ref, gt_ref, up_ref):
        @pl.when(pl.program_id(1) == 0)
        def _():
            n1_ref[...] = _rms(h0_ref[...], gpre_ref[...]).astype(BF16)

        n1 = n1_ref[...]
        gt_ref[...] = _dot_nt(n1, wg_ref[...]).astype(BF16)
        up_ref[...] = _dot_nt(n1, wu_ref[...]).astype(BF16)

    row = lambda i, j: (i, 0)
    wspec = pl.BlockSpec((FF_CHUNK, D_MODEL), lambda i, j: (j, 0))
    chunk = pl.BlockSpec((tt, FF_CHUNK), lambda i, j: (i, j))
    return _call(
        body, comm, name=name, grid=(t // tt, FF_VIRT // FF_CHUNK),
        in_specs=[pl.BlockSpec((tt, D_MODEL), row), _full((1, D_MODEL)), wspec, wspec],
        out_specs=[pl.BlockSpec((tt, D_MODEL), row), chunk, chunk],
        out_shape=[jax.ShapeDtypeStruct((t, D_MODEL), BF16), jax.ShapeDtypeStruct((t, FF_VIRT), BF16),
                   jax.ShapeDtypeStruct((t, FF_VIRT), BF16)],
        args=(h0, g_pre, wg, wu))


def _ffn_fwd_out(h0, gt, up, g_post, wd, comm, name):
    t = h0.shape[0]
    tt = 2 * _tile_rows(t)
    n_chunks = FF_VIRT // FF_CHUNK

    def body(h0_ref, gt_ref, up_ref, gpost_ref, wd_ref, h1_ref, f_ref):
        j = pl.program_id(1)
        g = gt_ref[...].astype(F32)
        term = _dot((g * _sigmoid(g) * up_ref[...].astype(F32)).astype(BF16), wd_ref[...])

        @pl.when(j == 0)
        def _():
            f_ref[...] = term

        @pl.when(j > 0)
        def _():
            f_ref[...] += term

        @pl.when(j == n_chunks - 1)
        def _():
            h1_ref[...] = h0_ref[...] + 0.5 * _rms(f_ref[...], gpost_ref[...])

    row = lambda i, j: (i, 0)
    chunk = pl.BlockSpec((tt, FF_CHUNK), lambda i, j: (i, j))
    return _call(
        body, comm, name=name, grid=(t // tt, n_chunks),
        in_specs=[pl.BlockSpec((tt, D_MODEL), row), chunk, chunk, _full((1, D_MODEL)),
                  pl.BlockSpec((FF_CHUNK, D_MODEL), lambda i, j: (j, 0))],
        out_specs=[pl.BlockSpec((tt, D_MODEL), row), pl.BlockSpec((tt, D_MODEL), row)],
        out_shape=[jax.ShapeDtypeStruct((t, D_MODEL), F32), jax.ShapeDtypeStruct((t, D_MODEL), F32)],
        args=(h0, gt, up, g_post, wd))


def _ffn_bwd_hidden(dh1, f, gt, up, g_post, wd, comm, name):
    t = f.shape[0]
    tt = _tile_rows(t)
    n_chunks = FF_VIRT // FF_CHUNK

    def body(dh1_ref, f_ref, gt_ref, up_ref, gpost_ref, wd_ref,
             df_ref, act_ref, dgt_ref, dup_ref, dgpost_ref, dfs_ref):
        i, j = pl.program_id(0), pl.program_id(1)

        @pl.when((i == 0) & (j == 0))
        def _():
            dgpost_ref[...] = jnp.zeros_like(dgpost_ref)

        @pl.when(j == 0)
        def _():
            df, dg = _rms_bwd(f_ref[...], gpost_ref[...], 0.5 * dh1_ref[...])
            dgpost_ref[...] += dg
            dfb = df.astype(BF16)
            dfs_ref[...] = dfb
            df_ref[...] = dfb

        da = _dot_nt(dfs_ref[...], wd_ref[...])
        g = gt_ref[...].astype(F32)
        u = up_ref[...].astype(F32)
        s = _sigmoid(g)
        sl = g * s
        act_ref[...] = (sl * u).astype(BF16)
        dgt_ref[...] = (da * u * (s * (1.0 + g * (1.0 - s)))).astype(BF16)
        dup_ref[...] = (da * sl).astype(BF16)

    row = lambda i, j: (i, 0)
    chunk = pl.BlockSpec((tt, FF_CHUNK), lambda i, j: (i, j))
    return _call(
        body, comm, name=name, grid=(t // tt, n_chunks),
        in_specs=[pl.BlockSpec((tt, D_MODEL), row), pl.BlockSpec((tt, D_MODEL), row), chunk, chunk,
                  _full((1, D_MODEL)), pl.BlockSpec((FF_CHUNK, D_MODEL), lambda i, j: (j, 0))],
        out_specs=[pl.BlockSpec((tt, D_MODEL), row), chunk, chunk, chunk, _full((1, D_MODEL))],
        out_shape=[jax.ShapeDtypeStruct((t, D_MODEL), BF16)] + [jax.ShapeDtypeStruct((t, FF_VIRT), BF16)] * 3 + [
            jax.ShapeDtypeStruct((1, D_MODEL), F32)],
        scratch_shapes=[pltpu.VMEM((tt, D_MODEL), BF16)],
        args=(dh1, f, gt, up, g_post, wd))


def _ffn_bwd_input(dh1, h0, dgt, dup, g_pre, wg, wu, comm, name):
    t = h0.shape[0]
    tt = 2 * _tile_rows(t)
    n_chunks = FF_VIRT // FF_CHUNK

    def body(dh1_ref, h0_ref, dgt_ref, dup_ref, gpre_ref, wg_ref, wu_ref, dh0_ref, dgpre_ref, acc_ref):
        i, j = pl.program_id(0), pl.program_id(1)

        @pl.when((i == 0) & (j == 0))
        def _():
            dgpre_ref[...] = jnp.zeros_like(dgpre_ref)

        @pl.when(j == 0)
        def _():
            acc_ref[...] = jnp.zeros_like(acc_ref)

        acc_ref[...] += _dot(dgt_ref[...], wg_ref[...]) + _dot(dup_ref[...], wu_ref[...])

        @pl.when(j == n_chunks - 1)
        def _():
            dx, dg = _rms_bwd(h0_ref[...], gpre_ref[...], acc_ref[...])
            dgpre_ref[...] += dg
            dh0_ref[...] = dh1_ref[...] + dx

    row = lambda i, j: (i, 0)
    chunk = pl.BlockSpec((tt, FF_CHUNK), lambda i, j: (i, j))
    wspec = pl.BlockSpec((FF_CHUNK, D_MODEL), lambda i, j: (j, 0))
    return _call(
        body, comm, name=name, grid=(t // tt, n_chunks),
        in_specs=[pl.BlockSpec((tt, D_MODEL), row), pl.BlockSpec((tt, D_MODEL), row), chunk, chunk,
                  _full((1, D_MODEL)), wspec, wspec],
        out_specs=[pl.BlockSpec((tt, D_MODEL), row), _full((1, D_MODEL))],
        out_shape=[jax.ShapeDtypeStruct((t, D_MODEL), F32), jax.ShapeDtypeStruct((1, D_MODEL), F32)],
        scratch_shapes=[pltpu.VMEM((tt, D_MODEL), F32)],
        args=(dh1, h0, dgt, dup, g_pre, wg, wu))


def _mm_tn(a, b, m, tm, tn, name, comm=None):
    t, n = b.shape
    tk = min(t, 2048)
    nk = t // tk

    def body(a_ref, b_ref, out_ref, acc_ref):
        k = pl.program_id(2)

        @pl.when(k == 0)
        def _():
            acc_ref[...] = jnp.zeros_like(acc_ref)

        acc_ref[...] += _dot_tn(a_ref[...], b_ref[...])

        @pl.when(k == nk - 1)
        def _():
            out_ref[...] = acc_ref[...].astype(out_ref.dtype)

    outs, got = _call(
        body, comm, name=name, grid=(m // tm, n // tn, nk),
        in_specs=[pl.BlockSpec((tk, tm), lambda i, j, k: (k, i)), pl.BlockSpec((tk, tn), lambda i, j, k: (k, j))],
        out_specs=[pl.BlockSpec((tm, tn), lambda i, j, k: (i, j))],
        out_shape=[jax.ShapeDtypeStruct((m, n), BF16)],
        scratch_shapes=[pltpu.VMEM((tm, tn), F32)], args=(a, b))
    return outs[0], got


_SPLITS = (0, LRU_W, 2 * LRU_W, 2 * LRU_W + Q_LORA, 2 * LRU_W + Q_LORA + KV_LORA, IN_PAD)
_WIDTHS = tuple(_SPLITS[k + 1] - _SPLITS[k] for k in range(5))


def _mixer_in(h1, g_pre, w_in):
    t = h1.shape[0]
    tt = _tile_rows(t)

    def body(h_ref, g_ref, w_ref, n_ref, *outs):
        n = _rms(h_ref[...], g_ref[...]).astype(BF16)
        n_ref[...] = n
        proj = _dot_nt(n, w_ref[...])
        for k in range(5):
            outs[k][...] = proj[:, _SPLITS[k]:_SPLITS[k + 1]]

    row = lambda i: (i, 0)
    return pl.pallas_call(
        body, name="mixer_in",
        grid=(t // tt,),
        in_specs=[pl.BlockSpec((tt, D_MODEL), row), _full((1, D_MODEL)), _full((IN_PAD, D_MODEL))],
        out_specs=[pl.BlockSpec((tt, D_MODEL), row)] + [pl.BlockSpec((tt, w), row) for w in _WIDTHS],
        out_shape=[jax.ShapeDtypeStruct((t, D_MODEL), BF16)] + [jax.ShapeDtypeStruct((t, w), F32) for w in _WIDTHS],
        compiler_params=_params(),
    )(h1, g_pre, w_in)


def _mixer_in_bwd(dh2, h1, g_pre, w_in, dparts):
    t = h1.shape[0]
    tt = _tile_rows(t)

    def body(dh2_ref, h_ref, g_ref, w_ref, p0, p1, p2, p3, p4, dh1_ref, dproj_ref, dg_ref):
        i = pl.program_id(0)

        @pl.when(i == 0)
        def _():
            dg_ref[...] = jnp.zeros_like(dg_ref)

        dproj = jnp.concatenate([p[...] for p in (p0, p1, p2, p3, p4)], axis=1)
        dproj_ref[...] = dproj
        dx, dg = _rms_bwd(h_ref[...], g_ref[...], _dot(dproj, w_ref[...]))
        dg_ref[...] += dg
        dh1_ref[...] = dh2_ref[...] + dx

    row = lambda i: (i, 0)
    return pl.pallas_call(
        body, name="mixer_in_bwd",
        grid=(t // tt,),
        in_specs=[pl.BlockSpec((tt, D_MODEL), row), pl.BlockSpec((tt, D_MODEL), row), _full((1, D_MODEL)),
                  _full((IN_PAD, D_MODEL))] + [pl.BlockSpec((tt, w), row) for w in _WIDTHS],
        out_specs=[pl.BlockSpec((tt, D_MODEL), row), pl.BlockSpec((tt, IN_PAD), row), _full((1, D_MODEL))],
        out_shape=[jax.ShapeDtypeStruct((t, D_MODEL), F32), jax.ShapeDtypeStruct((t, IN_PAD), BF16),
                   jax.ShapeDtypeStruct((1, D_MODEL), F32)],
        compiler_params=_params(),
    )(dh2, h1, g_pre, w_in, *dparts)


def _conv(x_prev8, x_tile, w_ref, b_ref):
    tt = x_tile.shape[0]
    xx = jnp.concatenate([x_prev8, x_tile], axis=0)
    y = b_ref[...] + w_ref[CONV_W - 1:CONV_W, :] * x_tile
    for k in range(CONV_W - 1):
        y = y + w_ref[k:k + 1, :] * pltpu.roll(xx, CONV_W - 1 - k, 0)[8:8 + tt]
    return y


def _neg_expm1(z):
    series = -z * (1.0 + z * 0.5 * (1.0 + z / 3.0 * (1.0 + z * 0.25 * (1.0 + z * 0.2 * (1.0 + z / 6.0)))))
    return jnp.where(z > -0.3, series, 1.0 - jnp.exp(z))


def _lru_elementwise(xc, pre_a, pre_i, lam):
    ra = _sigmoid(pre_a)
    ri = _sigmoid(pre_i)
    neg = -lam
    softplus = jnp.maximum(neg, 0.0) + jnp.log(1.0 + jnp.exp(-jnp.abs(neg)))
    log_a = -LRU_C * ra * softplus
    a = jnp.exp(log_a)
    mult = jnp.sqrt(_neg_expm1(2.0 * log_a))
    return a, mult * (ri * xc)


def _lru_gates(xl, conv_w, conv_b, wa, wx, ba, bx, lam):
    t = xl.shape[0]
    tt = _tile_rows(t)
    r8 = tt // 8

    def body(prev_ref, x_ref, cw_ref, cb_ref, wa_ref, wx_ref, ba_ref, bx_ref, lam_ref, xc_ref, a_ref, u_ref):
        i = pl.program_id(0)
        prev = jnp.where(i == 0, 0.0, prev_ref[...])
        xc = _conv(prev, x_ref[...], cw_ref, cb_ref)
        xc_ref[...] = xc
        xb = xc.astype(BF16)
        a, u = _lru_elementwise(xc, _dot(xb, wa_ref[...]) + ba_ref[...], _dot(xb, wx_ref[...]) + bx_ref[...],
                                lam_ref[...])
        a_ref[...] = a
        u_ref[...] = u

    row = lambda i: (i, 0)
    vec = _full((1, LRU_W))
    return pl.pallas_call(
        body, name="lru_gates",
        grid=(t // tt,),
        in_specs=[pl.BlockSpec((8, LRU_W), lambda i: (jnp.maximum(i * r8 - 1, 0), 0)),
                  pl.BlockSpec((tt, LRU_W), row), _full((8, LRU_W)), vec,
                  _full((LRU_W, LRU_W)), _full((LRU_W, LRU_W)), vec, vec, vec],
        out_specs=[pl.BlockSpec((tt, LRU_W), row)] * 3,
        out_shape=[jax.ShapeDtypeStruct((t, LRU_W), F32)] * 3,
        compiler_params=_params(),
    )(xl, xl, conv_w, conv_b, wa, wx, ba, bx, lam)


def _lru_scan(a, u, reverse):
    t = a.shape[0]
    nblk = t // 8

    def body(a_ref, u_ref, h_ref):
        def fwd(blk, h):
            base = pl.multiple_of(blk * 8, 8)
            for k in range(8):
                h = a_ref[pl.ds(base + k, 1), :] * h + u_ref[pl.ds(base + k, 1), :]
                h_ref[pl.ds(base + k, 1), :] = h
            return h

        def bwd(n, carry):
            base = pl.multiple_of((nblk - 1 - n) * 8, 8)
            for k in range(7, -1, -1):
                g = u_ref[pl.ds(base + k, 1), :] + carry
                h_ref[pl.ds(base + k, 1), :] = g
                carry = a_ref[pl.ds(base + k, 1), :] * g
            return carry

        lax.fori_loop(0, nblk, bwd if reverse else fwd, jnp.zeros((1, LRU_W), F32))

    return pl.pallas_call(
        body, name="lru_scan_rev" if reverse else "lru_scan",
        in_specs=[WHOLE] * 2, out_specs=WHOLE,
        out_shape=jax.ShapeDtypeStruct((t, LRU_W), F32),
        compiler_params=_params(),
    )(a, u)


def _lru_gates_bwd(dh, h, xc, wa, wx, ba, bx, lam):
    t = dh.shape[0]
    tt = _tile_rows(t)
    r8 = tt // 8

    def body(dh_ref, hprev_ref, h_ref, xc_ref, wa_ref, wx_ref, ba_ref, bx_ref, lam_ref,
             dxc_ref, dwa_ref, dwx_ref, dba_ref, dbx_ref, dlam_ref):
        i = pl.program_id(0)

        @pl.when(i == 0)
        def _():
            for r in (dwa_ref, dwx_ref, dba_ref, dbx_ref, dlam_ref):
                r[...] = jnp.zeros_like(r)

        prev = jnp.where(i == 0, 0.0, hprev_ref[...])
        h_before = pltpu.roll(jnp.concatenate([prev, h_ref[...]], axis=0), 1, 0)[8:8 + tt]
        dh_t = dh_ref[...]
        xc = xc_ref[...]
        xb = xc.astype(BF16)
        pre_a = _dot(xb, wa_ref[...]) + ba_ref[...]
        pre_i = _dot(xb, wx_ref[...]) + bx_ref[...]
        _, vjp = jax.vjp(_lru_elementwise, xc, pre_a, pre_i, lam_ref[...])
        dxc, dpre_a, dpre_i, dlam = vjp((dh_t * h_before, dh_t))
        da_b = dpre_a.astype(BF16)
        di_b = dpre_i.astype(BF16)
        dxc_ref[...] = dxc + _dot_nt(da_b, wa_ref[...]) + _dot_nt(di_b, wx_ref[...])
        dwa_ref[...] += _dot_tn(xb, da_b)
        dwx_ref[...] += _dot_tn(xb, di_b)
        dba_ref[...] += jnp.sum(dpre_a, axis=0, keepdims=True)
        dbx_ref[...] += jnp.sum(dpre_i, axis=0, keepdims=True)
        dlam_ref[...] += dlam

    row = lambda i: (i, 0)
    vec = _full((1, LRU_W))
    sq = _full((LRU_W, LRU_W))
    return pl.pallas_call(
        body, name="lru_gates_bwd",
        grid=(t // tt,),
        in_specs=[pl.BlockSpec((tt, LRU_W), row),
                  pl.BlockSpec((8, LRU_W), lambda i: (jnp.maximum(i * r8 - 1, 0), 0)),
                  pl.BlockSpec((tt, LRU_W), row), pl.BlockSpec((tt, LRU_W), row), sq, sq, vec, vec, vec],
        out_specs=[pl.BlockSpec((tt, LRU_W), row), sq, sq, vec, vec, vec],
        out_shape=[jax.ShapeDtypeStruct((t, LRU_W), F32), jax.ShapeDtypeStruct((LRU_W, LRU_W), F32),
                   jax.ShapeDtypeStruct((LRU_W, LRU_W), F32)] + [jax.ShapeDtypeStruct((1, LRU_W), F32)] * 3,
        compiler_params=_params(),
    )(dh, h, h, xc, wa, wx, ba, bx, lam)


def _conv_bwd(dxc, xl, conv_w):
    t = dxc.shape[0]
    tt = _tile_rows(t)
    r8 = tt // 8
    n_tiles = t // tt
    last = t // 8 - 1

    def body(d_ref, dnext_ref, xprev_ref, x_ref, cw_ref, dxl_ref, dw_ref, db_ref):
        i = pl.program_id(0)

        @pl.when(i == 0)
        def _():
            dw_ref[...] = jnp.zeros_like(dw_ref)
            db_ref[...] = jnp.zeros_like(db_ref)

        d = d_ref[...]
        nxt = jnp.where(i == n_tiles - 1, 0.0, dnext_ref[...])
        dd = jnp.concatenate([d, nxt], axis=0)
        dx = cw_ref[CONV_W - 1:CONV_W, :] * d
        for k in range(CONV_W - 1):
            shift = CONV_W - 1 - k
            dx = dx + cw_ref[k:k + 1, :] * pltpu.roll(dd, tt + 8 - shift, 0)[:tt]
        dxl_ref[...] = dx.astype(BF16)

        prev = jnp.where(i == 0, 0.0, xprev_ref[...])
        xx = jnp.concatenate([prev, x_ref[...]], axis=0)
        rows = []
        for k in range(CONV_W):
            shifted = x_ref[...] if k == CONV_W - 1 else pltpu.roll(xx, CONV_W - 1 - k, 0)[8:8 + tt]
            rows.append(jnp.sum(d * shifted, axis=0, keepdims=True))
        rows.append(jnp.zeros((8 - CONV_W, LRU_W), F32))
        dw_ref[...] += jnp.concatenate(rows, axis=0)
        db_ref[...] += jnp.sum(d, axis=0, keepdims=True)

    row = lambda i: (i, 0)
    return pl.pallas_call(
        body, name="conv_bwd",
        grid=(n_tiles,),
        in_specs=[pl.BlockSpec((tt, LRU_W), row),
                  pl.BlockSpec((8, LRU_W), lambda i: (jnp.minimum((i + 1) * r8, last), 0)),
                  pl.BlockSpec((8, LRU_W), lambda i: (jnp.maximum(i * r8 - 1, 0), 0)),
                  pl.BlockSpec((tt, LRU_W), row), _full((8, LRU_W))],
        out_specs=[pl.BlockSpec((tt, LRU_W), row), _full((8, LRU_W)), _full((1, LRU_W))],
        out_shape=[jax.ShapeDtypeStruct((t, LRU_W), BF16), jax.ShapeDtypeStruct((8, LRU_W), F32),
                   jax.ShapeDtypeStruct((1, LRU_W), F32)],
        compiler_params=_params(),
    )(dxc, dxc, xl, xl, conv_w)


def _rope(x, c, s1, s2):
    n = x.shape[-1]
    return x * c + pltpu.roll(x, n - 16, 1) * s1 + pltpu.roll(x, 16, 1) * s2


def _rope_t(d, c, s1, s2):
    n = d.shape[-1]
    return d * c + pltpu.roll(d * s1, 16, 1) + pltpu.roll(d * s2, n - 16, 1)


def _mla_proj(qlat, kvlat, krope, tables, gq, gkv, wq, wk, wv):
    t = qlat.shape[0]
    tt = _tile_rows(t)

    def body(q_ref, kv_ref, kr_ref, cq, s1q, s2q, ck, s1k, s2k, gq_ref, gkv_ref, wq_ref, wk_ref, wv_ref,
             qo_ref, ko_ref, vo_ref, qn_ref, kva_ref, vt_ref):
        qn = _rms(q_ref[...], gq_ref[...]).astype(BF16)
        kvn = _rms(kv_ref[...], gkv_ref[...]).astype(BF16)
        kr = _rope(kr_ref[...], ck[...], s1k[...], s2k[...]).astype(BF16)
        kva = jnp.concatenate([kvn, kr], axis=1)
        qn_ref[...] = qn
        kva_ref[...] = kva
        q = _dot_nt(qn, wq_ref[...])
        tile = lambda r: jnp.tile(r[...], (1, N_HEADS))
        qo_ref[...] = _rope(q, tile(cq), tile(s1q), tile(s2q)).astype(BF16)
        ko_ref[...] = _dot(kva, wk_ref[...]).astype(BF16)
        v = _dot(kvn, wv_ref[...])
        vo_ref[...] = v.astype(BF16)
        vt_ref[...] = jnp.transpose(v).astype(BF16)

    row = lambda i: (i, 0)
    tab = pl.BlockSpec((tt, HEAD_PAD), row)
    return pl.pallas_call(
        body, name="mla_proj",
        grid=(t // tt,),
        in_specs=[pl.BlockSpec((tt, Q_LORA), row), pl.BlockSpec((tt, KV_LORA), row), tab] + [tab] * 6 + [
            _full((1, Q_LORA)), _full((1, KV_LORA)), _full((HEADS_W, Q_LORA)), _full((KV_LORA + HEAD_PAD, HEADS_W)),
            _full((KV_LORA, HEADS_W))],
        out_specs=[pl.BlockSpec((tt, HEADS_W), row)] * 3 + [pl.BlockSpec((tt, Q_LORA), row),
                                                             pl.BlockSpec((tt, KV_LORA + HEAD_PAD), row),
                                                             pl.BlockSpec((HEADS_W, tt), lambda i: (0, i))],
        out_shape=[jax.ShapeDtypeStruct((t, HEADS_W), BF16)] * 3 + [
            jax.ShapeDtypeStruct((t, Q_LORA), BF16), jax.ShapeDtypeStruct((t, KV_LORA + HEAD_PAD), BF16),
            jax.ShapeDtypeStruct((HEADS_W, t), BF16)],
        compiler_params=_params(),
    )(qlat, kvlat, krope, *tables, gq, gkv, wq, wk, wv)


def _mla_proj_bwd(dq, dk, dv, qlat, kvlat, tables, gq, gkv, wq, wk, wv):
    t = qlat.shape[0]
    tt = _tile_rows(t)

    def body(dq_ref, dk_ref, dv_ref, q_ref, kv_ref, cq, s1q, s2q, ck, s1k, s2k, gq_ref, gkv_ref,
             wq_ref, wk_ref, wv_ref, dqr_ref, dql_ref, dkvl_ref, dkr_ref, dgq_ref, dgkv_ref):
        i = pl.program_id(0)

        @pl.when(i == 0)
        def _():
            dgq_ref[...] = jnp.zeros_like(dgq_ref)
            dgkv_ref[...] = jnp.zeros_like(dgkv_ref)

        tile = lambda r: jnp.tile(r[...], (1, N_HEADS))
        dqr = _rope_t(dq_ref[...], tile(cq), tile(s1q), tile(s2q)).astype(BF16)
        dqr_ref[...] = dqr
        dql, dg = _rms_bwd(q_ref[...], gq_ref[...], _dot(dqr, wq_ref[...]))
        dgq_ref[...] += dg
        dql_ref[...] = dql.astype(BF16)
        dkva = _dot_nt(dk_ref[...], wk_ref[...])
        dkvn = dkva[:, :KV_LORA] + _dot_nt(dv_ref[...], wv_ref[...])
        dkvl, dg = _rms_bwd(kv_ref[...], gkv_ref[...], dkvn)
        dgkv_ref[...] += dg
        dkvl_ref[...] = dkvl.astype(BF16)
        dkr_ref[...] = _rope_t(dkva[:, KV_LORA:], ck[...], s1k[...], s2k[...]).astype(BF16)

    row = lambda i: (i, 0)
    tab = pl.BlockSpec((tt, HEAD_PAD), row)
    wide = pl.BlockSpec((tt, HEADS_W), row)
    return pl.pallas_call(
        body, name="mla_proj_bwd",
        grid=(t // tt,),
        in_specs=[wide, wide, wide, pl.BlockSpec((tt, Q_LORA), row), pl.BlockSpec((tt, KV_LORA), row)] + [tab] * 6 + [
            _full((1, Q_LORA)), _full((1, KV_LORA)), _full((HEADS_W, Q_LORA)), _full((KV_LORA + HEAD_PAD, HEADS_W)),
            _full((KV_LORA, HEADS_W))],
        out_specs=[wide, pl.BlockSpec((tt, Q_LORA), row), pl.BlockSpec((tt, KV_LORA), row), tab,
                   _full((1, Q_LORA)), _full((1, KV_LORA))],
        out_shape=[jax.ShapeDtypeStruct((t, HEADS_W), BF16), jax.ShapeDtypeStruct((t, Q_LORA), BF16),
                   jax.ShapeDtypeStruct((t, KV_LORA), BF16), jax.ShapeDtypeStruct((t, HEAD_PAD), BF16),
                   jax.ShapeDtypeStruct((1, Q_LORA), F32), jax.ShapeDtypeStruct((1, KV_LORA), F32)],
        compiler_params=_params(),
    )(dq, dk, dv, qlat, kvlat, *tables, gq, gkv, wq, wk, wv)


NEG = -1e30


def _chunk_mask(rows, cols, row0):
    r = (lax.broadcasted_iota(jnp.int32, (rows, cols), 0) + row0) // CHUNK
    c = lax.broadcasted_iota(jnp.int32, (rows, cols), 1) // CHUNK
    return c <= r


def _chunk_mask_t(keys, key0, queries):
    kc = (lax.broadcasted_iota(jnp.int32, (keys, queries), 0) + key0) // CHUNK
    qc = lax.broadcasted_iota(jnp.int32, (keys, queries), 1) // CHUNK
    return kc <= qc


def _attn_fwd(q, k, vt, comm):
    t = q.shape[0]
    bq = _tile_rows(t)
    nq = t // bq

    def body(q_ref, k_ref, vt_ref, o_ref, lse_ref, m_ref, l_ref, acc_ref):
        i = pl.program_id(1)
        m_ref[...] = jnp.full_like(m_ref, NEG)
        l_ref[...] = jnp.zeros_like(l_ref)
        acc_ref[...] = jnp.zeros_like(acc_ref)
        qb = q_ref[...]

        def block(first_key, keys, diagonal):
            cols = pl.ds(pl.multiple_of(first_key, bq), keys)
            s = _dot_nt(k_ref[cols, :], qb)
            if diagonal:
                s = jnp.where(_chunk_mask_t(keys, bq - keys, bq), s, NEG)
            m_old = m_ref[...]
            m_new = jnp.maximum(m_old, jnp.max(s, axis=0, keepdims=True))
            alpha = jnp.exp(m_old - m_new)
            p = jnp.exp(s - m_new)
            l_ref[...] = alpha * l_ref[...] + jnp.sum(p, axis=0, keepdims=True)
            acc_ref[...] = alpha * acc_ref[...] + _dot(vt_ref[:, cols], p.astype(BF16))
            m_ref[...] = m_new

        def pair_below(jj, carry):
            block(jj * 2 * bq, 2 * bq, False)
            return carry

        lax.fori_loop(0, i // 2, pair_below, 0)

        @pl.when(i % 2 == 1)
        def _():
            block((i - 1) * bq, 2 * bq, True)

        @pl.when(i % 2 == 0)
        def _():
            block(i * bq, bq, True)

        l = l_ref[...]
        o_ref[...] = jnp.transpose(acc_ref[...] / l).astype(BF16)
        lse_ref[...] = m_ref[...] + jnp.log(l)

    qmap = lambda h, i: (i, h)
    return _call(
        body, comm, name="attn_fwd" if comm is None else "attn_fwd_gather", grid=(N_HEADS, nq),
        in_specs=[pl.BlockSpec((bq, HEAD_PAD), qmap), pl.BlockSpec((t, HEAD_PAD), lambda h, i: (0, h)),
                  pl.BlockSpec((HEAD_PAD, t), lambda h, i: (h, 0))],
        out_specs=[pl.BlockSpec((bq, HEAD_PAD), qmap), pl.BlockSpec((None, 1, bq), lambda h, i: (h, 0, i))],
        out_shape=[jax.ShapeDtypeStruct(q.shape, BF16), jax.ShapeDtypeStruct((N_HEADS, 1, t), F32)],
        scratch_shapes=[pltpu.VMEM((1, bq), F32), pltpu.VMEM((1, bq), F32), pltpu.VMEM((HEAD_PAD, bq), F32)],
        args=(q, k, vt))


def _attn_bwd(q, k, v, o, do, lse, comm):
    t = q.shape[0]
    bq = _tile_rows(t)
    nq = t // bq

    def body(q_ref, k_ref, v_ref, o_ref, do_ref, lse_ref, dq_ref, dk_ref, dv_ref, dk_acc, dv_acc, delta_ref):
        j = pl.program_id(1)

        @pl.when(j == 0)
        def _():
            dq_ref[...] = jnp.zeros_like(dq_ref)
            for blk in range(nq):
                rows = slice(blk * bq, (blk + 1) * bq)
                delta = jnp.sum(do_ref[rows, :].astype(F32) * o_ref[rows, :].astype(F32), axis=-1, keepdims=True)
                delta_ref[:, rows] = jnp.transpose(jnp.broadcast_to(delta, (bq, HEAD_PAD)))[:1, :]

        dk_acc[...] = jnp.zeros_like(dk_acc)
        dv_acc[...] = jnp.zeros_like(dv_acc)
        kb, vb = k_ref[...], v_ref[...]

        def block(first_query, queries, diagonal):
            rows = pl.ds(pl.multiple_of(first_query, bq), queries)
            qb, dob = q_ref[rows, :], do_ref[rows, :]
            s = _dot_nt(kb, qb)
            if diagonal:
                s = jnp.where(_chunk_mask_t(bq, 0, queries), s, NEG)
            p = jnp.exp(s - lse_ref[:, rows])
            dv_acc[...] += _dot(p.astype(BF16), dob)
            dp = _dot_nt(vb, dob)
            ds = (p * (dp - delta_ref[:, rows])).astype(BF16)
            dk_acc[...] += _dot(ds, qb)
            dq_ref[rows, :] += _dot_tn(ds, kb)

        after = nq - 1 - j
        odd = after % 2

        @pl.when(odd == 1)
        def _():
            block(j * bq, 2 * bq, True)

        @pl.when(odd == 0)
        def _():
            block(j * bq, bq, True)

        def pair_above(n, carry):
            block((j + 1 + odd + 2 * n) * bq, 2 * bq, False)
            return carry

        lax.fori_loop(0, after // 2, pair_above, 0)
        dk_ref[...] = dk_acc[...].astype(BF16)
        dv_ref[...] = dv_acc[...].astype(BF16)

    kmap = lambda h, j: (j, h)
    head = lambda h, j: (0, h)
    whole = pl.BlockSpec((t, HEAD_PAD), head)
    return _call(
        body, comm, name="attn_bwd" if comm is None else "attn_bwd_exchange", grid=(N_HEADS, nq),
        in_specs=[whole, pl.BlockSpec((bq, HEAD_PAD), kmap), pl.BlockSpec((bq, HEAD_PAD), kmap), whole, whole,
                  pl.BlockSpec((None, 1, t), lambda h, j: (h, 0, 0))],
        out_specs=[whole, pl.BlockSpec((bq, HEAD_PAD), kmap), pl.BlockSpec((bq, HEAD_PAD), kmap)],
        out_shape=[jax.ShapeDtypeStruct(q.shape, F32), jax.ShapeDtypeStruct(q.shape, BF16),
                   jax.ShapeDtypeStruct(q.shape, BF16)],
        scratch_shapes=[pltpu.VMEM((bq, HEAD_PAD), F32), pltpu.VMEM((bq, HEAD_PAD), F32), pltpu.VMEM((1, t), F32)],
        args=(q, k, v, o, do, lse))


def _mixer_out(h1, hl, gate, o, w_lru, w_mla, g_post):
    t = h1.shape[0]
    tt = _tile_rows(t)

    def body(h1_ref, hl_ref, gate_ref, o_ref, wl_ref, wm_ref, g_ref, h2_ref, m_ref, y_ref):
        y = (hl_ref[...] * _gelu(gate_ref[...])).astype(BF16)
        y_ref[...] = y
        m = _dot(y, wl_ref[...]) + _dot(o_ref[...], wm_ref[...])
        m_ref[...] = m
        h2_ref[...] = h1_ref[...] + _rms(m, g_ref[...])

    row = lambda i: (i, 0)
    return pl.pallas_call(
        body, name="mixer_out",
        grid=(t // tt,),
        in_specs=[pl.BlockSpec((tt, D_MODEL), row), pl.BlockSpec((tt, LRU_W), row), pl.BlockSpec((tt, LRU_W), row),
                  pl.BlockSpec((tt, HEADS_W), row), _full((LRU_W, D_MODEL)), _full((HEADS_W, D_MODEL)),
                  _full((1, D_MODEL))],
        out_specs=[pl.BlockSpec((tt, D_MODEL), row), pl.BlockSpec((tt, D_MODEL), row),
                   pl.BlockSpec((tt, LRU_W), row)],
        out_shape=[jax.ShapeDtypeStruct((t, D_MODEL), F32), jax.ShapeDtypeStruct((t, D_MODEL), F32),
                   jax.ShapeDtypeStruct((t, LRU_W), BF16)],
        compiler_params=_params(),
    )(h1, hl, gate, o, w_lru, w_mla, g_post)


def _mixer_out_bwd(dh2, m, hl, gate, w_lru, w_mla, g_post):
    t = m.shape[0]
    tt = _tile_rows(t)

    def body(dh2_ref, m_ref, hl_ref, gate_ref, wl_ref, wm_ref, g_ref, dm_ref, dgate_ref, dhl_ref, do_ref, dg_ref):
        i = pl.program_id(0)

        @pl.when(i == 0)
        def _():
            dg_ref[...] = jnp.zeros_like(dg_ref)

        dm, dg = _rms_bwd(m_ref[...], g_ref[...], dh2_ref[...])
        dg_ref[...] += dg
        dmb = dm.astype(BF16)
        dm_ref[...] = dmb
        dy = _dot_nt(dmb, wl_ref[...])
        gate = gate_ref[...]
        dgate_ref[...] = (dy * hl_ref[...] * _gelu_grad(gate)).astype(BF16)
        dhl_ref[...] = dy * _gelu(gate)
        do_ref[...] = _dot_nt(dmb, wm_ref[...]).astype(BF16)

    row = lambda i: (i, 0)
    return pl.pallas_call(
        body, name="mixer_out_bwd",
        grid=(t // tt,),
        in_specs=[pl.BlockSpec((tt, D_MODEL), row), pl.BlockSpec((tt, D_MODEL), row), pl.BlockSpec((tt, LRU_W), row),
                  pl.BlockSpec((tt, LRU_W), row), _full((LRU_W, D_MODEL)), _full((HEADS_W, D_MODEL)),
                  _full((1, D_MODEL))],
        out_specs=[pl.BlockSpec((tt, D_MODEL), row), pl.BlockSpec((tt, LRU_W), row), pl.BlockSpec((tt, LRU_W), row),
                   pl.BlockSpec((tt, HEADS_W), row), _full((1, D_MODEL))],
        out_shape=[jax.ShapeDtypeStruct((t, D_MODEL), BF16), jax.ShapeDtypeStruct((t, LRU_W), BF16),
                   jax.ShapeDtypeStruct((t, LRU_W), F32), jax.ShapeDtypeStruct((t, HEADS_W), BF16),
                   jax.ShapeDtypeStruct((1, D_MODEL), F32)],
        compiler_params=_params(),
    )(dh2, m, hl, gate, w_lru, w_mla, g_post)


def _loss_head(y, target):
    t = y.shape[0]
    tt = _tile_rows(t)

    def body(y_ref, t_ref, dy_ref, loss_ref):
        i = pl.program_id(0)

        @pl.when(i == 0)
        def _():
            loss_ref[...] = jnp.zeros_like(loss_ref)

        e = y_ref[...] - t_ref[...]
        dy_ref[...] = e * (1.0 / D_MODEL)
        per_token = jnp.mean(e * e, axis=-1, keepdims=True)
        loss_ref[...] += 0.5 * jnp.sum(per_token, axis=0, keepdims=True)

    row = lambda i: (i, 0)
    return pl.pallas_call(
        body, name="loss_head",
        grid=(t // tt,),
        in_specs=[pl.BlockSpec((tt, D_MODEL), row), pl.BlockSpec((tt, D_MODEL), row)],
        out_specs=[pl.BlockSpec((tt, D_MODEL), row), _full((8, 128))],
        out_shape=[jax.ShapeDtypeStruct((t, D_MODEL), F32), jax.ShapeDtypeStruct((8, 128), F32)],
        compiler_params=_params(),
    )(y, target)


def _adamw(w, g, m, v):
    m = ADAM_B1 * m + (1.0 - ADAM_B1) * g
    v = ADAM_B2 * v + (1.0 - ADAM_B2) * (g * g)
    m_hat = m / (1.0 - ADAM_B1 ** ADAM_STEP)
    v_hat = v / (1.0 - ADAM_B2 ** ADAM_STEP)
    delta = -ADAM_LR * (m_hat / (jnp.sqrt(v_hat) + ADAM_EPS) + ADAM_WD * w)
    return delta, m, v


def _sum_adamw(terms, w, m, v, name):
    r, c = w.shape
    n_terms, rp = terms.shape[:2]
    cb = 256 if c % 256 == 0 else 128

    def body(t_ref, w_ref, m_ref, v_ref, g_out, d_out, m_out, v_out):
        g = t_ref[0, :r, :].astype(F32)
        for q in range(1, n_terms):
            g = g + t_ref[q, :r, :].astype(F32)
        g_out[...] = g
        d_out[...], m_out[...], v_out[...] = _adamw(w_ref[...], g, m_ref[...], v_ref[...])

    cols = pl.BlockSpec((r, cb), lambda j: (0, j))
    return pl.pallas_call(
        body, name=name, grid=(c // cb,),
        in_specs=[pl.BlockSpec((n_terms, rp, cb), lambda j: (0, 0, j)), cols, cols, cols], out_specs=[cols] * 4,
        out_shape=[jax.ShapeDtypeStruct((r, c), F32)] * 4,
        compiler_params=_params(),
    )(terms, w, m, v)


def _sum_terms(terms):
    def body(t_ref, out_ref):
        g = t_ref[0]
        for q in range(1, N_DEV):
            g = g + t_ref[q]
        out_ref[...] = g

    return pl.pallas_call(
        body, name="small_grad_sum", in_specs=[WHOLE], out_specs=WHOLE,
        out_shape=jax.ShapeDtypeStruct(terms.shape[1:], F32),
        compiler_params=_params(),
    )(terms)


def _adamw_flat(w, g, m, v):
    def body(w_ref, g_ref, m_ref, v_ref, d_out, m_out, v_out):
        d_out[...], m_out[...], v_out[...] = _adamw(w_ref[...], g_ref[...], m_ref[...], v_ref[...])

    return pl.pallas_call(
        body, name="small_adamw", in_specs=[WHOLE] * 4, out_specs=[WHOLE] * 3,
        out_shape=[jax.ShapeDtypeStruct(w.shape, F32)] * 3,
        compiler_params=_params(),
    )(w, g, m, v)


def _pack(arrays):
    flat = jnp.concatenate([a.reshape(-1).astype(F32) for a in arrays])
    rows = -(-flat.shape[0] // 1024) * 8
    return jnp.pad(flat, (0, rows * 128 - flat.shape[0])).reshape(rows, 128)


def _unpack(packed, shapes):
    flat = packed.reshape(-1)
    out, pos = [], 0
    for s in shapes:
        n = math.prod(s)
        out.append(flat[pos:pos + n].reshape(s))
        pos += n
    return out


def _rope_tables(positions):
    t = positions.shape[-1]
    inv_freq = 1.0 / (ROPE_THETA ** (jnp.arange(0, QK_ROPE, 2, dtype=F32) / QK_ROPE))
    ang = positions.reshape(t, 1).astype(F32) * inv_freq
    cos, sin = jnp.cos(ang), jnp.sin(ang)
    z16 = jnp.zeros((t, 16), F32)

    def place(first, second, lead, lead_value):
        parts = [jnp.full((t, lead), lead_value, F32)] if lead else []
        parts += [first, second, jnp.zeros((t, HEAD_PAD - lead - 32), F32)]
        return jnp.concatenate(parts, axis=1)

    q_tabs = [place(cos, cos, QK_NOPE, 1.0) * ATT_SCALE, place(-sin, z16, QK_NOPE, 0.0) * ATT_SCALE,
              place(z16, sin, QK_NOPE, 0.0) * ATT_SCALE]
    k_tabs = [place(cos, cos, 0, 0.0), place(-sin, z16, 0, 0.0), place(z16, sin, 0, 0.0)]
    return q_tabs + k_tabs


def _block_diag(w):
    eye = jnp.eye(w.shape[0], dtype=w.dtype)
    return jnp.einsum("nde,nm->ndme", w, eye).reshape(LRU_W, LRU_W)


def _diag_blocks(g):
    g4 = g.reshape(8, 64, 8, 64)
    return jnp.moveaxis(jnp.diagonal(g4, axis1=0, axis2=2), -1, 0)


def _mixer_layouts(got):
    full = {}
    w_in = got["w_in"][:, :IN_SHARD].reshape(IN_COLS, D_MODEL)
    full["w_in"] = jnp.pad(w_in, ((0, IN_PAD - IN_COLS), (0, 0)))
    full["wq"] = got["w_q_b"].reshape(HEADS_W, Q_LORA)
    kvb = got["w_kv_b"]
    zeros = jnp.zeros((N_HEADS, KV_LORA, HEAD_PAD - QK_NOPE), BF16)
    k_part = jnp.transpose(jnp.concatenate([kvb[:, :, :QK_NOPE], zeros], axis=2), (1, 0, 2)).reshape(KV_LORA, HEADS_W)
    rope_rows = jnp.pad(jnp.eye(QK_ROPE, dtype=BF16), ((0, HEAD_PAD - QK_ROPE), (QK_NOPE, HEAD_PAD - QK_NOPE - QK_ROPE)))
    full["wk"] = jnp.concatenate([k_part, jnp.tile(rope_rows, (1, N_HEADS))], axis=0)
    full["wv"] = jnp.transpose(jnp.concatenate([kvb[:, :, QK_NOPE:], zeros], axis=2), (1, 0, 2)).reshape(KV_LORA, HEADS_W)
    w_out = got["w_out"].reshape(D_MODEL, D_MODEL)
    full["w_out_lru"] = w_out[:LRU_W]
    full["w_out_mla"] = jnp.pad(w_out[LRU_W:].reshape(N_HEADS, V_DIM, D_MODEL),
                                ((0, 0), (0, HEAD_PAD - V_DIM), (0, 0))).reshape(HEADS_W, D_MODEL)
    conv_w = jnp.transpose(got["conv_w"][:, :CONV_W, :LRU_W // N_DEV], (1, 0, 2)).reshape(CONV_W, LRU_W)
    full["conv_w"] = jnp.pad(conv_w, ((0, 8 - CONV_W), (0, 0)))
    return full


_FFN1 = ["w_ffn1_gate", "w_ffn1_up", "w_ffn1_down"]
_MIX = ["w_in", "w_q_b", "w_kv_b", "w_out"]
_FFN2 = ["w_ffn2_gate", "w_ffn2_up", "w_ffn2_down"]
_SHARDED = _FFN1 + _MIX + _FFN2
_TRANSPOSED = ("w_ffn1_gate", "w_ffn1_up", "w_in", "w_q_b", "w_ffn2_gate", "w_ffn2_up")
_SMALL = ["g_ffn1_pre", "g_ffn1_post", "g_mix_pre", "g_mix_post", "conv_b", "w_lru_a", "b_lru_a", "w_lru_x",
          "b_lru_x", "lru_lambda", "q_a_norm", "kv_a_norm", "g_ffn2_pre", "g_ffn2_post", "conv_w"]
_ORDER = ["g_ffn1_pre", "g_ffn1_post", "w_ffn1_gate", "w_ffn1_up", "w_ffn1_down", "g_mix_pre", "g_mix_post", "w_in",
          "conv_w", "conv_b", "w_lru_a", "b_lru_a", "w_lru_x", "b_lru_x", "lru_lambda", "q_a_norm", "w_q_b",
          "kv_a_norm", "w_kv_b", "w_out", "g_ffn2_pre", "g_ffn2_post", "w_ffn2_gate", "w_ffn2_up", "w_ffn2_down"]
_FF_BLOCK = (FF_PAD, D_MODEL)
_PADS = {"w_ffn1_gate": _FF_BLOCK, "w_ffn1_up": _FF_BLOCK, "w_ffn1_down": _FF_BLOCK,
         "w_in": (IN_SHARD_PAD, D_MODEL), "w_q_b": (HEAD_PAD, Q_LORA), "w_kv_b": (KV_LORA, HEAD_PAD),
         "w_out": (HEAD_PAD, D_MODEL), "conv_w": (8, 128),
         "w_ffn2_gate": _FF_BLOCK, "w_ffn2_up": _FF_BLOCK, "w_ffn2_down": _FF_BLOCK}


def _rows(name, a):
    return a.T if name in _TRANSPOSED else a


def _step(args):
    x = args["x"][0]
    target = args["loss_target"][0]
    positions = args["positions"][0]
    w = {n: args[n][0] for n in _ORDER}
    mom = {n: args["m_" + n][0] for n in _ORDER}
    var = {n: args["v_" + n][0] for n in _ORDER}
    vec = lambda name: w[name].reshape(1, -1)
    t = x.shape[0]
    ff_blocks = lambda a: a.reshape(N_DEV, FF_PAD, D_MODEL)
    ff_rows = lambda a: a.reshape(FF_VIRT, D_MODEL)

    names = _FFN1 + _MIX + ["conv_w"] + _FFN2
    (wg1, wu1), staged = _first_gather([_rows(n, w[n]) for n in names], [_PADS[n] for n in names],
                                       [F32 if n == "conv_w" else BF16 for n in names], 2)
    wg1, wu1 = ff_rows(wg1), ff_rows(wu1)
    n_mix = 1 + len(_MIX) + 1

    tables = _rope_tables(positions)
    wa = _block_diag(w["w_lru_a"]).astype(BF16)
    wx = _block_diag(w["w_lru_x"]).astype(BF16)

    (n1, gt1, up1), got = _ffn_fwd_hidden(x, vec("g_ffn1_pre"), wg1, wu1, _Gather(staged[:1]),
                                          "ffn_fwd_hidden_gather")
    wd1 = ff_rows(got[0])
    (h1, f1), got = _ffn_fwd_out(x, gt1, up1, vec("g_ffn1_post"), wd1, _Gather(staged[1:n_mix]),
                                 "ffn_fwd_out_gather")
    mix = _mixer_layouts(dict(zip(_MIX + ["conv_w"], got)))
    nmix, xl, gate, qlat, kvlat, krope = _mixer_in(h1, vec("g_mix_pre"), mix["w_in"])
    xc, a, u = _lru_gates(xl, mix["conv_w"], vec("conv_b"), wa, wx, vec("b_lru_a"), vec("b_lru_x"), vec("lru_lambda"))
    hl = _lru_scan(a, u, reverse=False)
    q, k, v, qn, kva, vt = _mla_proj(qlat, kvlat, krope, tables, vec("q_a_norm"), vec("kv_a_norm"),
                                 mix["wq"], mix["wk"], mix["wv"])
    (o, lse), ffn2 = _attn_fwd(q, k, vt, _Gather(staged[n_mix:]))
    wg2, wu2, wd2 = [ff_rows(a_) for a_ in ffn2]
    h2, m, ylru = _mixer_out(h1, hl, gate, o, mix["w_out_lru"], mix["w_out_mla"], vec("g_mix_post"))
    (h3, f2, n2, gt2, up2), _ = _ffn_fwd(h2, vec("g_ffn2_pre"), vec("g_ffn2_post"), wg2, wu2, wd2, None, "ffn_fwd")
    dy, loss_tile = _loss_head(h3, target)

    (df2, act2, dgt2, dup2, dg2post), _ = _ffn_bwd_hidden(dy, f2, gt2, up2, vec("g_ffn2_post"), wd2, None,
                                                          "ffn_bwd_hidden")
    (dh2, dg2pre), _ = _ffn_bwd_input(dy, h2, dgt2, dup2, vec("g_ffn2_pre"), wg2, wu2, None, "ffn_bwd_input")
    dw_ffn = lambda a_, b_, name, comm=None: _mm_tn(a_, b_, FF_VIRT, FF_CHUNK, D_MODEL, name, comm)
    big2 = [ff_blocks(dw_ffn(dgt2, n2, "dw_ffn_gate")[0]), ff_blocks(dw_ffn(dup2, n2, "dw_ffn_up")[0]),
            ff_blocks(dw_ffn(act2, df2, "dw_ffn_down")[0])]

    dm, dgate, dhl, do, dgmixpost = _mixer_out_bwd(dh2, m, hl, gate, mix["w_out_lru"], mix["w_out_mla"],
                                                   vec("g_mix_post"))
    dw_out_lru, _ = _mm_tn(ylru, dm, LRU_W, LRU_W, D_MODEL, "dw_out_lru")
    dw_out_mla, _ = _mm_tn(o, dm, HEADS_W, HEADS_W, D_MODEL, "dw_out_mla")
    dw_out = jnp.concatenate(
        [dw_out_lru, dw_out_mla.reshape(N_HEADS, HEAD_PAD, D_MODEL)[:, :V_DIM].reshape(N_HEADS * V_DIM, D_MODEL)], axis=0)

    (dq, dk, dv), terms2 = _attn_bwd(q, k, v, o, do, lse, _Exchange(big2))
    dqr, dqlat, dkvlat, dkrope, dgq, dgkv = _mla_proj_bwd(
        dq, dk, dv, qlat, kvlat, tables, vec("q_a_norm"), vec("kv_a_norm"), mix["wq"], mix["wk"], mix["wv"])
    dwq, _ = _mm_tn(dqr, qn, HEADS_W, HEADS_W, Q_LORA, "dw_q_b")
    dwk, _ = _mm_tn(kva, dk, KV_LORA, KV_LORA, HEADS_W, "dw_kv_b_k")
    dwv, _ = _mm_tn(kva, dv, KV_LORA, KV_LORA, HEADS_W, "dw_kv_b_v")
    dw_kvb = jnp.transpose(jnp.concatenate(
        [dwk.reshape(KV_LORA, N_HEADS, HEAD_PAD)[:, :, :QK_NOPE], dwv.reshape(KV_LORA, N_HEADS, HEAD_PAD)[:, :, :V_DIM]],
        axis=2), (1, 0, 2))

    dh = _lru_scan(a, dhl, reverse=True)
    dxc, dwa, dwx, dba, dbx, dlam = _lru_gates_bwd(dh, hl, xc, wa, wx, vec("b_lru_a"), vec("b_lru_x"),
                                                   vec("lru_lambda"))
    dxl, dconv_w8, dconv_b = _conv_bwd(dxc, xl, mix["conv_w"])
    dh1, dproj, dgmixpre = _mixer_in_bwd(dh2, h1, vec("g_mix_pre"), mix["w_in"], [dxl, dgate, dqlat, dkvlat, dkrope])
    dw_in, _ = _mm_tn(dproj, nmix, IN_PAD, IN_PAD // 2, D_MODEL, "dw_in")
    dw_in = jnp.pad(dw_in[:IN_COLS].reshape(N_DEV, IN_SHARD, D_MODEL), ((0, 0), (0, IN_SHARD_PAD - IN_SHARD), (0, 0)))
    big_mix = [dw_in, dwq.reshape(N_DEV, HEAD_PAD, Q_LORA), dw_kvb, dw_out.reshape(N_DEV, HEAD_PAD, D_MODEL)]

    (df1, act1, dgt1, dup1, dg1post), terms_mix = _ffn_bwd_hidden(
        dh1, f1, gt1, up1, vec("g_ffn1_post"), wd1, _Exchange(big_mix), "ffn_bwd_hidden_exchange")
    dwd1 = ff_blocks(dw_ffn(act1, df1, "dw_ffn_down")[0])
    dwg1, (swap_d,) = dw_ffn(dgt1, n1, "dw_ffn_gate_exchange", _PairSwap([dwd1]))
    dwg1 = ff_blocks(dwg1)
    dwu1, (terms_d1, swap_g) = dw_ffn(dup1, n1, "dw_ffn_up_exchange",
                                      _Several([_ChipExchange([_pair_sum(dwd1, swap_d)]), _PairSwap([dwg1])]))
    dwu1 = ff_blocks(dwu1)
    (dx, dg1pre), (terms_g1, swap_u) = _ffn_bwd_input(
        dh1, x, dgt1, dup1, vec("g_ffn1_pre"), wg1, wu1,
        _Several([_ChipExchange([_pair_sum(dwg1, swap_g)]), _PairSwap([dwu1])]), "ffn_bwd_input_exchange")

    small = {"g_ffn1_pre": dg1pre, "g_ffn1_post": dg1post, "g_mix_pre": dgmixpre, "g_mix_post": dgmixpost,
             "conv_b": dconv_b, "w_lru_a": _diag_blocks(dwa), "b_lru_a": dba, "w_lru_x": _diag_blocks(dwx),
             "b_lru_x": dbx, "lru_lambda": dlam, "q_a_norm": dgq, "kv_a_norm": dgkv, "g_ffn2_pre": dg2pre,
             "g_ffn2_post": dg2post, "conv_w": dconv_w8[:CONV_W]}
    packed = _pack([small[n] for n in _SMALL] + [loss_tile[:1, :1]])
    terms_u1, small_terms = _last_exchange(_Several([
        _ChipExchange([_pair_sum(dwu1, swap_u)]),
        _Exchange([jnp.broadcast_to(packed[None], (N_DEV,) + packed.shape)])]))

    terms = dict(zip(_FFN2, terms2))
    terms.update(zip(_MIX, terms_mix))
    terms.update({"w_ffn1_down": terms_d1, "w_ffn1_gate": terms_g1, "w_ffn1_up": terms_u1})

    grads, delta, new_m, new_v = {}, {}, {}, {}
    for n in _SHARDED:
        res = _sum_adamw(terms[n], _rows(n, w[n]), _rows(n, mom[n]), _rows(n, var[n]), "adamw_" + n)
        grads[n], delta[n], new_m[n], new_v[n] = [_rows(n, r) for r in res]

    summed = _sum_terms(small_terms)
    small_sum = dict(zip(_SMALL + ["loss"], _unpack(summed, [small[n].shape for n in _SMALL] + [(1, 1)])))
    loss = small_sum.pop("loss").reshape(())
    me = 4 * lax.axis_index("x") + 2 * lax.axis_index("y") + lax.axis_index("c")
    cshard = LRU_W // N_DEV
    small_sum["conv_w"] = lax.dynamic_slice(small_sum["conv_w"], (0, me * cshard), (CONV_W, cshard))
    for n in _SMALL:
        grads[n] = small_sum[n].reshape(w[n].shape)
    shapes = [w[n].shape for n in _SMALL]
    d_p, m_p, v_p = _adamw_flat(_pack([w[n] for n in _SMALL]), _pack([grads[n] for n in _SMALL]),
                                _pack([mom[n] for n in _SMALL]), _pack([var[n] for n in _SMALL]))
    for n, d_, m_, v_ in zip(_SMALL, _unpack(d_p, shapes), _unpack(m_p, shapes), _unpack(v_p, shapes)):
        delta[n], new_m[n], new_v[n] = d_, m_, v_

    lead = lambda d: [d[n][None] for n in _ORDER]
    return (loss, dx[None], *lead(grads), *lead(delta), *lead(new_m), *lead(new_v))


def kernel(x, positions, g_ffn1_pre, g_ffn1_post, w_ffn1_gate, w_ffn1_up, w_ffn1_down, g_mix_pre, g_mix_post, w_in, conv_w, conv_b, w_lru_a, b_lru_a, w_lru_x, b_lru_x, lru_lambda, q_a_norm, w_q_b, kv_a_norm, w_kv_b, w_out, g_ffn2_pre, g_ffn2_post, w_ffn2_gate, w_ffn2_up, w_ffn2_down, loss_target, m_g_ffn1_pre, m_g_ffn1_post, m_w_ffn1_gate, m_w_ffn1_up, m_w_ffn1_down, m_g_mix_pre, m_g_mix_post, m_w_in, m_conv_w, m_conv_b, m_w_lru_a, m_b_lru_a, m_w_lru_x, m_b_lru_x, m_lru_lambda, m_q_a_norm, m_w_q_b, m_kv_a_norm, m_w_kv_b, m_w_out, m_g_ffn2_pre, m_g_ffn2_post, m_w_ffn2_gate, m_w_ffn2_up, m_w_ffn2_down, v_g_ffn1_pre, v_g_ffn1_post, v_w_ffn1_gate, v_w_ffn1_up, v_w_ffn1_down, v_g_mix_pre, v_g_mix_post, v_w_in, v_conv_w, v_conv_b, v_w_lru_a, v_b_lru_a, v_w_lru_x, v_b_lru_x, v_lru_lambda, v_q_a_norm, v_w_q_b, v_kv_a_norm, v_w_kv_b, v_w_out, v_g_ffn2_pre, v_g_ffn2_post, v_w_ffn2_gate, v_w_ffn2_up, v_w_ffn2_down):
    return _step(dict(locals()))
```

```python
import functools
import math
import operator

import jax
import jax.numpy as jnp
from jax import lax
from jax.experimental import pallas as pl
from jax.experimental.pallas import tpu as pltpu

F32 = jnp.float32
BF16 = jnp.bfloat16
MESH = pl.DeviceIdType.MESH

N_DEV = 8
D_MODEL = 1024
D_FF = 2816
FF_SHARD = D_FF // N_DEV
FF_PAD = 384
FF_VIRT = N_DEV * FF_PAD
FF_CHUNK = 2 * FF_PAD
LRU_W = 512
N_HEADS = 8
HEAD_PAD = 128
HEADS_W = N_HEADS * HEAD_PAD
QK_NOPE = 64
QK_ROPE = 32
V_DIM = 64
Q_LORA = 384
KV_LORA = 256
IN_COLS = 2 * LRU_W + Q_LORA + KV_LORA + QK_ROPE
IN_SHARD = IN_COLS // N_DEV
IN_SHARD_PAD = 256
IN_PAD = 1792
QB_SHARD = 96
CONV_W = 4
CHUNK = 64
EPS = 1e-6
LRU_C = 8.0
ROPE_THETA = 10000.0
ATT_SCALE = (QK_NOPE + QK_ROPE) ** -0.5

ADAM_LR = 0.001
ADAM_B1 = 0.9
ADAM_B2 = 0.999
ADAM_EPS = 1e-08
ADAM_WD = 0.01
ADAM_STEP = 10

VMEM_LIMIT = 56 * 1024 * 1024
ANY = pl.BlockSpec(memory_space=pl.ANY)
WHOLE = pl.BlockSpec(memory_space=pltpu.VMEM)


def _params(**kw):
    return pltpu.CompilerParams(vmem_limit_bytes=VMEM_LIMIT, **kw)


def _full(shape):
    return pl.BlockSpec(shape, lambda *_: (0,) * len(shape))


def _dot(a, b):
    return jnp.dot(a, b, preferred_element_type=F32)


def _dot_nt(a, b):
    return lax.dot_general(a, b, (((1,), (1,)), ((), ())), preferred_element_type=F32)


def _dot_tn(a, b):
    return lax.dot_general(a, b, (((0,), (0,)), ((), ())), preferred_element_type=F32)


def _rms(x, g):
    r = lax.rsqrt(jnp.mean(x * x, axis=-1, keepdims=True) + EPS)
    return x * r * g


def _rms_bwd(x, g, dy):
    r = lax.rsqrt(jnp.mean(x * x, axis=-1, keepdims=True) + EPS)
    xh = x * r
    dg = jnp.sum(dy * xh, axis=0, keepdims=True)
    dxh = dy * g
    dx = r * (dxh - xh * jnp.mean(dxh * xh, axis=-1, keepdims=True))
    return dx, dg


def _sigmoid(x):
    return 0.5 * jnp.tanh(0.5 * x) + 0.5


_GELU_C = math.sqrt(2.0 / math.pi)


def _gelu(x):
    t = jnp.tanh(_GELU_C * (x + 0.044715 * x * x * x))
    return 0.5 * x * (1.0 + t)


def _gelu_grad(x):
    t = jnp.tanh(_GELU_C * (x + 0.044715 * x * x * x))
    return 0.5 * (1.0 + t) + 0.5 * x * (1.0 - t * t) * _GELU_C * (1.0 + 3.0 * 0.044715 * x * x)


def _tile_rows(t):
    return 512 if t >= 2048 else t // 2


def _dev_index(p):
    return 4 * p[0] + 2 * p[1] + p[2]


def _place():
    x, y, c = lax.axis_index("x"), lax.axis_index("y"), lax.axis_index("c")
    return (x, y, c), (x, y, 1 - c), [(1 - x, y), (x, 1 - y), (1 - x, 1 - y)]


def _dma_sems(n):
    return [pltpu.SemaphoreType.DMA((7 * n,)), pltpu.SemaphoreType.DMA((7 * n,)), pltpu.SemaphoreType.DMA((n,))]


class _Gather:
    has_middle = True

    def __init__(self, arrays):
        self.inputs = list(arrays)
        self.n = len(arrays)
        self.out_shape = [jax.ShapeDtypeStruct((N_DEV,) + a.shape, a.dtype) for a in arrays]
        self.scratch = _dma_sems(self.n)

    @staticmethod
    def _copy(outs, sems, k, s, block, to, src=None):
        rows = outs[k].at[_dev_index(block)]
        return pltpu.make_async_remote_copy(
            src_ref=rows if src is None else src, dst_ref=rows,
            send_sem=sems[0].at[7 * k + s], recv_sem=sems[1].at[7 * k + s], device_id=to, device_id_type=MESH)

    def _first(self, srcs, outs, sems):
        me, sibling, chips = _place()
        mine = [pltpu.make_async_copy(srcs[k], outs[k].at[_dev_index(me)], sems[2].at[k]) for k in range(self.n)]
        sends = []
        for k in range(self.n):
            sends.append(self._copy(outs, sems, k, 0, me, sibling, src=srcs[k]))
            sends += [self._copy(outs, sems, k, 1 + j, me, (*chip, me[2]), src=srcs[k])
                      for j, chip in enumerate(chips)]
        return mine, sends

    def _passed(self, outs, sems):
        me, sibling, chips = _place()
        return [[self._copy(outs, sems, k, 4 + j, (*chip, me[2]), sibling) for k in range(self.n)]
                for j, chip in enumerate(chips)]

    def start(self, srcs, outs, sems):
        mine, sends = self._first(srcs, outs, sems)
        for cp in mine + sends:
            cp.start()

    def middle(self, srcs, outs, sems):
        me, sibling, chips = _place()
        passed = self._passed(outs, sems)
        for j, chip in enumerate(chips):
            for k in range(self.n):
                self._copy(outs, sems, k, 1 + j, (*chip, me[2]), me).wait_recv()
                passed[j][k].start()

    def finish(self, srcs, outs, sems):
        me, sibling, chips = _place()
        for k in range(self.n):
            self._copy(outs, sems, k, 0, sibling, me).wait_recv()
        for j, chip in enumerate(chips):
            for k in range(self.n):
                self._copy(outs, sems, k, 4 + j, (*chip, 1 - me[2]), me).wait_recv()
        mine, sends = self._first(srcs, outs, sems)
        for cp in sends + [cp for row in self._passed(outs, sems) for cp in row]:
            cp.wait_send()
        for cp in mine:
            cp.wait()


class _Exchange:
    has_middle = False

    def __init__(self, arrays):
        self.inputs = list(arrays)
        self.n = len(arrays)
        self.out_shape = [jax.ShapeDtypeStruct(a.shape, a.dtype) for a in arrays]
        self.scratch = _dma_sems(self.n)

    def _copies(self, srcs, outs, sems):
        (x, y, c), _, _ = _place()
        me = _dev_index((x, y, c))
        local = [pltpu.make_async_copy(srcs[k].at[me], outs[k].at[me], sems[2].at[k]) for k in range(self.n)]
        remote = []
        for s in range(1, N_DEV):
            peer = (1 - x if s & 4 else x, 1 - y if s & 2 else y, 1 - c if s & 1 else c)
            for k in range(self.n):
                remote.append(pltpu.make_async_remote_copy(
                    src_ref=srcs[k].at[_dev_index(peer)], dst_ref=outs[k].at[me],
                    send_sem=sems[0].at[7 * k + s - 1], recv_sem=sems[1].at[7 * k + s - 1],
                    device_id=peer, device_id_type=MESH))
        return local, remote

    def start(self, srcs, outs, sems):
        local, remote = self._copies(srcs, outs, sems)
        for cp in local + remote:
            cp.start()

    def finish(self, srcs, outs, sems):
        local, remote = self._copies(srcs, outs, sems)
        for cp in remote:
            cp.wait_recv()
        for cp in remote:
            cp.wait_send()
        for cp in local:
            cp.wait()


class _PairSwap:
    has_middle = False

    def __init__(self, arrays):
        self.inputs = list(arrays)
        self.n = len(arrays)
        self.out_shape = [jax.ShapeDtypeStruct((4,) + a.shape[1:], a.dtype) for a in arrays]
        self.scratch = _dma_sems(self.n)

    def _copies(self, srcs, outs, sems):
        (x, y, c), sibling, _ = _place()
        return [pltpu.make_async_remote_copy(
            src_ref=srcs[k].at[2 * q + 1 - c], dst_ref=outs[k].at[q],
            send_sem=sems[0].at[7 * k + q], recv_sem=sems[1].at[7 * k + q], device_id=sibling, device_id_type=MESH)
            for q in range(4) for k in range(self.n)]

    def start(self, srcs, outs, sems):
        for cp in self._copies(srcs, outs, sems):
            cp.start()

    def finish(self, srcs, outs, sems):
        copies = self._copies(srcs, outs, sems)
        for cp in copies:
            cp.wait_recv()
        for cp in copies:
            cp.wait_send()


class _ChipExchange:
    has_middle = False

    def __init__(self, arrays):
        self.inputs = list(arrays)
        self.n = len(arrays)
        self.out_shape = [jax.ShapeDtypeStruct(a.shape, a.dtype) for a in arrays]
        self.scratch = _dma_sems(self.n)

    def _copies(self, srcs, outs, sems):
        (x, y, c), _, _ = _place()
        mine = 2 * x + y
        local = [pltpu.make_async_copy(srcs[k].at[mine], outs[k].at[mine], sems[2].at[k]) for k in range(self.n)]
        remote = []
        for s in range(1, 4):
            px, py = (1 - x if s & 2 else x), (1 - y if s & 1 else y)
            for k in range(self.n):
                remote.append(pltpu.make_async_remote_copy(
                    src_ref=srcs[k].at[2 * px + py], dst_ref=outs[k].at[mine],
                    send_sem=sems[0].at[7 * k + s], recv_sem=sems[1].at[7 * k + s],
                    device_id=(px, py, c), device_id_type=MESH))
        return local, remote

    start = _Exchange.start
    finish = _Exchange.finish


class _Several:
    def __init__(self, comms):
        self.comms = comms
        self.inputs = [a for cm in comms for a in cm.inputs]
        self.n = len(self.inputs)
        self.out_shape = [s for cm in comms for s in cm.out_shape]
        self.scratch = [s for cm in comms for s in cm.scratch]
        self.has_middle = any(cm.has_middle for cm in comms)

    def _each(self, method, srcs, outs, sems):
        pos = 0
        for i, cm in enumerate(self.comms):
            if hasattr(cm, method):
                getattr(cm, method)(srcs[pos:pos + cm.n], outs[pos:pos + cm.n], sems[3 * i:3 * i + 3])
            pos += cm.n

    def start(self, srcs, outs, sems):
        self._each("start", srcs, outs, sems)

    def middle(self, srcs, outs, sems):
        self._each("middle", srcs, outs, sems)

    def finish(self, srcs, outs, sems):
        self._each("finish", srcs, outs, sems)


def _call(body, comm, *, name, grid, in_specs, out_specs, out_shape, scratch_shapes=(), args):
    in_specs, out_specs, out_shape = list(in_specs), list(out_specs), list(out_shape)
    scratch_shapes = list(scratch_shapes)
    if comm is None:
        outs = pl.pallas_call(body, name=name, grid=grid, in_specs=in_specs, out_specs=out_specs, out_shape=out_shape,
                              scratch_shapes=scratch_shapes, compiler_params=_params())(*args)
        return list(outs), []
    n_in, n_out, n_scr, c_n = len(in_specs), len(out_specs), len(scratch_shapes), comm.n
    middle = tuple(g - 1 if d == 0 else 0 for d, g in enumerate(grid))

    def hosted(*refs):
        pos = 0
        parts = []
        for width in (n_in, c_n, n_out, c_n, n_scr, len(comm.scratch)):
            parts.append(refs[pos:pos + width])
            pos += width
        ins, c_in, outs, c_out, scr, sems = parts
        ids = [pl.program_id(d) for d in range(len(grid))]
        at = lambda where: functools.reduce(operator.and_, [i == w for i, w in zip(ids, where)])

        @pl.when(at([0] * len(grid)))
        def _():
            comm.start(c_in, c_out, sems)

        body(*ins, *outs, *scr)

        if comm.has_middle:
            @pl.when(at(middle))
            def _():
                comm.middle(c_in, c_out, sems)

        @pl.when(at([g - 1 for g in grid]))
        def _():
            comm.finish(c_in, c_out, sems)

    outs = pl.pallas_call(
        hosted, name=name, grid=grid,
        in_specs=in_specs + [ANY] * c_n, out_specs=out_specs + [ANY] * c_n,
        out_shape=out_shape + comm.out_shape, scratch_shapes=scratch_shapes + comm.scratch,
        compiler_params=_params())(*args, *comm.inputs)
    return list(outs[:n_out]), list(outs[n_out:])


def _first_gather(shards, pads, dtypes, n_gathered):
    n = len(shards)
    gather = _Gather([jax.ShapeDtypeStruct(tuple(pads[k]), dtypes[k]) for k in range(n_gathered)])

    def body(*refs):
        ins, outs = refs[:n], refs[n:2 * n]
        stages = list(refs[2 * n:2 * n + n_gathered]) + list(outs[n_gathered:])
        sems = refs[2 * n + n_gathered:]
        for k in range(n):
            r, cc = ins[k].shape
            if (r, cc) != tuple(stages[k].shape):
                stages[k][...] = jnp.zeros(stages[k].shape, stages[k].dtype)
            stages[k][:r, :cc] = ins[k][...].astype(stages[k].dtype)
        gather.start(stages[:n_gathered], outs[:n_gathered], sems)
        gather.middle(stages[:n_gathered], outs[:n_gathered], sems)
        gather.finish(stages[:n_gathered], outs[:n_gathered], sems)

    staged_shape = [jax.ShapeDtypeStruct(tuple(pads[k]), dtypes[k]) for k in range(n_gathered, n)]
    outs = pl.pallas_call(
        body, name="first_gather",
        out_shape=gather.out_shape + staged_shape,
        in_specs=[WHOLE] * n,
        out_specs=[ANY] * n_gathered + [WHOLE] * (n - n_gathered),
        scratch_shapes=[pltpu.VMEM(tuple(pads[k]), dtypes[k]) for k in range(n_gathered)] + gather.scratch,
        compiler_params=_params(),
    )(*shards)
    return list(outs[:n_gathered]), list(outs[n_gathered:])


def _last_exchange(comm):
    def body(*refs):
        srcs, outs, sems = refs[:comm.n], refs[comm.n:2 * comm.n], refs[2 * comm.n:]
        comm.start(srcs, outs, sems)
        comm.finish(srcs, outs, sems)

    return pl.pallas_call(
        body, name="last_exchange", out_shape=comm.out_shape, in_specs=[ANY] * comm.n, out_specs=[ANY] * comm.n,
        scratch_shapes=comm.scratch, compiler_params=_params())(*comm.inputs)


def _pair_sum(blocks, got):
    _, r, c = blocks.shape
    mine = lax.dynamic_index_in_dim(blocks.reshape(4, 2, r, c), lax.axis_index("c"), axis=1, keepdims=False)

    def body(a_ref, b_ref, out_ref):
        out_ref[...] = (a_ref[...].astype(F32) + b_ref[...].astype(F32)).astype(out_ref.dtype)

    spec = pl.BlockSpec((None, r, c), lambda q: (q, 0, 0))
    return pl.pallas_call(
        body, name="pair_sum", grid=(4,), in_specs=[spec, spec], out_specs=spec,
        out_shape=jax.ShapeDtypeStruct((4, r, c), blocks.dtype), compiler_params=_params())(mine, got)


def _ffn_fwd(h0, g_pre, g_post, wg, wu, wd, comm, name):
    t = h0.shape[0]
    tt = _tile_rows(t)
    n_chunks = FF_VIRT // FF_CHUNK

    def body(h0_ref, gpre_ref, gpost_ref, wg_ref, wu_ref, wd_ref,
             h1_ref, f_ref, n1_ref, gt_ref, up_ref, acc_ref, n1s_ref):
        j = pl.program_id(1)

        @pl.when(j == 0)
        def _():
            n1 = _rms(h0_ref[...], gpre_ref[...]).astype(BF16)
            n1s_ref[...] = n1
            n1_ref[...] = n1
            acc_ref[...] = jnp.zeros_like(acc_ref)

        n1 = n1s_ref[...]
        gt = _dot_nt(n1, wg_ref[...])
        up = _dot_nt(n1, wu_ref[...])
        gt_ref[...] = gt.astype(BF16)
        up_ref[...] = up.astype(BF16)
        act = (gt * _sigmoid(gt) * up).astype(BF16)
        acc_ref[...] += _dot(act, wd_ref[...])

        @pl.when(j == n_chunks - 1)
        def _():
            f = acc_ref[...]
            f_ref[...] = f
            h1_ref[...] = h0_ref[...] + 0.5 * _rms(f, gpost_ref[...])

    row = lambda i, j: (i, 0)
    wspec = pl.BlockSpec((FF_CHUNK, D_MODEL), lambda i, j: (j, 0))
    return _call(
        body, comm, name=name, grid=(t // tt, n_chunks),
        in_specs=[pl.BlockSpec((tt, D_MODEL), row), _full((1, D_MODEL)), _full((1, D_MODEL)), wspec, wspec, wspec],
        out_specs=[pl.BlockSpec((tt, D_MODEL), row), pl.BlockSpec((tt, D_MODEL), row),
                   pl.BlockSpec((tt, D_MODEL), row),
                   pl.BlockSpec((tt, FF_CHUNK), lambda i, j: (i, j)),
                   pl.BlockSpec((tt, FF_CHUNK), lambda i, j: (i, j))],
        out_shape=[jax.ShapeDtypeStruct((t, D_MODEL), F32), jax.ShapeDtypeStruct((t, D_MODEL), F32),
                   jax.ShapeDtypeStruct((t, D_MODEL), BF16),
                   jax.ShapeDtypeStruct((t, FF_VIRT), BF16), jax.ShapeDtypeStruct((t, FF_VIRT), BF16)],
        scratch_shapes=[pltpu.VMEM((tt, D_MODEL), F32), pltpu.VMEM((tt, D_MODEL), BF16)],
        args=(h0, g_pre, g_post, wg, wu, wd))


def _ffn_fwd_gate(h0, g_pre, wg, comm, name):
    t = h0.shape[0]
    tt = 2 * _tile_rows(t)

    def body(h0_ref, gpre_ref, wg_ref, n1_ref, gt_ref):
        @pl.when(pl.program_id(1) == 0)
        def _():
            n1_ref[...] = _rms(h0_ref[...], gpre_ref[...]).astype(BF16)

        gt_ref[...] = _dot_nt(n1_ref[...], wg_ref[...]).astype(BF16)

    row = lambda i, j: (i, 0)
    return _call(
        body, comm, name=name, grid=(t // tt, FF_VIRT // FF_CHUNK),
        in_specs=[pl.BlockSpec((tt, D_MODEL), row), _full((1, D_MODEL)),
                  pl.BlockSpec((FF_CHUNK, D_MODEL), lambda i, j: (j, 0))],
        out_specs=[pl.BlockSpec((tt, D_MODEL), row), pl.BlockSpec((tt, FF_CHUNK), lambda i, j: (i, j))],
        out_shape=[jax.ShapeDtypeStruct((t, D_MODEL), BF16), jax.ShapeDtypeStruct((t, FF_VIRT), BF16)],
        args=(h0, g_pre, wg))


def _ffn_fwd_up(n1, wu, comm, name):
    t = n1.shape[0]
    tt = 2 * _tile_rows(t)

    def body(n1_ref, wu_ref, up_ref):
        up_ref[...] = _dot_nt(n1_ref[...], wu_ref[...]).astype(BF16)

    return _call(
        body, comm, name=name, grid=(t // tt, FF_VIRT // FF_CHUNK),
        in_specs=[pl.BlockSpec((tt, D_MODEL), lambda i, j: (i, 0)),
                  pl.BlockSpec((FF_CHUNK, D_MODEL), lambda i, j: (j, 0))],
        out_specs=[pl.BlockSpec((tt, FF_CHUNK), lambda i, j: (i, j))],
        out_shape=[jax.ShapeDtypeStruct((t, FF_VIRT), BF16)],
        args=(n1, wu))


def _ffn_fwd_out(h0, gt, up, g_post, wd, comm, name):
    t = h0.shape[0]
    tt = 2 * _tile_rows(t)
    n_chunks = FF_VIRT // FF_CHUNK

    def body(h0_ref, gt_ref, up_ref, gpost_ref, wd_ref, h1_ref, f_ref):
        j = pl.program_id(1)
        g = gt_ref[...].astype(F32)
        term = _dot((g * _sigmoid(g) * up_ref[...].astype(F32)).astype(BF16), wd_ref[...])

        @pl.when(j == 0)
        def _():
            f_ref[...] = term

        @pl.when(j > 0)
        def _():
            f_ref[...] += term

        @pl.when(j == n_chunks - 1)
        def _():
            h1_ref[...] = h0_ref[...] + 0.5 * _rms(f_ref[...], gpost_ref[...])

    row = lambda i, j: (i, 0)
    chunk = pl.BlockSpec((tt, FF_CHUNK), lambda i, j: (i, j))
    return _call(
        body, comm, name=name, grid=(t // tt, n_chunks),
        in_specs=[pl.BlockSpec((tt, D_MODEL), row), chunk, chunk, _full((1, D_MODEL)),
                  pl.BlockSpec((FF_CHUNK, D_MODEL), lambda i, j: (j, 0))],
        out_specs=[pl.BlockSpec((tt, D_MODEL), row), pl.BlockSpec((tt, D_MODEL), row)],
        out_shape=[jax.ShapeDtypeStruct((t, D_MODEL), F32), jax.ShapeDtypeStruct((t, D_MODEL), F32)],
        args=(h0, gt, up, g_post, wd))


def _ffn_bwd_hidden(dh1, f, gt, up, g_post, wd, comm, name):
    t = f.shape[0]
    tt = _tile_rows(t)
    n_chunks = FF_VIRT // FF_CHUNK

    def body(dh1_ref, f_ref, gt_ref, up_ref, gpost_ref, wd_ref,
             df_ref, act_ref, dgt_ref, dup_ref, dgpost_ref, dfs_ref):
        i, j = pl.program_id(0), pl.program_id(1)

        @pl.when((i == 0) & (j == 0))
        def _():
            dgpost_ref[...] = jnp.zeros_like(dgpost_ref)

        @pl.when(j == 0)
        def _():
            df, dg = _rms_bwd(f_ref[...], gpost_ref[...], 0.5 * dh1_ref[...])
            dgpost_ref[...] += dg
            dfb = df.astype(BF16)
            dfs_ref[...] = dfb
            df_ref[...] = dfb

        da = _dot_nt(dfs_ref[...], wd_ref[...])
        g = gt_ref[...].astype(F32)
        u = up_ref[...].astype(F32)
        s = _sigmoid(g)
        sl = g * s
        act_ref[...] = (sl * u).astype(BF16)
        dgt_ref[...] = (da * u * (s * (1.0 + g * (1.0 - s)))).astype(BF16)
        dup_ref[...] = (da * sl).astype(BF16)

    row = lambda i, j: (i, 0)
    chunk = pl.BlockSpec((tt, FF_CHUNK), lambda i, j: (i, j))
    return _call(
        body, comm, name=name, grid=(t // tt, n_chunks),
        in_specs=[pl.BlockSpec((tt, D_MODEL), row), pl.BlockSpec((tt, D_MODEL), row), chunk, chunk,
                  _full((1, D_MODEL)), pl.BlockSpec((FF_CHUNK, D_MODEL), lambda i, j: (j, 0))],
        out_specs=[pl.BlockSpec((tt, D_MODEL), row), chunk, chunk, chunk, _full((1, D_MODEL))],
        out_shape=[jax.ShapeDtypeStruct((t, D_MODEL), BF16)] + [jax.ShapeDtypeStruct((t, FF_VIRT), BF16)] * 3 + [
            jax.ShapeDtypeStruct((1, D_MODEL), F32)],
        scratch_shapes=[pltpu.VMEM((tt, D_MODEL), BF16)],
        args=(dh1, f, gt, up, g_post, wd))


def _ffn_bwd_input(dh1, h0, dgt, dup, g_pre, wg, wu, comm, name):
    t = h0.shape[0]
    tt = 2 * _tile_rows(t)
    n_chunks = FF_VIRT // FF_CHUNK

    def body(dh1_ref, h0_ref, dgt_ref, dup_ref, gpre_ref, wg_ref, wu_ref, dh0_ref, dgpre_ref, acc_ref):
        i, j = pl.program_id(0), pl.program_id(1)

        @pl.when((i == 0) & (j == 0))
        def _():
            dgpre_ref[...] = jnp.zeros_like(dgpre_ref)

        @pl.when(j == 0)
        def _():
            acc_ref[...] = jnp.zeros_like(acc_ref)

        acc_ref[...] += _dot(dgt_ref[...], wg_ref[...]) + _dot(dup_ref[...], wu_ref[...])

        @pl.when(j == n_chunks - 1)
        def _():
            dx, dg = _rms_bwd(h0_ref[...], gpre_ref[...], acc_ref[...])
            dgpre_ref[...] += dg
            dh0_ref[...] = dh1_ref[...] + dx

    row = lambda i, j: (i, 0)
    chunk = pl.BlockSpec((tt, FF_CHUNK), lambda i, j: (i, j))
    wspec = pl.BlockSpec((FF_CHUNK, D_MODEL), lambda i, j: (j, 0))
    return _call(
        body, comm, name=name, grid=(t // tt, n_chunks),
        in_specs=[pl.BlockSpec((tt, D_MODEL), row), pl.BlockSpec((tt, D_MODEL), row), chunk, chunk,
                  _full((1, D_MODEL)), wspec, wspec],
        out_specs=[pl.BlockSpec((tt, D_MODEL), row), _full((1, D_MODEL))],
        out_shape=[jax.ShapeDtypeStruct((t, D_MODEL), F32), jax.ShapeDtypeStruct((1, D_MODEL), F32)],
        scratch_shapes=[pltpu.VMEM((tt, D_MODEL), F32)],
        args=(dh1, h0, dgt, dup, g_pre, wg, wu))


def _mm_tn(a, b, m, tm, tn, name, comm=None):
    t, n = b.shape
    tk = min(t, 2048)
    nk = t // tk

    def body(a_ref, b_ref, out_ref, acc_ref):
        k = pl.program_id(2)

        @pl.when(k == 0)
        def _():
            acc_ref[...] = jnp.zeros_like(acc_ref)

        acc_ref[...] += _dot_tn(a_ref[...], b_ref[...])

        @pl.when(k == nk - 1)
        def _():
            out_ref[...] = acc_ref[...].astype(out_ref.dtype)

    outs, got = _call(
        body, comm, name=name, grid=(m // tm, n // tn, nk),
        in_specs=[pl.BlockSpec((tk, tm), lambda i, j, k: (k, i)), pl.BlockSpec((tk, tn), lambda i, j, k: (k, j))],
        out_specs=[pl.BlockSpec((tm, tn), lambda i, j, k: (i, j))],
        out_shape=[jax.ShapeDtypeStruct((m, n), BF16)],
        scratch_shapes=[pltpu.VMEM((tm, tn), F32)], args=(a, b))
    return outs[0], got


_SPLITS = (0, LRU_W, 2 * LRU_W, 2 * LRU_W + Q_LORA, 2 * LRU_W + Q_LORA + KV_LORA, IN_PAD)
_WIDTHS = tuple(_SPLITS[k + 1] - _SPLITS[k] for k in range(5))


def _mixer_in(h1, g_pre, w_in):
    t = h1.shape[0]
    tt = _tile_rows(t)

    def body(h_ref, g_ref, w_ref, n_ref, *outs):
        n = _rms(h_ref[...], g_ref[...]).astype(BF16)
        n_ref[...] = n
        proj = _dot_nt(n, w_ref[...])
        for k in range(5):
            outs[k][...] = proj[:, _SPLITS[k]:_SPLITS[k + 1]]

    row = lambda i: (i, 0)
    return pl.pallas_call(
        body, name="mixer_in",
        grid=(t // tt,),
        in_specs=[pl.BlockSpec((tt, D_MODEL), row), _full((1, D_MODEL)), _full((IN_PAD, D_MODEL))],
        out_specs=[pl.BlockSpec((tt, D_MODEL), row)] + [pl.BlockSpec((tt, w), row) for w in _WIDTHS],
        out_shape=[jax.ShapeDtypeStruct((t, D_MODEL), BF16)] + [jax.ShapeDtypeStruct((t, w), F32) for w in _WIDTHS],
        compiler_params=_params(),
    )(h1, g_pre, w_in)


def _mixer_in_bwd(dh2, h1, g_pre, w_in, dparts):
    t = h1.shape[0]
    tt = _tile_rows(t)

    def body(dh2_ref, h_ref, g_ref, w_ref, p0, p1, p2, p3, p4, dh1_ref, dproj_ref, dg_ref):
        i = pl.program_id(0)

        @pl.when(i == 0)
        def _():
            dg_ref[...] = jnp.zeros_like(dg_ref)

        dproj = jnp.concatenate([p[...] for p in (p0, p1, p2, p3, p4)], axis=1)
        dproj_ref[...] = dproj
        dx, dg = _rms_bwd(h_ref[...], g_ref[...], _dot(dproj, w_ref[...]))
        dg_ref[...] += dg
        dh1_ref[...] = dh2_ref[...] + dx

    row = lambda i: (i, 0)
    return pl.pallas_call(
        body, name="mixer_in_bwd",
        grid=(t // tt,),
        in_specs=[pl.BlockSpec((tt, D_MODEL), row), pl.BlockSpec((tt, D_MODEL), row), _full((1, D_MODEL)),
                  _full((IN_PAD, D_MODEL))] + [pl.BlockSpec((tt, w), row) for w in _WIDTHS],
        out_specs=[pl.BlockSpec((tt, D_MODEL), row), pl.BlockSpec((tt, IN_PAD), row), _full((1, D_MODEL))],
        out_shape=[jax.ShapeDtypeStruct((t, D_MODEL), F32), jax.ShapeDtypeStruct((t, IN_PAD), BF16),
                   jax.ShapeDtypeStruct((1, D_MODEL), F32)],
        compiler_params=_params(),
    )(dh2, h1, g_pre, w_in, *dparts)


def _conv(x_prev8, x_tile, w_ref, b_ref):
    tt = x_tile.shape[0]
    xx = jnp.concatenate([x_prev8, x_tile], axis=0)
    y = b_ref[...] + w_ref[CONV_W - 1:CONV_W, :] * x_tile
    for k in range(CONV_W - 1):
        y = y + w_ref[k:k + 1, :] * pltpu.roll(xx, CONV_W - 1 - k, 0)[8:8 + tt]
    return y


def _neg_expm1(z):
    series = -z * (1.0 + z * 0.5 * (1.0 + z / 3.0 * (1.0 + z * 0.25 * (1.0 + z * 0.2 * (1.0 + z / 6.0)))))
    return jnp.where(z > -0.3, series, 1.0 - jnp.exp(z))


def _lru_elementwise(xc, pre_a, pre_i, lam):
    ra = _sigmoid(pre_a)
    ri = _sigmoid(pre_i)
    neg = -lam
    softplus = jnp.maximum(neg, 0.0) + jnp.log(1.0 + jnp.exp(-jnp.abs(neg)))
    log_a = -LRU_C * ra * softplus
    a = jnp.exp(log_a)
    mult = jnp.sqrt(_neg_expm1(2.0 * log_a))
    return a, mult * (ri * xc)


def _lru_gates(xl, conv_w, conv_b, wa, wx, ba, bx, lam):
    t = xl.shape[0]
    tt = _tile_rows(t)
    r8 = tt // 8

    def body(prev_ref, x_ref, cw_ref, cb_ref, wa_ref, wx_ref, ba_ref, bx_ref, lam_ref, xc_ref, a_ref, u_ref):
        i = pl.program_id(0)
        prev = jnp.where(i == 0, 0.0, prev_ref[...])
        xc = _conv(prev, x_ref[...], cw_ref, cb_ref)
        xc_ref[...] = xc
        xb = xc.astype(BF16)
        a, u = _lru_elementwise(xc, _dot(xb, wa_ref[...]) + ba_ref[...], _dot(xb, wx_ref[...]) + bx_ref[...],
                                lam_ref[...])
        a_ref[...] = a
        u_ref[...] = u

    row = lambda i: (i, 0)
    vec = _full((1, LRU_W))
    return pl.pallas_call(
        body, name="lru_gates",
        grid=(t // tt,),
        in_specs=[pl.BlockSpec((8, LRU_W), lambda i: (jnp.maximum(i * r8 - 1, 0), 0)),
                  pl.BlockSpec((tt, LRU_W), row), _full((8, LRU_W)), vec,
                  _full((LRU_W, LRU_W)), _full((LRU_W, LRU_W)), vec, vec, vec],
        out_specs=[pl.BlockSpec((tt, LRU_W), row)] * 3,
        out_shape=[jax.ShapeDtypeStruct((t, LRU_W), F32)] * 3,
        compiler_params=_params(),
    )(xl, xl, conv_w, conv_b, wa, wx, ba, bx, lam)


def _lru_scan(a, u, reverse):
    t = a.shape[0]
    nblk = t // 8

    def body(a_ref, u_ref, h_ref):
        def fwd(blk, h):
            base = pl.multiple_of(blk * 8, 8)
            for k in range(8):
                h = a_ref[pl.ds(base + k, 1), :] * h + u_ref[pl.ds(base + k, 1), :]
                h_ref[pl.ds(base + k, 1), :] = h
            return h

        def bwd(n, carry):
            base = pl.multiple_of((nblk - 1 - n) * 8, 8)
            for k in range(7, -1, -1):
                g = u_ref[pl.ds(base + k, 1), :] + carry
                h_ref[pl.ds(base + k, 1), :] = g
                carry = a_ref[pl.ds(base + k, 1), :] * g
            return carry

        lax.fori_loop(0, nblk, bwd if reverse else fwd, jnp.zeros((1, LRU_W), F32))

    return pl.pallas_call(
        body, name="lru_scan_rev" if reverse else "lru_scan",
        in_specs=[WHOLE] * 2, out_specs=WHOLE,
        out_shape=jax.ShapeDtypeStruct((t, LRU_W), F32),
        compiler_params=_params(),
    )(a, u)


def _lru_gates_bwd(dh, h, xc, wa, wx, ba, bx, lam):
    t = dh.shape[0]
    tt = _tile_rows(t)
    r8 = tt // 8

    def body(dh_ref, hprev_ref, h_ref, xc_ref, wa_ref, wx_ref, ba_ref, bx_ref, lam_ref,
             dxc_ref, dwa_ref, dwx_ref, dba_ref, dbx_ref, dlam_ref):
        i = pl.program_id(0)

        @pl.when(i == 0)
        def _():
            for r in (dwa_ref, dwx_ref, dba_ref, dbx_ref, dlam_ref):
                r[...] = jnp.zeros_like(r)

        prev = jnp.where(i == 0, 0.0, hprev_ref[...])
        h_before = pltpu.roll(jnp.concatenate([prev, h_ref[...]], axis=0), 1, 0)[8:8 + tt]
        dh_t = dh_ref[...]
        xc = xc_ref[...]
        xb = xc.astype(BF16)
        pre_a = _dot(xb, wa_ref[...]) + ba_ref[...]
        pre_i = _dot(xb, wx_ref[...]) + bx_ref[...]
        _, vjp = jax.vjp(_lru_elementwise, xc, pre_a, pre_i, lam_ref[...])
        dxc, dpre_a, dpre_i, dlam = vjp((dh_t * h_before, dh_t))
        da_b = dpre_a.astype(BF16)
        di_b = dpre_i.astype(BF16)
        dxc_ref[...] = dxc + _dot_nt(da_b, wa_ref[...]) + _dot_nt(di_b, wx_ref[...])
        dwa_ref[...] += _dot_tn(xb, da_b)
        dwx_ref[...] += _dot_tn(xb, di_b)
        dba_ref[...] += jnp.sum(dpre_a, axis=0, keepdims=True)
        dbx_ref[...] += jnp.sum(dpre_i, axis=0, keepdims=True)
        dlam_ref[...] += dlam

    row = lambda i: (i, 0)
    vec = _full((1, LRU_W))
    sq = _full((LRU_W, LRU_W))
    return pl.pallas_call(
        body, name="lru_gates_bwd",
        grid=(t // tt,),
        in_specs=[pl.BlockSpec((tt, LRU_W), row),
                  pl.BlockSpec((8, LRU_W), lambda i: (jnp.maximum(i * r8 - 1, 0), 0)),
                  pl.BlockSpec((tt, LRU_W), row), pl.BlockSpec((tt, LRU_W), row), sq, sq, vec, vec, vec],
        out_specs=[pl.BlockSpec((tt, LRU_W), row), sq, sq, vec, vec, vec],
        out_shape=[jax.ShapeDtypeStruct((t, LRU_W), F32), jax.ShapeDtypeStruct((LRU_W, LRU_W), F32),
                   jax.ShapeDtypeStruct((LRU_W, LRU_W), F32)] + [jax.ShapeDtypeStruct((1, LRU_W), F32)] * 3,
        compiler_params=_params(),
    )(dh, h, h, xc, wa, wx, ba, bx, lam)


def _conv_bwd(dxc, xl, conv_w):
    t = dxc.shape[0]
    tt = _tile_rows(t)
    r8 = tt // 8
    n_tiles = t // tt
    last = t // 8 - 1

    def body(d_ref, dnext_ref, xprev_ref, x_ref, cw_ref, dxl_ref, dw_ref, db_ref):
        i = pl.program_id(0)

        @pl.when(i == 0)
        def _():
            dw_ref[...] = jnp.zeros_like(dw_ref)
            db_ref[...] = jnp.zeros_like(db_ref)

        d = d_ref[...]
        nxt = jnp.where(i == n_tiles - 1, 0.0, dnext_ref[...])
        dd = jnp.concatenate([d, nxt], axis=0)
        dx = cw_ref[CONV_W - 1:CONV_W, :] * d
        for k in range(CONV_W - 1):
            shift = CONV_W - 1 - k
            dx = dx + cw_ref[k:k + 1, :] * pltpu.roll(dd, tt + 8 - shift, 0)[:tt]
        dxl_ref[...] = dx.astype(BF16)

        prev = jnp.where(i == 0, 0.0, xprev_ref[...])
        xx = jnp.concatenate([prev, x_ref[...]], axis=0)
        rows = []
        for k in range(CONV_W):
            shifted = x_ref[...] if k == CONV_W - 1 else pltpu.roll(xx, CONV_W - 1 - k, 0)[8:8 + tt]
            rows.append(jnp.sum(d * shifted, axis=0, keepdims=True))
        rows.append(jnp.zeros((8 - CONV_W, LRU_W), F32))
        dw_ref[...] += jnp.concatenate(rows, axis=0)
        db_ref[...] += jnp.sum(d, axis=0, keepdims=True)

    row = lambda i: (i, 0)
    return pl.pallas_call(
        body, name="conv_bwd",
        grid=(n_tiles,),
        in_specs=[pl.BlockSpec((tt, LRU_W), row),
                  pl.BlockSpec((8, LRU_W), lambda i: (jnp.minimum((i + 1) * r8, last), 0)),
                  pl.BlockSpec((8, LRU_W), lambda i: (jnp.maximum(i * r8 - 1, 0), 0)),
                  pl.BlockSpec((tt, LRU_W), row), _full((8, LRU_W))],
        out_specs=[pl.BlockSpec((tt, LRU_W), row), _full((8, LRU_W)), _full((1, LRU_W))],
        out_shape=[jax.ShapeDtypeStruct((t, LRU_W), BF16), jax.ShapeDtypeStruct((8, LRU_W), F32),
                   jax.ShapeDtypeStruct((1, LRU_W), F32)],
        compiler_params=_params(),
    )(dxc, dxc, xl, xl, conv_w)


def _rope(x, c, s1, s2):
    n = x.shape[-1]
    return x * c + pltpu.roll(x, n - 16, 1) * s1 + pltpu.roll(x, 16, 1) * s2


def _rope_t(d, c, s1, s2):
    n = d.shape[-1]
    return d * c + pltpu.roll(d * s1, 16, 1) + pltpu.roll(d * s2, n - 16, 1)


def _mla_proj(qlat, kvlat, krope, tables, gq, gkv, wq, wk, wv):
    t = qlat.shape[0]
    tt = _tile_rows(t)

    def body(q_ref, kv_ref, kr_ref, cq, s1q, s2q, ck, s1k, s2k, gq_ref, gkv_ref, wq_ref, wk_ref, wv_ref,
             qo_ref, ko_ref, vo_ref, qn_ref, kva_ref, vt_ref):
        qn = _rms(q_ref[...], gq_ref[...]).astype(BF16)
        kvn = _rms(kv_ref[...], gkv_ref[...]).astype(BF16)
        kr = _rope(kr_ref[...], ck[...], s1k[...], s2k[...]).astype(BF16)
        kva = jnp.concatenate([kvn, kr], axis=1)
        qn_ref[...] = qn
        kva_ref[...] = kva
        q = _dot_nt(qn, wq_ref[...])
        tile = lambda r: jnp.tile(r[...], (1, N_HEADS))
        qo_ref[...] = _rope(q, tile(cq), tile(s1q), tile(s2q)).astype(BF16)
        ko_ref[...] = _dot(kva, wk_ref[...]).astype(BF16)
        v = _dot(kvn, wv_ref[...])
        vo_ref[...] = v.astype(BF16)
        vt_ref[...] = jnp.transpose(v).astype(BF16)

    row = lambda i: (i, 0)
    tab = pl.BlockSpec((tt, HEAD_PAD), row)
    return pl.pallas_call(
        body, name="mla_proj",
        grid=(t // tt,),
        in_specs=[pl.BlockSpec((tt, Q_LORA), row), pl.BlockSpec((tt, KV_LORA), row), tab] + [tab] * 6 + [
            _full((1, Q_LORA)), _full((1, KV_LORA)), _full((HEADS_W, Q_LORA)), _full((KV_LORA + HEAD_PAD, HEADS_W)),
            _full((KV_LORA, HEADS_W))],
        out_specs=[pl.BlockSpec((tt, HEADS_W), row)] * 3 + [pl.BlockSpec((tt, Q_LORA), row),
                                                             pl.BlockSpec((tt, KV_LORA + HEAD_PAD), row),
                                                             pl.BlockSpec((HEADS_W, tt), lambda i: (0, i))],
        out_shape=[jax.ShapeDtypeStruct((t, HEADS_W), BF16)] * 3 + [
            jax.ShapeDtypeStruct((t, Q_LORA), BF16), jax.ShapeDtypeStruct((t, KV_LORA + HEAD_PAD), BF16),
            jax.ShapeDtypeStruct((HEADS_W, t), BF16)],
        compiler_params=_params(),
    )(qlat, kvlat, krope, *tables, gq, gkv, wq, wk, wv)


def _mla_proj_bwd(dq, dk, dv, qlat, kvlat, tables, gq, gkv, wq, wk, wv):
    t = qlat.shape[0]
    tt = _tile_rows(t)

    def body(dq_ref, dk_ref, dv_ref, q_ref, kv_ref, cq, s1q, s2q, ck, s1k, s2k, gq_ref, gkv_ref,
             wq_ref, wk_ref, wv_ref, dqr_ref, dql_ref, dkvl_ref, dkr_ref, dgq_ref, dgkv_ref):
        i = pl.program_id(0)

        @pl.when(i == 0)
        def _():
            dgq_ref[...] = jnp.zeros_like(dgq_ref)
            dgkv_ref[...] = jnp.zeros_like(dgkv_ref)

        tile = lambda r: jnp.tile(r[...], (1, N_HEADS))
        dqr = _rope_t(dq_ref[...], tile(cq), tile(s1q), tile(s2q)).astype(BF16)
        dqr_ref[...] = dqr
        dql, dg = _rms_bwd(q_ref[...], gq_ref[...], _dot(dqr, wq_ref[...]))
        dgq_ref[...] += dg
        dql_ref[...] = dql.astype(BF16)
        dkva = _dot_nt(dk_ref[...], wk_ref[...])
        dkvn = dkva[:, :KV_LORA] + _dot_nt(dv_ref[...], wv_ref[...])
        dkvl, dg = _rms_bwd(kv_ref[...], gkv_ref[...], dkvn)
        dgkv_ref[...] += dg
        dkvl_ref[...] = dkvl.astype(BF16)
        dkr_ref[...] = _rope_t(dkva[:, KV_LORA:], ck[...], s1k[...], s2k[...]).astype(BF16)

    row = lambda i: (i, 0)
    tab = pl.BlockSpec((tt, HEAD_PAD), row)
    wide = pl.BlockSpec((tt, HEADS_W), row)
    return pl.pallas_call(
        body, name="mla_proj_bwd",
        grid=(t // tt,),
        in_specs=[wide, wide, wide, pl.BlockSpec((tt, Q_LORA), row), pl.BlockSpec((tt, KV_LORA), row)] + [tab] * 6 + [
            _full((1, Q_LORA)), _full((1, KV_LORA)), _full((HEADS_W, Q_LORA)), _full((KV_LORA + HEAD_PAD, HEADS_W)),
            _full((KV_LORA, HEADS_W))],
        out_specs=[wide, pl.BlockSpec((tt, Q_LORA), row), pl.BlockSpec((tt, KV_LORA), row), tab,
                   _full((1, Q_LORA)), _full((1, KV_LORA))],
        out_shape=[jax.ShapeDtypeStruct((t, HEADS_W), BF16), jax.ShapeDtypeStruct((t, Q_LORA), BF16),
                   jax.ShapeDtypeStruct((t, KV_LORA), BF16), jax.ShapeDtypeStruct((t, HEAD_PAD), BF16),
                   jax.ShapeDtypeStruct((1, Q_LORA), F32), jax.ShapeDtypeStruct((1, KV_LORA), F32)],
        compiler_params=_params(),
    )(dq, dk, dv, qlat, kvlat, *tables, gq, gkv, wq, wk, wv)


NEG = -1e30


def _chunk_mask(rows, cols, row0):
    r = (lax.broadcasted_iota(jnp.int32, (rows, cols), 0) + row0) // CHUNK
    c = lax.broadcasted_iota(jnp.int32, (rows, cols), 1) // CHUNK
    return c <= r


def _chunk_mask_t(keys, key0, queries):
    kc = (lax.broadcasted_iota(jnp.int32, (keys, queries), 0) + key0) // CHUNK
    qc = lax.broadcasted_iota(jnp.int32, (keys, queries), 1) // CHUNK
    return kc <= qc


def _attn_fwd(q, k, vt, comm):
    t = q.shape[0]
    bq = _tile_rows(t)
    nq = t // bq

    def body(q_ref, k_ref, vt_ref, o_ref, lse_ref, m_ref, l_ref, acc_ref):
        i = pl.program_id(1)
        m_ref[...] = jnp.full_like(m_ref, NEG)
        l_ref[...] = jnp.zeros_like(l_ref)
        acc_ref[...] = jnp.zeros_like(acc_ref)
        qb = q_ref[...]

        def block(first_key, keys, diagonal):
            cols = pl.ds(pl.multiple_of(first_key, bq), keys)
            s = _dot_nt(k_ref[cols, :], qb)
            if diagonal:
                s = jnp.where(_chunk_mask_t(keys, bq - keys, bq), s, NEG)
            m_old = m_ref[...]
            m_new = jnp.maximum(m_old, jnp.max(s, axis=0, keepdims=True))
            alpha = jnp.exp(m_old - m_new)
            p = jnp.exp(s - m_new)
            l_ref[...] = alpha * l_ref[...] + jnp.sum(p, axis=0, keepdims=True)
            acc_ref[...] = alpha * acc_ref[...] + _dot(vt_ref[:, cols], p.astype(BF16))
            m_ref[...] = m_new

        def pair_below(jj, carry):
            block(jj * 2 * bq, 2 * bq, False)
            return carry

        lax.fori_loop(0, i // 2, pair_below, 0)

        @pl.when(i % 2 == 1)
        def _():
            block((i - 1) * bq, 2 * bq, True)

        @pl.when(i % 2 == 0)
        def _():
            block(i * bq, bq, True)

        l = l_ref[...]
        o_ref[...] = jnp.transpose(acc_ref[...] / l).astype(BF16)
        lse_ref[...] = m_ref[...] + jnp.log(l)

    qmap = lambda h, i: (i, h)
    return _call(
        body, comm, name="attn_fwd" if comm is None else "attn_fwd_gather", grid=(N_HEADS, nq),
        in_specs=[pl.BlockSpec((bq, HEAD_PAD), qmap), pl.BlockSpec((t, HEAD_PAD), lambda h, i: (0, h)),
                  pl.BlockSpec((HEAD_PAD, t), lambda h, i: (h, 0))],
        out_specs=[pl.BlockSpec((bq, HEAD_PAD), qmap), pl.BlockSpec((None, 1, bq), lambda h, i: (h, 0, i))],
        out_shape=[jax.ShapeDtypeStruct(q.shape, BF16), jax.ShapeDtypeStruct((N_HEADS, 1, t), F32)],
        scratch_shapes=[pltpu.VMEM((1, bq), F32), pltpu.VMEM((1, bq), F32), pltpu.VMEM((HEAD_PAD, bq), F32)],
        args=(q, k, vt))


def _attn_bwd(q, k, v, o, do, lse, comm):
    t = q.shape[0]
    bq = _tile_rows(t)
    nq = t // bq

    def body(q_ref, k_ref, v_ref, o_ref, do_ref, lse_ref, dq_ref, dk_ref, dv_ref, dk_acc, dv_acc, delta_ref):
        j = pl.program_id(1)

        @pl.when(j == 0)
        def _():
            dq_ref[...] = jnp.zeros_like(dq_ref)
            for blk in range(nq):
                rows = slice(blk * bq, (blk + 1) * bq)
                delta = jnp.sum(do_ref[rows, :].astype(F32) * o_ref[rows, :].astype(F32), axis=-1, keepdims=True)
                delta_ref[:, rows] = jnp.transpose(jnp.broadcast_to(delta, (bq, HEAD_PAD)))[:1, :]

        dk_acc[...] = jnp.zeros_like(dk_acc)
        dv_acc[...] = jnp.zeros_like(dv_acc)
        kb, vb = k_ref[...], v_ref[...]

        def block(first_query, queries, diagonal):
            rows = pl.ds(pl.multiple_of(first_query, bq), queries)
            qb, dob = q_ref[rows, :], do_ref[rows, :]
            s = _dot_nt(kb, qb)
            if diagonal:
                s = jnp.where(_chunk_mask_t(bq, 0, queries), s, NEG)
            p = jnp.exp(s - lse_ref[:, rows])
            dv_acc[...] += _dot(p.astype(BF16), dob)
            dp = _dot_nt(vb, dob)
            ds = (p * (dp - delta_ref[:, rows])).astype(BF16)
            dk_acc[...] += _dot(ds, qb)
            dq_ref[rows, :] += _dot_tn(ds, kb)

        after = nq - 1 - j
        odd = after % 2

        @pl.when(odd == 1)
        def _():
            block(j * bq, 2 * bq, True)

        @pl.when(odd == 0)
        def _():
            block(j * bq, bq, True)

        def pair_above(n, carry):
            block((j + 1 + odd + 2 * n) * bq, 2 * bq, False)
            return carry

        lax.fori_loop(0, after // 2, pair_above, 0)
        dk_ref[...] = dk_acc[...].astype(BF16)
        dv_ref[...] = dv_acc[...].astype(BF16)

    kmap = lambda h, j: (j, h)
    head = lambda h, j: (0, h)
    whole = pl.BlockSpec((t, HEAD_PAD), head)
    return _call(
        body, comm, name="attn_bwd" if comm is None else "attn_bwd_exchange", grid=(N_HEADS, nq),
        in_specs=[whole, pl.BlockSpec((bq, HEAD_PAD), kmap), pl.BlockSpec((bq, HEAD_PAD), kmap), whole, whole,
                  pl.BlockSpec((None, 1, t), lambda h, j: (h, 0, 0))],
        out_specs=[whole, pl.BlockSpec((bq, HEAD_PAD), kmap), pl.BlockSpec((bq, HEAD_PAD), kmap)],
        out_shape=[jax.ShapeDtypeStruct(q.shape, F32), jax.ShapeDtypeStruct(q.shape, BF16),
                   jax.ShapeDtypeStruct(q.shape, BF16)],
        scratch_shapes=[pltpu.VMEM((bq, HEAD_PAD), F32), pltpu.VMEM((bq, HEAD_PAD), F32), pltpu.VMEM((1, t), F32)],
        args=(q, k, v, o, do, lse))


def _mixer_out(h1, hl, gate, o, w_lru, w_mla, g_post):
    t = h1.shape[0]
    tt = _tile_rows(t)

    def body(h1_ref, hl_ref, gate_ref, o_ref, wl_ref, wm_ref, g_ref, h2_ref, m_ref, y_ref):
        y = (hl_ref[...] * _gelu(gate_ref[...])).astype(BF16)
        y_ref[...] = y
        m = _dot(y, wl_ref[...]) + _dot(o_ref[...], wm_ref[...])
        m_ref[...] = m
        h2_ref[...] = h1_ref[...] + _rms(m, g_ref[...])

    row = lambda i: (i, 0)
    return pl.pallas_call(
        body, name="mixer_out",
        grid=(t // tt,),
        in_specs=[pl.BlockSpec((tt, D_MODEL), row), pl.BlockSpec((tt, LRU_W), row), pl.BlockSpec((tt, LRU_W), row),
                  pl.BlockSpec((tt, HEADS_W), row), _full((LRU_W, D_MODEL)), _full((HEADS_W, D_MODEL)),
                  _full((1, D_MODEL))],
        out_specs=[pl.BlockSpec((tt, D_MODEL), row), pl.BlockSpec((tt, D_MODEL), row),
                   pl.BlockSpec((tt, LRU_W), row)],
        out_shape=[jax.ShapeDtypeStruct((t, D_MODEL), F32), jax.ShapeDtypeStruct((t, D_MODEL), F32),
                   jax.ShapeDtypeStruct((t, LRU_W), BF16)],
        compiler_params=_params(),
    )(h1, hl, gate, o, w_lru, w_mla, g_post)


def _mixer_out_bwd(dh2, m, hl, gate, w_lru, w_mla, g_post):
    t = m.shape[0]
    tt = _tile_rows(t)

    def body(dh2_ref, m_ref, hl_ref, gate_ref, wl_ref, wm_ref, g_ref, dm_ref, dgate_ref, dhl_ref, do_ref, dg_ref):
        i = pl.program_id(0)

        @pl.when(i == 0)
        def _():
            dg_ref[...] = jnp.zeros_like(dg_ref)

        dm, dg = _rms_bwd(m_ref[...], g_ref[...], dh2_ref[...])
        dg_ref[...] += dg
        dmb = dm.astype(BF16)
        dm_ref[...] = dmb
        dy = _dot_nt(dmb, wl_ref[...])
        gate = gate_ref[...]
        dgate_ref[...] = (dy * hl_ref[...] * _gelu_grad(gate)).astype(BF16)
        dhl_ref[...] = dy * _gelu(gate)
        do_ref[...] = _dot_nt(dmb, wm_ref[...]).astype(BF16)

    row = lambda i: (i, 0)
    return pl.pallas_call(
        body, name="mixer_out_bwd",
        grid=(t // tt,),
        in_specs=[pl.BlockSpec((tt, D_MODEL), row), pl.BlockSpec((tt, D_MODEL), row), pl.BlockSpec((tt, LRU_W), row),
                  pl.BlockSpec((tt, LRU_W), row), _full((LRU_W, D_MODEL)), _full((HEADS_W, D_MODEL)),
                  _full((1, D_MODEL))],
        out_specs=[pl.BlockSpec((tt, D_MODEL), row), pl.BlockSpec((tt, LRU_W), row), pl.BlockSpec((tt, LRU_W), row),
                   pl.BlockSpec((tt, HEADS_W), row), _full((1, D_MODEL))],
        out_shape=[jax.ShapeDtypeStruct((t, D_MODEL), BF16), jax.ShapeDtypeStruct((t, LRU_W), BF16),
                   jax.ShapeDtypeStruct((t, LRU_W), F32), jax.ShapeDtypeStruct((t, HEADS_W), BF16),
                   jax.ShapeDtypeStruct((1, D_MODEL), F32)],
        compiler_params=_params(),
    )(dh2, m, hl, gate, w_lru, w_mla, g_post)


def _loss_head(y, target):
    t = y.shape[0]
    tt = _tile_rows(t)

    def body(y_ref, t_ref, dy_ref, loss_ref):
        i = pl.program_id(0)

        @pl.when(i == 0)
        def _():
            loss_ref[...] = jnp.zeros_like(loss_ref)

        e = y_ref[...] - t_ref[...]
        dy_ref[...] = e * (1.0 / D_MODEL)
        per_token = jnp.mean(e * e, axis=-1, keepdims=True)
        loss_ref[...] += 0.5 * jnp.sum(per_token, axis=0, keepdims=True)

    row = lambda i: (i, 0)
    return pl.pallas_call(
        body, name="loss_head",
        grid=(t // tt,),
        in_specs=[pl.BlockSpec((tt, D_MODEL), row), pl.BlockSpec((tt, D_MODEL), row)],
        out_specs=[pl.BlockSpec((tt, D_MODEL), row), _full((8, 128))],
        out_shape=[jax.ShapeDtypeStruct((t, D_MODEL), F32), jax.ShapeDtypeStruct((8, 128), F32)],
        compiler_params=_params(),
    )(y, target)


def _adamw(w, g, m, v):
    m = ADAM_B1 * m + (1.0 - ADAM_B1) * g
    v = ADAM_B2 * v + (1.0 - ADAM_B2) * (g * g)
    m_hat = m / (1.0 - ADAM_B1 ** ADAM_STEP)
    v_hat = v / (1.0 - ADAM_B2 ** ADAM_STEP)
    delta = -ADAM_LR * (m_hat / (jnp.sqrt(v_hat) + ADAM_EPS) + ADAM_WD * w)
    return delta, m, v


def _sum_adamw(terms, w, m, v, name):
    r, c = w.shape
    n_terms, rp = terms.shape[:2]
    cb = 256 if c % 256 == 0 else 128

    def body(t_ref, w_ref, m_ref, v_ref, g_out, d_out, m_out, v_out):
        g = t_ref[0, :r, :].astype(F32)
        for q in range(1, n_terms):
            g = g + t_ref[q, :r, :].astype(F32)
        g_out[...] = g
        d_out[...], m_out[...], v_out[...] = _adamw(w_ref[...], g, m_ref[...], v_ref[...])

    cols = pl.BlockSpec((r, cb), lambda j: (0, j))
    return pl.pallas_call(
        body, name=name, grid=(c // cb,),
        in_specs=[pl.BlockSpec((n_terms, rp, cb), lambda j: (0, 0, j)), cols, cols, cols], out_specs=[cols] * 4,
        out_shape=[jax.ShapeDtypeStruct((r, c), F32)] * 4,
        compiler_params=_params(),
    )(terms, w, m, v)


def _sum_terms(terms):
    def body(t_ref, out_ref):
        g = t_ref[0]
        for q in range(1, N_DEV):
            g = g + t_ref[q]
        out_ref[...] = g

    return pl.pallas_call(
        body, name="small_grad_sum", in_specs=[WHOLE], out_specs=WHOLE,
        out_shape=jax.ShapeDtypeStruct(terms.shape[1:], F32),
        compiler_params=_params(),
    )(terms)


def _adamw_flat(w, g, m, v):
    def body(w_ref, g_ref, m_ref, v_ref, d_out, m_out, v_out):
        d_out[...], m_out[...], v_out[...] = _adamw(w_ref[...], g_ref[...], m_ref[...], v_ref[...])

    return pl.pallas_call(
        body, name="small_adamw", in_specs=[WHOLE] * 4, out_specs=[WHOLE] * 3,
        out_shape=[jax.ShapeDtypeStruct(w.shape, F32)] * 3,
        compiler_params=_params(),
    )(w, g, m, v)


def _pack(arrays):
    flat = jnp.concatenate([a.reshape(-1).astype(F32) for a in arrays])
    rows = -(-flat.shape[0] // 1024) * 8
    return jnp.pad(flat, (0, rows * 128 - flat.shape[0])).reshape(rows, 128)


def _unpack(packed, shapes):
    flat = packed.reshape(-1)
    out, pos = [], 0
    for s in shapes:
        n = math.prod(s)
        out.append(flat[pos:pos + n].reshape(s))
        pos += n
    return out


def _rope_tables(positions):
    t = positions.shape[-1]
    inv_freq = 1.0 / (ROPE_THETA ** (jnp.arange(0, QK_ROPE, 2, dtype=F32) / QK_ROPE))
    ang = positions.reshape(t, 1).astype(F32) * inv_freq
    cos, sin = jnp.cos(ang), jnp.sin(ang)
    z16 = jnp.zeros((t, 16), F32)

    def place(first, second, lead, lead_value):
        parts = [jnp.full((t, lead), lead_value, F32)] if lead else []
        parts += [first, second, jnp.zeros((t, HEAD_PAD - lead - 32), F32)]
        return jnp.concatenate(parts, axis=1)

    q_tabs = [place(cos, cos, QK_NOPE, 1.0) * ATT_SCALE, place(-sin, z16, QK_NOPE, 0.0) * ATT_SCALE,
              place(z16, sin, QK_NOPE, 0.0) * ATT_SCALE]
    k_tabs = [place(cos, cos, 0, 0.0), place(-sin, z16, 0, 0.0), place(z16, sin, 0, 0.0)]
    return q_tabs + k_tabs


def _block_diag(w):
    eye = jnp.eye(w.shape[0], dtype=w.dtype)
    return jnp.einsum("nde,nm->ndme", w, eye).reshape(LRU_W, LRU_W)


def _diag_blocks(g):
    g4 = g.reshape(8, 64, 8, 64)
    return jnp.moveaxis(jnp.diagonal(g4, axis1=0, axis2=2), -1, 0)


def _mixer_layouts(got):
    full = {}
    w_in = got["w_in"][:, :IN_SHARD].reshape(IN_COLS, D_MODEL)
    full["w_in"] = jnp.pad(w_in, ((0, IN_PAD - IN_COLS), (0, 0)))
    full["wq"] = got["w_q_b"].reshape(HEADS_W, Q_LORA)
    kvb = got["w_kv_b"]
    zeros = jnp.zeros((N_HEADS, KV_LORA, HEAD_PAD - QK_NOPE), BF16)
    k_part = jnp.transpose(jnp.concatenate([kvb[:, :, :QK_NOPE], zeros], axis=2), (1, 0, 2)).reshape(KV_LORA, HEADS_W)
    rope_rows = jnp.pad(jnp.eye(QK_ROPE, dtype=BF16), ((0, HEAD_PAD - QK_ROPE), (QK_NOPE, HEAD_PAD - QK_NOPE - QK_ROPE)))
    full["wk"] = jnp.concatenate([k_part, jnp.tile(rope_rows, (1, N_HEADS))], axis=0)
    full["wv"] = jnp.transpose(jnp.concatenate([kvb[:, :, QK_NOPE:], zeros], axis=2), (1, 0, 2)).reshape(KV_LORA, HEADS_W)
    w_out = got["w_out"].reshape(D_MODEL, D_MODEL)
    full["w_out_lru"] = w_out[:LRU_W]
    full["w_out_mla"] = jnp.pad(w_out[LRU_W:].reshape(N_HEADS, V_DIM, D_MODEL),
                                ((0, 0), (0, HEAD_PAD - V_DIM), (0, 0))).reshape(HEADS_W, D_MODEL)
    conv_w = jnp.transpose(got["conv_w"][:, :CONV_W, :LRU_W // N_DEV], (1, 0, 2)).reshape(CONV_W, LRU_W)
    full["conv_w"] = jnp.pad(conv_w, ((0, 8 - CONV_W), (0, 0)))
    return full


_FFN1 = ["w_ffn1_gate", "w_ffn1_up", "w_ffn1_down"]
_MIX = ["w_in", "w_q_b", "w_kv_b", "w_out"]
_FFN2 = ["w_ffn2_gate", "w_ffn2_up", "w_ffn2_down"]
_SHARDED = _FFN1 + _MIX + _FFN2
_TRANSPOSED = ("w_ffn1_gate", "w_ffn1_up", "w_in", "w_q_b", "w_ffn2_gate", "w_ffn2_up")
_SMALL = ["g_ffn1_pre", "g_ffn1_post", "g_mix_pre", "g_mix_post", "conv_b", "w_lru_a", "b_lru_a", "w_lru_x",
          "b_lru_x", "lru_lambda", "q_a_norm", "kv_a_norm", "g_ffn2_pre", "g_ffn2_post", "conv_w"]
_SMALL_EARLY = [n for n in _SMALL if n != "g_ffn1_pre"]
_ORDER = ["g_ffn1_pre", "g_ffn1_post", "w_ffn1_gate", "w_ffn1_up", "w_ffn1_down", "g_mix_pre", "g_mix_post", "w_in",
          "conv_w", "conv_b", "w_lru_a", "b_lru_a", "w_lru_x", "b_lru_x", "lru_lambda", "q_a_norm", "w_q_b",
          "kv_a_norm", "w_kv_b", "w_out", "g_ffn2_pre", "g_ffn2_post", "w_ffn2_gate", "w_ffn2_up", "w_ffn2_down"]
_FF_BLOCK = (FF_PAD, D_MODEL)
_PADS = {"w_ffn1_gate": _FF_BLOCK, "w_ffn1_up": _FF_BLOCK, "w_ffn1_down": _FF_BLOCK,
         "w_in": (IN_SHARD_PAD, D_MODEL), "w_q_b": (HEAD_PAD, Q_LORA), "w_kv_b": (KV_LORA, HEAD_PAD),
         "w_out": (HEAD_PAD, D_MODEL), "conv_w": (8, 128),
         "w_ffn2_gate": _FF_BLOCK, "w_ffn2_up": _FF_BLOCK, "w_ffn2_down": _FF_BLOCK}


def _rows(name, a):
    return a.T if name in _TRANSPOSED else a


def _step(args):
    x = args["x"][0]
    target = args["loss_target"][0]
    positions = args["positions"][0]
    w = {n: args[n][0] for n in _ORDER}
    mom = {n: args["m_" + n][0] for n in _ORDER}
    var = {n: args["v_" + n][0] for n in _ORDER}
    vec = lambda name: w[name].reshape(1, -1)
    t = x.shape[0]
    ff_blocks = lambda a: a.reshape(N_DEV, FF_PAD, D_MODEL)
    ff_rows = lambda a: a.reshape(FF_VIRT, D_MODEL)

    names = _FFN1 + _MIX + ["conv_w"] + _FFN2
    (wg1,), staged = _first_gather([_rows(n, w[n]) for n in names], [_PADS[n] for n in names],
                                   [F32 if n == "conv_w" else BF16 for n in names], 1)
    wg1 = ff_rows(wg1)
    n_mix = 2 + len(_MIX) + 1

    tables = _rope_tables(positions)
    wa = _block_diag(w["w_lru_a"]).astype(BF16)
    wx = _block_diag(w["w_lru_x"]).astype(BF16)

    (n1, gt1), got = _ffn_fwd_gate(x, vec("g_ffn1_pre"), wg1, _Gather(staged[:1]), "ffn_fwd_gate_gather")
    wu1 = ff_rows(got[0])
    (up1,), got = _ffn_fwd_up(n1, wu1, _Gather(staged[1:2]), "ffn_fwd_up_gather")
    wd1 = ff_rows(got[0])
    (h1, f1), got = _ffn_fwd_out(x, gt1, up1, vec("g_ffn1_post"), wd1, _Gather(staged[2:n_mix]),
                                 "ffn_fwd_out_gather")
    mix = _mixer_layouts(dict(zip(_MIX + ["conv_w"], got)))
    nmix, xl, gate, qlat, kvlat, krope = _mixer_in(h1, vec("g_mix_pre"), mix["w_in"])
    xc, a, u = _lru_gates(xl, mix["conv_w"], vec("conv_b"), wa, wx, vec("b_lru_a"), vec("b_lru_x"), vec("lru_lambda"))
    hl = _lru_scan(a, u, reverse=False)
    q, k, v, qn, kva, vt = _mla_proj(qlat, kvlat, krope, tables, vec("q_a_norm"), vec("kv_a_norm"),
                                 mix["wq"], mix["wk"], mix["wv"])
    (o, lse), ffn2 = _attn_fwd(q, k, vt, _Gather(staged[n_mix:]))
    wg2, wu2, wd2 = [ff_rows(a_) for a_ in ffn2]
    h2, m, ylru = _mixer_out(h1, hl, gate, o, mix["w_out_lru"], mix["w_out_mla"], vec("g_mix_post"))
    (h3, f2, n2, gt2, up2), _ = _ffn_fwd(h2, vec("g_ffn2_pre"), vec("g_ffn2_post"), wg2, wu2, wd2, None, "ffn_fwd")
    dy, loss_tile = _loss_head(h3, target)

    (df2, act2, dgt2, dup2, dg2post), _ = _ffn_bwd_hidden(dy, f2, gt2, up2, vec("g_ffn2_post"), wd2, None,
                                                          "ffn_bwd_hidden")
    (dh2, dg2pre), _ = _ffn_bwd_input(dy, h2, dgt2, dup2, vec("g_ffn2_pre"), wg2, wu2, None, "ffn_bwd_input")
    dw_ffn = lambda a_, b_, name, comm=None: _mm_tn(a_, b_, FF_VIRT, FF_CHUNK, D_MODEL, name, comm)
    big2 = [ff_blocks(dw_ffn(dgt2, n2, "dw_ffn_gate")[0]), ff_blocks(dw_ffn(dup2, n2, "dw_ffn_up")[0]),
            ff_blocks(dw_ffn(act2, df2, "dw_ffn_down")[0])]

    dm, dgate, dhl, do, dgmixpost = _mixer_out_bwd(dh2, m, hl, gate, mix["w_out_lru"], mix["w_out_mla"],
                                                   vec("g_mix_post"))
    dw_out_lru, _ = _mm_tn(ylru, dm, LRU_W, LRU_W, D_MODEL, "dw_out_lru")
    dw_out_mla, _ = _mm_tn(o, dm, HEADS_W, HEADS_W, D_MODEL, "dw_out_mla")
    dw_out = jnp.concatenate(
        [dw_out_lru, dw_out_mla.reshape(N_HEADS, HEAD_PAD, D_MODEL)[:, :V_DIM].reshape(N_HEADS * V_DIM, D_MODEL)], axis=0)

    (dq, dk, dv), terms2 = _attn_bwd(q, k, v, o, do, lse, _Exchange(big2))
    dqr, dqlat, dkvlat, dkrope, dgq, dgkv = _mla_proj_bwd(
        dq, dk, dv, qlat, kvlat, tables, vec("q_a_norm"), vec("kv_a_norm"), mix["wq"], mix["wk"], mix["wv"])
    dwq, _ = _mm_tn(dqr, qn, HEADS_W, HEADS_W, Q_LORA, "dw_q_b")
    dwk, _ = _mm_tn(kva, dk, KV_LORA, KV_LORA, HEADS_W, "dw_kv_b_k")
    dwv, _ = _mm_tn(kva, dv, KV_LORA, KV_LORA, HEADS_W, "dw_kv_b_v")
    dw_kvb = jnp.transpose(jnp.concatenate(
        [dwk.reshape(KV_LORA, N_HEADS, HEAD_PAD)[:, :, :QK_NOPE], dwv.reshape(KV_LORA, N_HEADS, HEAD_PAD)[:, :, :V_DIM]],
        axis=2), (1, 0, 2))

    dh = _lru_scan(a, dhl, reverse=True)
    dxc, dwa, dwx, dba, dbx, dlam = _lru_gates_bwd(dh, hl, xc, wa, wx, vec("b_lru_a"), vec("b_lru_x"),
                                                   vec("lru_lambda"))
    dxl, dconv_w8, dconv_b = _conv_bwd(dxc, xl, mix["conv_w"])
    dh1, dproj, dgmixpre = _mixer_in_bwd(dh2, h1, vec("g_mix_pre"), mix["w_in"], [dxl, dgate, dqlat, dkvlat, dkrope])
    dw_in, _ = _mm_tn(dproj, nmix, IN_PAD, IN_PAD // 2, D_MODEL, "dw_in")
    dw_in = jnp.pad(dw_in[:IN_COLS].reshape(N_DEV, IN_SHARD, D_MODEL), ((0, 0), (0, IN_SHARD_PAD - IN_SHARD), (0, 0)))
    big_mix = [dw_in, dwq.reshape(N_DEV, HEAD_PAD, Q_LORA), dw_kvb, dw_out.reshape(N_DEV, HEAD_PAD, D_MODEL)]

    (df1, act1, dgt1, dup1, dg1post), terms_mix = _ffn_bwd_hidden(
        dh1, f1, gt1, up1, vec("g_ffn1_post"), wd1, _Exchange(big_mix), "ffn_bwd_hidden_exchange")
    dwd1 = ff_blocks(dw_ffn(act1, df1, "dw_ffn_down")[0])
    dwg1, (swap_d,) = dw_ffn(dgt1, n1, "dw_ffn_gate_exchange", _PairSwap([dwd1]))
    dwg1 = ff_blocks(dwg1)
    dwu1, (terms_d1, swap_g) = dw_ffn(dup1, n1, "dw_ffn_up_exchange",
                                      _Several([_ChipExchange([_pair_sum(dwd1, swap_d)]), _PairSwap([dwg1])]))
    dwu1 = ff_blocks(dwu1)
    small = {"g_ffn1_post": dg1post, "g_mix_pre": dgmixpre, "g_mix_post": dgmixpost,
             "conv_b": dconv_b, "w_lru_a": _diag_blocks(dwa), "b_lru_a": dba, "w_lru_x": _diag_blocks(dwx),
             "b_lru_x": dbx, "lru_lambda": dlam, "q_a_norm": dgq, "kv_a_norm": dgkv, "g_ffn2_pre": dg2pre,
             "g_ffn2_post": dg2post, "conv_w": dconv_w8[:CONV_W]}
    to_all = lambda packed: jnp.broadcast_to(packed[None], (N_DEV,) + packed.shape)
    early = _pack([small[n] for n in _SMALL_EARLY] + [loss_tile[:1, :1]])
    (dx, dg1pre), (terms_g1, swap_u, early_terms) = _ffn_bwd_input(
        dh1, x, dgt1, dup1, vec("g_ffn1_pre"), wg1, wu1,
        _Several([_ChipExchange([_pair_sum(dwg1, swap_g)]), _PairSwap([dwu1]), _Exchange([to_all(early)])]),
        "ffn_bwd_input_exchange")
    small["g_ffn1_pre"] = dg1pre
    terms_u1, late_terms = _last_exchange(_Several([
        _ChipExchange([_pair_sum(dwu1, swap_u)]), _Exchange([to_all(_pack([dg1pre]))])]))

    terms = dict(zip(_FFN2, terms2))
    terms.update(zip(_MIX, terms_mix))
    terms.update({"w_ffn1_down": terms_d1, "w_ffn1_gate": terms_g1, "w_ffn1_up": terms_u1})

    grads, delta, new_m, new_v = {}, {}, {}, {}
    for n in _SHARDED:
        res = _sum_adamw(terms[n], _rows(n, w[n]), _rows(n, mom[n]), _rows(n, var[n]), "adamw_" + n)
        grads[n], delta[n], new_m[n], new_v[n] = [_rows(n, r) for r in res]

    small_sum = dict(zip(_SMALL_EARLY + ["loss"],
                         _unpack(_sum_terms(early_terms), [small[n].shape for n in _SMALL_EARLY] + [(1, 1)])))
    small_sum["g_ffn1_pre"], = _unpack(_sum_terms(late_terms), [dg1pre.shape])
    loss = small_sum.pop("loss").reshape(())
    me = 4 * lax.axis_index("x") + 2 * lax.axis_index("y") + lax.axis_index("c")
    cshard = LRU_W // N_DEV
    small_sum["conv_w"] = lax.dynamic_slice(small_sum["conv_w"], (0, me * cshard), (CONV_W, cshard))
    for n in _SMALL:
        grads[n] = small_sum[n].reshape(w[n].shape)
    shapes = [w[n].shape for n in _SMALL]
    d_p, m_p, v_p = _adamw_flat(_pack([w[n] for n in _SMALL]), _pack([grads[n] for n in _SMALL]),
                                _pack([mom[n] for n in _SMALL]), _pack([var[n] for n in _SMALL]))
    for n, d_, m_, v_ in zip(_SMALL, _unpack(d_p, shapes), _unpack(m_p, shapes), _unpack(v_p, shapes)):
        delta[n], new_m[n], new_v[n] = d_, m_, v_

    lead = lambda d: [d[n][None] for n in _ORDER]
    return (loss, dx[None], *lead(grads), *lead(delta), *lead(new_m), *lead(new_v))


def kernel(x, positions, g_ffn1_pre, g_ffn1_post, w_ffn1_gate, w_ffn1_up, w_ffn1_down, g_mix_pre, g_mix_post, w_in, conv_w, conv_b, w_lru_a, b_lru_a, w_lru_x, b_lru_x, lru_lambda, q_a_norm, w_q_b, kv_a_norm, w_kv_b, w_out, g_ffn2_pre, g_ffn2_post, w_ffn2_gate, w_ffn2_up, w_ffn2_down, loss_target, m_g_ffn1_pre, m_g_ffn1_post, m_w_ffn1_gate, m_w_ffn1_up, m_w_ffn1_down, m_g_mix_pre, m_g_mix_post, m_w_in, m_conv_w, m_conv_b, m_w_lru_a, m_b_lru_a, m_w_lru_x, m_b_lru_x, m_lru_lambda, m_q_a_norm, m_w_q_b, m_kv_a_norm, m_w_kv_b, m_w_out, m_g_ffn2_pre, m_g_ffn2_post, m_w_ffn2_gate, m_w_ffn2_up, m_w_ffn2_down, v_g_ffn1_pre, v_g_ffn1_post, v_w_ffn1_gate, v_w_ffn1_up, v_w_ffn1_down, v_g_mix_pre, v_g_mix_post, v_w_in, v_conv_w, v_conv_b, v_w_lru_a, v_b_lru_a, v_w_lru_x, v_b_lru_x, v_lru_lambda, v_q_a_norm, v_w_q_b, v_kv_a_norm, v_w_kv_b, v_w_out, v_g_ffn2_pre, v_g_ffn2_post, v_w_ffn2_gate, v_w_ffn2_up, v_w_ffn2_down):
    return _step(dict(locals()))
```

```python
import functools
import math
import operator

import jax
import jax.numpy as jnp
from jax import lax
from jax.experimental import pallas as pl
from jax.experimental.pallas import tpu as pltpu

F32 = jnp.float32
BF16 = jnp.bfloat16
MESH = pl.DeviceIdType.MESH

N_DEV = 8
D_MODEL = 1024
D_FF = 2816
FF_SHARD = D_FF // N_DEV
FF_PAD = 384
FF_VIRT = N_DEV * FF_PAD
FF_CHUNK = 2 * FF_PAD
LRU_W = 512
N_HEADS = 8
HEAD_PAD = 128
HEADS_W = N_HEADS * HEAD_PAD
QK_NOPE = 64
QK_ROPE = 32
V_DIM = 64
Q_LORA = 384
KV_LORA = 256
IN_COLS = 2 * LRU_W + Q_LORA + KV_LORA + QK_ROPE
IN_SHARD = IN_COLS // N_DEV
IN_SHARD_PAD = 256
IN_PAD = 1792
QB_SHARD = 96
CONV_W = 4
SCAN_SEGMENTS = 4
CHUNK = 64
EPS = 1e-6
LRU_C = 8.0
ROPE_THETA = 10000.0
ATT_SCALE = (QK_NOPE + QK_ROPE) ** -0.5

ADAM_LR = 0.001
ADAM_B1 = 0.9
ADAM_B2 = 0.999
ADAM_EPS = 1e-08
ADAM_WD = 0.01
ADAM_STEP = 10

VMEM_LIMIT = 56 * 1024 * 1024
ANY = pl.BlockSpec(memory_space=pl.ANY)
WHOLE = pl.BlockSpec(memory_space=pltpu.VMEM)


def _params(**kw):
    return pltpu.CompilerParams(vmem_limit_bytes=VMEM_LIMIT, **kw)


def _full(shape):
    return pl.BlockSpec(shape, lambda *_: (0,) * len(shape))


def _dot(a, b):
    return jnp.dot(a, b, preferred_element_type=F32)


def _dot_nt(a, b):
    return lax.dot_general(a, b, (((1,), (1,)), ((), ())), preferred_element_type=F32)


def _dot_tn(a, b):
    return lax.dot_general(a, b, (((0,), (0,)), ((), ())), preferred_element_type=F32)


def _rms(x, g):
    r = lax.rsqrt(jnp.mean(x * x, axis=-1, keepdims=True) + EPS)
    return x * r * g


def _rms_bwd(x, g, dy):
    r = lax.rsqrt(jnp.mean(x * x, axis=-1, keepdims=True) + EPS)
    xh = x * r
    dg = jnp.sum(dy * xh, axis=0, keepdims=True)
    dxh = dy * g
    dx = r * (dxh - xh * jnp.mean(dxh * xh, axis=-1, keepdims=True))
    return dx, dg


def _sigmoid(x):
    return 0.5 * jnp.tanh(0.5 * x) + 0.5


_GELU_C = math.sqrt(2.0 / math.pi)


def _gelu(x):
    t = jnp.tanh(_GELU_C * (x + 0.044715 * x * x * x))
    return 0.5 * x * (1.0 + t)


def _gelu_grad(x):
    t = jnp.tanh(_GELU_C * (x + 0.044715 * x * x * x))
    return 0.5 * (1.0 + t) + 0.5 * x * (1.0 - t * t) * _GELU_C * (1.0 + 3.0 * 0.044715 * x * x)


def _tile_rows(t):
    return 512 if t >= 2048 else t // 2


def _dev_index(p):
    return 4 * p[0] + 2 * p[1] + p[2]


def _place():
    x, y, c = lax.axis_index("x"), lax.axis_index("y"), lax.axis_index("c")
    return (x, y, c), (x, y, 1 - c), [(1 - x, y), (x, 1 - y), (1 - x, 1 - y)]


def _dma_sems(n):
    return [pltpu.SemaphoreType.DMA((7 * n,)), pltpu.SemaphoreType.DMA((7 * n,)), pltpu.SemaphoreType.DMA((n,))]


class _Gather:
    has_middle = True

    def __init__(self, arrays):
        self.inputs = list(arrays)
        self.n = len(arrays)
        self.out_shape = [jax.ShapeDtypeStruct((N_DEV,) + a.shape, a.dtype) for a in arrays]
        self.scratch = _dma_sems(self.n)

    @staticmethod
    def _copy(outs, sems, k, s, block, to, src=None):
        rows = outs[k].at[_dev_index(block)]
        return pltpu.make_async_remote_copy(
            src_ref=rows if src is None else src, dst_ref=rows,
            send_sem=sems[0].at[7 * k + s], recv_sem=sems[1].at[7 * k + s], device_id=to, device_id_type=MESH)

    def _first(self, srcs, outs, sems):
        me, sibling, chips = _place()
        mine = [pltpu.make_async_copy(srcs[k], outs[k].at[_dev_index(me)], sems[2].at[k]) for k in range(self.n)]
        sends = []
        for k in range(self.n):
            sends.append(self._copy(outs, sems, k, 0, me, sibling, src=srcs[k]))
            sends += [self._copy(outs, sems, k, 1 + j, me, (*chip, me[2]), src=srcs[k])
                      for j, chip in enumerate(chips)]
        return mine, sends

    def _passed(self, outs, sems):
        me, sibling, chips = _place()
        return [[self._copy(outs, sems, k, 4 + j, (*chip, me[2]), sibling) for k in range(self.n)]
                for j, chip in enumerate(chips)]

    def start(self, srcs, outs, sems):
        mine, sends = self._first(srcs, outs, sems)
        for cp in mine + sends:
            cp.start()

    def middle(self, srcs, outs, sems):
        me, sibling, chips = _place()
        passed = self._passed(outs, sems)
        for j, chip in enumerate(chips):
            for k in range(self.n):
                self._copy(outs, sems, k, 1 + j, (*chip, me[2]), me).wait_recv()
                passed[j][k].start()

    def finish(self, srcs, outs, sems):
        me, sibling, chips = _place()
        for k in range(self.n):
            self._copy(outs, sems, k, 0, sibling, me).wait_recv()
        for j, chip in enumerate(chips):
            for k in range(self.n):
                self._copy(outs, sems, k, 4 + j, (*chip, 1 - me[2]), me).wait_recv()
        mine, sends = self._first(srcs, outs, sems)
        for cp in sends + [cp for row in self._passed(outs, sems) for cp in row]:
            cp.wait_send()
        for cp in mine:
            cp.wait()


class _Exchange:
    has_middle = False

    def __init__(self, arrays):
        self.inputs = list(arrays)
        self.n = len(arrays)
        self.out_shape = [jax.ShapeDtypeStruct(a.shape, a.dtype) for a in arrays]
        self.scratch = _dma_sems(self.n)

    def _copies(self, srcs, outs, sems):
        (x, y, c), _, _ = _place()
        me = _dev_index((x, y, c))
        local = [pltpu.make_async_copy(srcs[k].at[me], outs[k].at[me], sems[2].at[k]) for k in range(self.n)]
        remote = []
        for s in range(1, N_DEV):
            peer = (1 - x if s & 4 else x, 1 - y if s & 2 else y, 1 - c if s & 1 else c)
            for k in range(self.n):
                remote.append(pltpu.make_async_remote_copy(
                    src_ref=srcs[k].at[_dev_index(peer)], dst_ref=outs[k].at[me],
                    send_sem=sems[0].at[7 * k + s - 1], recv_sem=sems[1].at[7 * k + s - 1],
                    device_id=peer, device_id_type=MESH))
        return local, remote

    def start(self, srcs, outs, sems):
        local, remote = self._copies(srcs, outs, sems)
        for cp in local + remote:
            cp.start()

    def finish(self, srcs, outs, sems):
        local, remote = self._copies(srcs, outs, sems)
        for cp in remote:
            cp.wait_recv()
        for cp in remote:
            cp.wait_send()
        for cp in local:
            cp.wait()


class _PairSwap:
    has_middle = False

    def __init__(self, arrays):
        self.inputs = list(arrays)
        self.n = len(arrays)
        self.out_shape = [jax.ShapeDtypeStruct((4,) + a.shape[1:], a.dtype) for a in arrays]
        self.scratch = _dma_sems(self.n)

    def _copies(self, srcs, outs, sems):
        (x, y, c), sibling, _ = _place()
        return [pltpu.make_async_remote_copy(
            src_ref=srcs[k].at[2 * q + 1 - c], dst_ref=outs[k].at[q],
            send_sem=sems[0].at[7 * k + q], recv_sem=sems[1].at[7 * k + q], device_id=sibling, device_id_type=MESH)
            for q in range(4) for k in range(self.n)]

    def start(self, srcs, outs, sems):
        for cp in self._copies(srcs, outs, sems):
            cp.start()

    def finish(self, srcs, outs, sems):
        copies = self._copies(srcs, outs, sems)
        for cp in copies:
            cp.wait_recv()
        for cp in copies:
            cp.wait_send()


class _ChipExchange:
    has_middle = False

    def __init__(self, arrays):
        self.inputs = list(arrays)
        self.n = len(arrays)
        self.out_shape = [jax.ShapeDtypeStruct(a.shape, a.dtype) for a in arrays]
        self.scratch = _dma_sems(self.n)

    def _copies(self, srcs, outs, sems):
        (x, y, c), _, _ = _place()
        mine = 2 * x + y
        local = [pltpu.make_async_copy(srcs[k].at[mine], outs[k].at[mine], sems[2].at[k]) for k in range(self.n)]
        remote = []
        for s in range(1, 4):
            px, py = (1 - x if s & 2 else x), (1 - y if s & 1 else y)
            for k in range(self.n):
                remote.append(pltpu.make_async_remote_copy(
                    src_ref=srcs[k].at[2 * px + py], dst_ref=outs[k].at[mine],
                    send_sem=sems[0].at[7 * k + s], recv_sem=sems[1].at[7 * k + s],
                    device_id=(px, py, c), device_id_type=MESH))
        return local, remote

    start = _Exchange.start
    finish = _Exchange.finish


class _Several:
    def __init__(self, comms):
        self.comms = comms
        self.inputs = [a for cm in comms for a in cm.inputs]
        self.n = len(self.inputs)
        self.out_shape = [s for cm in comms for s in cm.out_shape]
        self.scratch = [s for cm in comms for s in cm.scratch]
        self.has_middle = any(cm.has_middle for cm in comms)

    def _each(self, method, srcs, outs, sems):
        pos = 0
        for i, cm in enumerate(self.comms):
            if hasattr(cm, method):
                getattr(cm, method)(srcs[pos:pos + cm.n], outs[pos:pos + cm.n], sems[3 * i:3 * i + 3])
            pos += cm.n

    def start(self, srcs, outs, sems):
        self._each("start", srcs, outs, sems)

    def middle(self, srcs, outs, sems):
        self._each("middle", srcs, outs, sems)

    def finish(self, srcs, outs, sems):
        self._each("finish", srcs, outs, sems)


def _call(body, comm, *, name, grid, in_specs, out_specs, out_shape, scratch_shapes=(), args):
    in_specs, out_specs, out_shape = list(in_specs), list(out_specs), list(out_shape)
    scratch_shapes = list(scratch_shapes)
    if comm is None:
        outs = pl.pallas_call(body, name=name, grid=grid, in_specs=in_specs, out_specs=out_specs, out_shape=out_shape,
                              scratch_shapes=scratch_shapes, compiler_params=_params())(*args)
        return list(outs), []
    n_in, n_out, n_scr, c_n = len(in_specs), len(out_specs), len(scratch_shapes), comm.n
    middle = tuple(g - 1 if d == 0 else 0 for d, g in enumerate(grid))

    def hosted(*refs):
        pos = 0
        parts = []
        for width in (n_in, c_n, n_out, c_n, n_scr, len(comm.scratch)):
            parts.append(refs[pos:pos + width])
            pos += width
        ins, c_in, outs, c_out, scr, sems = parts
        ids = [pl.program_id(d) for d in range(len(grid))]
        at = lambda where: functools.reduce(operator.and_, [i == w for i, w in zip(ids, where)])

        @pl.when(at([0] * len(grid)))
        def _():
            comm.start(c_in, c_out, sems)

        body(*ins, *outs, *scr)

        if comm.has_middle:
            @pl.when(at(middle))
            def _():
                comm.middle(c_in, c_out, sems)

        @pl.when(at([g - 1 for g in grid]))
        def _():
            comm.finish(c_in, c_out, sems)

    outs = pl.pallas_call(
        hosted, name=name, grid=grid,
        in_specs=in_specs + [ANY] * c_n, out_specs=out_specs + [ANY] * c_n,
        out_shape=out_shape + comm.out_shape, scratch_shapes=scratch_shapes + comm.scratch,
        compiler_params=_params())(*args, *comm.inputs)
    return list(outs[:n_out]), list(outs[n_out:])


def _first_gather(shards, pads, dtypes, n_gathered):
    n = len(shards)
    gather = _Gather([jax.ShapeDtypeStruct(tuple(pads[k]), dtypes[k]) for k in range(n_gathered)])

    def body(*refs):
        ins, outs = refs[:n], refs[n:2 * n]
        stages = list(refs[2 * n:2 * n + n_gathered]) + list(outs[n_gathered:])
        sems = refs[2 * n + n_gathered:]
        for k in range(n):
            r, cc = ins[k].shape
            if (r, cc) != tuple(stages[k].shape):
                stages[k][...] = jnp.zeros(stages[k].shape, stages[k].dtype)
            stages[k][:r, :cc] = ins[k][...].astype(stages[k].dtype)
        gather.start(stages[:n_gathered], outs[:n_gathered], sems)
        gather.middle(stages[:n_gathered], outs[:n_gathered], sems)
        gather.finish(stages[:n_gathered], outs[:n_gathered], sems)

    staged_shape = [jax.ShapeDtypeStruct(tuple(pads[k]), dtypes[k]) for k in range(n_gathered, n)]
    outs = pl.pallas_call(
        body, name="first_gather",
        out_shape=gather.out_shape + staged_shape,
        in_specs=[WHOLE] * n,
        out_specs=[ANY] * n_gathered + [WHOLE] * (n - n_gathered),
        scratch_shapes=[pltpu.VMEM(tuple(pads[k]), dtypes[k]) for k in range(n_gathered)] + gather.scratch,
        compiler_params=_params(),
    )(*shards)
    return list(outs[:n_gathered]), list(outs[n_gathered:])


def _last_exchange(comm):
    def body(*refs):
        srcs, outs, sems = refs[:comm.n], refs[comm.n:2 * comm.n], refs[2 * comm.n:]
        comm.start(srcs, outs, sems)
        comm.finish(srcs, outs, sems)

    return pl.pallas_call(
        body, name="last_exchange", out_shape=comm.out_shape, in_specs=[ANY] * comm.n, out_specs=[ANY] * comm.n,
        scratch_shapes=comm.scratch, compiler_params=_params())(*comm.inputs)


def _pair_sum(blocks, got):
    _, r, c = blocks.shape
    side = lax.axis_index("c").astype(jnp.int32).reshape(1)

    def body(side_ref, a_ref, b_ref, out_ref):
        out_ref[...] = (a_ref[...].astype(F32) + b_ref[...].astype(F32)).astype(out_ref.dtype)

    spec = pl.BlockSpec((None, r, c), lambda q, side_ref: (q, 0, 0))
    return pl.pallas_call(
        body, name="pair_sum",
        grid_spec=pltpu.PrefetchScalarGridSpec(
            num_scalar_prefetch=1, grid=(4,),
            in_specs=[pl.BlockSpec((None, r, c), lambda q, side_ref: (2 * q + side_ref[0], 0, 0)), spec],
            out_specs=spec),
        out_shape=jax.ShapeDtypeStruct((4, r, c), blocks.dtype), compiler_params=_params())(side, blocks, got)


def _ffn_fwd(h0, g_pre, g_post, wg, wu, wd, comm, name):
    t = h0.shape[0]
    tt = _tile_rows(t)
    n_chunks = FF_VIRT // FF_CHUNK

    def body(h0_ref, gpre_ref, gpost_ref, wg_ref, wu_ref, wd_ref,
             h1_ref, f_ref, n1_ref, gt_ref, up_ref, acc_ref, n1s_ref):
        j = pl.program_id(1)

        @pl.when(j == 0)
        def _():
            n1 = _rms(h0_ref[...], gpre_ref[...]).astype(BF16)
            n1s_ref[...] = n1
            n1_ref[...] = n1
            acc_ref[...] = jnp.zeros_like(acc_ref)

        n1 = n1s_ref[...]
        gt = _dot_nt(n1, wg_ref[...])
        up = _dot_nt(n1, wu_ref[...])
        gt_ref[...] = gt.astype(BF16)
        up_ref[...] = up.astype(BF16)
        act = (gt * _sigmoid(gt) * up).astype(BF16)
        acc_ref[...] += _dot(act, wd_ref[...])

        @pl.when(j == n_chunks - 1)
        def _():
            f = acc_ref[...]
            f_ref[...] = f
            h1_ref[...] = h0_ref[...] + 0.5 * _rms(f, gpost_ref[...])

    row = lambda i, j: (i, 0)
    wspec = pl.BlockSpec((FF_CHUNK, D_MODEL), lambda i, j: (j, 0))
    return _call(
        body, comm, name=name, grid=(t // tt, n_chunks),
        in_specs=[pl.BlockSpec((tt, D_MODEL), row), _full((1, D_MODEL)), _full((1, D_MODEL)), wspec, wspec, wspec],
        out_specs=[pl.BlockSpec((tt, D_MODEL), row), pl.BlockSpec((tt, D_MODEL), row),
                   pl.BlockSpec((tt, D_MODEL), row),
                   pl.BlockSpec((tt, FF_CHUNK), lambda i, j: (i, j)),
                   pl.BlockSpec((tt, FF_CHUNK), lambda i, j: (i, j))],
        out_shape=[jax.ShapeDtypeStruct((t, D_MODEL), F32), jax.ShapeDtypeStruct((t, D_MODEL), F32),
                   jax.ShapeDtypeStruct((t, D_MODEL), BF16),
                   jax.ShapeDtypeStruct((t, FF_VIRT), BF16), jax.ShapeDtypeStruct((t, FF_VIRT), BF16)],
        scratch_shapes=[pltpu.VMEM((tt, D_MODEL), F32), pltpu.VMEM((tt, D_MODEL), BF16)],
        args=(h0, g_pre, g_post, wg, wu, wd))


def _ffn_fwd_gate(h0, g_pre, wg, comm, name):
    t = h0.shape[0]
    tt = 2 * _tile_rows(t)

    def body(h0_ref, gpre_ref, wg_ref, n1_ref, gt_ref):
        @pl.when(pl.program_id(1) == 0)
        def _():
            n1_ref[...] = _rms(h0_ref[...], gpre_ref[...]).astype(BF16)

        gt_ref[...] = _dot_nt(n1_ref[...], wg_ref[...]).astype(BF16)

    row = lambda i, j: (i, 0)
    return _call(
        body, comm, name=name, grid=(t // tt, FF_VIRT // FF_CHUNK),
        in_specs=[pl.BlockSpec((tt, D_MODEL), row), _full((1, D_MODEL)),
                  pl.BlockSpec((FF_CHUNK, D_MODEL), lambda i, j: (j, 0))],
        out_specs=[pl.BlockSpec((tt, D_MODEL), row), pl.BlockSpec((tt, FF_CHUNK), lambda i, j: (i, j))],
        out_shape=[jax.ShapeDtypeStruct((t, D_MODEL), BF16), jax.ShapeDtypeStruct((t, FF_VIRT), BF16)],
        args=(h0, g_pre, wg))


def _ffn_fwd_up(n1, wu, comm, name):
    t = n1.shape[0]
    tt = 2 * _tile_rows(t)

    def body(n1_ref, wu_ref, up_ref):
        up_ref[...] = _dot_nt(n1_ref[...], wu_ref[...]).astype(BF16)

    return _call(
        body, comm, name=name, grid=(t // tt, FF_VIRT // FF_CHUNK),
        in_specs=[pl.BlockSpec((tt, D_MODEL), lambda i, j: (i, 0)),
                  pl.BlockSpec((FF_CHUNK, D_MODEL), lambda i, j: (j, 0))],
        out_specs=[pl.BlockSpec((tt, FF_CHUNK), lambda i, j: (i, j))],
        out_shape=[jax.ShapeDtypeStruct((t, FF_VIRT), BF16)],
        args=(n1, wu))


def _ffn_fwd_out(h0, gt, up, g_post, wd, comm, name):
    t = h0.shape[0]
    tt = 2 * _tile_rows(t)
    n_chunks = FF_VIRT // FF_CHUNK

    def body(h0_ref, gt_ref, up_ref, gpost_ref, wd_ref, h1_ref, f_ref):
        j = pl.program_id(1)
        g = gt_ref[...].astype(F32)
        term = _dot((g * _sigmoid(g) * up_ref[...].astype(F32)).astype(BF16), wd_ref[...])

        @pl.when(j == 0)
        def _():
            f_ref[...] = term

        @pl.when(j > 0)
        def _():
            f_ref[...] += term

        @pl.when(j == n_chunks - 1)
        def _():
            h1_ref[...] = h0_ref[...] + 0.5 * _rms(f_ref[...], gpost_ref[...])

    row = lambda i, j: (i, 0)
    chunk = pl.BlockSpec((tt, FF_CHUNK), lambda i, j: (i, j))
    return _call(
        body, comm, name=name, grid=(t // tt, n_chunks),
        in_specs=[pl.BlockSpec((tt, D_MODEL), row), chunk, chunk, _full((1, D_MODEL)),
                  pl.BlockSpec((FF_CHUNK, D_MODEL), lambda i, j: (j, 0))],
        out_specs=[pl.BlockSpec((tt, D_MODEL), row), pl.BlockSpec((tt, D_MODEL), row)],
        out_shape=[jax.ShapeDtypeStruct((t, D_MODEL), F32), jax.ShapeDtypeStruct((t, D_MODEL), F32)],
        args=(h0, gt, up, g_post, wd))


def _ffn_bwd_hidden(dh1, f, gt, up, g_post, wd, comm, name):
    t = f.shape[0]
    tt = _tile_rows(t)
    n_chunks = FF_VIRT // FF_CHUNK

    def body(dh1_ref, f_ref, gt_ref, up_ref, gpost_ref, wd_ref,
             df_ref, act_ref, dgt_ref, dup_ref, dgpost_ref, dfs_ref):
        i, j = pl.program_id(0), pl.program_id(1)

        @pl.when((i == 0) & (j == 0))
        def _():
            dgpost_ref[...] = jnp.zeros_like(dgpost_ref)

        @pl.when(j == 0)
        def _():
            df, dg = _rms_bwd(f_ref[...], gpost_ref[...], 0.5 * dh1_ref[...])
            dgpost_ref[...] += dg
            dfb = df.astype(BF16)
            dfs_ref[...] = dfb
            df_ref[...] = dfb

        da = _dot_nt(dfs_ref[...], wd_ref[...])
        g = gt_ref[...].astype(F32)
        u = up_ref[...].astype(F32)
        s = _sigmoid(g)
        sl = g * s
        act_ref[...] = (sl * u).astype(BF16)
        dgt_ref[...] = (da * u * (s * (1.0 + g * (1.0 - s)))).astype(BF16)
        dup_ref[...] = (da * sl).astype(BF16)

    row = lambda i, j: (i, 0)
    chunk = pl.BlockSpec((tt, FF_CHUNK), lambda i, j: (i, j))
    return _call(
        body, comm, name=name, grid=(t // tt, n_chunks),
        in_specs=[pl.BlockSpec((tt, D_MODEL), row), pl.BlockSpec((tt, D_MODEL), row), chunk, chunk,
                  _full((1, D_MODEL)), pl.BlockSpec((FF_CHUNK, D_MODEL), lambda i, j: (j, 0))],
        out_specs=[pl.BlockSpec((tt, D_MODEL), row), chunk, chunk, chunk, _full((1, D_MODEL))],
        out_shape=[jax.ShapeDtypeStruct((t, D_MODEL), BF16)] + [jax.ShapeDtypeStruct((t, FF_VIRT), BF16)] * 3 + [
            jax.ShapeDtypeStruct((1, D_MODEL), F32)],
        scratch_shapes=[pltpu.VMEM((tt, D_MODEL), BF16)],
        args=(dh1, f, gt, up, g_post, wd))


def _ffn_bwd_input(dh1, h0, dgt, dup, g_pre, wg, wu, comm, name):
    t = h0.shape[0]
    tt = 2 * _tile_rows(t)
    n_chunks = FF_VIRT // FF_CHUNK

    def body(dh1_ref, h0_ref, dgt_ref, dup_ref, gpre_ref, wg_ref, wu_ref, dh0_ref, dgpre_ref, acc_ref):
        i, j = pl.program_id(0), pl.program_id(1)

        @pl.when((i == 0) & (j == 0))
        def _():
            dgpre_ref[...] = jnp.zeros_like(dgpre_ref)

        @pl.when(j == 0)
        def _():
            acc_ref[...] = jnp.zeros_like(acc_ref)

        acc_ref[...] += _dot(dgt_ref[...], wg_ref[...]) + _dot(dup_ref[...], wu_ref[...])

        @pl.when(j == n_chunks - 1)
        def _():
            dx, dg = _rms_bwd(h0_ref[...], gpre_ref[...], acc_ref[...])
            dgpre_ref[...] += dg
            dh0_ref[...] = dh1_ref[...] + dx

    row = lambda i, j: (i, 0)
    chunk = pl.BlockSpec((tt, FF_CHUNK), lambda i, j: (i, j))
    wspec = pl.BlockSpec((FF_CHUNK, D_MODEL), lambda i, j: (j, 0))
    return _call(
        body, comm, name=name, grid=(t // tt, n_chunks),
        in_specs=[pl.BlockSpec((tt, D_MODEL), row), pl.BlockSpec((tt, D_MODEL), row), chunk, chunk,
                  _full((1, D_MODEL)), wspec, wspec],
        out_specs=[pl.BlockSpec((tt, D_MODEL), row), _full((1, D_MODEL))],
        out_shape=[jax.ShapeDtypeStruct((t, D_MODEL), F32), jax.ShapeDtypeStruct((1, D_MODEL), F32)],
        scratch_shapes=[pltpu.VMEM((tt, D_MODEL), F32)],
        args=(dh1, h0, dgt, dup, g_pre, wg, wu))


def _mm_tn(a, b, m, tm, tn, name, comm=None):
    t, n = b.shape
    tk = min(t, 2048)
    nk = t // tk

    def body(a_ref, b_ref, out_ref, acc_ref):
        k = pl.program_id(2)

        @pl.when(k == 0)
        def _():
            acc_ref[...] = jnp.zeros_like(acc_ref)

        acc_ref[...] += _dot_tn(a_ref[...], b_ref[...])

        @pl.when(k == nk - 1)
        def _():
            out_ref[...] = acc_ref[...].astype(out_ref.dtype)

    outs, got = _call(
        body, comm, name=name, grid=(m // tm, n // tn, nk),
        in_specs=[pl.BlockSpec((tk, tm), lambda i, j, k: (k, i)), pl.BlockSpec((tk, tn), lambda i, j, k: (k, j))],
        out_specs=[pl.BlockSpec((tm, tn), lambda i, j, k: (i, j))],
        out_shape=[jax.ShapeDtypeStruct((m, n), BF16)],
        scratch_shapes=[pltpu.VMEM((tm, tn), F32)], args=(a, b))
    return outs[0], got


_SPLITS = (0, LRU_W, 2 * LRU_W, 2 * LRU_W + Q_LORA, 2 * LRU_W + Q_LORA + KV_LORA, IN_PAD)
_WIDTHS = tuple(_SPLITS[k + 1] - _SPLITS[k] for k in range(5))


def _mixer_in(h1, g_pre, w_in):
    t = h1.shape[0]
    tt = _tile_rows(t)

    def body(h_ref, g_ref, w_ref, n_ref, *outs):
        n = _rms(h_ref[...], g_ref[...]).astype(BF16)
        n_ref[...] = n
        proj = _dot_nt(n, w_ref[...])
        for k in range(5):
            outs[k][...] = proj[:, _SPLITS[k]:_SPLITS[k + 1]]

    row = lambda i: (i, 0)
    return pl.pallas_call(
        body, name="mixer_in",
        grid=(t // tt,),
        in_specs=[pl.BlockSpec((tt, D_MODEL), row), _full((1, D_MODEL)), _full((IN_PAD, D_MODEL))],
        out_specs=[pl.BlockSpec((tt, D_MODEL), row)] + [pl.BlockSpec((tt, w), row) for w in _WIDTHS],
        out_shape=[jax.ShapeDtypeStruct((t, D_MODEL), BF16)] + [jax.ShapeDtypeStruct((t, w), F32) for w in _WIDTHS],
        compiler_params=_params(),
    )(h1, g_pre, w_in)


def _mixer_in_bwd(dh2, h1, g_pre, w_in, dparts):
    t = h1.shape[0]
    tt = _tile_rows(t)

    def body(dh2_ref, h_ref, g_ref, w_ref, p0, p1, p2, p3, p4, dh1_ref, dproj_ref, dg_ref):
        i = pl.program_id(0)

        @pl.when(i == 0)
        def _():
            dg_ref[...] = jnp.zeros_like(dg_ref)

        dproj = jnp.concatenate([p[...] for p in (p0, p1, p2, p3, p4)], axis=1)
        dproj_ref[...] = dproj
        dx, dg = _rms_bwd(h_ref[...], g_ref[...], _dot(dproj, w_ref[...]))
        dg_ref[...] += dg
        dh1_ref[...] = dh2_ref[...] + dx

    row = lambda i: (i, 0)
    return pl.pallas_call(
        body, name="mixer_in_bwd",
        grid=(t // tt,),
        in_specs=[pl.BlockSpec((tt, D_MODEL), row), pl.BlockSpec((tt, D_MODEL), row), _full((1, D_MODEL)),
                  _full((IN_PAD, D_MODEL))] + [pl.BlockSpec((tt, w), row) for w in _WIDTHS],
        out_specs=[pl.BlockSpec((tt, D_MODEL), row), pl.BlockSpec((tt, IN_PAD), row), _full((1, D_MODEL))],
        out_shape=[jax.ShapeDtypeStruct((t, D_MODEL), F32), jax.ShapeDtypeStruct((t, IN_PAD), BF16),
                   jax.ShapeDtypeStruct((1, D_MODEL), F32)],
        compiler_params=_params(),
    )(dh2, h1, g_pre, w_in, *dparts)


def _conv(x_prev8, x_tile, w_ref, b_ref):
    tt = x_tile.shape[0]
    xx = jnp.concatenate([x_prev8, x_tile], axis=0)
    y = b_ref[...] + w_ref[CONV_W - 1:CONV_W, :] * x_tile
    for k in range(CONV_W - 1):
        y = y + w_ref[k:k + 1, :] * pltpu.roll(xx, CONV_W - 1 - k, 0)[8:8 + tt]
    return y


def _neg_expm1(z):
    series = -z * (1.0 + z * 0.5 * (1.0 + z / 3.0 * (1.0 + z * 0.25 * (1.0 + z * 0.2 * (1.0 + z / 6.0)))))
    return jnp.where(z > -0.3, series, 1.0 - jnp.exp(z))


def _lru_elementwise(xc, pre_a, pre_i, lam):
    ra = _sigmoid(pre_a)
    ri = _sigmoid(pre_i)
    neg = -lam
    softplus = jnp.maximum(neg, 0.0) + jnp.log(1.0 + jnp.exp(-jnp.abs(neg)))
    log_a = -LRU_C * ra * softplus
    a = jnp.exp(log_a)
    mult = jnp.sqrt(_neg_expm1(2.0 * log_a))
    return a, mult * (ri * xc)


def _lru_gates(xl, conv_w, conv_b, wa, wx, ba, bx, lam):
    t = xl.shape[0]
    tt = _tile_rows(t)
    r8 = tt // 8

    def body(prev_ref, x_ref, cw_ref, cb_ref, wa_ref, wx_ref, ba_ref, bx_ref, lam_ref, xc_ref, a_ref, u_ref):
        i = pl.program_id(0)
        prev = jnp.where(i == 0, 0.0, prev_ref[...])
        xc = _conv(prev, x_ref[...], cw_ref, cb_ref)
        xc_ref[...] = xc
        xb = xc.astype(BF16)
        a, u = _lru_elementwise(xc, _dot(xb, wa_ref[...]) + ba_ref[...], _dot(xb, wx_ref[...]) + bx_ref[...],
                                lam_ref[...])
        a_ref[...] = a
        u_ref[...] = u

    row = lambda i: (i, 0)
    vec = _full((1, LRU_W))
    return pl.pallas_call(
        body, name="lru_gates",
        grid=(t // tt,),
        in_specs=[pl.BlockSpec((8, LRU_W), lambda i: (jnp.maximum(i * r8 - 1, 0), 0)),
                  pl.BlockSpec((tt, LRU_W), row), _full((8, LRU_W)), vec,
                  _full((LRU_W, LRU_W)), _full((LRU_W, LRU_W)), vec, vec, vec],
        out_specs=[pl.BlockSpec((tt, LRU_W), row)] * 3,
        out_shape=[jax.ShapeDtypeStruct((t, LRU_W), F32)] * 3,
        compiler_params=_params(),
    )(xl, xl, conv_w, conv_b, wa, wx, ba, bx, lam)


def _lru_scan(a, u, reverse):
    t = a.shape[0]
    n_seg = SCAN_SEGMENTS
    seg = t // n_seg
    nblk = seg // 8
    tile = min(seg, 256)

    def body(a_ref, u_ref, h_ref, p_ref):
        zeros = (jnp.zeros((1, LRU_W), F32),) * n_seg
        ones = (jnp.ones((1, LRU_W), F32),) * n_seg

        def fwd(blk, carry):
            hs, ps = list(carry[0]), list(carry[1])
            base = pl.multiple_of(blk * 8, 8)
            for k in range(8):
                for s in range(n_seg):
                    row = pl.ds(s * seg + base + k, 1)
                    decay = a_ref[row, :]
                    hs[s] = decay * hs[s] + u_ref[row, :]
                    ps[s] = decay * ps[s]
                    h_ref[row, :] = hs[s]
                    p_ref[row, :] = ps[s]
            return tuple(hs), tuple(ps)

        def bwd(n, carry):
            cs, qs = list(carry[0]), list(carry[1])
            base = pl.multiple_of((nblk - 1 - n) * 8, 8)
            for k in range(7, -1, -1):
                for s in range(n_seg):
                    row = pl.ds(s * seg + base + k, 1)
                    g = u_ref[row, :] + cs[s]
                    h_ref[row, :] = g
                    p_ref[row, :] = qs[s]
                    decay = a_ref[row, :]
                    cs[s] = decay * g
                    qs[s] = decay * qs[s]
            return tuple(cs), tuple(qs)

        lax.fori_loop(0, nblk, bwd if reverse else fwd, (zeros, ones))

        def add_entering(s, entering):
            for k in range(seg // tile):
                rows = slice(s * seg + k * tile, s * seg + (k + 1) * tile)
                h_ref[rows, :] = h_ref[rows, :] + p_ref[rows, :] * entering

        if reverse:
            for s in range(n_seg - 2, -1, -1):
                first = (s + 1) * seg
                add_entering(s, a_ref[first:first + 1, :] * h_ref[first:first + 1, :])
        else:
            for s in range(1, n_seg):
                add_entering(s, h_ref[s * seg - 1:s * seg, :])

    return pl.pallas_call(
        body, name="lru_scan_rev" if reverse else "lru_scan",
        in_specs=[WHOLE] * 2, out_specs=WHOLE,
        out_shape=jax.ShapeDtypeStruct((t, LRU_W), F32),
        scratch_shapes=[pltpu.VMEM((t, LRU_W), F32)],
        compiler_params=_params(),
    )(a, u)


def _lru_gates_bwd(dh, h, xc, wa, wx, ba, bx, lam):
    t = dh.shape[0]
    tt = _tile_rows(t)
    r8 = tt // 8

    def body(dh_ref, hprev_ref, h_ref, xc_ref, wa_ref, wx_ref, ba_ref, bx_ref, lam_ref,
             dxc_ref, dwa_ref, dwx_ref, dba_ref, dbx_ref, dlam_ref):
        i = pl.program_id(0)

        @pl.when(i == 0)
        def _():
            for r in (dwa_ref, dwx_ref, dba_ref, dbx_ref, dlam_ref):
                r[...] = jnp.zeros_like(r)

        prev = jnp.where(i == 0, 0.0, hprev_ref[...])
        h_before = pltpu.roll(jnp.concatenate([prev, h_ref[...]], axis=0), 1, 0)[8:8 + tt]
        dh_t = dh_ref[...]
        xc = xc_ref[...]
        xb = xc.astype(BF16)
        pre_a = _dot(xb, wa_ref[...]) + ba_ref[...]
        pre_i = _dot(xb, wx_ref[...]) + bx_ref[...]
        _, vjp = jax.vjp(_lru_elementwise, xc, pre_a, pre_i, lam_ref[...])
        dxc, dpre_a, dpre_i, dlam = vjp((dh_t * h_before, dh_t))
        da_b = dpre_a.astype(BF16)
        di_b = dpre_i.astype(BF16)
        dxc_ref[...] = dxc + _dot_nt(da_b, wa_ref[...]) + _dot_nt(di_b, wx_ref[...])
        dwa_ref[...] += _dot_tn(xb, da_b)
        dwx_ref[...] += _dot_tn(xb, di_b)
        dba_ref[...] += jnp.sum(dpre_a, axis=0, keepdims=True)
        dbx_ref[...] += jnp.sum(dpre_i, axis=0, keepdims=True)
        dlam_ref[...] += dlam

    row = lambda i: (i, 0)
    vec = _full((1, LRU_W))
    sq = _full((LRU_W, LRU_W))
    return pl.pallas_call(
        body, name="lru_gates_bwd",
        grid=(t // tt,),
        in_specs=[pl.BlockSpec((tt, LRU_W), row),
                  pl.BlockSpec((8, LRU_W), lambda i: (jnp.maximum(i * r8 - 1, 0), 0)),
                  pl.BlockSpec((tt, LRU_W), row), pl.BlockSpec((tt, LRU_W), row), sq, sq, vec, vec, vec],
        out_specs=[pl.BlockSpec((tt, LRU_W), row), sq, sq, vec, vec, vec],
        out_shape=[jax.ShapeDtypeStruct((t, LRU_W), F32), jax.ShapeDtypeStruct((LRU_W, LRU_W), F32),
                   jax.ShapeDtypeStruct((LRU_W, LRU_W), F32)] + [jax.ShapeDtypeStruct((1, LRU_W), F32)] * 3,
        compiler_params=_params(),
    )(dh, h, h, xc, wa, wx, ba, bx, lam)


def _conv_bwd(dxc, xl, conv_w):
    t = dxc.shape[0]
    tt = _tile_rows(t)
    r8 = tt // 8
    n_tiles = t // tt
    last = t // 8 - 1

    def body(d_ref, dnext_ref, xprev_ref, x_ref, cw_ref, dxl_ref, dw_ref, db_ref):
        i = pl.program_id(0)

        @pl.when(i == 0)
        def _():
            dw_ref[...] = jnp.zeros_like(dw_ref)
            db_ref[...] = jnp.zeros_like(db_ref)

        d = d_ref[...]
        nxt = jnp.where(i == n_tiles - 1, 0.0, dnext_ref[...])
        dd = jnp.concatenate([d, nxt], axis=0)
        dx = cw_ref[CONV_W - 1:CONV_W, :] * d
        for k in range(CONV_W - 1):
            shift = CONV_W - 1 - k
            dx = dx + cw_ref[k:k + 1, :] * pltpu.roll(dd, tt + 8 - shift, 0)[:tt]
        dxl_ref[...] = dx.astype(BF16)

        prev = jnp.where(i == 0, 0.0, xprev_ref[...])
        xx = jnp.concatenate([prev, x_ref[...]], axis=0)
        rows = []
        for k in range(CONV_W):
            shifted = x_ref[...] if k == CONV_W - 1 else pltpu.roll(xx, CONV_W - 1 - k, 0)[8:8 + tt]
            rows.append(jnp.sum(d * shifted, axis=0, keepdims=True))
        rows.append(jnp.zeros((8 - CONV_W, LRU_W), F32))
        dw_ref[...] += jnp.concatenate(rows, axis=0)
        db_ref[...] += jnp.sum(d, axis=0, keepdims=True)

    row = lambda i: (i, 0)
    return pl.pallas_call(
        body, name="conv_bwd",
        grid=(n_tiles,),
        in_specs=[pl.BlockSpec((tt, LRU_W), row),
                  pl.BlockSpec((8, LRU_W), lambda i: (jnp.minimum((i + 1) * r8, last), 0)),
                  pl.BlockSpec((8, LRU_W), lambda i: (jnp.maximum(i * r8 - 1, 0), 0)),
                  pl.BlockSpec((tt, LRU_W), row), _full((8, LRU_W))],
        out_specs=[pl.BlockSpec((tt, LRU_W), row), _full((8, LRU_W)), _full((1, LRU_W))],
        out_shape=[jax.ShapeDtypeStruct((t, LRU_W), BF16), jax.ShapeDtypeStruct((8, LRU_W), F32),
                   jax.ShapeDtypeStruct((1, LRU_W), F32)],
        compiler_params=_params(),
    )(dxc, dxc, xl, xl, conv_w)


def _rope(x, c, s1, s2):
    n = x.shape[-1]
    return x * c + pltpu.roll(x, n - 16, 1) * s1 + pltpu.roll(x, 16, 1) * s2


def _rope_t(d, c, s1, s2):
    n = d.shape[-1]
    return d * c + pltpu.roll(d * s1, 16, 1) + pltpu.roll(d * s2, n - 16, 1)


def _mla_proj(qlat, kvlat, krope, tables, gq, gkv, wq, wk, wv):
    t = qlat.shape[0]
    tt = _tile_rows(t)

    def body(q_ref, kv_ref, kr_ref, cq, s1q, s2q, ck, s1k, s2k, gq_ref, gkv_ref, wq_ref, wk_ref, wv_ref,
             qo_ref, ko_ref, vo_ref, qn_ref, kva_ref, vt_ref):
        qn = _rms(q_ref[...], gq_ref[...]).astype(BF16)
        kvn = _rms(kv_ref[...], gkv_ref[...]).astype(BF16)
        kr = _rope(kr_ref[...], ck[...], s1k[...], s2k[...]).astype(BF16)
        kva = jnp.concatenate([kvn, kr], axis=1)
        qn_ref[...] = qn
        kva_ref[...] = kva
        q = _dot_nt(qn, wq_ref[...])
        tile = lambda r: jnp.tile(r[...], (1, N_HEADS))
        qo_ref[...] = _rope(q, tile(cq), tile(s1q), tile(s2q)).astype(BF16)
        ko_ref[...] = _dot(kva, wk_ref[...]).astype(BF16)
        v = _dot(kvn, wv_ref[...])
        vo_ref[...] = v.astype(BF16)
        vt_ref[...] = jnp.transpose(v).astype(BF16)

    row = lambda i: (i, 0)
    tab = pl.BlockSpec((tt, HEAD_PAD), row)
    return pl.pallas_call(
        body, name="mla_proj",
        grid=(t // tt,),
        in_specs=[pl.BlockSpec((tt, Q_LORA), row), pl.BlockSpec((tt, KV_LORA), row), tab] + [tab] * 6 + [
            _full((1, Q_LORA)), _full((1, KV_LORA)), _full((HEADS_W, Q_LORA)), _full((KV_LORA + HEAD_PAD, HEADS_W)),
            _full((KV_LORA, HEADS_W))],
        out_specs=[pl.BlockSpec((tt, HEADS_W), row)] * 3 + [pl.BlockSpec((tt, Q_LORA), row),
                                                             pl.BlockSpec((tt, KV_LORA + HEAD_PAD), row),
                                                             pl.BlockSpec((HEADS_W, tt), lambda i: (0, i))],
        out_shape=[jax.ShapeDtypeStruct((t, HEADS_W), BF16)] * 3 + [
            jax.ShapeDtypeStruct((t, Q_LORA), BF16), jax.ShapeDtypeStruct((t, KV_LORA + HEAD_PAD), BF16),
            jax.ShapeDtypeStruct((HEADS_W, t), BF16)],
        compiler_params=_params(),
    )(qlat, kvlat, krope, *tables, gq, gkv, wq, wk, wv)


def _mla_proj_bwd(dq, dk, dv, qlat, kvlat, tables, gq, gkv, wq, wk, wv):
    t = qlat.shape[0]
    tt = _tile_rows(t)

    def body(dq_ref, dk_ref, dv_ref, q_ref, kv_ref, cq, s1q, s2q, ck, s1k, s2k, gq_ref, gkv_ref,
             wq_ref, wk_ref, wv_ref, dqr_ref, dql_ref, dkvl_ref, dkr_ref, dgq_ref, dgkv_ref):
        i = pl.program_id(0)

        @pl.when(i == 0)
        def _():
            dgq_ref[...] = jnp.zeros_like(dgq_ref)
            dgkv_ref[...] = jnp.zeros_like(dgkv_ref)

        tile = lambda r: jnp.tile(r[...], (1, N_HEADS))
        dqr = _rope_t(dq_ref[...], tile(cq), tile(s1q), tile(s2q)).astype(BF16)
        dqr_ref[...] = dqr
        dql, dg = _rms_bwd(q_ref[...], gq_ref[...], _dot(dqr, wq_ref[...]))
        dgq_ref[...] += dg
        dql_ref[...] = dql.astype(BF16)
        dkva = _dot_nt(dk_ref[...], wk_ref[...])
        dkvn = dkva[:, :KV_LORA] + _dot_nt(dv_ref[...], wv_ref[...])
        dkvl, dg = _rms_bwd(kv_ref[...], gkv_ref[...], dkvn)
        dgkv_ref[...] += dg
        dkvl_ref[...] = dkvl.astype(BF16)
        dkr_ref[...] = _rope_t(dkva[:, KV_LORA:], ck[...], s1k[...], s2k[...]).astype(BF16)

    row = lambda i: (i, 0)
    tab = pl.BlockSpec((tt, HEAD_PAD), row)
    wide = pl.BlockSpec((tt, HEADS_W), row)
    return pl.pallas_call(
        body, name="mla_proj_bwd",
        grid=(t // tt,),
        in_specs=[wide, wide, wide, pl.BlockSpec((tt, Q_LORA), row), pl.BlockSpec((tt, KV_LORA), row)] + [tab] * 6 + [
            _full((1, Q_LORA)), _full((1, KV_LORA)), _full((HEADS_W, Q_LORA)), _full((KV_LORA + HEAD_PAD, HEADS_W)),
            _full((KV_LORA, HEADS_W))],
        out_specs=[wide, pl.BlockSpec((tt, Q_LORA), row), pl.BlockSpec((tt, KV_LORA), row), tab,
                   _full((1, Q_LORA)), _full((1, KV_LORA))],
        out_shape=[jax.ShapeDtypeStruct((t, HEADS_W), BF16), jax.ShapeDtypeStruct((t, Q_LORA), BF16),
                   jax.ShapeDtypeStruct((t, KV_LORA), BF16), jax.ShapeDtypeStruct((t, HEAD_PAD), BF16),
                   jax.ShapeDtypeStruct((1, Q_LORA), F32), jax.ShapeDtypeStruct((1, KV_LORA), F32)],
        compiler_params=_params(),
    )(dq, dk, dv, qlat, kvlat, *tables, gq, gkv, wq, wk, wv)


NEG = -1e30


def _chunk_mask(rows, cols, row0):
    r = (lax.broadcasted_iota(jnp.int32, (rows, cols), 0) + row0) // CHUNK
    c = lax.broadcasted_iota(jnp.int32, (rows, cols), 1) // CHUNK
    return c <= r


def _chunk_mask_t(keys, key0, queries):
    kc = (lax.broadcasted_iota(jnp.int32, (keys, queries), 0) + key0) // CHUNK
    qc = lax.broadcasted_iota(jnp.int32, (keys, queries), 1) // CHUNK
    return kc <= qc


def _attn_fwd(q, k, vt, comm):
    t = q.shape[0]
    bq = _tile_rows(t)
    nq = t // bq

    def body(q_ref, k_ref, vt_ref, o_ref, lse_ref, m_ref, l_ref, acc_ref):
        i = pl.program_id(1)
        m_ref[...] = jnp.full_like(m_ref, NEG)
        l_ref[...] = jnp.zeros_like(l_ref)
        acc_ref[...] = jnp.zeros_like(acc_ref)
        qb = q_ref[...]

        def block(first_key, keys, diagonal):
            cols = pl.ds(pl.multiple_of(first_key, bq), keys)
            s = _dot_nt(k_ref[cols, :], qb)
            if diagonal:
                s = jnp.where(_chunk_mask_t(keys, bq - keys, bq), s, NEG)
            m_old = m_ref[...]
            m_new = jnp.maximum(m_old, jnp.max(s, axis=0, keepdims=True))
            alpha = jnp.exp(m_old - m_new)
            p = jnp.exp(s - m_new)
            l_ref[...] = alpha * l_ref[...] + jnp.sum(p, axis=0, keepdims=True)
            acc_ref[...] = alpha * acc_ref[...] + _dot(vt_ref[:, cols], p.astype(BF16))
            m_ref[...] = m_new

        def pair_below(jj, carry):
            block(jj * 2 * bq, 2 * bq, False)
            return carry

        lax.fori_loop(0, i // 2, pair_below, 0)

        @pl.when(i % 2 == 1)
        def _():
            block((i - 1) * bq, 2 * bq, True)

        @pl.when(i % 2 == 0)
        def _():
            block(i * bq, bq, True)

        l = l_ref[...]
        o_ref[...] = jnp.transpose(acc_ref[...] / l).astype(BF16)
        lse_ref[...] = m_ref[...] + jnp.log(l)

    qmap = lambda h, i: (i, h)
    return _call(
        body, comm, name="attn_fwd" if comm is None else "attn_fwd_gather", grid=(N_HEADS, nq),
        in_specs=[pl.BlockSpec((bq, HEAD_PAD), qmap), pl.BlockSpec((t, HEAD_PAD), lambda h, i: (0, h)),
                  pl.BlockSpec((HEAD_PAD, t), lambda h, i: (h, 0))],
        out_specs=[pl.BlockSpec((bq, HEAD_PAD), qmap), pl.BlockSpec((None, 1, bq), lambda h, i: (h, 0, i))],
        out_shape=[jax.ShapeDtypeStruct(q.shape, BF16), jax.ShapeDtypeStruct((N_HEADS, 1, t), F32)],
        scratch_shapes=[pltpu.VMEM((1, bq), F32), pltpu.VMEM((1, bq), F32), pltpu.VMEM((HEAD_PAD, bq), F32)],
        args=(q, k, vt))


def _attn_bwd(q, k, v, o, do, lse, comm):
    t = q.shape[0]
    bq = _tile_rows(t)
    nq = t // bq

    def body(q_ref, k_ref, v_ref, o_ref, do_ref, lse_ref, dq_ref, dk_ref, dv_ref, dk_acc, dv_acc, delta_ref):
        j = pl.program_id(1)

        @pl.when(j == 0)
        def _():
            dq_ref[...] = jnp.zeros_like(dq_ref)
            for blk in range(nq):
                rows = slice(blk * bq, (blk + 1) * bq)
                delta = jnp.sum(do_ref[rows, :].astype(F32) * o_ref[rows, :].astype(F32), axis=-1, keepdims=True)
                delta_ref[:, rows] = jnp.transpose(jnp.broadcast_to(delta, (bq, HEAD_PAD)))[:1, :]

        dk_acc[...] = jnp.zeros_like(dk_acc)
        dv_acc[...] = jnp.zeros_like(dv_acc)
        kb, vb = k_ref[...], v_ref[...]

        def block(first_query, queries, diagonal):
            rows = pl.ds(pl.multiple_of(first_query, bq), queries)
            qb, dob = q_ref[rows, :], do_ref[rows, :]
            s = _dot_nt(kb, qb)
            if diagonal:
                s = jnp.where(_chunk_mask_t(bq, 0, queries), s, NEG)
            p = jnp.exp(s - lse_ref[:, rows])
            dv_acc[...] += _dot(p.astype(BF16), dob)
            dp = _dot_nt(vb, dob)
            ds = (p * (dp - delta_ref[:, rows])).astype(BF16)
            dk_acc[...] += _dot(ds, qb)
            dq_ref[rows, :] += _dot_tn(ds, kb)

        after = nq - 1 - j
        odd = after % 2

        @pl.when(odd == 1)
        def _():
            block(j * bq, 2 * bq, True)

        @pl.when(odd == 0)
        def _():
            block(j * bq, bq, True)

        def pair_above(n, carry):
            block((j + 1 + odd + 2 * n) * bq, 2 * bq, False)
            return carry

        lax.fori_loop(0, after // 2, pair_above, 0)
        dk_ref[...] = dk_acc[...].astype(BF16)
        dv_ref[...] = dv_acc[...].astype(BF16)

    kmap = lambda h, j: (j, h)
    head = lambda h, j: (0, h)
    whole = pl.BlockSpec((t, HEAD_PAD), head)
    return _call(
        body, comm, name="attn_bwd" if comm is None else "attn_bwd_exchange", grid=(N_HEADS, nq),
        in_specs=[whole, pl.BlockSpec((bq, HEAD_PAD), kmap), pl.BlockSpec((bq, HEAD_PAD), kmap), whole, whole,
                  pl.BlockSpec((None, 1, t), lambda h, j: (h, 0, 0))],
        out_specs=[whole, pl.BlockSpec((bq, HEAD_PAD), kmap), pl.BlockSpec((bq, HEAD_PAD), kmap)],
        out_shape=[jax.ShapeDtypeStruct(q.shape, F32), jax.ShapeDtypeStruct(q.shape, BF16),
                   jax.ShapeDtypeStruct(q.shape, BF16)],
        scratch_shapes=[pltpu.VMEM((bq, HEAD_PAD), F32), pltpu.VMEM((bq, HEAD_PAD), F32), pltpu.VMEM((1, t), F32)],
        args=(q, k, v, o, do, lse))


def _mixer_out(h1, hl, gate, o, w_lru, w_mla, g_post):
    t = h1.shape[0]
    tt = _tile_rows(t)

    def body(h1_ref, hl_ref, gate_ref, o_ref, wl_ref, wm_ref, g_ref, h2_ref, m_ref, y_ref):
        y = (hl_ref[...] * _gelu(gate_ref[...])).astype(BF16)
        y_ref[...] = y
        m = _dot(y, wl_ref[...]) + _dot(o_ref[...], wm_ref[...])
        m_ref[...] = m
        h2_ref[...] = h1_ref[...] + _rms(m, g_ref[...])

    row = lambda i: (i, 0)
    return pl.pallas_call(
        body, name="mixer_out",
        grid=(t // tt,),
        in_specs=[pl.BlockSpec((tt, D_MODEL), row), pl.BlockSpec((tt, LRU_W), row), pl.BlockSpec((tt, LRU_W), row),
                  pl.BlockSpec((tt, HEADS_W), row), _full((LRU_W, D_MODEL)), _full((HEADS_W, D_MODEL)),
                  _full((1, D_MODEL))],
        out_specs=[pl.BlockSpec((tt, D_MODEL), row), pl.BlockSpec((tt, D_MODEL), row),
                   pl.BlockSpec((tt, LRU_W), row)],
        out_shape=[jax.ShapeDtypeStruct((t, D_MODEL), F32), jax.ShapeDtypeStruct((t, D_MODEL), F32),
                   jax.ShapeDtypeStruct((t, LRU_W), BF16)],
        compiler_params=_params(),
    )(h1, hl, gate, o, w_lru, w_mla, g_post)


def _mixer_out_bwd(dh2, m, hl, gate, w_lru, w_mla, g_post):
    t = m.shape[0]
    tt = _tile_rows(t)

    def body(dh2_ref, m_ref, hl_ref, gate_ref, wl_ref, wm_ref, g_ref, dm_ref, dgate_ref, dhl_ref, do_ref, dg_ref):
        i = pl.program_id(0)

        @pl.when(i == 0)
        def _():
            dg_ref[...] = jnp.zeros_like(dg_ref)

        dm, dg = _rms_bwd(m_ref[...], g_ref[...], dh2_ref[...])
        dg_ref[...] += dg
        dmb = dm.astype(BF16)
        dm_ref[...] = dmb
        dy = _dot_nt(dmb, wl_ref[...])
        gate = gate_ref[...]
        dgate_ref[...] = (dy * hl_ref[...] * _gelu_grad(gate)).astype(BF16)
        dhl_ref[...] = dy * _gelu(gate)
        do_ref[...] = _dot_nt(dmb, wm_ref[...]).astype(BF16)

    row = lambda i: (i, 0)
    return pl.pallas_call(
        body, name="mixer_out_bwd",
        grid=(t // tt,),
        in_specs=[pl.BlockSpec((tt, D_MODEL), row), pl.BlockSpec((tt, D_MODEL), row), pl.BlockSpec((tt, LRU_W), row),
                  pl.BlockSpec((tt, LRU_W), row), _full((LRU_W, D_MODEL)), _full((HEADS_W, D_MODEL)),
                  _full((1, D_MODEL))],
        out_specs=[pl.BlockSpec((tt, D_MODEL), row), pl.BlockSpec((tt, LRU_W), row), pl.BlockSpec((tt, LRU_W), row),
                   pl.BlockSpec((tt, HEADS_W), row), _full((1, D_MODEL))],
        out_shape=[jax.ShapeDtypeStruct((t, D_MODEL), BF16), jax.ShapeDtypeStruct((t, LRU_W), BF16),
                   jax.ShapeDtypeStruct((t, LRU_W), F32), jax.ShapeDtypeStruct((t, HEADS_W), BF16),
                   jax.ShapeDtypeStruct((1, D_MODEL), F32)],
        compiler_params=_params(),
    )(dh2, m, hl, gate, w_lru, w_mla, g_post)


def _loss_head(y, target):
    t = y.shape[0]
    tt = _tile_rows(t)

    def body(y_ref, t_ref, dy_ref, loss_ref):
        i = pl.program_id(0)

        @pl.when(i == 0)
        def _():
            loss_ref[...] = jnp.zeros_like(loss_ref)

        e = y_ref[...] - t_ref[...]
        dy_ref[...] = e * (1.0 / D_MODEL)
        per_token = jnp.mean(e * e, axis=-1, keepdims=True)
        loss_ref[...] += 0.5 * jnp.sum(per_token, axis=0, keepdims=True)

    row = lambda i: (i, 0)
    return pl.pallas_call(
        body, name="loss_head",
        grid=(t // tt,),
        in_specs=[pl.BlockSpec((tt, D_MODEL), row), pl.BlockSpec((tt, D_MODEL), row)],
        out_specs=[pl.BlockSpec((tt, D_MODEL), row), _full((8, 128))],
        out_shape=[jax.ShapeDtypeStruct((t, D_MODEL), F32), jax.ShapeDtypeStruct((8, 128), F32)],
        compiler_params=_params(),
    )(y, target)


def _adamw(w, g, m, v):
    m = ADAM_B1 * m + (1.0 - ADAM_B1) * g
    v = ADAM_B2 * v + (1.0 - ADAM_B2) * (g * g)
    m_hat = m / (1.0 - ADAM_B1 ** ADAM_STEP)
    v_hat = v / (1.0 - ADAM_B2 ** ADAM_STEP)
    delta = -ADAM_LR * (m_hat / (jnp.sqrt(v_hat) + ADAM_EPS) + ADAM_WD * w)
    return delta, m, v


def _sum_adamw(terms, w, m, v, name):
    r, c = w.shape
    n_terms, rp = terms.shape[:2]
    cb = 256 if c % 256 == 0 else 128

    def body(t_ref, w_ref, m_ref, v_ref, g_out, d_out, m_out, v_out):
        g = t_ref[0, :r, :].astype(F32)
        for q in range(1, n_terms):
            g = g + t_ref[q, :r, :].astype(F32)
        g_out[...] = g
        d_out[...], m_out[...], v_out[...] = _adamw(w_ref[...], g, m_ref[...], v_ref[...])

    cols = pl.BlockSpec((r, cb), lambda j: (0, j))
    return pl.pallas_call(
        body, name=name, grid=(c // cb,),
        in_specs=[pl.BlockSpec((n_terms, rp, cb), lambda j: (0, 0, j)), cols, cols, cols], out_specs=[cols] * 4,
        out_shape=[jax.ShapeDtypeStruct((r, c), F32)] * 4,
        compiler_params=_params(),
    )(terms, w, m, v)


def _sum_terms(terms):
    def body(t_ref, out_ref):
        g = t_ref[0]
        for q in range(1, N_DEV):
            g = g + t_ref[q]
        out_ref[...] = g

    return pl.pallas_call(
        body, name="small_grad_sum", in_specs=[WHOLE], out_specs=WHOLE,
        out_shape=jax.ShapeDtypeStruct(terms.shape[1:], F32),
        compiler_params=_params(),
    )(terms)


def _adamw_flat(w, g, m, v):
    def body(w_ref, g_ref, m_ref, v_ref, d_out, m_out, v_out):
        d_out[...], m_out[...], v_out[...] = _adamw(w_ref[...], g_ref[...], m_ref[...], v_ref[...])

    return pl.pallas_call(
        body, name="small_adamw", in_specs=[WHOLE] * 4, out_specs=[WHOLE] * 3,
        out_shape=[jax.ShapeDtypeStruct(w.shape, F32)] * 3,
        compiler_params=_params(),
    )(w, g, m, v)


def _pack(arrays):
    flat = jnp.concatenate([a.reshape(-1).astype(F32) for a in arrays])
    rows = -(-flat.shape[0] // 1024) * 8
    return jnp.pad(flat, (0, rows * 128 - flat.shape[0])).reshape(rows, 128)


def _unpack(packed, shapes):
    flat = packed.reshape(-1)
    out, pos = [], 0
    for s in shapes:
        n = math.prod(s)
        out.append(flat[pos:pos + n].reshape(s))
        pos += n
    return out


def _rope_tables(positions):
    t = positions.shape[-1]
    inv_freq = 1.0 / (ROPE_THETA ** (jnp.arange(0, QK_ROPE, 2, dtype=F32) / QK_ROPE))
    ang = positions.reshape(t, 1).astype(F32) * inv_freq
    cos, sin = jnp.cos(ang), jnp.sin(ang)
    z16 = jnp.zeros((t, 16), F32)

    def place(first, second, lead, lead_value):
        parts = [jnp.full((t, lead), lead_value, F32)] if lead else []
        parts += [first, second, jnp.zeros((t, HEAD_PAD - lead - 32), F32)]
        return jnp.concatenate(parts, axis=1)

    q_tabs = [place(cos, cos, QK_NOPE, 1.0) * ATT_SCALE, place(-sin, z16, QK_NOPE, 0.0) * ATT_SCALE,
              place(z16, sin, QK_NOPE, 0.0) * ATT_SCALE]
    k_tabs = [place(cos, cos, 0, 0.0), place(-sin, z16, 0, 0.0), place(z16, sin, 0, 0.0)]
    return q_tabs + k_tabs


def _block_diag(w):
    eye = jnp.eye(w.shape[0], dtype=w.dtype)
    return jnp.einsum("nde,nm->ndme", w, eye).reshape(LRU_W, LRU_W)


def _diag_blocks(g):
    g4 = g.reshape(8, 64, 8, 64)
    return jnp.moveaxis(jnp.diagonal(g4, axis1=0, axis2=2), -1, 0)


def _mixer_layouts(got):
    full = {}
    w_in = got["w_in"][:, :IN_SHARD].reshape(IN_COLS, D_MODEL)
    full["w_in"] = jnp.pad(w_in, ((0, IN_PAD - IN_COLS), (0, 0)))
    full["wq"] = got["w_q_b"].reshape(HEADS_W, Q_LORA)
    kvb = got["w_kv_b"]
    zeros = jnp.zeros((N_HEADS, KV_LORA, HEAD_PAD - QK_NOPE), BF16)
    k_part = jnp.transpose(jnp.concatenate([kvb[:, :, :QK_NOPE], zeros], axis=2), (1, 0, 2)).reshape(KV_LORA, HEADS_W)
    rope_rows = jnp.pad(jnp.eye(QK_ROPE, dtype=BF16), ((0, HEAD_PAD - QK_ROPE), (QK_NOPE, HEAD_PAD - QK_NOPE - QK_ROPE)))
    full["wk"] = jnp.concatenate([k_part, jnp.tile(rope_rows, (1, N_HEADS))], axis=0)
    full["wv"] = jnp.transpose(jnp.concatenate([kvb[:, :, QK_NOPE:], zeros], axis=2), (1, 0, 2)).reshape(KV_LORA, HEADS_W)
    w_out = got["w_out"].reshape(D_MODEL, D_MODEL)
    full["w_out_lru"] = w_out[:LRU_W]
    full["w_out_mla"] = jnp.pad(w_out[LRU_W:].reshape(N_HEADS, V_DIM, D_MODEL),
                                ((0, 0), (0, HEAD_PAD - V_DIM), (0, 0))).reshape(HEADS_W, D_MODEL)
    conv_w = jnp.transpose(got["conv_w"][:, :CONV_W, :LRU_W // N_DEV], (1, 0, 2)).reshape(CONV_W, LRU_W)
    full["conv_w"] = jnp.pad(conv_w, ((0, 8 - CONV_W), (0, 0)))
    return full


_FFN1 = ["w_ffn1_gate", "w_ffn1_up", "w_ffn1_down"]
_MIX = ["w_in", "w_q_b", "w_kv_b", "w_out"]
_FFN2 = ["w_ffn2_gate", "w_ffn2_up", "w_ffn2_down"]
_SHARDED = _FFN1 + _MIX + _FFN2
_TRANSPOSED = ("w_ffn1_gate", "w_ffn1_up", "w_in", "w_q_b", "w_ffn2_gate", "w_ffn2_up")
_SMALL = ["g_ffn1_pre", "g_ffn1_post", "g_mix_pre", "g_mix_post", "conv_b", "w_lru_a", "b_lru_a", "w_lru_x",
          "b_lru_x", "lru_lambda", "q_a_norm", "kv_a_norm", "g_ffn2_pre", "g_ffn2_post", "conv_w"]
_SMALL_EARLY = [n for n in _SMALL if n != "g_ffn1_pre"]
_ORDER = ["g_ffn1_pre", "g_ffn1_post", "w_ffn1_gate", "w_ffn1_up", "w_ffn1_down", "g_mix_pre", "g_mix_post", "w_in",
          "conv_w", "conv_b", "w_lru_a", "b_lru_a", "w_lru_x", "b_lru_x", "lru_lambda", "q_a_norm", "w_q_b",
          "kv_a_norm", "w_kv_b", "w_out", "g_ffn2_pre", "g_ffn2_post", "w_ffn2_gate", "w_ffn2_up", "w_ffn2_down"]
_FF_BLOCK = (FF_PAD, D_MODEL)
_PADS = {"w_ffn1_gate": _FF_BLOCK, "w_ffn1_up": _FF_BLOCK, "w_ffn1_down": _FF_BLOCK,
         "w_in": (IN_SHARD_PAD, D_MODEL), "w_q_b": (HEAD_PAD, Q_LORA), "w_kv_b": (KV_LORA, HEAD_PAD),
         "w_out": (HEAD_PAD, D_MODEL), "conv_w": (8, 128),
         "w_ffn2_gate": _FF_BLOCK, "w_ffn2_up": _FF_BLOCK, "w_ffn2_down": _FF_BLOCK}


def _rows(name, a):
    return a.T if name in _TRANSPOSED else a


def _step(args):
    x = args["x"][0]
    target = args["loss_target"][0]
    positions = args["positions"][0]
    w = {n: args[n][0] for n in _ORDER}
    mom = {n: args["m_" + n][0] for n in _ORDER}
    var = {n: args["v_" + n][0] for n in _ORDER}
    vec = lambda name: w[name].reshape(1, -1)
    t = x.shape[0]
    ff_blocks = lambda a: a.reshape(N_DEV, FF_PAD, D_MODEL)
    ff_rows = lambda a: a.reshape(FF_VIRT, D_MODEL)

    names = _FFN1 + _MIX + ["conv_w"] + _FFN2
    (wg1,), staged = _first_gather([_rows(n, w[n]) for n in names], [_PADS[n] for n in names],
                                   [F32 if n == "conv_w" else BF16 for n in names], 1)
    wg1 = ff_rows(wg1)
    n_mix = 2 + len(_MIX) + 1

    tables = _rope_tables(positions)
    wa = _block_diag(w["w_lru_a"]).astype(BF16)
    wx = _block_diag(w["w_lru_x"]).astype(BF16)

    (n1, gt1), got = _ffn_fwd_gate(x, vec("g_ffn1_pre"), wg1, _Gather(staged[:1]), "ffn_fwd_gate_gather")
    wu1 = ff_rows(got[0])
    (up1,), got = _ffn_fwd_up(n1, wu1, _Gather(staged[1:2]), "ffn_fwd_up_gather")
    wd1 = ff_rows(got[0])
    (h1, f1), got = _ffn_fwd_out(x, gt1, up1, vec("g_ffn1_post"), wd1, _Gather(staged[2:n_mix]),
                                 "ffn_fwd_out_gather")
    mix = _mixer_layouts(dict(zip(_MIX + ["conv_w"], got)))
    nmix, xl, gate, qlat, kvlat, krope = _mixer_in(h1, vec("g_mix_pre"), mix["w_in"])
    xc, a, u = _lru_gates(xl, mix["conv_w"], vec("conv_b"), wa, wx, vec("b_lru_a"), vec("b_lru_x"), vec("lru_lambda"))
    hl = _lru_scan(a, u, reverse=False)
    q, k, v, qn, kva, vt = _mla_proj(qlat, kvlat, krope, tables, vec("q_a_norm"), vec("kv_a_norm"),
                                 mix["wq"], mix["wk"], mix["wv"])
    (o, lse), ffn2 = _attn_fwd(q, k, vt, _Gather(staged[n_mix:]))
    wg2, wu2, wd2 = [ff_rows(a_) for a_ in ffn2]
    h2, m, ylru = _mixer_out(h1, hl, gate, o, mix["w_out_lru"], mix["w_out_mla"], vec("g_mix_post"))
    (h3, f2, n2, gt2, up2), _ = _ffn_fwd(h2, vec("g_ffn2_pre"), vec("g_ffn2_post"), wg2, wu2, wd2, None, "ffn_fwd")
    dy, loss_tile = _loss_head(h3, target)

    (df2, act2, dgt2, dup2, dg2post), _ = _ffn_bwd_hidden(dy, f2, gt2, up2, vec("g_ffn2_post"), wd2, None,
                                                          "ffn_bwd_hidden")
    (dh2, dg2pre), _ = _ffn_bwd_input(dy, h2, dgt2, dup2, vec("g_ffn2_pre"), wg2, wu2, None, "ffn_bwd_input")
    dw_ffn = lambda a_, b_, name, comm=None: _mm_tn(a_, b_, FF_VIRT, FF_CHUNK, D_MODEL, name, comm)
    big2 = [ff_blocks(dw_ffn(dgt2, n2, "dw_ffn_gate")[0]), ff_blocks(dw_ffn(dup2, n2, "dw_ffn_up")[0]),
            ff_blocks(dw_ffn(act2, df2, "dw_ffn_down")[0])]

    dm, dgate, dhl, do, dgmixpost = _mixer_out_bwd(dh2, m, hl, gate, mix["w_out_lru"], mix["w_out_mla"],
                                                   vec("g_mix_post"))
    dw_out_lru, _ = _mm_tn(ylru, dm, LRU_W, LRU_W, D_MODEL, "dw_out_lru")
    dw_out_mla, _ = _mm_tn(o, dm, HEADS_W, HEADS_W, D_MODEL, "dw_out_mla")
    dw_out = jnp.concatenate(
        [dw_out_lru, dw_out_mla.reshape(N_HEADS, HEAD_PAD, D_MODEL)[:, :V_DIM].reshape(N_HEADS * V_DIM, D_MODEL)], axis=0)

    (dq, dk, dv), terms2 = _attn_bwd(q, k, v, o, do, lse, _Exchange(big2))
    dqr, dqlat, dkvlat, dkrope, dgq, dgkv = _mla_proj_bwd(
        dq, dk, dv, qlat, kvlat, tables, vec("q_a_norm"), vec("kv_a_norm"), mix["wq"], mix["wk"], mix["wv"])
    dwq, _ = _mm_tn(dqr, qn, HEADS_W, HEADS_W, Q_LORA, "dw_q_b")
    dwk, _ = _mm_tn(kva, dk, KV_LORA, KV_LORA, HEADS_W, "dw_kv_b_k")
    dwv, _ = _mm_tn(kva, dv, KV_LORA, KV_LORA, HEADS_W, "dw_kv_b_v")
    dw_kvb = jnp.transpose(jnp.concatenate(
        [dwk.reshape(KV_LORA, N_HEADS, HEAD_PAD)[:, :, :QK_NOPE], dwv.reshape(KV_LORA, N_HEADS, HEAD_PAD)[:, :, :V_DIM]],
        axis=2), (1, 0, 2))

    dh = _lru_scan(a, dhl, reverse=True)
    dxc, dwa, dwx, dba, dbx, dlam = _lru_gates_bwd(dh, hl, xc, wa, wx, vec("b_lru_a"), vec("b_lru_x"),
                                                   vec("lru_lambda"))
    dxl, dconv_w8, dconv_b = _conv_bwd(dxc, xl, mix["conv_w"])
    dh1, dproj, dgmixpre = _mixer_in_bwd(dh2, h1, vec("g_mix_pre"), mix["w_in"], [dxl, dgate, dqlat, dkvlat, dkrope])
    dw_in, _ = _mm_tn(dproj, nmix, IN_PAD, IN_PAD // 2, D_MODEL, "dw_in")
    dw_in = jnp.pad(dw_in[:IN_COLS].reshape(N_DEV, IN_SHARD, D_MODEL), ((0, 0), (0, IN_SHARD_PAD - IN_SHARD), (0, 0)))
    big_mix = [dw_in, dwq.reshape(N_DEV, HEAD_PAD, Q_LORA), dw_kvb, dw_out.reshape(N_DEV, HEAD_PAD, D_MODEL)]

    (df1, act1, dgt1, dup1, dg1post), terms_mix = _ffn_bwd_hidden(
        dh1, f1, gt1, up1, vec("g_ffn1_post"), wd1, _Exchange(big_mix), "ffn_bwd_hidden_exchange")
    dwd1 = ff_blocks(dw_ffn(act1, df1, "dw_ffn_down")[0])
    dwg1, (swap_d,) = dw_ffn(dgt1, n1, "dw_ffn_gate_exchange", _PairSwap([dwd1]))
    dwg1 = ff_blocks(dwg1)
    dwu1, (terms_d1, swap_g) = dw_ffn(dup1, n1, "dw_ffn_up_exchange",
                                      _Several([_ChipExchange([_pair_sum(dwd1, swap_d)]), _PairSwap([dwg1])]))
    dwu1 = ff_blocks(dwu1)
    small = {"g_ffn1_post": dg1post, "g_mix_pre": dgmixpre, "g_mix_post": dgmixpost,
             "conv_b": dconv_b, "w_lru_a": _diag_blocks(dwa), "b_lru_a": dba, "w_lru_x": _diag_blocks(dwx),
             "b_lru_x": dbx, "lru_lambda": dlam, "q_a_norm": dgq, "kv_a_norm": dgkv, "g_ffn2_pre": dg2pre,
             "g_ffn2_post": dg2post, "conv_w": dconv_w8[:CONV_W]}
    to_all = lambda packed: jnp.broadcast_to(packed[None], (N_DEV,) + packed.shape)
    early = _pack([small[n] for n in _SMALL_EARLY] + [loss_tile[:1, :1]])
    (dx, dg1pre), (terms_g1, swap_u, early_terms) = _ffn_bwd_input(
        dh1, x, dgt1, dup1, vec("g_ffn1_pre"), wg1, wu1,
        _Several([_ChipExchange([_pair_sum(dwg1, swap_g)]), _PairSwap([dwu1]), _Exchange([to_all(early)])]),
        "ffn_bwd_input_exchange")
    small["g_ffn1_pre"] = dg1pre
    terms_u1, late_terms = _last_exchange(_Several([
        _ChipExchange([_pair_sum(dwu1, swap_u)]), _Exchange([to_all(_pack([dg1pre]))])]))

    terms = dict(zip(_FFN2, terms2))
    terms.update(zip(_MIX, terms_mix))
    terms.update({"w_ffn1_down": terms_d1, "w_ffn1_gate": terms_g1, "w_ffn1_up": terms_u1})

    grads, delta, new_m, new_v = {}, {}, {}, {}
    for n in _SHARDED:
        res = _sum_adamw(terms[n], _rows(n, w[n]), _rows(n, mom[n]), _rows(n, var[n]), "adamw_" + n)
        grads[n], delta[n], new_m[n], new_v[n] = [_rows(n, r) for r in res]

    small_sum = dict(zip(_SMALL_EARLY + ["loss"],
                         _unpack(_sum_terms(early_terms), [small[n].shape for n in _SMALL_EARLY] + [(1, 1)])))
    small_sum["g_ffn1_pre"], = _unpack(_sum_terms(late_terms), [dg1pre.shape])
    loss = small_sum.pop("loss").reshape(())
    me = 4 * lax.axis_index("x") + 2 * lax.axis_index("y") + lax.axis_index("c")
    cshard = LRU_W // N_DEV
    small_sum["conv_w"] = lax.dynamic_slice(small_sum["conv_w"], (0, me * cshard), (CONV_W, cshard))
    for n in _SMALL:
        grads[n] = small_sum[n].reshape(w[n].shape)
    shapes = [w[n].shape for n in _SMALL]
    d_p, m_p, v_p = _adamw_flat(_pack([w[n] for n in _SMALL]), _pack([grads[n] for n in _SMALL]),
                                _pack([mom[n] for n in _SMALL]), _pack([var[n] for n in _SMALL]))
    for n, d_, m_, v_ in zip(_SMALL, _unpack(d_p, shapes), _unpack(m_p, shapes), _unpack(v_p, shapes)):
        delta[n], new_m[n], new_v[n] = d_, m_, v_

    lead = lambda d: [d[n][None] for n in _ORDER]
    return (loss, dx[None], *lead(grads), *lead(delta), *lead(new_m), *lead(new_v))


def kernel(x, positions, g_ffn1_pre, g_ffn1_post, w_ffn1_gate, w_ffn1_up, w_ffn1_down, g_mix_pre, g_mix_post, w_in, conv_w, conv_b, w_lru_a, b_lru_a, w_lru_x, b_lru_x, lru_lambda, q_a_norm, w_q_b, kv_a_norm, w_kv_b, w_out, g_ffn2_pre, g_ffn2_post, w_ffn2_gate, w_ffn2_up, w_ffn2_down, loss_target, m_g_ffn1_pre, m_g_ffn1_post, m_w_ffn1_gate, m_w_ffn1_up, m_w_ffn1_down, m_g_mix_pre, m_g_mix_post, m_w_in, m_conv_w, m_conv_b, m_w_lru_a, m_b_lru_a, m_w_lru_x, m_b_lru_x, m_lru_lambda, m_q_a_norm, m_w_q_b, m_kv_a_norm, m_w_kv_b, m_w_out, m_g_ffn2_pre, m_g_ffn2_post, m_w_ffn2_gate, m_w_ffn2_up, m_w_ffn2_down, v_g_ffn1_pre, v_g_ffn1_post, v_w_ffn1_gate, v_w_ffn1_up, v_w_ffn1_down, v_g_mix_pre, v_g_mix_post, v_w_in, v_conv_w, v_conv_b, v_w_lru_a, v_b_lru_a, v_w_lru_x, v_b_lru_x, v_lru_lambda, v_q_a_norm, v_w_q_b, v_kv_a_norm, v_w_kv_b, v_w_out, v_g_ffn2_pre, v_g_ffn2_post, v_w_ffn2_gate, v_w_ffn2_up, v_w_ffn2_down):
    return _step(dict(locals()))
```

```python
import functools
import math
import operator

import jax
import jax.numpy as jnp
from jax import lax
from jax.experimental import pallas as pl
from jax.experimental.pallas import tpu as pltpu

F32 = jnp.float32
BF16 = jnp.bfloat16
MESH = pl.DeviceIdType.MESH

N_DEV = 8
D_MODEL = 1024
D_FF = 2816
FF_SHARD = D_FF // N_DEV
FF_PAD = 384
FF_VIRT = N_DEV * FF_PAD
FF_CHUNK = 2 * FF_PAD
LRU_W = 512
N_HEADS = 8
HEAD_PAD = 128
HEADS_W = N_HEADS * HEAD_PAD
QK_NOPE = 64
QK_ROPE = 32
V_DIM = 64
Q_LORA = 384
KV_LORA = 256
IN_COLS = 2 * LRU_W + Q_LORA + KV_LORA + QK_ROPE
IN_SHARD = IN_COLS // N_DEV
IN_SHARD_PAD = 256
IN_PAD = 1792
QB_SHARD = 96
CONV_W = 4
SCAN_PIECES = 4
CHUNK = 64
EPS = 1e-6
LRU_C = 8.0
ROPE_THETA = 10000.0
ATT_SCALE = (QK_NOPE + QK_ROPE) ** -0.5

ADAM_LR = 0.001
ADAM_B1 = 0.9
ADAM_B2 = 0.999
ADAM_EPS = 1e-08
ADAM_WD = 0.01
ADAM_STEP = 10

VMEM_LIMIT = 56 * 1024 * 1024
ANY = pl.BlockSpec(memory_space=pl.ANY)
WHOLE = pl.BlockSpec(memory_space=pltpu.VMEM)


def _params(**kw):
    return pltpu.CompilerParams(vmem_limit_bytes=VMEM_LIMIT, **kw)


def _full(shape):
    return pl.BlockSpec(shape, lambda *_: (0,) * len(shape))


def _dot(a, b):
    return jnp.dot(a, b, preferred_element_type=F32)


def _dot_nt(a, b):
    return lax.dot_general(a, b, (((1,), (1,)), ((), ())), preferred_element_type=F32)


def _dot_tn(a, b):
    return lax.dot_general(a, b, (((0,), (0,)), ((), ())), preferred_element_type=F32)


def _rms(x, g):
    r = lax.rsqrt(jnp.mean(x * x, axis=-1, keepdims=True) + EPS)
    return x * r * g


def _rms_bwd(x, g, dy):
    r = lax.rsqrt(jnp.mean(x * x, axis=-1, keepdims=True) + EPS)
    xh = x * r
    dg = jnp.sum(dy * xh, axis=0, keepdims=True)
    dxh = dy * g
    dx = r * (dxh - xh * jnp.mean(dxh * xh, axis=-1, keepdims=True))
    return dx, dg


def _sigmoid(x):
    return 0.5 * jnp.tanh(0.5 * x) + 0.5


_GELU_C = math.sqrt(2.0 / math.pi)


def _gelu(x):
    t = jnp.tanh(_GELU_C * (x + 0.044715 * x * x * x))
    return 0.5 * x * (1.0 + t)


def _gelu_grad(x):
    t = jnp.tanh(_GELU_C * (x + 0.044715 * x * x * x))
    return 0.5 * (1.0 + t) + 0.5 * x * (1.0 - t * t) * _GELU_C * (1.0 + 3.0 * 0.044715 * x * x)


def _tile_rows(t):
    return 512 if t >= 2048 else t // 2


def _dev_index(p):
    return 4 * p[0] + 2 * p[1] + p[2]


def _place():
    x, y, c = lax.axis_index("x"), lax.axis_index("y"), lax.axis_index("c")
    return (x, y, c), (x, y, 1 - c), [(1 - x, y), (x, 1 - y), (1 - x, 1 - y)]


def _dma_sems(n):
    return [pltpu.SemaphoreType.DMA((7 * n,)), pltpu.SemaphoreType.DMA((7 * n,)), pltpu.SemaphoreType.DMA((n,))]


class _Gather:
    has_middle = True

    def __init__(self, arrays):
        self.inputs = list(arrays)
        self.n = len(arrays)
        self.out_shape = [jax.ShapeDtypeStruct((N_DEV,) + a.shape, a.dtype) for a in arrays]
        self.scratch = _dma_sems(self.n)

    @staticmethod
    def _copy(outs, sems, k, s, block, to, src=None):
        rows = outs[k].at[_dev_index(block)]
        return pltpu.make_async_remote_copy(
            src_ref=rows if src is None else src, dst_ref=rows,
            send_sem=sems[0].at[7 * k + s], recv_sem=sems[1].at[7 * k + s], device_id=to, device_id_type=MESH)

    def _first(self, srcs, outs, sems):
        me, sibling, chips = _place()
        mine = [pltpu.make_async_copy(srcs[k], outs[k].at[_dev_index(me)], sems[2].at[k]) for k in range(self.n)]
        sends = []
        for k in range(self.n):
            sends.append(self._copy(outs, sems, k, 0, me, sibling, src=srcs[k]))
            sends += [self._copy(outs, sems, k, 1 + j, me, (*chip, me[2]), src=srcs[k])
                      for j, chip in enumerate(chips)]
        return mine, sends

    def _passed(self, outs, sems):
        me, sibling, chips = _place()
        return [[self._copy(outs, sems, k, 4 + j, (*chip, me[2]), sibling) for k in range(self.n)]
                for j, chip in enumerate(chips)]

    def start(self, srcs, outs, sems):
        mine, sends = self._first(srcs, outs, sems)
        for cp in mine + sends:
            cp.start()

    def middle(self, srcs, outs, sems):
        me, sibling, chips = _place()
        passed = self._passed(outs, sems)
        for j, chip in enumerate(chips):
            for k in range(self.n):
                self._copy(outs, sems, k, 1 + j, (*chip, me[2]), me).wait_recv()
                passed[j][k].start()

    def finish(self, srcs, outs, sems):
        me, sibling, chips = _place()
        for k in range(self.n):
            self._copy(outs, sems, k, 0, sibling, me).wait_recv()
        for j, chip in enumerate(chips):
            for k in range(self.n):
                self._copy(outs, sems, k, 4 + j, (*chip, 1 - me[2]), me).wait_recv()
        mine, sends = self._first(srcs, outs, sems)
        for cp in sends + [cp for row in self._passed(outs, sems) for cp in row]:
            cp.wait_send()
        for cp in mine:
            cp.wait()


class _Exchange:
    has_middle = False

    def __init__(self, arrays):
        self.inputs = list(arrays)
        self.n = len(arrays)
        self.out_shape = [jax.ShapeDtypeStruct(a.shape, a.dtype) for a in arrays]
        self.scratch = _dma_sems(self.n)

    def _copies(self, srcs, outs, sems):
        (x, y, c), _, _ = _place()
        me = _dev_index((x, y, c))
        local = [pltpu.make_async_copy(srcs[k].at[me], outs[k].at[me], sems[2].at[k]) for k in range(self.n)]
        remote = []
        for s in range(1, N_DEV):
            peer = (1 - x if s & 4 else x, 1 - y if s & 2 else y, 1 - c if s & 1 else c)
            for k in range(self.n):
                remote.append(pltpu.make_async_remote_copy(
                    src_ref=srcs[k].at[_dev_index(peer)], dst_ref=outs[k].at[me],
                    send_sem=sems[0].at[7 * k + s - 1], recv_sem=sems[1].at[7 * k + s - 1],
                    device_id=peer, device_id_type=MESH))
        return local, remote

    def start(self, srcs, outs, sems):
        local, remote = self._copies(srcs, outs, sems)
        for cp in local + remote:
            cp.start()

    def finish(self, srcs, outs, sems):
        local, remote = self._copies(srcs, outs, sems)
        for cp in remote:
            cp.wait_recv()
        for cp in remote:
            cp.wait_send()
        for cp in local:
            cp.wait()


class _PairSwap:
    has_middle = False

    def __init__(self, arrays):
        self.inputs = list(arrays)
        self.n = len(arrays)
        self.out_shape = [jax.ShapeDtypeStruct((4,) + a.shape[1:], a.dtype) for a in arrays]
        self.scratch = _dma_sems(self.n)

    def _copies(self, srcs, outs, sems):
        (x, y, c), sibling, _ = _place()
        return [pltpu.make_async_remote_copy(
            src_ref=srcs[k].at[2 * q + 1 - c], dst_ref=outs[k].at[q],
            send_sem=sems[0].at[7 * k + q], recv_sem=sems[1].at[7 * k + q], device_id=sibling, device_id_type=MESH)
            for q in range(4) for k in range(self.n)]

    def start(self, srcs, outs, sems):
        for cp in self._copies(srcs, outs, sems):
            cp.start()

    def finish(self, srcs, outs, sems):
        copies = self._copies(srcs, outs, sems)
        for cp in copies:
            cp.wait_recv()
        for cp in copies:
            cp.wait_send()


class _ChipExchange:
    has_middle = False

    def __init__(self, arrays):
        self.inputs = list(arrays)
        self.n = len(arrays)
        self.out_shape = [jax.ShapeDtypeStruct(a.shape, a.dtype) for a in arrays]
        self.scratch = _dma_sems(self.n)

    def _copies(self, srcs, outs, sems):
        (x, y, c), _, _ = _place()
        mine = 2 * x + y
        local = [pltpu.make_async_copy(srcs[k].at[mine], outs[k].at[mine], sems[2].at[k]) for k in range(self.n)]
        remote = []
        for s in range(1, 4):
            px, py = (1 - x if s & 2 else x), (1 - y if s & 1 else y)
            for k in range(self.n):
                remote.append(pltpu.make_async_remote_copy(
                    src_ref=srcs[k].at[2 * px + py], dst_ref=outs[k].at[mine],
                    send_sem=sems[0].at[7 * k + s], recv_sem=sems[1].at[7 * k + s],
                    device_id=(px, py, c), device_id_type=MESH))
        return local, remote

    start = _Exchange.start
    finish = _Exchange.finish


class _Several:
    def __init__(self, comms):
        self.comms = comms
        self.inputs = [a for cm in comms for a in cm.inputs]
        self.n = len(self.inputs)
        self.out_shape = [s for cm in comms for s in cm.out_shape]
        self.scratch = [s for cm in comms for s in cm.scratch]
        self.has_middle = any(cm.has_middle for cm in comms)

    def _each(self, method, srcs, outs, sems):
        pos = 0
        for i, cm in enumerate(self.comms):
            if hasattr(cm, method):
                getattr(cm, method)(srcs[pos:pos + cm.n], outs[pos:pos + cm.n], sems[3 * i:3 * i + 3])
            pos += cm.n

    def start(self, srcs, outs, sems):
        self._each("start", srcs, outs, sems)

    def middle(self, srcs, outs, sems):
        self._each("middle", srcs, outs, sems)

    def finish(self, srcs, outs, sems):
        self._each("finish", srcs, outs, sems)


def _call(body, comm, *, name, grid, in_specs, out_specs, out_shape, scratch_shapes=(), args):
    in_specs, out_specs, out_shape = list(in_specs), list(out_specs), list(out_shape)
    scratch_shapes = list(scratch_shapes)
    if comm is None:
        outs = pl.pallas_call(body, name=name, grid=grid, in_specs=in_specs, out_specs=out_specs, out_shape=out_shape,
                              scratch_shapes=scratch_shapes, compiler_params=_params())(*args)
        return list(outs), []
    n_in, n_out, n_scr, c_n = len(in_specs), len(out_specs), len(scratch_shapes), comm.n
    middle = tuple(g - 1 if d == 0 else 0 for d, g in enumerate(grid))

    def hosted(*refs):
        pos = 0
        parts = []
        for width in (n_in, c_n, n_out, c_n, n_scr, len(comm.scratch)):
            parts.append(refs[pos:pos + width])
            pos += width
        ins, c_in, outs, c_out, scr, sems = parts
        ids = [pl.program_id(d) for d in range(len(grid))]
        at = lambda where: functools.reduce(operator.and_, [i == w for i, w in zip(ids, where)])

        @pl.when(at([0] * len(grid)))
        def _():
            comm.start(c_in, c_out, sems)

        body(*ins, *outs, *scr)

        if comm.has_middle:
            @pl.when(at(middle))
            def _():
                comm.middle(c_in, c_out, sems)

        @pl.when(at([g - 1 for g in grid]))
        def _():
            comm.finish(c_in, c_out, sems)

    outs = pl.pallas_call(
        hosted, name=name, grid=grid,
        in_specs=in_specs + [ANY] * c_n, out_specs=out_specs + [ANY] * c_n,
        out_shape=out_shape + comm.out_shape, scratch_shapes=scratch_shapes + comm.scratch,
        compiler_params=_params())(*args, *comm.inputs)
    return list(outs[:n_out]), list(outs[n_out:])


def _first_gather(shards, pads, dtypes, n_gathered):
    n = len(shards)
    gather = _Gather([jax.ShapeDtypeStruct(tuple(pads[k]), dtypes[k]) for k in range(n_gathered)])

    def body(*refs):
        ins, outs = refs[:n], refs[n:2 * n]
        stages = list(refs[2 * n:2 * n + n_gathered]) + list(outs[n_gathered:])
        sems = refs[2 * n + n_gathered:]
        for k in range(n):
            r, cc = ins[k].shape
            if (r, cc) != tuple(stages[k].shape):
                stages[k][...] = jnp.zeros(stages[k].shape, stages[k].dtype)
            stages[k][:r, :cc] = ins[k][...].astype(stages[k].dtype)
        gather.start(stages[:n_gathered], outs[:n_gathered], sems)
        gather.middle(stages[:n_gathered], outs[:n_gathered], sems)
        gather.finish(stages[:n_gathered], outs[:n_gathered], sems)

    staged_shape = [jax.ShapeDtypeStruct(tuple(pads[k]), dtypes[k]) for k in range(n_gathered, n)]
    outs = pl.pallas_call(
        body, name="first_gather",
        out_shape=gather.out_shape + staged_shape,
        in_specs=[WHOLE] * n,
        out_specs=[ANY] * n_gathered + [WHOLE] * (n - n_gathered),
        scratch_shapes=[pltpu.VMEM(tuple(pads[k]), dtypes[k]) for k in range(n_gathered)] + gather.scratch,
        compiler_params=_params(),
    )(*shards)
    return list(outs[:n_gathered]), list(outs[n_gathered:])


def _last_exchange(comm):
    def body(*refs):
        srcs, outs, sems = refs[:comm.n], refs[comm.n:2 * comm.n], refs[2 * comm.n:]
        comm.start(srcs, outs, sems)
        comm.finish(srcs, outs, sems)

    return pl.pallas_call(
        body, name="last_exchange", out_shape=comm.out_shape, in_specs=[ANY] * comm.n, out_specs=[ANY] * comm.n,
        scratch_shapes=comm.scratch, compiler_params=_params())(*comm.inputs)


def _pair_sum(blocks, got):
    _, r, c = blocks.shape
    side = lax.axis_index("c").astype(jnp.int32).reshape(1)

    def body(side_ref, a_ref, b_ref, out_ref):
        out_ref[...] = (a_ref[...].astype(F32) + b_ref[...].astype(F32)).astype(out_ref.dtype)

    spec = pl.BlockSpec((None, r, c), lambda q, side_ref: (q, 0, 0))
    return pl.pallas_call(
        body, name="pair_sum",
        grid_spec=pltpu.PrefetchScalarGridSpec(
            num_scalar_prefetch=1, grid=(4,),
            in_specs=[pl.BlockSpec((None, r, c), lambda q, side_ref: (2 * q + side_ref[0], 0, 0)), spec],
            out_specs=spec),
        out_shape=jax.ShapeDtypeStruct((4, r, c), blocks.dtype), compiler_params=_params())(side, blocks, got)


def _ffn_fwd_loss(h0, target, g_pre, g_post, wg, wu, wd, name):
    t = h0.shape[0]
    tt = _tile_rows(t)
    n_chunks = FF_VIRT // FF_CHUNK

    def body(h0_ref, tgt_ref, gpre_ref, gpost_ref, wg_ref, wu_ref, wd_ref,
             dy_ref, f_ref, n1_ref, gt_ref, up_ref, loss_ref, acc_ref, n1s_ref):
        i, j = pl.program_id(0), pl.program_id(1)

        @pl.when((i == 0) & (j == 0))
        def _():
            loss_ref[...] = jnp.zeros_like(loss_ref)

        @pl.when(j == 0)
        def _():
            n1 = _rms(h0_ref[...], gpre_ref[...]).astype(BF16)
            n1s_ref[...] = n1
            n1_ref[...] = n1
            acc_ref[...] = jnp.zeros_like(acc_ref)

        n1 = n1s_ref[...]
        gt = _dot_nt(n1, wg_ref[...])
        up = _dot_nt(n1, wu_ref[...])
        gt_ref[...] = gt.astype(BF16)
        up_ref[...] = up.astype(BF16)
        act = (gt * _sigmoid(gt) * up).astype(BF16)
        acc_ref[...] += _dot(act, wd_ref[...])

        @pl.when(j == n_chunks - 1)
        def _():
            f = acc_ref[...]
            f_ref[...] = f
            e = h0_ref[...] + 0.5 * _rms(f, gpost_ref[...]) - tgt_ref[...]
            dy_ref[...] = e * (1.0 / D_MODEL)
            loss_ref[...] += 0.5 * jnp.sum(jnp.mean(e * e, axis=-1, keepdims=True), axis=0, keepdims=True)

    row = lambda i, j: (i, 0)
    wspec = pl.BlockSpec((FF_CHUNK, D_MODEL), lambda i, j: (j, 0))
    outs, _ = _call(
        body, None, name=name, grid=(t // tt, n_chunks),
        in_specs=[pl.BlockSpec((tt, D_MODEL), row), pl.BlockSpec((tt, D_MODEL), row), _full((1, D_MODEL)),
                  _full((1, D_MODEL)), wspec, wspec, wspec],
        out_specs=[pl.BlockSpec((tt, D_MODEL), row), pl.BlockSpec((tt, D_MODEL), row),
                   pl.BlockSpec((tt, D_MODEL), row),
                   pl.BlockSpec((tt, FF_CHUNK), lambda i, j: (i, j)),
                   pl.BlockSpec((tt, FF_CHUNK), lambda i, j: (i, j)), _full((8, 128))],
        out_shape=[jax.ShapeDtypeStruct((t, D_MODEL), F32), jax.ShapeDtypeStruct((t, D_MODEL), F32),
                   jax.ShapeDtypeStruct((t, D_MODEL), BF16),
                   jax.ShapeDtypeStruct((t, FF_VIRT), BF16), jax.ShapeDtypeStruct((t, FF_VIRT), BF16),
                   jax.ShapeDtypeStruct((8, 128), F32)],
        scratch_shapes=[pltpu.VMEM((tt, D_MODEL), F32), pltpu.VMEM((tt, D_MODEL), BF16)],
        args=(h0, target, g_pre, g_post, wg, wu, wd))
    return outs


def _ffn_fwd_gate(h0, g_pre, wg, comm, name):
    t = h0.shape[0]
    tt = 2 * _tile_rows(t)

    def body(h0_ref, gpre_ref, wg_ref, n1_ref, gt_ref):
        @pl.when(pl.program_id(1) == 0)
        def _():
            n1_ref[...] = _rms(h0_ref[...], gpre_ref[...]).astype(BF16)

        gt_ref[...] = _dot_nt(n1_ref[...], wg_ref[...]).astype(BF16)

    row = lambda i, j: (i, 0)
    return _call(
        body, comm, name=name, grid=(t // tt, FF_VIRT // FF_CHUNK),
        in_specs=[pl.BlockSpec((tt, D_MODEL), row), _full((1, D_MODEL)),
                  pl.BlockSpec((FF_CHUNK, D_MODEL), lambda i, j: (j, 0))],
        out_specs=[pl.BlockSpec((tt, D_MODEL), row), pl.BlockSpec((tt, FF_CHUNK), lambda i, j: (i, j))],
        out_shape=[jax.ShapeDtypeStruct((t, D_MODEL), BF16), jax.ShapeDtypeStruct((t, FF_VIRT), BF16)],
        args=(h0, g_pre, wg))


def _ffn_fwd_up(n1, wu, comm, name):
    t = n1.shape[0]
    tt = 2 * _tile_rows(t)

    def body(n1_ref, wu_ref, up_ref):
        up_ref[...] = _dot_nt(n1_ref[...], wu_ref[...]).astype(BF16)

    return _call(
        body, comm, name=name, grid=(t // tt, FF_VIRT // FF_CHUNK),
        in_specs=[pl.BlockSpec((tt, D_MODEL), lambda i, j: (i, 0)),
                  pl.BlockSpec((FF_CHUNK, D_MODEL), lambda i, j: (j, 0))],
        out_specs=[pl.BlockSpec((tt, FF_CHUNK), lambda i, j: (i, j))],
        out_shape=[jax.ShapeDtypeStruct((t, FF_VIRT), BF16)],
        args=(n1, wu))


def _ffn_fwd_out(h0, gt, up, g_post, wd, comm, name):
    t = h0.shape[0]
    tt = 2 * _tile_rows(t)
    n_chunks = FF_VIRT // FF_CHUNK

    def body(h0_ref, gt_ref, up_ref, gpost_ref, wd_ref, h1_ref, f_ref):
        j = pl.program_id(1)
        g = gt_ref[...].astype(F32)
        term = _dot((g * _sigmoid(g) * up_ref[...].astype(F32)).astype(BF16), wd_ref[...])

        @pl.when(j == 0)
        def _():
            f_ref[...] = term

        @pl.when(j > 0)
        def _():
            f_ref[...] += term

        @pl.when(j == n_chunks - 1)
        def _():
            h1_ref[...] = h0_ref[...] + 0.5 * _rms(f_ref[...], gpost_ref[...])

    row = lambda i, j: (i, 0)
    chunk = pl.BlockSpec((tt, FF_CHUNK), lambda i, j: (i, j))
    return _call(
        body, comm, name=name, grid=(t // tt, n_chunks),
        in_specs=[pl.BlockSpec((tt, D_MODEL), row), chunk, chunk, _full((1, D_MODEL)),
                  pl.BlockSpec((FF_CHUNK, D_MODEL), lambda i, j: (j, 0))],
        out_specs=[pl.BlockSpec((tt, D_MODEL), row), pl.BlockSpec((tt, D_MODEL), row)],
        out_shape=[jax.ShapeDtypeStruct((t, D_MODEL), F32), jax.ShapeDtypeStruct((t, D_MODEL), F32)],
        args=(h0, gt, up, g_post, wd))


def _ffn_bwd_hidden(dh1, f, gt, up, g_post, wd, comm, name):
    t = f.shape[0]
    tt = _tile_rows(t)
    n_chunks = FF_VIRT // FF_CHUNK

    def body(dh1_ref, f_ref, gt_ref, up_ref, gpost_ref, wd_ref,
             df_ref, act_ref, dgt_ref, dup_ref, dgpost_ref, dfs_ref):
        i, j = pl.program_id(0), pl.program_id(1)

        @pl.when((i == 0) & (j == 0))
        def _():
            dgpost_ref[...] = jnp.zeros_like(dgpost_ref)

        @pl.when(j == 0)
        def _():
            df, dg = _rms_bwd(f_ref[...], gpost_ref[...], 0.5 * dh1_ref[...])
            dgpost_ref[...] += dg
            dfb = df.astype(BF16)
            dfs_ref[...] = dfb
            df_ref[...] = dfb

        da = _dot_nt(dfs_ref[...], wd_ref[...])
        g = gt_ref[...].astype(F32)
        u = up_ref[...].astype(F32)
        s = _sigmoid(g)
        sl = g * s
        act_ref[...] = (sl * u).astype(BF16)
        dgt_ref[...] = (da * u * (s * (1.0 + g * (1.0 - s)))).astype(BF16)
        dup_ref[...] = (da * sl).astype(BF16)

    row = lambda i, j: (i, 0)
    chunk = pl.BlockSpec((tt, FF_CHUNK), lambda i, j: (i, j))
    return _call(
        body, comm, name=name, grid=(t // tt, n_chunks),
        in_specs=[pl.BlockSpec((tt, D_MODEL), row), pl.BlockSpec((tt, D_MODEL), row), chunk, chunk,
                  _full((1, D_MODEL)), pl.BlockSpec((FF_CHUNK, D_MODEL), lambda i, j: (j, 0))],
        out_specs=[pl.BlockSpec((tt, D_MODEL), row), chunk, chunk, chunk, _full((1, D_MODEL))],
        out_shape=[jax.ShapeDtypeStruct((t, D_MODEL), BF16)] + [jax.ShapeDtypeStruct((t, FF_VIRT), BF16)] * 3 + [
            jax.ShapeDtypeStruct((1, D_MODEL), F32)],
        scratch_shapes=[pltpu.VMEM((tt, D_MODEL), BF16)],
        args=(dh1, f, gt, up, g_post, wd))


def _ffn_bwd_input(dh1, h0, dgt, dup, g_pre, wg, wu, comm, name):
    t = h0.shape[0]
    tt = 2 * _tile_rows(t)
    n_chunks = FF_VIRT // FF_CHUNK

    def body(dh1_ref, h0_ref, dgt_ref, dup_ref, gpre_ref, wg_ref, wu_ref, dh0_ref, dgpre_ref, acc_ref):
        i, j = pl.program_id(0), pl.program_id(1)

        @pl.when((i == 0) & (j == 0))
        def _():
            dgpre_ref[...] = jnp.zeros_like(dgpre_ref)

        @pl.when(j == 0)
        def _():
            acc_ref[...] = jnp.zeros_like(acc_ref)

        acc_ref[...] += _dot(dgt_ref[...], wg_ref[...]) + _dot(dup_ref[...], wu_ref[...])

        @pl.when(j == n_chunks - 1)
        def _():
            dx, dg = _rms_bwd(h0_ref[...], gpre_ref[...], acc_ref[...])
            dgpre_ref[...] += dg
            dh0_ref[...] = dh1_ref[...] + dx

    row = lambda i, j: (i, 0)
    chunk = pl.BlockSpec((tt, FF_CHUNK), lambda i, j: (i, j))
    wspec = pl.BlockSpec((FF_CHUNK, D_MODEL), lambda i, j: (j, 0))
    return _call(
        body, comm, name=name, grid=(t // tt, n_chunks),
        in_specs=[pl.BlockSpec((tt, D_MODEL), row), pl.BlockSpec((tt, D_MODEL), row), chunk, chunk,
                  _full((1, D_MODEL)), wspec, wspec],
        out_specs=[pl.BlockSpec((tt, D_MODEL), row), _full((1, D_MODEL))],
        out_shape=[jax.ShapeDtypeStruct((t, D_MODEL), F32), jax.ShapeDtypeStruct((1, D_MODEL), F32)],
        scratch_shapes=[pltpu.VMEM((tt, D_MODEL), F32)],
        args=(dh1, h0, dgt, dup, g_pre, wg, wu))


def _mm_tn(a, b, m, tm, tn, name, comm=None):
    t, n = b.shape
    tk = min(t, 2048)
    nk = t // tk

    def body(a_ref, b_ref, out_ref, acc_ref):
        k = pl.program_id(2)

        @pl.when(k == 0)
        def _():
            acc_ref[...] = jnp.zeros_like(acc_ref)

        acc_ref[...] += _dot_tn(a_ref[...], b_ref[...])

        @pl.when(k == nk - 1)
        def _():
            out_ref[...] = acc_ref[...].astype(out_ref.dtype)

    outs, got = _call(
        body, comm, name=name, grid=(m // tm, n // tn, nk),
        in_specs=[pl.BlockSpec((tk, tm), lambda i, j, k: (k, i)), pl.BlockSpec((tk, tn), lambda i, j, k: (k, j))],
        out_specs=[pl.BlockSpec((tm, tn), lambda i, j, k: (i, j))],
        out_shape=[jax.ShapeDtypeStruct((m, n), BF16)],
        scratch_shapes=[pltpu.VMEM((tm, tn), F32)], args=(a, b))
    return outs[0], got


_SPLITS = (0, LRU_W, 2 * LRU_W, 2 * LRU_W + Q_LORA, 2 * LRU_W + Q_LORA + KV_LORA, IN_PAD)
_WIDTHS = tuple(_SPLITS[k + 1] - _SPLITS[k] for k in range(5))


def _mixer_in(h1, g_pre, w_in):
    t = h1.shape[0]
    tt = _tile_rows(t)

    def body(h_ref, g_ref, w_ref, n_ref, *outs):
        n = _rms(h_ref[...], g_ref[...]).astype(BF16)
        n_ref[...] = n
        proj = _dot_nt(n, w_ref[...])
        for k in range(5):
            outs[k][...] = proj[:, _SPLITS[k]:_SPLITS[k + 1]]

    row = lambda i: (i, 0)
    return pl.pallas_call(
        body, name="mixer_in",
        grid=(t // tt,),
        in_specs=[pl.BlockSpec((tt, D_MODEL), row), _full((1, D_MODEL)), _full((IN_PAD, D_MODEL))],
        out_specs=[pl.BlockSpec((tt, D_MODEL), row)] + [pl.BlockSpec((tt, w), row) for w in _WIDTHS],
        out_shape=[jax.ShapeDtypeStruct((t, D_MODEL), BF16)] + [jax.ShapeDtypeStruct((t, w), F32) for w in _WIDTHS],
        compiler_params=_params(),
    )(h1, g_pre, w_in)


def _mixer_in_bwd(dh2, h1, g_pre, w_in, dparts):
    t = h1.shape[0]
    tt = _tile_rows(t)

    def body(dh2_ref, h_ref, g_ref, w_ref, p0, p1, p2, p3, p4, dh1_ref, dproj_ref, dg_ref):
        i = pl.program_id(0)

        @pl.when(i == 0)
        def _():
            dg_ref[...] = jnp.zeros_like(dg_ref)

        dproj = jnp.concatenate([p[...] for p in (p0, p1, p2, p3, p4)], axis=1)
        dproj_ref[...] = dproj
        dx, dg = _rms_bwd(h_ref[...], g_ref[...], _dot(dproj, w_ref[...]))
        dg_ref[...] += dg
        dh1_ref[...] = dh2_ref[...] + dx

    row = lambda i: (i, 0)
    return pl.pallas_call(
        body, name="mixer_in_bwd",
        grid=(t // tt,),
        in_specs=[pl.BlockSpec((tt, D_MODEL), row), pl.BlockSpec((tt, D_MODEL), row), _full((1, D_MODEL)),
                  _full((IN_PAD, D_MODEL))] + [pl.BlockSpec((tt, w), row) for w in _WIDTHS],
        out_specs=[pl.BlockSpec((tt, D_MODEL), row), pl.BlockSpec((tt, IN_PAD), row), _full((1, D_MODEL))],
        out_shape=[jax.ShapeDtypeStruct((t, D_MODEL), F32), jax.ShapeDtypeStruct((t, IN_PAD), BF16),
                   jax.ShapeDtypeStruct((1, D_MODEL), F32)],
        compiler_params=_params(),
    )(dh2, h1, g_pre, w_in, *dparts)


def _conv(x_prev8, x_tile, w_ref, b_ref):
    tt = x_tile.shape[0]
    xx = jnp.concatenate([x_prev8, x_tile], axis=0)
    y = b_ref[...] + w_ref[CONV_W - 1:CONV_W, :] * x_tile
    for k in range(CONV_W - 1):
        y = y + w_ref[k:k + 1, :] * pltpu.roll(xx, CONV_W - 1 - k, 0)[8:8 + tt]
    return y


def _neg_expm1(z):
    series = -z * (1.0 + z * 0.5 * (1.0 + z / 3.0 * (1.0 + z * 0.25 * (1.0 + z * 0.2 * (1.0 + z / 6.0)))))
    return jnp.where(z > -0.3, series, 1.0 - jnp.exp(z))


def _lru_elementwise(xc, pre_a, pre_i, lam):
    ra = _sigmoid(pre_a)
    ri = _sigmoid(pre_i)
    neg = -lam
    softplus = jnp.maximum(neg, 0.0) + jnp.log(1.0 + jnp.exp(-jnp.abs(neg)))
    log_a = -LRU_C * ra * softplus
    a = jnp.exp(log_a)
    mult = jnp.sqrt(_neg_expm1(2.0 * log_a))
    return a, mult * (ri * xc)


def _lru_gates(xl, conv_w, conv_b, wa, wx, ba, bx, lam):
    t = xl.shape[0]
    tt = _tile_rows(t)
    r8 = tt // 8

    def body(prev_ref, x_ref, cw_ref, cb_ref, wa_ref, wx_ref, ba_ref, bx_ref, lam_ref, xc_ref, a_ref, u_ref):
        i = pl.program_id(0)
        prev = jnp.where(i == 0, 0.0, prev_ref[...])
        xc = _conv(prev, x_ref[...], cw_ref, cb_ref)
        xc_ref[...] = xc
        xb = xc.astype(BF16)
        a, u = _lru_elementwise(xc, _dot(xb, wa_ref[...]) + ba_ref[...], _dot(xb, wx_ref[...]) + bx_ref[...],
                                lam_ref[...])
        a_ref[...] = a
        u_ref[...] = u

    row = lambda i: (i, 0)
    vec = _full((1, LRU_W))
    return pl.pallas_call(
        body, name="lru_gates",
        grid=(t // tt,),
        in_specs=[pl.BlockSpec((8, LRU_W), lambda i: (jnp.maximum(i * r8 - 1, 0), 0)),
                  pl.BlockSpec((tt, LRU_W), row), _full((8, LRU_W)), vec,
                  _full((LRU_W, LRU_W)), _full((LRU_W, LRU_W)), vec, vec, vec],
        out_specs=[pl.BlockSpec((tt, LRU_W), row)] * 3,
        out_shape=[jax.ShapeDtypeStruct((t, LRU_W), F32)] * 3,
        compiler_params=_params(),
    )(xl, xl, conv_w, conv_b, wa, wx, ba, bx, lam)


def _lru_scan(a, u, reverse):
    t = a.shape[0]
    pieces = SCAN_PIECES
    rows = t // pieces
    nblk = rows // 8

    def body(a_ref, u_ref, h_ref, carry_ref):
        @pl.when(pl.program_id(0) == 0)
        def _():
            carry_ref[...] = jnp.zeros_like(carry_ref)

        def fwd(blk, h):
            base = pl.multiple_of(blk * 8, 8)
            for k in range(8):
                h = a_ref[pl.ds(base + k, 1), :] * h + u_ref[pl.ds(base + k, 1), :]
                h_ref[pl.ds(base + k, 1), :] = h
            return h

        def bwd(n, carry):
            base = pl.multiple_of((nblk - 1 - n) * 8, 8)
            for k in range(7, -1, -1):
                g = u_ref[pl.ds(base + k, 1), :] + carry
                h_ref[pl.ds(base + k, 1), :] = g
                carry = a_ref[pl.ds(base + k, 1), :] * g
            return carry

        carry_ref[...] = lax.fori_loop(0, nblk, bwd if reverse else fwd, carry_ref[...])

    piece = pl.BlockSpec((rows, LRU_W), (lambda i: (pieces - 1 - i, 0)) if reverse else (lambda i: (i, 0)))
    return pl.pallas_call(
        body, name="lru_scan_rev" if reverse else "lru_scan", grid=(pieces,),
        in_specs=[piece, piece], out_specs=piece,
        out_shape=jax.ShapeDtypeStruct((t, LRU_W), F32),
        scratch_shapes=[pltpu.VMEM((1, LRU_W), F32)],
        compiler_params=_params(),
    )(a, u)


def _lru_gates_bwd(dh, h, xc, wa, wx, ba, bx, lam):
    t = dh.shape[0]
    tt = _tile_rows(t)
    r8 = tt // 8

    def body(dh_ref, hprev_ref, h_ref, xc_ref, wa_ref, wx_ref, ba_ref, bx_ref, lam_ref,
             dxc_ref, dwa_ref, dwx_ref, dba_ref, dbx_ref, dlam_ref):
        i = pl.program_id(0)

        @pl.when(i == 0)
        def _():
            for r in (dwa_ref, dwx_ref, dba_ref, dbx_ref, dlam_ref):
                r[...] = jnp.zeros_like(r)

        prev = jnp.where(i == 0, 0.0, hprev_ref[...])
        h_before = pltpu.roll(jnp.concatenate([prev, h_ref[...]], axis=0), 1, 0)[8:8 + tt]
        dh_t = dh_ref[...]
        xc = xc_ref[...]
        xb = xc.astype(BF16)
        pre_a = _dot(xb, wa_ref[...]) + ba_ref[...]
        pre_i = _dot(xb, wx_ref[...]) + bx_ref[...]
        _, vjp = jax.vjp(_lru_elementwise, xc, pre_a, pre_i, lam_ref[...])
        dxc, dpre_a, dpre_i, dlam = vjp((dh_t * h_before, dh_t))
        da_b = dpre_a.astype(BF16)
        di_b = dpre_i.astype(BF16)
        dxc_ref[...] = dxc + _dot_nt(da_b, wa_ref[...]) + _dot_nt(di_b, wx_ref[...])
        dwa_ref[...] += _dot_tn(xb, da_b)
        dwx_ref[...] += _dot_tn(xb, di_b)
        dba_ref[...] += jnp.sum(dpre_a, axis=0, keepdims=True)
        dbx_ref[...] += jnp.sum(dpre_i, axis=0, keepdims=True)
        dlam_ref[...] += dlam

    row = lambda i: (i, 0)
    vec = _full((1, LRU_W))
    sq = _full((LRU_W, LRU_W))
    return pl.pallas_call(
        body, name="lru_gates_bwd",
        grid=(t // tt,),
        in_specs=[pl.BlockSpec((tt, LRU_W), row),
                  pl.BlockSpec((8, LRU_W), lambda i: (jnp.maximum(i * r8 - 1, 0), 0)),
                  pl.BlockSpec((tt, LRU_W), row), pl.BlockSpec((tt, LRU_W), row), sq, sq, vec, vec, vec],
        out_specs=[pl.BlockSpec((tt, LRU_W), row), sq, sq, vec, vec, vec],
        out_shape=[jax.ShapeDtypeStruct((t, LRU_W), F32), jax.ShapeDtypeStruct((LRU_W, LRU_W), F32),
                   jax.ShapeDtypeStruct((LRU_W, LRU_W), F32)] + [jax.ShapeDtypeStruct((1, LRU_W), F32)] * 3,
        compiler_params=_params(),
    )(dh, h, h, xc, wa, wx, ba, bx, lam)


def _conv_bwd(dxc, xl, conv_w):
    t = dxc.shape[0]
    tt = _tile_rows(t)
    r8 = tt // 8
    n_tiles = t // tt
    last = t // 8 - 1

    def body(d_ref, dnext_ref, xprev_ref, x_ref, cw_ref, dxl_ref, dw_ref, db_ref):
        i = pl.program_id(0)

        @pl.when(i == 0)
        def _():
            dw_ref[...] = jnp.zeros_like(dw_ref)
            db_ref[...] = jnp.zeros_like(db_ref)

        d = d_ref[...]
        nxt = jnp.where(i == n_tiles - 1, 0.0, dnext_ref[...])
        dd = jnp.concatenate([d, nxt], axis=0)
        dx = cw_ref[CONV_W - 1:CONV_W, :] * d
        for k in range(CONV_W - 1):
            shift = CONV_W - 1 - k
            dx = dx + cw_ref[k:k + 1, :] * pltpu.roll(dd, tt + 8 - shift, 0)[:tt]
        dxl_ref[...] = dx.astype(BF16)

        prev = jnp.where(i == 0, 0.0, xprev_ref[...])
        xx = jnp.concatenate([prev, x_ref[...]], axis=0)
        rows = []
        for k in range(CONV_W):
            shifted = x_ref[...] if k == CONV_W - 1 else pltpu.roll(xx, CONV_W - 1 - k, 0)[8:8 + tt]
            rows.append(jnp.sum(d * shifted, axis=0, keepdims=True))
        rows.append(jnp.zeros((8 - CONV_W, LRU_W), F32))
        dw_ref[...] += jnp.concatenate(rows, axis=0)
        db_ref[...] += jnp.sum(d, axis=0, keepdims=True)

    row = lambda i: (i, 0)
    return pl.pallas_call(
        body, name="conv_bwd",
        grid=(n_tiles,),
        in_specs=[pl.BlockSpec((tt, LRU_W), row),
                  pl.BlockSpec((8, LRU_W), lambda i: (jnp.minimum((i + 1) * r8, last), 0)),
                  pl.BlockSpec((8, LRU_W), lambda i: (jnp.maximum(i * r8 - 1, 0), 0)),
                  pl.BlockSpec((tt, LRU_W), row), _full((8, LRU_W))],
        out_specs=[pl.BlockSpec((tt, LRU_W), row), _full((8, LRU_W)), _full((1, LRU_W))],
        out_shape=[jax.ShapeDtypeStruct((t, LRU_W), BF16), jax.ShapeDtypeStruct((8, LRU_W), F32),
                   jax.ShapeDtypeStruct((1, LRU_W), F32)],
        compiler_params=_params(),
    )(dxc, dxc, xl, xl, conv_w)


def _rope(x, c, s1, s2):
    n = x.shape[-1]
    return x * c + pltpu.roll(x, n - 16, 1) * s1 + pltpu.roll(x, 16, 1) * s2


def _rope_t(d, c, s1, s2):
    n = d.shape[-1]
    return d * c + pltpu.roll(d * s1, 16, 1) + pltpu.roll(d * s2, n - 16, 1)


def _mla_proj(qlat, kvlat, krope, tables, gq, gkv, wq, wk, wv):
    t = qlat.shape[0]
    tt = _tile_rows(t)

    def body(q_ref, kv_ref, kr_ref, cq, s1q, s2q, ck, s1k, s2k, gq_ref, gkv_ref, wq_ref, wk_ref, wv_ref,
             qo_ref, ko_ref, vo_ref, qn_ref, kva_ref, vt_ref):
        qn = _rms(q_ref[...], gq_ref[...]).astype(BF16)
        kvn = _rms(kv_ref[...], gkv_ref[...]).astype(BF16)
        kr = _rope(kr_ref[...], ck[...], s1k[...], s2k[...]).astype(BF16)
        kva = jnp.concatenate([kvn, kr], axis=1)
        qn_ref[...] = qn
        kva_ref[...] = kva
        q = _dot_nt(qn, wq_ref[...])
        tile = lambda r: jnp.tile(r[...], (1, N_HEADS))
        qo_ref[...] = _rope(q, tile(cq), tile(s1q), tile(s2q)).astype(BF16)
        ko_ref[...] = _dot(kva, wk_ref[...]).astype(BF16)
        v = _dot(kvn, wv_ref[...])
        vo_ref[...] = v.astype(BF16)
        vt_ref[...] = jnp.transpose(v).astype(BF16)

    row = lambda i: (i, 0)
    tab = pl.BlockSpec((tt, HEAD_PAD), row)
    return pl.pallas_call(
        body, name="mla_proj",
        grid=(t // tt,),
        in_specs=[pl.BlockSpec((tt, Q_LORA), row), pl.BlockSpec((tt, KV_LORA), row), tab] + [tab] * 6 + [
            _full((1, Q_LORA)), _full((1, KV_LORA)), _full((HEADS_W, Q_LORA)), _full((KV_LORA + HEAD_PAD, HEADS_W)),
            _full((KV_LORA, HEADS_W))],
        out_specs=[pl.BlockSpec((tt, HEADS_W), row)] * 3 + [pl.BlockSpec((tt, Q_LORA), row),
                                                             pl.BlockSpec((tt, KV_LORA + HEAD_PAD), row),
                                                             pl.BlockSpec((HEADS_W, tt), lambda i: (0, i))],
        out_shape=[jax.ShapeDtypeStruct((t, HEADS_W), BF16)] * 3 + [
            jax.ShapeDtypeStruct((t, Q_LORA), BF16), jax.ShapeDtypeStruct((t, KV_LORA + HEAD_PAD), BF16),
            jax.ShapeDtypeStruct((HEADS_W, t), BF16)],
        compiler_params=_params(),
    )(qlat, kvlat, krope, *tables, gq, gkv, wq, wk, wv)


def _mla_proj_bwd(dq, dk, dv, qlat, kvlat, tables, gq, gkv, wq, wk, wv):
    t = qlat.shape[0]
    tt = _tile_rows(t)

    def body(dq_ref, dk_ref, dv_ref, q_ref, kv_ref, cq, s1q, s2q, ck, s1k, s2k, gq_ref, gkv_ref,
             wq_ref, wk_ref, wv_ref, dqr_ref, dql_ref, dkvl_ref, dkr_ref, dgq_ref, dgkv_ref):
        i = pl.program_id(0)

        @pl.when(i == 0)
        def _():
            dgq_ref[...] = jnp.zeros_like(dgq_ref)
            dgkv_ref[...] = jnp.zeros_like(dgkv_ref)

        tile = lambda r: jnp.tile(r[...], (1, N_HEADS))
        dqr = _rope_t(dq_ref[...], tile(cq), tile(s1q), tile(s2q)).astype(BF16)
        dqr_ref[...] = dqr
        dql, dg = _rms_bwd(q_ref[...], gq_ref[...], _dot(dqr, wq_ref[...]))
        dgq_ref[...] += dg
        dql_ref[...] = dql.astype(BF16)
        dkva = _dot_nt(dk_ref[...], wk_ref[...])
        dkvn = dkva[:, :KV_LORA] + _dot_nt(dv_ref[...], wv_ref[...])
        dkvl, dg = _rms_bwd(kv_ref[...], gkv_ref[...], dkvn)
        dgkv_ref[...] += dg
        dkvl_ref[...] = dkvl.astype(BF16)
        dkr_ref[...] = _rope_t(dkva[:, KV_LORA:], ck[...], s1k[...], s2k[...]).astype(BF16)

    row = lambda i: (i, 0)
    tab = pl.BlockSpec((tt, HEAD_PAD), row)
    wide = pl.BlockSpec((tt, HEADS_W), row)
    return pl.pallas_call(
        body, name="mla_proj_bwd",
        grid=(t // tt,),
        in_specs=[wide, wide, wide, pl.BlockSpec((tt, Q_LORA), row), pl.BlockSpec((tt, KV_LORA), row)] + [tab] * 6 + [
            _full((1, Q_LORA)), _full((1, KV_LORA)), _full((HEADS_W, Q_LORA)), _full((KV_LORA + HEAD_PAD, HEADS_W)),
            _full((KV_LORA, HEADS_W))],
        out_specs=[wide, pl.BlockSpec((tt, Q_LORA), row), pl.BlockSpec((tt, KV_LORA), row), tab,
                   _full((1, Q_LORA)), _full((1, KV_LORA))],
        out_shape=[jax.ShapeDtypeStruct((t, HEADS_W), BF16), jax.ShapeDtypeStruct((t, Q_LORA), BF16),
                   jax.ShapeDtypeStruct((t, KV_LORA), BF16), jax.ShapeDtypeStruct((t, HEAD_PAD), BF16),
                   jax.ShapeDtypeStruct((1, Q_LORA), F32), jax.ShapeDtypeStruct((1, KV_LORA), F32)],
        compiler_params=_params(),
    )(dq, dk, dv, qlat, kvlat, *tables, gq, gkv, wq, wk, wv)


NEG = -1e30


def _chunk_mask_t(keys, key0, queries):
    kc = (lax.broadcasted_iota(jnp.int32, (keys, queries), 0) + key0) // CHUNK
    qc = lax.broadcasted_iota(jnp.int32, (keys, queries), 1) // CHUNK
    return kc <= qc


def _attn_fwd(q, k, vt, comm):
    t = q.shape[0]
    bq = _tile_rows(t)
    nq = t // bq

    def body(q_ref, k_ref, vt_ref, o_ref, lse_ref, m_ref, l_ref, acc_ref):
        i = pl.program_id(1)
        m_ref[...] = jnp.full_like(m_ref, NEG)
        l_ref[...] = jnp.zeros_like(l_ref)
        acc_ref[...] = jnp.zeros_like(acc_ref)
        qb = q_ref[...]

        def block(first_key, keys, diagonal):
            cols = pl.ds(pl.multiple_of(first_key, bq), keys)
            s = _dot_nt(k_ref[cols, :], qb)
            if diagonal:
                s = jnp.where(_chunk_mask_t(keys, bq - keys, bq), s, NEG)
            m_old = m_ref[...]
            m_new = jnp.maximum(m_old, jnp.max(s, axis=0, keepdims=True))
            alpha = jnp.exp(m_old - m_new)
            p = jnp.exp(s - m_new)
            l_ref[...] = alpha * l_ref[...] + jnp.sum(p, axis=0, keepdims=True)
            acc_ref[...] = alpha * acc_ref[...] + _dot(vt_ref[:, cols], p.astype(BF16))
            m_ref[...] = m_new

        def pair_below(jj, carry):
            block(jj * 2 * bq, 2 * bq, False)
            return carry

        lax.fori_loop(0, i // 2, pair_below, 0)

        @pl.when(i % 2 == 1)
        def _():
            block((i - 1) * bq, 2 * bq, True)

        @pl.when(i % 2 == 0)
        def _():
            block(i * bq, bq, True)

        l = l_ref[...]
        o_ref[...] = jnp.transpose(acc_ref[...] / l).astype(BF16)
        lse_ref[...] = m_ref[...] + jnp.log(l)

    qmap = lambda h, i: (i, h)
    return _call(
        body, comm, name="attn_fwd" if comm is None else "attn_fwd_gather", grid=(N_HEADS, nq),
        in_specs=[pl.BlockSpec((bq, HEAD_PAD), qmap), pl.BlockSpec((t, HEAD_PAD), lambda h, i: (0, h)),
                  pl.BlockSpec((HEAD_PAD, t), lambda h, i: (h, 0))],
        out_specs=[pl.BlockSpec((bq, HEAD_PAD), qmap), pl.BlockSpec((None, 1, bq), lambda h, i: (h, 0, i))],
        out_shape=[jax.ShapeDtypeStruct(q.shape, BF16), jax.ShapeDtypeStruct((N_HEADS, 1, t), F32)],
        scratch_shapes=[pltpu.VMEM((1, bq), F32), pltpu.VMEM((1, bq), F32), pltpu.VMEM((HEAD_PAD, bq), F32)],
        args=(q, k, vt))


def _attn_bwd(q, k, v, o, do, lse, comm):
    t = q.shape[0]
    bq = _tile_rows(t)
    nq = t // bq

    def body(q_ref, k_ref, v_ref, o_ref, do_ref, lse_ref, dq_ref, dk_ref, dv_ref, dk_acc, dv_acc, delta_ref):
        j = pl.program_id(1)

        @pl.when(j == 0)
        def _():
            dq_ref[...] = jnp.zeros_like(dq_ref)
            for blk in range(nq):
                rows = slice(blk * bq, (blk + 1) * bq)
                delta = jnp.sum(do_ref[rows, :].astype(F32) * o_ref[rows, :].astype(F32), axis=-1, keepdims=True)
                delta_ref[:, rows] = jnp.transpose(jnp.broadcast_to(delta, (bq, HEAD_PAD)))[:1, :]

        dk_acc[...] = jnp.zeros_like(dk_acc)
        dv_acc[...] = jnp.zeros_like(dv_acc)
        kb, vb = k_ref[...], v_ref[...]

        def block(first_query, queries, diagonal):
            rows = pl.ds(pl.multiple_of(first_query, bq), queries)
            qb, dob = q_ref[rows, :], do_ref[rows, :]
            s = _dot_nt(kb, qb)
            if diagonal:
                s = jnp.where(_chunk_mask_t(bq, 0, queries), s, NEG)
            p = jnp.exp(s - lse_ref[:, rows])
            dv_acc[...] += _dot(p.astype(BF16), dob)
            dp = _dot_nt(vb, dob)
            ds = (p * (dp - delta_ref[:, rows])).astype(BF16)
            dk_acc[...] += _dot(ds, qb)
            dq_ref[rows, :] += _dot_tn(ds, kb)

        after = nq - 1 - j
        odd = after % 2

        @pl.when(odd == 1)
        def _():
            block(j * bq, 2 * bq, True)

        @pl.when(odd == 0)
        def _():
            block(j * bq, bq, True)

        def pair_above(n, carry):
            block((j + 1 + odd + 2 * n) * bq, 2 * bq, False)
            return carry

        lax.fori_loop(0, after // 2, pair_above, 0)
        dk_ref[...] = dk_acc[...].astype(BF16)
        dv_ref[...] = dv_acc[...].astype(BF16)

    kmap = lambda h, j: (j, h)
    head = lambda h, j: (0, h)
    whole = pl.BlockSpec((t, HEAD_PAD), head)
    return _call(
        body, comm, name="attn_bwd" if comm is None else "attn_bwd_exchange", grid=(N_HEADS, nq),
        in_specs=[whole, pl.BlockSpec((bq, HEAD_PAD), kmap), pl.BlockSpec((bq, HEAD_PAD), kmap), whole, whole,
                  pl.BlockSpec((None, 1, t), lambda h, j: (h, 0, 0))],
        out_specs=[whole, pl.BlockSpec((bq, HEAD_PAD), kmap), pl.BlockSpec((bq, HEAD_PAD), kmap)],
        out_shape=[jax.ShapeDtypeStruct(q.shape, F32), jax.ShapeDtypeStruct(q.shape, BF16),
                   jax.ShapeDtypeStruct(q.shape, BF16)],
        scratch_shapes=[pltpu.VMEM((bq, HEAD_PAD), F32), pltpu.VMEM((bq, HEAD_PAD), F32), pltpu.VMEM((1, t), F32)],
        args=(q, k, v, o, do, lse))


def _mixer_out(h1, hl, gate, o, w_lru, w_mla, g_post):
    t = h1.shape[0]
    tt = _tile_rows(t)

    def body(h1_ref, hl_ref, gate_ref, o_ref, wl_ref, wm_ref, g_ref, h2_ref, m_ref, y_ref):
        y = (hl_ref[...] * _gelu(gate_ref[...])).astype(BF16)
        y_ref[...] = y
        m = _dot(y, wl_ref[...]) + _dot(o_ref[...], wm_ref[...])
        m_ref[...] = m
        h2_ref[...] = h1_ref[...] + _rms(m, g_ref[...])

    row = lambda i: (i, 0)
    return pl.pallas_call(
        body, name="mixer_out",
        grid=(t // tt,),
        in_specs=[pl.BlockSpec((tt, D_MODEL), row), pl.BlockSpec((tt, LRU_W), row), pl.BlockSpec((tt, LRU_W), row),
                  pl.BlockSpec((tt, HEADS_W), row), _full((LRU_W, D_MODEL)), _full((HEADS_W, D_MODEL)),
                  _full((1, D_MODEL))],
        out_specs=[pl.BlockSpec((tt, D_MODEL), row), pl.BlockSpec((tt, D_MODEL), row),
                   pl.BlockSpec((tt, LRU_W), row)],
        out_shape=[jax.ShapeDtypeStruct((t, D_MODEL), F32), jax.ShapeDtypeStruct((t, D_MODEL), F32),
                   jax.ShapeDtypeStruct((t, LRU_W), BF16)],
        compiler_params=_params(),
    )(h1, hl, gate, o, w_lru, w_mla, g_post)


def _mixer_out_bwd(dh2, m, hl, gate, w_lru, w_mla, g_post):
    t = m.shape[0]
    tt = _tile_rows(t)

    def body(dh2_ref, m_ref, hl_ref, gate_ref, wl_ref, wm_ref, g_ref, dm_ref, dgate_ref, dhl_ref, do_ref, dg_ref):
        i = pl.program_id(0)

        @pl.when(i == 0)
        def _():
            dg_ref[...] = jnp.zeros_like(dg_ref)

        dm, dg = _rms_bwd(m_ref[...], g_ref[...], dh2_ref[...])
        dg_ref[...] += dg
        dmb = dm.astype(BF16)
        dm_ref[...] = dmb
        dy = _dot_nt(dmb, wl_ref[...])
        gate = gate_ref[...]
        dgate_ref[...] = (dy * hl_ref[...] * _gelu_grad(gate)).astype(BF16)
        dhl_ref[...] = dy * _gelu(gate)
        do_ref[...] = _dot_nt(dmb, wm_ref[...]).astype(BF16)

    row = lambda i: (i, 0)
    return pl.pallas_call(
        body, name="mixer_out_bwd",
        grid=(t // tt,),
        in_specs=[pl.BlockSpec((tt, D_MODEL), row), pl.BlockSpec((tt, D_MODEL), row), pl.BlockSpec((tt, LRU_W), row),
                  pl.BlockSpec((tt, LRU_W), row), _full((LRU_W, D_MODEL)), _full((HEADS_W, D_MODEL)),
                  _full((1, D_MODEL))],
        out_specs=[pl.BlockSpec((tt, D_MODEL), row), pl.BlockSpec((tt, LRU_W), row), pl.BlockSpec((tt, LRU_W), row),
                   pl.BlockSpec((tt, HEADS_W), row), _full((1, D_MODEL))],
        out_shape=[jax.ShapeDtypeStruct((t, D_MODEL), BF16), jax.ShapeDtypeStruct((t, LRU_W), BF16),
                   jax.ShapeDtypeStruct((t, LRU_W), F32), jax.ShapeDtypeStruct((t, HEADS_W), BF16),
                   jax.ShapeDtypeStruct((1, D_MODEL), F32)],
        compiler_params=_params(),
    )(dh2, m, hl, gate, w_lru, w_mla, g_post)


def _adamw(w, g, m, v):
    m = ADAM_B1 * m + (1.0 - ADAM_B1) * g
    v = ADAM_B2 * v + (1.0 - ADAM_B2) * (g * g)
    m_hat = m / (1.0 - ADAM_B1 ** ADAM_STEP)
    v_hat = v / (1.0 - ADAM_B2 ** ADAM_STEP)
    delta = -ADAM_LR * (m_hat / (jnp.sqrt(v_hat) + ADAM_EPS) + ADAM_WD * w)
    return delta, m, v


def _sum_adamw(terms, w, m, v, name):
    r, c = w.shape
    n_terms, rp = terms.shape[:2]
    cb = 256 if c % 256 == 0 else 128

    def body(t_ref, w_ref, m_ref, v_ref, g_out, d_out, m_out, v_out):
        g = t_ref[0, :r, :].astype(F32)
        for q in range(1, n_terms):
            g = g + t_ref[q, :r, :].astype(F32)
        g_out[...] = g
        d_out[...], m_out[...], v_out[...] = _adamw(w_ref[...], g, m_ref[...], v_ref[...])

    cols = pl.BlockSpec((r, cb), lambda j: (0, j))
    return pl.pallas_call(
        body, name=name, grid=(c // cb,),
        in_specs=[pl.BlockSpec((n_terms, rp, cb), lambda j: (0, 0, j)), cols, cols, cols], out_specs=[cols] * 4,
        out_shape=[jax.ShapeDtypeStruct((r, c), F32)] * 4,
        compiler_params=_params(),
    )(terms, w, m, v)


def _sum_terms(terms):
    def body(t_ref, out_ref):
        g = t_ref[0]
        for q in range(1, N_DEV):
            g = g + t_ref[q]
        out_ref[...] = g

    return pl.pallas_call(
        body, name="small_grad_sum", in_specs=[WHOLE], out_specs=WHOLE,
        out_shape=jax.ShapeDtypeStruct(terms.shape[1:], F32),
        compiler_params=_params(),
    )(terms)


def _adamw_flat(w, g, m, v):
    def body(w_ref, g_ref, m_ref, v_ref, d_out, m_out, v_out):
        d_out[...], m_out[...], v_out[...] = _adamw(w_ref[...], g_ref[...], m_ref[...], v_ref[...])

    return pl.pallas_call(
        body, name="small_adamw", in_specs=[WHOLE] * 4, out_specs=[WHOLE] * 3,
        out_shape=[jax.ShapeDtypeStruct(w.shape, F32)] * 3,
        compiler_params=_params(),
    )(w, g, m, v)


def _pack(arrays):
    flat = jnp.concatenate([a.reshape(-1).astype(F32) for a in arrays])
    rows = -(-flat.shape[0] // 1024) * 8
    return jnp.pad(flat, (0, rows * 128 - flat.shape[0])).reshape(rows, 128)


def _unpack(packed, shapes):
    flat = packed.reshape(-1)
    out, pos = [], 0
    for s in shapes:
        n = math.prod(s)
        out.append(flat[pos:pos + n].reshape(s))
        pos += n
    return out


def _rope_tables(positions):
    t = positions.shape[-1]
    inv_freq = 1.0 / (ROPE_THETA ** (jnp.arange(0, QK_ROPE, 2, dtype=F32) / QK_ROPE))
    ang = positions.reshape(t, 1).astype(F32) * inv_freq
    cos, sin = jnp.cos(ang), jnp.sin(ang)
    z16 = jnp.zeros((t, 16), F32)

    def place(first, second, lead, lead_value):
        parts = [jnp.full((t, lead), lead_value, F32)] if lead else []
        parts += [first, second, jnp.zeros((t, HEAD_PAD - lead - 32), F32)]
        return jnp.concatenate(parts, axis=1)

    q_tabs = [place(cos, cos, QK_NOPE, 1.0) * ATT_SCALE, place(-sin, z16, QK_NOPE, 0.0) * ATT_SCALE,
              place(z16, sin, QK_NOPE, 0.0) * ATT_SCALE]
    k_tabs = [place(cos, cos, 0, 0.0), place(-sin, z16, 0, 0.0), place(z16, sin, 0, 0.0)]
    return q_tabs + k_tabs


def _block_diag(w):
    eye = jnp.eye(w.shape[0], dtype=w.dtype)
    return jnp.einsum("nde,nm->ndme", w, eye).reshape(LRU_W, LRU_W)


def _diag_blocks(g):
    g4 = g.reshape(8, 64, 8, 64)
    return jnp.moveaxis(jnp.diagonal(g4, axis1=0, axis2=2), -1, 0)


def _mixer_layouts(got):
    full = {}
    w_in = got["w_in"][:, :IN_SHARD].reshape(IN_COLS, D_MODEL)
    full["w_in"] = jnp.pad(w_in, ((0, IN_PAD - IN_COLS), (0, 0)))
    full["wq"] = got["w_q_b"].reshape(HEADS_W, Q_LORA)
    kvb = got["w_kv_b"]
    zeros = jnp.zeros((N_HEADS, KV_LORA, HEAD_PAD - QK_NOPE), BF16)
    k_part = jnp.transpose(jnp.concatenate([kvb[:, :, :QK_NOPE], zeros], axis=2), (1, 0, 2)).reshape(KV_LORA, HEADS_W)
    rope_rows = jnp.pad(jnp.eye(QK_ROPE, dtype=BF16), ((0, HEAD_PAD - QK_ROPE), (QK_NOPE, HEAD_PAD - QK_NOPE - QK_ROPE)))
    full["wk"] = jnp.concatenate([k_part, jnp.tile(rope_rows, (1, N_HEADS))], axis=0)
    full["wv"] = jnp.transpose(jnp.concatenate([kvb[:, :, QK_NOPE:], zeros], axis=2), (1, 0, 2)).reshape(KV_LORA, HEADS_W)
    w_out = got["w_out"].reshape(D_MODEL, D_MODEL)
    full["w_out_lru"] = w_out[:LRU_W]
    full["w_out_mla"] = jnp.pad(w_out[LRU_W:].reshape(N_HEADS, V_DIM, D_MODEL),
                                ((0, 0), (0, HEAD_PAD - V_DIM), (0, 0))).reshape(HEADS_W, D_MODEL)
    conv_w = jnp.transpose(got["conv_w"][:, :CONV_W, :LRU_W // N_DEV], (1, 0, 2)).reshape(CONV_W, LRU_W)
    full["conv_w"] = jnp.pad(conv_w, ((0, 8 - CONV_W), (0, 0)))
    return full


_FFN1 = ["w_ffn1_gate", "w_ffn1_up", "w_ffn1_down"]
_MIX = ["w_in", "w_q_b", "w_kv_b", "w_out"]
_FFN2 = ["w_ffn2_gate", "w_ffn2_up", "w_ffn2_down"]
_SHARDED = _FFN1 + _MIX + _FFN2
_TRANSPOSED = ("w_ffn1_gate", "w_ffn1_up", "w_in", "w_q_b", "w_ffn2_gate", "w_ffn2_up")
_SMALL = ["g_ffn1_pre", "g_ffn1_post", "g_mix_pre", "g_mix_post", "conv_b", "w_lru_a", "b_lru_a", "w_lru_x",
          "b_lru_x", "lru_lambda", "q_a_norm", "kv_a_norm", "g_ffn2_pre", "g_ffn2_post", "conv_w"]
_SMALL_EARLY = [n for n in _SMALL if n != "g_ffn1_pre"]
_ORDER = ["g_ffn1_pre", "g_ffn1_post", "w_ffn1_gate", "w_ffn1_up", "w_ffn1_down", "g_mix_pre", "g_mix_post", "w_in",
          "conv_w", "conv_b", "w_lru_a", "b_lru_a", "w_lru_x", "b_lru_x", "lru_lambda", "q_a_norm", "w_q_b",
          "kv_a_norm", "w_kv_b", "w_out", "g_ffn2_pre", "g_ffn2_post", "w_ffn2_gate", "w_ffn2_up", "w_ffn2_down"]
_FF_BLOCK = (FF_PAD, D_MODEL)
_PADS = {"w_ffn1_gate": _FF_BLOCK, "w_ffn1_up": _FF_BLOCK, "w_ffn1_down": _FF_BLOCK,
         "w_in": (IN_SHARD_PAD, D_MODEL), "w_q_b": (HEAD_PAD, Q_LORA), "w_kv_b": (KV_LORA, HEAD_PAD),
         "w_out": (HEAD_PAD, D_MODEL), "conv_w": (8, 128),
         "w_ffn2_gate": _FF_BLOCK, "w_ffn2_up": _FF_BLOCK, "w_ffn2_down": _FF_BLOCK}


def _rows(name, a):
    return a.T if name in _TRANSPOSED else a


def _step(args):
    x = args["x"][0]
    target = args["loss_target"][0]
    positions = args["positions"][0]
    w = {n: args[n][0] for n in _ORDER}
    mom = {n: args["m_" + n][0] for n in _ORDER}
    var = {n: args["v_" + n][0] for n in _ORDER}
    vec = lambda name: w[name].reshape(1, -1)
    t = x.shape[0]
    ff_blocks = lambda a: a.reshape(N_DEV, FF_PAD, D_MODEL)
    ff_rows = lambda a: a.reshape(FF_VIRT, D_MODEL)

    names = _FFN1 + _MIX + ["conv_w"] + _FFN2
    (wg1,), staged = _first_gather([_rows(n, w[n]) for n in names], [_PADS[n] for n in names],
                                   [F32 if n == "conv_w" else BF16 for n in names], 1)
    wg1 = ff_rows(wg1)
    n_mix = 2 + len(_MIX) + 1

    tables = _rope_tables(positions)
    wa = _block_diag(w["w_lru_a"]).astype(BF16)
    wx = _block_diag(w["w_lru_x"]).astype(BF16)

    (n1, gt1), got = _ffn_fwd_gate(x, vec("g_ffn1_pre"), wg1, _Gather(staged[:1]), "ffn_fwd_gate_gather")
    wu1 = ff_rows(got[0])
    (up1,), got = _ffn_fwd_up(n1, wu1, _Gather(staged[1:2]), "ffn_fwd_up_gather")
    wd1 = ff_rows(got[0])
    (h1, f1), got = _ffn_fwd_out(x, gt1, up1, vec("g_ffn1_post"), wd1, _Gather(staged[2:n_mix]),
                                 "ffn_fwd_out_gather")
    mix = _mixer_layouts(dict(zip(_MIX + ["conv_w"], got)))
    nmix, xl, gate, qlat, kvlat, krope = _mixer_in(h1, vec("g_mix_pre"), mix["w_in"])
    xc, a, u = _lru_gates(xl, mix["conv_w"], vec("conv_b"), wa, wx, vec("b_lru_a"), vec("b_lru_x"), vec("lru_lambda"))
    hl = _lru_scan(a, u, reverse=False)
    q, k, v, qn, kva, vt = _mla_proj(qlat, kvlat, krope, tables, vec("q_a_norm"), vec("kv_a_norm"),
                                 mix["wq"], mix["wk"], mix["wv"])
    (o, lse), ffn2 = _attn_fwd(q, k, vt, _Gather(staged[n_mix:]))
    wg2, wu2, wd2 = [ff_rows(a_) for a_ in ffn2]
    h2, m, ylru = _mixer_out(h1, hl, gate, o, mix["w_out_lru"], mix["w_out_mla"], vec("g_mix_post"))
    dy, f2, n2, gt2, up2, loss_tile = _ffn_fwd_loss(h2, target, vec("g_ffn2_pre"), vec("g_ffn2_post"), wg2, wu2, wd2,
                                                    "ffn_fwd_loss")

    (df2, act2, dgt2, dup2, dg2post), _ = _ffn_bwd_hidden(dy, f2, gt2, up2, vec("g_ffn2_post"), wd2, None,
                                                          "ffn_bwd_hidden")
    (dh2, dg2pre), _ = _ffn_bwd_input(dy, h2, dgt2, dup2, vec("g_ffn2_pre"), wg2, wu2, None, "ffn_bwd_input")
    dw_ffn = lambda a_, b_, name, comm=None: _mm_tn(a_, b_, FF_VIRT, FF_CHUNK, D_MODEL, name, comm)
    big2 = [ff_blocks(dw_ffn(dgt2, n2, "dw_ffn_gate")[0]), ff_blocks(dw_ffn(dup2, n2, "dw_ffn_up")[0]),
            ff_blocks(dw_ffn(act2, df2, "dw_ffn_down")[0])]

    dm, dgate, dhl, do, dgmixpost = _mixer_out_bwd(dh2, m, hl, gate, mix["w_out_lru"], mix["w_out_mla"],
                                                   vec("g_mix_post"))
    dw_out_lru, _ = _mm_tn(ylru, dm, LRU_W, LRU_W, D_MODEL, "dw_out_lru")
    dw_out_mla, _ = _mm_tn(o, dm, HEADS_W, HEADS_W, D_MODEL, "dw_out_mla")
    dw_out = jnp.concatenate(
        [dw_out_lru, dw_out_mla.reshape(N_HEADS, HEAD_PAD, D_MODEL)[:, :V_DIM].reshape(N_HEADS * V_DIM, D_MODEL)], axis=0)

    (dq, dk, dv), terms2 = _attn_bwd(q, k, v, o, do, lse, _Exchange(big2))
    dqr, dqlat, dkvlat, dkrope, dgq, dgkv = _mla_proj_bwd(
        dq, dk, dv, qlat, kvlat, tables, vec("q_a_norm"), vec("kv_a_norm"), mix["wq"], mix["wk"], mix["wv"])
    dwq, _ = _mm_tn(dqr, qn, HEADS_W, HEADS_W, Q_LORA, "dw_q_b")
    dwk, _ = _mm_tn(kva, dk, KV_LORA, KV_LORA, HEADS_W, "dw_kv_b_k")
    dwv, _ = _mm_tn(kva, dv, KV_LORA, KV_LORA, HEADS_W, "dw_kv_b_v")
    dw_kvb = jnp.transpose(jnp.concatenate(
        [dwk.reshape(KV_LORA, N_HEADS, HEAD_PAD)[:, :, :QK_NOPE], dwv.reshape(KV_LORA, N_HEADS, HEAD_PAD)[:, :, :V_DIM]],
        axis=2), (1, 0, 2))

    dh = _lru_scan(a, dhl, reverse=True)
    dxc, dwa, dwx, dba, dbx, dlam = _lru_gates_bwd(dh, hl, xc, wa, wx, vec("b_lru_a"), vec("b_lru_x"),
                                                   vec("lru_lambda"))
    dxl, dconv_w8, dconv_b = _conv_bwd(dxc, xl, mix["conv_w"])
    dh1, dproj, dgmixpre = _mixer_in_bwd(dh2, h1, vec("g_mix_pre"), mix["w_in"], [dxl, dgate, dqlat, dkvlat, dkrope])
    dw_in, _ = _mm_tn(dproj, nmix, IN_PAD, IN_PAD // 2, D_MODEL, "dw_in")
    dw_in = jnp.pad(dw_in[:IN_COLS].reshape(N_DEV, IN_SHARD, D_MODEL), ((0, 0), (0, IN_SHARD_PAD - IN_SHARD), (0, 0)))
    big_mix = [dw_in, dwq.reshape(N_DEV, HEAD_PAD, Q_LORA), dw_kvb, dw_out.reshape(N_DEV, HEAD_PAD, D_MODEL)]

    (df1, act1, dgt1, dup1, dg1post), terms_mix = _ffn_bwd_hidden(
        dh1, f1, gt1, up1, vec("g_ffn1_post"), wd1, _Exchange(big_mix), "ffn_bwd_hidden_exchange")
    dwd1 = ff_blocks(dw_ffn(act1, df1, "dw_ffn_down")[0])
    dwg1, (swap_d,) = dw_ffn(dgt1, n1, "dw_ffn_gate_exchange", _PairSwap([dwd1]))
    dwg1 = ff_blocks(dwg1)
    dwu1, (terms_d1, swap_g) = dw_ffn(dup1, n1, "dw_ffn_up_exchange",
                                      _Several([_ChipExchange([_pair_sum(dwd1, swap_d)]), _PairSwap([dwg1])]))
    dwu1 = ff_blocks(dwu1)
    small = {"g_ffn1_post": dg1post, "g_mix_pre": dgmixpre, "g_mix_post": dgmixpost,
             "conv_b": dconv_b, "w_lru_a": _diag_blocks(dwa), "b_lru_a": dba, "w_lru_x": _diag_blocks(dwx),
             "b_lru_x": dbx, "lru_lambda": dlam, "q_a_norm": dgq, "kv_a_norm": dgkv, "g_ffn2_pre": dg2pre,
             "g_ffn2_post": dg2post, "conv_w": dconv_w8[:CONV_W]}
    to_all = lambda packed: jnp.broadcast_to(packed[None], (N_DEV,) + packed.shape)
    early = _pack([small[n] for n in _SMALL_EARLY] + [loss_tile[:1, :1]])
    (dx, dg1pre), (terms_g1, swap_u, early_terms) = _ffn_bwd_input(
        dh1, x, dgt1, dup1, vec("g_ffn1_pre"), wg1, wu1,
        _Several([_ChipExchange([_pair_sum(dwg1, swap_g)]), _PairSwap([dwu1]), _Exchange([to_all(early)])]),
        "ffn_bwd_input_exchange")
    small["g_ffn1_pre"] = dg1pre
    terms_u1, late_terms = _last_exchange(_Several([
        _ChipExchange([_pair_sum(dwu1, swap_u)]), _Exchange([to_all(_pack([dg1pre]))])]))

    terms = dict(zip(_FFN2, terms2))
    terms.update(zip(_MIX, terms_mix))
    terms.update({"w_ffn1_down": terms_d1, "w_ffn1_gate": terms_g1, "w_ffn1_up": terms_u1})

    grads, delta, new_m, new_v = {}, {}, {}, {}
    for n in _SHARDED:
        res = _sum_adamw(terms[n], _rows(n, w[n]), _rows(n, mom[n]), _rows(n, var[n]), "adamw_" + n)
        grads[n], delta[n], new_m[n], new_v[n] = [_rows(n, r) for r in res]

    small_sum = dict(zip(_SMALL_EARLY + ["loss"],
                         _unpack(_sum_terms(early_terms), [small[n].shape for n in _SMALL_EARLY] + [(1, 1)])))
    small_sum["g_ffn1_pre"], = _unpack(_sum_terms(late_terms), [dg1pre.shape])
    loss = small_sum.pop("loss").reshape(())
    me = 4 * lax.axis_index("x") + 2 * lax.axis_index("y") + lax.axis_index("c")
    cshard = LRU_W // N_DEV
    small_sum["conv_w"] = lax.dynamic_slice(small_sum["conv_w"], (0, me * cshard), (CONV_W, cshard))
    for n in _SMALL:
        grads[n] = small_sum[n].reshape(w[n].shape)
    shapes = [w[n].shape for n in _SMALL]
    d_p, m_p, v_p = _adamw_flat(_pack([w[n] for n in _SMALL]), _pack([grads[n] for n in _SMALL]),
                                _pack([mom[n] for n in _SMALL]), _pack([var[n] for n in _SMALL]))
    for n, d_, m_, v_ in zip(_SMALL, _unpack(d_p, shapes), _unpack(m_p, shapes), _unpack(v_p, shapes)):
        delta[n], new_m[n], new_v[n] = d_, m_, v_

    lead = lambda d: [d[n][None] for n in _ORDER]
    return (loss, dx[None], *lead(grads), *lead(delta), *lead(new_m), *lead(new_v))


def kernel(x, positions, g_ffn1_pre, g_ffn1_post, w_ffn1_gate, w_ffn1_up, w_ffn1_down, g_mix_pre, g_mix_post, w_in, conv_w, conv_b, w_lru_a, b_lru_a, w_lru_x, b_lru_x, lru_lambda, q_a_norm, w_q_b, kv_a_norm, w_kv_b, w_out, g_ffn2_pre, g_ffn2_post, w_ffn2_gate, w_ffn2_up, w_ffn2_down, loss_target, m_g_ffn1_pre, m_g_ffn1_post, m_w_ffn1_gate, m_w_ffn1_up, m_w_ffn1_down, m_g_mix_pre, m_g_mix_post, m_w_in, m_conv_w, m_conv_b, m_w_lru_a, m_b_lru_a, m_w_lru_x, m_b_lru_x, m_lru_lambda, m_q_a_norm, m_w_q_b, m_kv_a_norm, m_w_kv_b, m_w_out, m_g_ffn2_pre, m_g_ffn2_post, m_w_ffn2_gate, m_w_ffn2_up, m_w_ffn2_down, v_g_ffn1_pre, v_g_ffn1_post, v_w_ffn1_gate, v_w_ffn1_up, v_w_ffn1_down, v_g_mix_pre, v_g_mix_post, v_w_in, v_conv_w, v_conv_b, v_w_lru_a, v_b_lru_a, v_w_lru_x, v_b_lru_x, v_lru_lambda, v_q_a_norm, v_w_q_b, v_kv_a_norm, v_w_kv_b, v_w_out, v_g_ffn2_pre, v_g_ffn2_post, v_w_ffn2_gate, v_w_ffn2_up, v_w_ffn2_down):
    return _step(dict(locals()))
```

```python
import functools
import math
import operator

import jax
import jax.numpy as jnp
from jax import lax
from jax.experimental import pallas as pl
from jax.experimental.pallas import tpu as pltpu

F32 = jnp.float32
BF16 = jnp.bfloat16
MESH = pl.DeviceIdType.MESH

N_DEV = 8
D_MODEL = 1024
D_FF = 2816
FF_SHARD = D_FF // N_DEV
FF_PAD = 384
FF_VIRT = N_DEV * FF_PAD
FF_CHUNK = 2 * FF_PAD
LRU_W = 512
N_HEADS = 8
HEAD_PAD = 128
HEADS_W = N_HEADS * HEAD_PAD
QK_NOPE = 64
QK_ROPE = 32
V_DIM = 64
Q_LORA = 384
KV_LORA = 256
IN_COLS = 2 * LRU_W + Q_LORA + KV_LORA + QK_ROPE
IN_SHARD = IN_COLS // N_DEV
IN_SHARD_PAD = 256
IN_PAD = 1792
QB_SHARD = 96
CONV_W = 4
SCAN_PIECES = 8
CHUNK = 64
EPS = 1e-6
LRU_C = 8.0
ROPE_THETA = 10000.0
ATT_SCALE = (QK_NOPE + QK_ROPE) ** -0.5

ADAM_LR = 0.001
ADAM_B1 = 0.9
ADAM_B2 = 0.999
ADAM_EPS = 1e-08
ADAM_WD = 0.01
ADAM_STEP = 10

VMEM_LIMIT = 56 * 1024 * 1024
ANY = pl.BlockSpec(memory_space=pl.ANY)
WHOLE = pl.BlockSpec(memory_space=pltpu.VMEM)


def _params(**kw):
    return pltpu.CompilerParams(vmem_limit_bytes=VMEM_LIMIT, **kw)


def _full(shape):
    return pl.BlockSpec(shape, lambda *_: (0,) * len(shape))


def _dot(a, b):
    return jnp.dot(a, b, preferred_element_type=F32)


def _dot_nt(a, b):
    return lax.dot_general(a, b, (((1,), (1,)), ((), ())), preferred_element_type=F32)


def _dot_tn(a, b):
    return lax.dot_general(a, b, (((0,), (0,)), ((), ())), preferred_element_type=F32)


def _rms(x, g):
    r = lax.rsqrt(jnp.mean(x * x, axis=-1, keepdims=True) + EPS)
    return x * r * g


def _rms_bwd(x, g, dy):
    r = lax.rsqrt(jnp.mean(x * x, axis=-1, keepdims=True) + EPS)
    xh = x * r
    dg = jnp.sum(dy * xh, axis=0, keepdims=True)
    dxh = dy * g
    dx = r * (dxh - xh * jnp.mean(dxh * xh, axis=-1, keepdims=True))
    return dx, dg


def _sigmoid(x):
    return 0.5 * jnp.tanh(0.5 * x) + 0.5


_GELU_C = math.sqrt(2.0 / math.pi)


def _gelu(x):
    t = jnp.tanh(_GELU_C * (x + 0.044715 * x * x * x))
    return 0.5 * x * (1.0 + t)


def _gelu_grad(x):
    t = jnp.tanh(_GELU_C * (x + 0.044715 * x * x * x))
    return 0.5 * (1.0 + t) + 0.5 * x * (1.0 - t * t) * _GELU_C * (1.0 + 3.0 * 0.044715 * x * x)


def _tile_rows(t):
    return 512 if t >= 2048 else t // 2


def _dev_index(p):
    return 4 * p[0] + 2 * p[1] + p[2]


def _place():
    x, y, c = lax.axis_index("x"), lax.axis_index("y"), lax.axis_index("c")
    return (x, y, c), (x, y, 1 - c), [(1 - x, y), (x, 1 - y), (1 - x, 1 - y)]


def _dma_sems(n):
    return [pltpu.SemaphoreType.DMA((7 * n,)), pltpu.SemaphoreType.DMA((7 * n,)), pltpu.SemaphoreType.DMA((n,))]


class _Gather:
    has_middle = True

    def __init__(self, arrays):
        self.inputs = list(arrays)
        self.n = len(arrays)
        self.out_shape = [jax.ShapeDtypeStruct((N_DEV,) + a.shape, a.dtype) for a in arrays]
        self.scratch = _dma_sems(self.n)

    @staticmethod
    def _copy(outs, sems, k, s, block, to, src=None):
        rows = outs[k].at[_dev_index(block)]
        return pltpu.make_async_remote_copy(
            src_ref=rows if src is None else src, dst_ref=rows,
            send_sem=sems[0].at[7 * k + s], recv_sem=sems[1].at[7 * k + s], device_id=to, device_id_type=MESH)

    def _first(self, srcs, outs, sems):
        me, sibling, chips = _place()
        mine = [pltpu.make_async_copy(srcs[k], outs[k].at[_dev_index(me)], sems[2].at[k]) for k in range(self.n)]
        sends = []
        for k in range(self.n):
            sends.append(self._copy(outs, sems, k, 0, me, sibling, src=srcs[k]))
            sends += [self._copy(outs, sems, k, 1 + j, me, (*chip, me[2]), src=srcs[k])
                      for j, chip in enumerate(chips)]
        return mine, sends

    def _passed(self, outs, sems):
        me, sibling, chips = _place()
        return [[self._copy(outs, sems, k, 4 + j, (*chip, me[2]), sibling) for k in range(self.n)]
                for j, chip in enumerate(chips)]

    def start(self, srcs, outs, sems):
        mine, sends = self._first(srcs, outs, sems)
        for cp in mine + sends:
            cp.start()

    def middle(self, srcs, outs, sems):
        me, sibling, chips = _place()
        passed = self._passed(outs, sems)
        for j, chip in enumerate(chips):
            for k in range(self.n):
                self._copy(outs, sems, k, 1 + j, (*chip, me[2]), me).wait_recv()
                passed[j][k].start()

    def finish(self, srcs, outs, sems):
        me, sibling, chips = _place()
        for k in range(self.n):
            self._copy(outs, sems, k, 0, sibling, me).wait_recv()
        for j, chip in enumerate(chips):
            for k in range(self.n):
                self._copy(outs, sems, k, 4 + j, (*chip, 1 - me[2]), me).wait_recv()
        mine, sends = self._first(srcs, outs, sems)
        for cp in sends + [cp for row in self._passed(outs, sems) for cp in row]:
            cp.wait_send()
        for cp in mine:
            cp.wait()


class _Exchange:
    has_middle = False

    def __init__(self, arrays):
        self.inputs = list(arrays)
        self.n = len(arrays)
        self.out_shape = [jax.ShapeDtypeStruct(a.shape, a.dtype) for a in arrays]
        self.scratch = _dma_sems(self.n)

    def _copies(self, srcs, outs, sems):
        (x, y, c), _, _ = _place()
        me = _dev_index((x, y, c))
        local = [pltpu.make_async_copy(srcs[k].at[me], outs[k].at[me], sems[2].at[k]) for k in range(self.n)]
        remote = []
        for s in range(1, N_DEV):
            peer = (1 - x if s & 4 else x, 1 - y if s & 2 else y, 1 - c if s & 1 else c)
            for k in range(self.n):
                remote.append(pltpu.make_async_remote_copy(
                    src_ref=srcs[k].at[_dev_index(peer)], dst_ref=outs[k].at[me],
                    send_sem=sems[0].at[7 * k + s - 1], recv_sem=sems[1].at[7 * k + s - 1],
                    device_id=peer, device_id_type=MESH))
        return local, remote

    def start(self, srcs, outs, sems):
        local, remote = self._copies(srcs, outs, sems)
        for cp in local + remote:
            cp.start()

    def finish(self, srcs, outs, sems):
        local, remote = self._copies(srcs, outs, sems)
        for cp in remote:
            cp.wait_recv()
        for cp in remote:
            cp.wait_send()
        for cp in local:
            cp.wait()


class _PairSwap:
    has_middle = False

    def __init__(self, arrays):
        self.inputs = list(arrays)
        self.n = len(arrays)
        self.out_shape = [jax.ShapeDtypeStruct((4,) + a.shape[1:], a.dtype) for a in arrays]
        self.scratch = _dma_sems(self.n)

    def _copies(self, srcs, outs, sems):
        (x, y, c), sibling, _ = _place()
        return [pltpu.make_async_remote_copy(
            src_ref=srcs[k].at[2 * q + 1 - c], dst_ref=outs[k].at[q],
            send_sem=sems[0].at[7 * k + q], recv_sem=sems[1].at[7 * k + q], device_id=sibling, device_id_type=MESH)
            for q in range(4) for k in range(self.n)]

    def start(self, srcs, outs, sems):
        for cp in self._copies(srcs, outs, sems):
            cp.start()

    def finish(self, srcs, outs, sems):
        copies = self._copies(srcs, outs, sems)
        for cp in copies:
            cp.wait_recv()
        for cp in copies:
            cp.wait_send()


class _ChipExchange:
    has_middle = False

    def __init__(self, arrays):
        self.inputs = list(arrays)
        self.n = len(arrays)
        self.out_shape = [jax.ShapeDtypeStruct(a.shape, a.dtype) for a in arrays]
        self.scratch = _dma_sems(self.n)

    def _copies(self, srcs, outs, sems):
        (x, y, c), _, _ = _place()
        mine = 2 * x + y
        local = [pltpu.make_async_copy(srcs[k].at[mine], outs[k].at[mine], sems[2].at[k]) for k in range(self.n)]
        remote = []
        for s in range(1, 4):
            px, py = (1 - x if s & 2 else x), (1 - y if s & 1 else y)
            for k in range(self.n):
                remote.append(pltpu.make_async_remote_copy(
                    src_ref=srcs[k].at[2 * px + py], dst_ref=outs[k].at[mine],
                    send_sem=sems[0].at[7 * k + s], recv_sem=sems[1].at[7 * k + s],
                    device_id=(px, py, c), device_id_type=MESH))
        return local, remote

    start = _Exchange.start
    finish = _Exchange.finish


class _Several:
    def __init__(self, comms):
        self.comms = comms
        self.inputs = [a for cm in comms for a in cm.inputs]
        self.n = len(self.inputs)
        self.out_shape = [s for cm in comms for s in cm.out_shape]
        self.scratch = [s for cm in comms for s in cm.scratch]
        self.has_middle = any(cm.has_middle for cm in comms)

    def _each(self, method, srcs, outs, sems):
        pos = 0
        for i, cm in enumerate(self.comms):
            if hasattr(cm, method):
                getattr(cm, method)(srcs[pos:pos + cm.n], outs[pos:pos + cm.n], sems[3 * i:3 * i + 3])
            pos += cm.n

    def start(self, srcs, outs, sems):
        self._each("start", srcs, outs, sems)

    def middle(self, srcs, outs, sems):
        self._each("middle", srcs, outs, sems)

    def finish(self, srcs, outs, sems):
        self._each("finish", srcs, outs, sems)


def _call(body, comm, *, name, grid, in_specs, out_specs, out_shape, scratch_shapes=(), args):
    in_specs, out_specs, out_shape = list(in_specs), list(out_specs), list(out_shape)
    scratch_shapes = list(scratch_shapes)
    if comm is None:
        outs = pl.pallas_call(body, name=name, grid=grid, in_specs=in_specs, out_specs=out_specs, out_shape=out_shape,
                              scratch_shapes=scratch_shapes, compiler_params=_params())(*args)
        return list(outs), []
    n_in, n_out, n_scr, c_n = len(in_specs), len(out_specs), len(scratch_shapes), comm.n
    middle = tuple(g - 1 if d == 0 else 0 for d, g in enumerate(grid))

    def hosted(*refs):
        pos = 0
        parts = []
        for width in (n_in, c_n, n_out, c_n, n_scr, len(comm.scratch)):
            parts.append(refs[pos:pos + width])
            pos += width
        ins, c_in, outs, c_out, scr, sems = parts
        ids = [pl.program_id(d) for d in range(len(grid))]
        at = lambda where: functools.reduce(operator.and_, [i == w for i, w in zip(ids, where)])

        @pl.when(at([0] * len(grid)))
        def _():
            comm.start(c_in, c_out, sems)

        body(*ins, *outs, *scr)

        if comm.has_middle:
            @pl.when(at(middle))
            def _():
                comm.middle(c_in, c_out, sems)

        @pl.when(at([g - 1 for g in grid]))
        def _():
            comm.finish(c_in, c_out, sems)

    outs = pl.pallas_call(
        hosted, name=name, grid=grid,
        in_specs=in_specs + [ANY] * c_n, out_specs=out_specs + [ANY] * c_n,
        out_shape=out_shape + comm.out_shape, scratch_shapes=scratch_shapes + comm.scratch,
        compiler_params=_params())(*args, *comm.inputs)
    return list(outs[:n_out]), list(outs[n_out:])


def _first_gather(shards, pads, dtypes, n_gathered):
    n = len(shards)
    gather = _Gather([jax.ShapeDtypeStruct(tuple(pads[k]), dtypes[k]) for k in range(n_gathered)])

    def body(*refs):
        ins, outs = refs[:n], refs[n:2 * n]
        stages = list(refs[2 * n:2 * n + n_gathered]) + list(outs[n_gathered:])
        sems = refs[2 * n + n_gathered:]
        for k in range(n):
            r, cc = ins[k].shape
            if (r, cc) != tuple(stages[k].shape):
                stages[k][...] = jnp.zeros(stages[k].shape, stages[k].dtype)
            stages[k][:r, :cc] = ins[k][...].astype(stages[k].dtype)
        gather.start(stages[:n_gathered], outs[:n_gathered], sems)
        gather.middle(stages[:n_gathered], outs[:n_gathered], sems)
        gather.finish(stages[:n_gathered], outs[:n_gathered], sems)

    staged_shape = [jax.ShapeDtypeStruct(tuple(pads[k]), dtypes[k]) for k in range(n_gathered, n)]
    outs = pl.pallas_call(
        body, name="first_gather",
        out_shape=gather.out_shape + staged_shape,
        in_specs=[WHOLE] * n,
        out_specs=[ANY] * n_gathered + [WHOLE] * (n - n_gathered),
        scratch_shapes=[pltpu.VMEM(tuple(pads[k]), dtypes[k]) for k in range(n_gathered)] + gather.scratch,
        compiler_params=_params(),
    )(*shards)
    return list(outs[:n_gathered]), list(outs[n_gathered:])


def _last_exchange(comm):
    def body(*refs):
        srcs, outs, sems = refs[:comm.n], refs[comm.n:2 * comm.n], refs[2 * comm.n:]
        comm.start(srcs, outs, sems)
        comm.finish(srcs, outs, sems)

    return pl.pallas_call(
        body, name="last_exchange", out_shape=comm.out_shape, in_specs=[ANY] * comm.n, out_specs=[ANY] * comm.n,
        scratch_shapes=comm.scratch, compiler_params=_params())(*comm.inputs)


def _pair_sum(blocks, got):
    _, r, c = blocks.shape
    side = lax.axis_index("c").astype(jnp.int32).reshape(1)

    def body(side_ref, a_ref, b_ref, out_ref):
        out_ref[...] = (a_ref[...].astype(F32) + b_ref[...].astype(F32)).astype(out_ref.dtype)

    spec = pl.BlockSpec((None, r, c), lambda q, side_ref: (q, 0, 0))
    return pl.pallas_call(
        body, name="pair_sum",
        grid_spec=pltpu.PrefetchScalarGridSpec(
            num_scalar_prefetch=1, grid=(4,),
            in_specs=[pl.BlockSpec((None, r, c), lambda q, side_ref: (2 * q + side_ref[0], 0, 0)), spec],
            out_specs=spec),
        out_shape=jax.ShapeDtypeStruct((4, r, c), blocks.dtype), compiler_params=_params())(side, blocks, got)


def _ffn_fwd_loss(h0, target, g_pre, g_post, wg, wu, wd, name):
    t = h0.shape[0]
    tt = _tile_rows(t)
    n_chunks = FF_VIRT // FF_CHUNK

    def body(h0_ref, tgt_ref, gpre_ref, gpost_ref, wg_ref, wu_ref, wd_ref,
             dy_ref, f_ref, n1_ref, gt_ref, up_ref, loss_ref, acc_ref, n1s_ref):
        i, j = pl.program_id(0), pl.program_id(1)

        @pl.when((i == 0) & (j == 0))
        def _():
            loss_ref[...] = jnp.zeros_like(loss_ref)

        @pl.when(j == 0)
        def _():
            n1 = _rms(h0_ref[...], gpre_ref[...]).astype(BF16)
            n1s_ref[...] = n1
            n1_ref[...] = n1
            acc_ref[...] = jnp.zeros_like(acc_ref)

        n1 = n1s_ref[...]
        gt = _dot_nt(n1, wg_ref[...])
        up = _dot_nt(n1, wu_ref[...])
        gt_ref[...] = gt.astype(BF16)
        up_ref[...] = up.astype(BF16)
        act = (gt * _sigmoid(gt) * up).astype(BF16)
        acc_ref[...] += _dot(act, wd_ref[...])

        @pl.when(j == n_chunks - 1)
        def _():
            f = acc_ref[...]
            f_ref[...] = f
            e = h0_ref[...] + 0.5 * _rms(f, gpost_ref[...]) - tgt_ref[...]
            dy_ref[...] = e * (1.0 / D_MODEL)
            loss_ref[...] += 0.5 * jnp.sum(jnp.mean(e * e, axis=-1, keepdims=True), axis=0, keepdims=True)

    row = lambda i, j: (i, 0)
    wspec = pl.BlockSpec((FF_CHUNK, D_MODEL), lambda i, j: (j, 0))
    outs, _ = _call(
        body, None, name=name, grid=(t // tt, n_chunks),
        in_specs=[pl.BlockSpec((tt, D_MODEL), row), pl.BlockSpec((tt, D_MODEL), row), _full((1, D_MODEL)),
                  _full((1, D_MODEL)), wspec, wspec, wspec],
        out_specs=[pl.BlockSpec((tt, D_MODEL), row), pl.BlockSpec((tt, D_MODEL), row),
                   pl.BlockSpec((tt, D_MODEL), row),
                   pl.BlockSpec((tt, FF_CHUNK), lambda i, j: (i, j)),
                   pl.BlockSpec((tt, FF_CHUNK), lambda i, j: (i, j)), _full((8, 128))],
        out_shape=[jax.ShapeDtypeStruct((t, D_MODEL), F32), jax.ShapeDtypeStruct((t, D_MODEL), F32),
                   jax.ShapeDtypeStruct((t, D_MODEL), BF16),
                   jax.ShapeDtypeStruct((t, FF_VIRT), BF16), jax.ShapeDtypeStruct((t, FF_VIRT), BF16),
                   jax.ShapeDtypeStruct((8, 128), F32)],
        scratch_shapes=[pltpu.VMEM((tt, D_MODEL), F32), pltpu.VMEM((tt, D_MODEL), BF16)],
        args=(h0, target, g_pre, g_post, wg, wu, wd))
    return outs


def _ffn_fwd_gate(h0, g_pre, wg, comm, name):
    t = h0.shape[0]
    tt = 2 * _tile_rows(t)

    def body(h0_ref, gpre_ref, wg_ref, n1_ref, gt_ref):
        @pl.when(pl.program_id(1) == 0)
        def _():
            n1_ref[...] = _rms(h0_ref[...], gpre_ref[...]).astype(BF16)

        gt_ref[...] = _dot_nt(n1_ref[...], wg_ref[...]).astype(BF16)

    row = lambda i, j: (i, 0)
    return _call(
        body, comm, name=name, grid=(t // tt, FF_VIRT // FF_CHUNK),
        in_specs=[pl.BlockSpec((tt, D_MODEL), row), _full((1, D_MODEL)),
                  pl.BlockSpec((FF_CHUNK, D_MODEL), lambda i, j: (j, 0))],
        out_specs=[pl.BlockSpec((tt, D_MODEL), row), pl.BlockSpec((tt, FF_CHUNK), lambda i, j: (i, j))],
        out_shape=[jax.ShapeDtypeStruct((t, D_MODEL), BF16), jax.ShapeDtypeStruct((t, FF_VIRT), BF16)],
        args=(h0, g_pre, wg))


def _ffn_fwd_up(n1, wu, comm, name):
    t = n1.shape[0]
    tt = 2 * _tile_rows(t)

    def body(n1_ref, wu_ref, up_ref):
        up_ref[...] = _dot_nt(n1_ref[...], wu_ref[...]).astype(BF16)

    return _call(
        body, comm, name=name, grid=(t // tt, FF_VIRT // FF_CHUNK),
        in_specs=[pl.BlockSpec((tt, D_MODEL), lambda i, j: (i, 0)),
                  pl.BlockSpec((FF_CHUNK, D_MODEL), lambda i, j: (j, 0))],
        out_specs=[pl.BlockSpec((tt, FF_CHUNK), lambda i, j: (i, j))],
        out_shape=[jax.ShapeDtypeStruct((t, FF_VIRT), BF16)],
        args=(n1, wu))


def _ffn_fwd_out(h0, gt, up, g_post, wd, comm, name):
    t = h0.shape[0]
    tt = 2 * _tile_rows(t)
    n_chunks = FF_VIRT // FF_CHUNK

    def body(h0_ref, gt_ref, up_ref, gpost_ref, wd_ref, h1_ref, f_ref):
        j = pl.program_id(1)
        g = gt_ref[...].astype(F32)
        term = _dot((g * _sigmoid(g) * up_ref[...].astype(F32)).astype(BF16), wd_ref[...])

        @pl.when(j == 0)
        def _():
            f_ref[...] = term

        @pl.when(j > 0)
        def _():
            f_ref[...] += term

        @pl.when(j == n_chunks - 1)
        def _():
            h1_ref[...] = h0_ref[...] + 0.5 * _rms(f_ref[...], gpost_ref[...])

    row = lambda i, j: (i, 0)
    chunk = pl.BlockSpec((tt, FF_CHUNK), lambda i, j: (i, j))
    return _call(
        body, comm, name=name, grid=(t // tt, n_chunks),
        in_specs=[pl.BlockSpec((tt, D_MODEL), row), chunk, chunk, _full((1, D_MODEL)),
                  pl.BlockSpec((FF_CHUNK, D_MODEL), lambda i, j: (j, 0))],
        out_specs=[pl.BlockSpec((tt, D_MODEL), row), pl.BlockSpec((tt, D_MODEL), row)],
        out_shape=[jax.ShapeDtypeStruct((t, D_MODEL), F32), jax.ShapeDtypeStruct((t, D_MODEL), F32)],
        args=(h0, gt, up, g_post, wd))


def _ffn_bwd_hidden(dh1, f, gt, up, g_post, wd, comm, name):
    t = f.shape[0]
    tt = _tile_rows(t)
    n_chunks = FF_VIRT // FF_CHUNK

    def body(dh1_ref, f_ref, gt_ref, up_ref, gpost_ref, wd_ref,
             df_ref, act_ref, dgt_ref, dup_ref, dgpost_ref, dfs_ref):
        i, j = pl.program_id(0), pl.program_id(1)

        @pl.when((i == 0) & (j == 0))
        def _():
            dgpost_ref[...] = jnp.zeros_like(dgpost_ref)

        @pl.when(j == 0)
        def _():
            df, dg = _rms_bwd(f_ref[...], gpost_ref[...], 0.5 * dh1_ref[...])
            dgpost_ref[...] += dg
            dfb = df.astype(BF16)
            dfs_ref[...] = dfb
            df_ref[...] = dfb

        da = _dot_nt(dfs_ref[...], wd_ref[...])
        g = gt_ref[...].astype(F32)
        u = up_ref[...].astype(F32)
        s = _sigmoid(g)
        sl = g * s
        act_ref[...] = (sl * u).astype(BF16)
        dgt_ref[...] = (da * u * (s * (1.0 + g * (1.0 - s)))).astype(BF16)
        dup_ref[...] = (da * sl).astype(BF16)

    row = lambda i, j: (i, 0)
    chunk = pl.BlockSpec((tt, FF_CHUNK), lambda i, j: (i, j))
    return _call(
        body, comm, name=name, grid=(t // tt, n_chunks),
        in_specs=[pl.BlockSpec((tt, D_MODEL), row), pl.BlockSpec((tt, D_MODEL), row), chunk, chunk,
                  _full((1, D_MODEL)), pl.BlockSpec((FF_CHUNK, D_MODEL), lambda i, j: (j, 0))],
        out_specs=[pl.BlockSpec((tt, D_MODEL), row), chunk, chunk, chunk, _full((1, D_MODEL))],
        out_shape=[jax.ShapeDtypeStruct((t, D_MODEL), BF16)] + [jax.ShapeDtypeStruct((t, FF_VIRT), BF16)] * 3 + [
            jax.ShapeDtypeStruct((1, D_MODEL), F32)],
        scratch_shapes=[pltpu.VMEM((tt, D_MODEL), BF16)],
        args=(dh1, f, gt, up, g_post, wd))


def _ffn_bwd_input(dh1, h0, dgt, dup, g_pre, wg, wu, comm, name):
    t = h0.shape[0]
    tt = 2 * _tile_rows(t)
    n_chunks = FF_VIRT // FF_CHUNK

    def body(dh1_ref, h0_ref, dgt_ref, dup_ref, gpre_ref, wg_ref, wu_ref, dh0_ref, dgpre_ref, acc_ref):
        i, j = pl.program_id(0), pl.program_id(1)

        @pl.when((i == 0) & (j == 0))
        def _():
            dgpre_ref[...] = jnp.zeros_like(dgpre_ref)

        @pl.when(j == 0)
        def _():
            acc_ref[...] = jnp.zeros_like(acc_ref)

        acc_ref[...] += _dot(dgt_ref[...], wg_ref[...]) + _dot(dup_ref[...], wu_ref[...])

        @pl.when(j == n_chunks - 1)
        def _():
            dx, dg = _rms_bwd(h0_ref[...], gpre_ref[...], acc_ref[...])
            dgpre_ref[...] += dg
            dh0_ref[...] = dh1_ref[...] + dx

    row = lambda i, j: (i, 0)
    chunk = pl.BlockSpec((tt, FF_CHUNK), lambda i, j: (i, j))
    wspec = pl.BlockSpec((FF_CHUNK, D_MODEL), lambda i, j: (j, 0))
    return _call(
        body, comm, name=name, grid=(t // tt, n_chunks),
        in_specs=[pl.BlockSpec((tt, D_MODEL), row), pl.BlockSpec((tt, D_MODEL), row), chunk, chunk,
                  _full((1, D_MODEL)), wspec, wspec],
        out_specs=[pl.BlockSpec((tt, D_MODEL), row), _full((1, D_MODEL))],
        out_shape=[jax.ShapeDtypeStruct((t, D_MODEL), F32), jax.ShapeDtypeStruct((1, D_MODEL), F32)],
        scratch_shapes=[pltpu.VMEM((tt, D_MODEL), F32)],
        args=(dh1, h0, dgt, dup, g_pre, wg, wu))


def _mm_tn(a, b, m, tm, tn, name, comm=None):
    t, n = b.shape
    tk = min(t, 2048)
    nk = t // tk

    def body(a_ref, b_ref, out_ref, acc_ref):
        k = pl.program_id(2)

        @pl.when(k == 0)
        def _():
            acc_ref[...] = jnp.zeros_like(acc_ref)

        acc_ref[...] += _dot_tn(a_ref[...], b_ref[...])

        @pl.when(k == nk - 1)
        def _():
            out_ref[...] = acc_ref[...].astype(out_ref.dtype)

    outs, got = _call(
        body, comm, name=name, grid=(m // tm, n // tn, nk),
        in_specs=[pl.BlockSpec((tk, tm), lambda i, j, k: (k, i)), pl.BlockSpec((tk, tn), lambda i, j, k: (k, j))],
        out_specs=[pl.BlockSpec((tm, tn), lambda i, j, k: (i, j))],
        out_shape=[jax.ShapeDtypeStruct((m, n), BF16)],
        scratch_shapes=[pltpu.VMEM((tm, tn), F32)], args=(a, b))
    return outs[0], got


_SPLITS = (0, LRU_W, 2 * LRU_W, 2 * LRU_W + Q_LORA, 2 * LRU_W + Q_LORA + KV_LORA, IN_PAD)
_WIDTHS = tuple(_SPLITS[k + 1] - _SPLITS[k] for k in range(5))


def _mixer_in(h1, g_pre, w_in):
    t = h1.shape[0]
    tt = _tile_rows(t)

    def body(h_ref, g_ref, w_ref, n_ref, *outs):
        n = _rms(h_ref[...], g_ref[...]).astype(BF16)
        n_ref[...] = n
        proj = _dot_nt(n, w_ref[...])
        for k in range(5):
            outs[k][...] = proj[:, _SPLITS[k]:_SPLITS[k + 1]]

    row = lambda i: (i, 0)
    return pl.pallas_call(
        body, name="mixer_in",
        grid=(t // tt,),
        in_specs=[pl.BlockSpec((tt, D_MODEL), row), _full((1, D_MODEL)), _full((IN_PAD, D_MODEL))],
        out_specs=[pl.BlockSpec((tt, D_MODEL), row)] + [pl.BlockSpec((tt, w), row) for w in _WIDTHS],
        out_shape=[jax.ShapeDtypeStruct((t, D_MODEL), BF16)] + [jax.ShapeDtypeStruct((t, w), F32) for w in _WIDTHS],
        compiler_params=_params(),
    )(h1, g_pre, w_in)


def _mixer_in_bwd(dh2, h1, g_pre, w_in, dparts):
    t = h1.shape[0]
    tt = _tile_rows(t)

    def body(dh2_ref, h_ref, g_ref, w_ref, p0, p1, p2, p3, p4, dh1_ref, dproj_ref, dg_ref):
        i = pl.program_id(0)

        @pl.when(i == 0)
        def _():
            dg_ref[...] = jnp.zeros_like(dg_ref)

        dproj = jnp.concatenate([p[...] for p in (p0, p1, p2, p3, p4)], axis=1)
        dproj_ref[...] = dproj
        dx, dg = _rms_bwd(h_ref[...], g_ref[...], _dot(dproj, w_ref[...]))
        dg_ref[...] += dg
        dh1_ref[...] = dh2_ref[...] + dx

    row = lambda i: (i, 0)
    return pl.pallas_call(
        body, name="mixer_in_bwd",
        grid=(t // tt,),
        in_specs=[pl.BlockSpec((tt, D_MODEL), row), pl.BlockSpec((tt, D_MODEL), row), _full((1, D_MODEL)),
                  _full((IN_PAD, D_MODEL))] + [pl.BlockSpec((tt, w), row) for w in _WIDTHS],
        out_specs=[pl.BlockSpec((tt, D_MODEL), row), pl.BlockSpec((tt, IN_PAD), row), _full((1, D_MODEL))],
        out_shape=[jax.ShapeDtypeStruct((t, D_MODEL), F32), jax.ShapeDtypeStruct((t, IN_PAD), BF16),
                   jax.ShapeDtypeStruct((1, D_MODEL), F32)],
        compiler_params=_params(),
    )(dh2, h1, g_pre, w_in, *dparts)


def _conv(x_prev8, x_tile, w_ref, b_ref):
    tt = x_tile.shape[0]
    xx = jnp.concatenate([x_prev8, x_tile], axis=0)
    y = b_ref[...] + w_ref[CONV_W - 1:CONV_W, :] * x_tile
    for k in range(CONV_W - 1):
        y = y + w_ref[k:k + 1, :] * pltpu.roll(xx, CONV_W - 1 - k, 0)[8:8 + tt]
    return y


def _neg_expm1(z):
    series = -z * (1.0 + z * 0.5 * (1.0 + z / 3.0 * (1.0 + z * 0.25 * (1.0 + z * 0.2 * (1.0 + z / 6.0)))))
    return jnp.where(z > -0.3, series, 1.0 - jnp.exp(z))


def _lru_elementwise(xc, pre_a, pre_i, lam):
    ra = _sigmoid(pre_a)
    ri = _sigmoid(pre_i)
    neg = -lam
    softplus = jnp.maximum(neg, 0.0) + jnp.log(1.0 + jnp.exp(-jnp.abs(neg)))
    log_a = -LRU_C * ra * softplus
    a = jnp.exp(log_a)
    mult = jnp.sqrt(_neg_expm1(2.0 * log_a))
    return a, mult * (ri * xc)


def _lru_gates(xl, conv_w, conv_b, wa, wx, ba, bx, lam):
    t = xl.shape[0]
    tt = _tile_rows(t)
    r8 = tt // 8

    def body(prev_ref, x_ref, cw_ref, cb_ref, wa_ref, wx_ref, ba_ref, bx_ref, lam_ref, xc_ref, a_ref, u_ref):
        i = pl.program_id(0)
        prev = jnp.where(i == 0, 0.0, prev_ref[...])
        xc = _conv(prev, x_ref[...], cw_ref, cb_ref)
        xc_ref[...] = xc
        xb = xc.astype(BF16)
        a, u = _lru_elementwise(xc, _dot(xb, wa_ref[...]) + ba_ref[...], _dot(xb, wx_ref[...]) + bx_ref[...],
                                lam_ref[...])
        a_ref[...] = a
        u_ref[...] = u

    row = lambda i: (i, 0)
    vec = _full((1, LRU_W))
    return pl.pallas_call(
        body, name="lru_gates",
        grid=(t // tt,),
        in_specs=[pl.BlockSpec((8, LRU_W), lambda i: (jnp.maximum(i * r8 - 1, 0), 0)),
                  pl.BlockSpec((tt, LRU_W), row), _full((8, LRU_W)), vec,
                  _full((LRU_W, LRU_W)), _full((LRU_W, LRU_W)), vec, vec, vec],
        out_specs=[pl.BlockSpec((tt, LRU_W), row)] * 3,
        out_shape=[jax.ShapeDtypeStruct((t, LRU_W), F32)] * 3,
        compiler_params=_params(),
    )(xl, xl, conv_w, conv_b, wa, wx, ba, bx, lam)


def _lru_scan(a, u, reverse):
    t = a.shape[0]
    pieces = SCAN_PIECES
    rows = t // pieces
    nblk = rows // 8

    def body(a_ref, u_ref, h_ref, carry_ref):
        @pl.when(pl.program_id(0) == 0)
        def _():
            carry_ref[...] = jnp.zeros_like(carry_ref)

        def fwd(blk, h):
            base = pl.multiple_of(blk * 8, 8)
            for k in range(8):
                h = a_ref[pl.ds(base + k, 1), :] * h + u_ref[pl.ds(base + k, 1), :]
                h_ref[pl.ds(base + k, 1), :] = h
            return h

        def bwd(n, carry):
            base = pl.multiple_of((nblk - 1 - n) * 8, 8)
            for k in range(7, -1, -1):
                g = u_ref[pl.ds(base + k, 1), :] + carry
                h_ref[pl.ds(base + k, 1), :] = g
                carry = a_ref[pl.ds(base + k, 1), :] * g
            return carry

        carry_ref[...] = lax.fori_loop(0, nblk, bwd if reverse else fwd, carry_ref[...])

    piece = pl.BlockSpec((rows, LRU_W), (lambda i: (pieces - 1 - i, 0)) if reverse else (lambda i: (i, 0)))
    return pl.pallas_call(
        body, name="lru_scan_rev" if reverse else "lru_scan", grid=(pieces,),
        in_specs=[piece, piece], out_specs=piece,
        out_shape=jax.ShapeDtypeStruct((t, LRU_W), F32),
        scratch_shapes=[pltpu.VMEM((1, LRU_W), F32)],
        compiler_params=_params(),
    )(a, u)


def _lru_gates_bwd(dh, h, xc, wa, wx, ba, bx, lam):
    t = dh.shape[0]
    tt = _tile_rows(t)
    r8 = tt // 8

    def body(dh_ref, hprev_ref, h_ref, xc_ref, wa_ref, wx_ref, ba_ref, bx_ref, lam_ref,
             dxc_ref, dwa_ref, dwx_ref, dba_ref, dbx_ref, dlam_ref):
        i = pl.program_id(0)

        @pl.when(i == 0)
        def _():
            for r in (dwa_ref, dwx_ref, dba_ref, dbx_ref, dlam_ref):
                r[...] = jnp.zeros_like(r)

        prev = jnp.where(i == 0, 0.0, hprev_ref[...])
        h_before = pltpu.roll(jnp.concatenate([prev, h_ref[...]], axis=0), 1, 0)[8:8 + tt]
        dh_t = dh_ref[...]
        xc = xc_ref[...]
        xb = xc.astype(BF16)
        pre_a = _dot(xb, wa_ref[...]) + ba_ref[...]
        pre_i = _dot(xb, wx_ref[...]) + bx_ref[...]
        _, vjp = jax.vjp(_lru_elementwise, xc, pre_a, pre_i, lam_ref[...])
        dxc, dpre_a, dpre_i, dlam = vjp((dh_t * h_before, dh_t))
        da_b = dpre_a.astype(BF16)
        di_b = dpre_i.astype(BF16)
        dxc_ref[...] = dxc + _dot_nt(da_b, wa_ref[...]) + _dot_nt(di_b, wx_ref[...])
        dwa_ref[...] += _dot_tn(xb, da_b)
        dwx_ref[...] += _dot_tn(xb, di_b)
        dba_ref[...] += jnp.sum(dpre_a, axis=0, keepdims=True)
        dbx_ref[...] += jnp.sum(dpre_i, axis=0, keepdims=True)
        dlam_ref[...] += dlam

    row = lambda i: (i, 0)
    vec = _full((1, LRU_W))
    sq = _full((LRU_W, LRU_W))
    return pl.pallas_call(
        body, name="lru_gates_bwd",
        grid=(t // tt,),
        in_specs=[pl.BlockSpec((tt, LRU_W), row),
                  pl.BlockSpec((8, LRU_W), lambda i: (jnp.maximum(i * r8 - 1, 0), 0)),
                  pl.BlockSpec((tt, LRU_W), row), pl.BlockSpec((tt, LRU_W), row), sq, sq, vec, vec, vec],
        out_specs=[pl.BlockSpec((tt, LRU_W), row), sq, sq, vec, vec, vec],
        out_shape=[jax.ShapeDtypeStruct((t, LRU_W), F32), jax.ShapeDtypeStruct((LRU_W, LRU_W), F32),
                   jax.ShapeDtypeStruct((LRU_W, LRU_W), F32)] + [jax.ShapeDtypeStruct((1, LRU_W), F32)] * 3,
        compiler_params=_params(),
    )(dh, h, h, xc, wa, wx, ba, bx, lam)


def _conv_bwd(dxc, xl, conv_w):
    t = dxc.shape[0]
    tt = _tile_rows(t)
    r8 = tt // 8
    n_tiles = t // tt
    last = t // 8 - 1

    def body(d_ref, dnext_ref, xprev_ref, x_ref, cw_ref, dxl_ref, dw_ref, db_ref):
        i = pl.program_id(0)

        @pl.when(i == 0)
        def _():
            dw_ref[...] = jnp.zeros_like(dw_ref)
            db_ref[...] = jnp.zeros_like(db_ref)

        d = d_ref[...]
        nxt = jnp.where(i == n_tiles - 1, 0.0, dnext_ref[...])
        dd = jnp.concatenate([d, nxt], axis=0)
        dx = cw_ref[CONV_W - 1:CONV_W, :] * d
        for k in range(CONV_W - 1):
            shift = CONV_W - 1 - k
            dx = dx + cw_ref[k:k + 1, :] * pltpu.roll(dd, tt + 8 - shift, 0)[:tt]
        dxl_ref[...] = dx.astype(BF16)

        prev = jnp.where(i == 0, 0.0, xprev_ref[...])
        xx = jnp.concatenate([prev, x_ref[...]], axis=0)
        rows = []
        for k in range(CONV_W):
            shifted = x_ref[...] if k == CONV_W - 1 else pltpu.roll(xx, CONV_W - 1 - k, 0)[8:8 + tt]
            rows.append(jnp.sum(d * shifted, axis=0, keepdims=True))
        rows.append(jnp.zeros((8 - CONV_W, LRU_W), F32))
        dw_ref[...] += jnp.concatenate(rows, axis=0)
        db_ref[...] += jnp.sum(d, axis=0, keepdims=True)

    row = lambda i: (i, 0)
    return pl.pallas_call(
        body, name="conv_bwd",
        grid=(n_tiles,),
        in_specs=[pl.BlockSpec((tt, LRU_W), row),
                  pl.BlockSpec((8, LRU_W), lambda i: (jnp.minimum((i + 1) * r8, last), 0)),
                  pl.BlockSpec((8, LRU_W), lambda i: (jnp.maximum(i * r8 - 1, 0), 0)),
                  pl.BlockSpec((tt, LRU_W), row), _full((8, LRU_W))],
        out_specs=[pl.BlockSpec((tt, LRU_W), row), _full((8, LRU_W)), _full((1, LRU_W))],
        out_shape=[jax.ShapeDtypeStruct((t, LRU_W), BF16), jax.ShapeDtypeStruct((8, LRU_W), F32),
                   jax.ShapeDtypeStruct((1, LRU_W), F32)],
        compiler_params=_params(),
    )(dxc, dxc, xl, xl, conv_w)


def _rope(x, c, s1, s2):
    n = x.shape[-1]
    return x * c + pltpu.roll(x, n - 16, 1) * s1 + pltpu.roll(x, 16, 1) * s2


def _rope_t(d, c, s1, s2):
    n = d.shape[-1]
    return d * c + pltpu.roll(d * s1, 16, 1) + pltpu.roll(d * s2, n - 16, 1)


def _mla_proj(qlat, kvlat, krope, tables, gq, gkv, wq, wk, wv):
    t = qlat.shape[0]
    tt = _tile_rows(t)

    def body(q_ref, kv_ref, kr_ref, cq, s1q, s2q, ck, s1k, s2k, gq_ref, gkv_ref, wq_ref, wk_ref, wv_ref,
             qo_ref, ko_ref, vo_ref, qn_ref, kva_ref, vt_ref):
        qn = _rms(q_ref[...], gq_ref[...]).astype(BF16)
        kvn = _rms(kv_ref[...], gkv_ref[...]).astype(BF16)
        kr = _rope(kr_ref[...], ck[...], s1k[...], s2k[...]).astype(BF16)
        kva = jnp.concatenate([kvn, kr], axis=1)
        qn_ref[...] = qn
        kva_ref[...] = kva
        q = _dot_nt(qn, wq_ref[...])
        tile = lambda r: jnp.tile(r[...], (1, N_HEADS))
        qo_ref[...] = _rope(q, tile(cq), tile(s1q), tile(s2q)).astype(BF16)
        ko_ref[...] = _dot(kva, wk_ref[...]).astype(BF16)
        v = _dot(kvn, wv_ref[...])
        vo_ref[...] = v.astype(BF16)
        vt_ref[...] = jnp.transpose(v).astype(BF16)

    row = lambda i: (i, 0)
    tab = pl.BlockSpec((tt, HEAD_PAD), row)
    return pl.pallas_call(
        body, name="mla_proj",
        grid=(t // tt,),
        in_specs=[pl.BlockSpec((tt, Q_LORA), row), pl.BlockSpec((tt, KV_LORA), row), tab] + [tab] * 6 + [
            _full((1, Q_LORA)), _full((1, KV_LORA)), _full((HEADS_W, Q_LORA)), _full((KV_LORA + HEAD_PAD, HEADS_W)),
            _full((KV_LORA, HEADS_W))],
        out_specs=[pl.BlockSpec((tt, HEADS_W), row)] * 3 + [pl.BlockSpec((tt, Q_LORA), row),
                                                             pl.BlockSpec((tt, KV_LORA + HEAD_PAD), row),
                                                             pl.BlockSpec((HEADS_W, tt), lambda i: (0, i))],
        out_shape=[jax.ShapeDtypeStruct((t, HEADS_W), BF16)] * 3 + [
            jax.ShapeDtypeStruct((t, Q_LORA), BF16), jax.ShapeDtypeStruct((t, KV_LORA + HEAD_PAD), BF16),
            jax.ShapeDtypeStruct((HEADS_W, t), BF16)],
        compiler_params=_params(),
    )(qlat, kvlat, krope, *tables, gq, gkv, wq, wk, wv)


def _mla_proj_bwd(dq, dk, dv, qlat, kvlat, tables, gq, gkv, wq, wk, wv):
    t = qlat.shape[0]
    tt = _tile_rows(t)

    def body(dq_ref, dk_ref, dv_ref, q_ref, kv_ref, cq, s1q, s2q, ck, s1k, s2k, gq_ref, gkv_ref,
             wq_ref, wk_ref, wv_ref, dqr_ref, dql_ref, dkvl_ref, dkr_ref, dgq_ref, dgkv_ref):
        i = pl.program_id(0)

        @pl.when(i == 0)
        def _():
            dgq_ref[...] = jnp.zeros_like(dgq_ref)
            dgkv_ref[...] = jnp.zeros_like(dgkv_ref)

        tile = lambda r: jnp.tile(r[...], (1, N_HEADS))
        dqr = _rope_t(dq_ref[...], tile(cq), tile(s1q), tile(s2q)).astype(BF16)
        dqr_ref[...] = dqr
        dql, dg = _rms_bwd(q_ref[...], gq_ref[...], _dot(dqr, wq_ref[...]))
        dgq_ref[...] += dg
        dql_ref[...] = dql.astype(BF16)
        dkva = _dot_nt(dk_ref[...], wk_ref[...])
        dkvn = dkva[:, :KV_LORA] + _dot_nt(dv_ref[...], wv_ref[...])
        dkvl, dg = _rms_bwd(kv_ref[...], gkv_ref[...], dkvn)
        dgkv_ref[...] += dg
        dkvl_ref[...] = dkvl.astype(BF16)
        dkr_ref[...] = _rope_t(dkva[:, KV_LORA:], ck[...], s1k[...], s2k[...]).astype(BF16)

    row = lambda i: (i, 0)
    tab = pl.BlockSpec((tt, HEAD_PAD), row)
    wide = pl.BlockSpec((tt, HEADS_W), row)
    return pl.pallas_call(
        body, name="mla_proj_bwd",
        grid=(t // tt,),
        in_specs=[wide, wide, wide, pl.BlockSpec((tt, Q_LORA), row), pl.BlockSpec((tt, KV_LORA), row)] + [tab] * 6 + [
            _full((1, Q_LORA)), _full((1, KV_LORA)), _full((HEADS_W, Q_LORA)), _full((KV_LORA + HEAD_PAD, HEADS_W)),
            _full((KV_LORA, HEADS_W))],
        out_specs=[wide, pl.BlockSpec((tt, Q_LORA), row), pl.BlockSpec((tt, KV_LORA), row), tab,
                   _full((1, Q_LORA)), _full((1, KV_LORA))],
        out_shape=[jax.ShapeDtypeStruct((t, HEADS_W), BF16), jax.ShapeDtypeStruct((t, Q_LORA), BF16),
                   jax.ShapeDtypeStruct((t, KV_LORA), BF16), jax.ShapeDtypeStruct((t, HEAD_PAD), BF16),
                   jax.ShapeDtypeStruct((1, Q_LORA), F32), jax.ShapeDtypeStruct((1, KV_LORA), F32)],
        compiler_params=_params(),
    )(dq, dk, dv, qlat, kvlat, *tables, gq, gkv, wq, wk, wv)


NEG = -1e30


def _chunk_mask_t(keys, key0, queries):
    kc = (lax.broadcasted_iota(jnp.int32, (keys, queries), 0) + key0) // CHUNK
    qc = lax.broadcasted_iota(jnp.int32, (keys, queries), 1) // CHUNK
    return kc <= qc


def _attn_fwd(q, k, vt, comm):
    t = q.shape[0]
    bq = _tile_rows(t)
    nq = t // bq

    def body(q_ref, k_ref, vt_ref, o_ref, lse_ref, m_ref, l_ref, acc_ref):
        i = pl.program_id(1)
        m_ref[...] = jnp.full_like(m_ref, NEG)
        l_ref[...] = jnp.zeros_like(l_ref)
        acc_ref[...] = jnp.zeros_like(acc_ref)
        qb = q_ref[...]

        def block(first_key, keys, diagonal):
            cols = pl.ds(pl.multiple_of(first_key, bq), keys)
            s = _dot_nt(k_ref[cols, :], qb)
            if diagonal:
                s = jnp.where(_chunk_mask_t(keys, bq - keys, bq), s, NEG)
            m_old = m_ref[...]
            m_new = jnp.maximum(m_old, jnp.max(s, axis=0, keepdims=True))
            alpha = jnp.exp(m_old - m_new)
            p = jnp.exp(s - m_new)
            l_ref[...] = alpha * l_ref[...] + jnp.sum(p, axis=0, keepdims=True)
            acc_ref[...] = alpha * acc_ref[...] + _dot(vt_ref[:, cols], p.astype(BF16))
            m_ref[...] = m_new

        def pair_below(jj, carry):
            block(jj * 2 * bq, 2 * bq, False)
            return carry

        lax.fori_loop(0, i // 2, pair_below, 0)

        @pl.when(i % 2 == 1)
        def _():
            block((i - 1) * bq, 2 * bq, True)

        @pl.when(i % 2 == 0)
        def _():
            block(i * bq, bq, True)

        l = l_ref[...]
        o_ref[...] = jnp.transpose(acc_ref[...] / l).astype(BF16)
        lse_ref[...] = m_ref[...] + jnp.log(l)

    qmap = lambda h, i: (i, h)
    return _call(
        body, comm, name="attn_fwd" if comm is None else "attn_fwd_gather", grid=(N_HEADS, nq),
        in_specs=[pl.BlockSpec((bq, HEAD_PAD), qmap), pl.BlockSpec((t, HEAD_PAD), lambda h, i: (0, h)),
                  pl.BlockSpec((HEAD_PAD, t), lambda h, i: (h, 0))],
        out_specs=[pl.BlockSpec((bq, HEAD_PAD), qmap), pl.BlockSpec((None, 1, bq), lambda h, i: (h, 0, i))],
        out_shape=[jax.ShapeDtypeStruct(q.shape, BF16), jax.ShapeDtypeStruct((N_HEADS, 1, t), F32)],
        scratch_shapes=[pltpu.VMEM((1, bq), F32), pltpu.VMEM((1, bq), F32), pltpu.VMEM((HEAD_PAD, bq), F32)],
        args=(q, k, vt))


def _attn_bwd(q, k, v, o, do, lse, comm):
    t = q.shape[0]
    bq = _tile_rows(t)
    nq = t // bq

    def body(q_ref, k_ref, v_ref, o_ref, do_ref, lse_ref, dq_ref, dk_ref, dv_ref, dk_acc, dv_acc, delta_ref):
        j = pl.program_id(1)

        @pl.when(j == 0)
        def _():
            dq_ref[...] = jnp.zeros_like(dq_ref)
            for blk in range(nq):
                rows = slice(blk * bq, (blk + 1) * bq)
                delta = jnp.sum(do_ref[rows, :].astype(F32) * o_ref[rows, :].astype(F32), axis=-1, keepdims=True)
                delta_ref[:, rows] = jnp.transpose(jnp.broadcast_to(delta, (bq, HEAD_PAD)))[:1, :]

        dk_acc[...] = jnp.zeros_like(dk_acc)
        dv_acc[...] = jnp.zeros_like(dv_acc)
        kb, vb = k_ref[...], v_ref[...]

        def block(first_query, queries, diagonal):
            rows = pl.ds(pl.multiple_of(first_query, bq), queries)
            qb, dob = q_ref[rows, :], do_ref[rows, :]
            s = _dot_nt(kb, qb)
            if diagonal:
                s = jnp.where(_chunk_mask_t(bq, 0, queries), s, NEG)
            p = jnp.exp(s - lse_ref[:, rows])
            dv_acc[...] += _dot(p.astype(BF16), dob)
            dp = _dot_nt(vb, dob)
            ds = (p * (dp - delta_ref[:, rows])).astype(BF16)
            dk_acc[...] += _dot(ds, qb)
            dq_ref[rows, :] += _dot_tn(ds, kb)

        after = nq - 1 - j
        odd = after % 2

        @pl.when(odd == 1)
        def _():
            block(j * bq, 2 * bq, True)

        @pl.when(odd == 0)
        def _():
            block(j * bq, bq, True)

        def pair_above(n, carry):
            block((j + 1 + odd + 2 * n) * bq, 2 * bq, False)
            return carry

        lax.fori_loop(0, after // 2, pair_above, 0)
        dk_ref[...] = dk_acc[...].astype(BF16)
        dv_ref[...] = dv_acc[...].astype(BF16)

    kmap = lambda h, j: (j, h)
    head = lambda h, j: (0, h)
    whole = pl.BlockSpec((t, HEAD_PAD), head)
    return _call(
        body, comm, name="attn_bwd" if comm is None else "attn_bwd_exchange", grid=(N_HEADS, nq),
        in_specs=[whole, pl.BlockSpec((bq, HEAD_PAD), kmap), pl.BlockSpec((bq, HEAD_PAD), kmap), whole, whole,
                  pl.BlockSpec((None, 1, t), lambda h, j: (h, 0, 0))],
        out_specs=[whole, pl.BlockSpec((bq, HEAD_PAD), kmap), pl.BlockSpec((bq, HEAD_PAD), kmap)],
        out_shape=[jax.ShapeDtypeStruct(q.shape, F32), jax.ShapeDtypeStruct(q.shape, BF16),
                   jax.ShapeDtypeStruct(q.shape, BF16)],
        scratch_shapes=[pltpu.VMEM((bq, HEAD_PAD), F32), pltpu.VMEM((bq, HEAD_PAD), F32), pltpu.VMEM((1, t), F32)],
        args=(q, k, v, o, do, lse))


def _mixer_out(h1, hl, gate, o, w_lru, w_mla, g_post):
    t = h1.shape[0]
    tt = _tile_rows(t)

    def body(h1_ref, hl_ref, gate_ref, o_ref, wl_ref, wm_ref, g_ref, h2_ref, m_ref, y_ref):
        y = (hl_ref[...] * _gelu(gate_ref[...])).astype(BF16)
        y_ref[...] = y
        m = _dot(y, wl_ref[...]) + _dot(o_ref[...], wm_ref[...])
        m_ref[...] = m
        h2_ref[...] = h1_ref[...] + _rms(m, g_ref[...])

    row = lambda i: (i, 0)
    return pl.pallas_call(
        body, name="mixer_out",
        grid=(t // tt,),
        in_specs=[pl.BlockSpec((tt, D_MODEL), row), pl.BlockSpec((tt, LRU_W), row), pl.BlockSpec((tt, LRU_W), row),
                  pl.BlockSpec((tt, HEADS_W), row), _full((LRU_W, D_MODEL)), _full((HEADS_W, D_MODEL)),
                  _full((1, D_MODEL))],
        out_specs=[pl.BlockSpec((tt, D_MODEL), row), pl.BlockSpec((tt, D_MODEL), row),
                   pl.BlockSpec((tt, LRU_W), row)],
        out_shape=[jax.ShapeDtypeStruct((t, D_MODEL), F32), jax.ShapeDtypeStruct((t, D_MODEL), F32),
                   jax.ShapeDtypeStruct((t, LRU_W), BF16)],
        compiler_params=_params(),
    )(h1, hl, gate, o, w_lru, w_mla, g_post)


def _mixer_out_bwd(dh2, m, hl, gate, w_lru, w_mla, g_post):
    t = m.shape[0]
    tt = _tile_rows(t)

    def body(dh2_ref, m_ref, hl_ref, gate_ref, wl_ref, wm_ref, g_ref, dm_ref, dgate_ref, dhl_ref, do_ref, dg_ref):
        i = pl.program_id(0)

        @pl.when(i == 0)
        def _():
            dg_ref[...] = jnp.zeros_like(dg_ref)

        dm, dg = _rms_bwd(m_ref[...], g_ref[...], dh2_ref[...])
        dg_ref[...] += dg
        dmb = dm.astype(BF16)
        dm_ref[...] = dmb
        dy = _dot_nt(dmb, wl_ref[...])
        gate = gate_ref[...]
        dgate_ref[...] = (dy * hl_ref[...] * _gelu_grad(gate)).astype(BF16)
        dhl_ref[...] = dy * _gelu(gate)
        do_ref[...] = _dot_nt(dmb, wm_ref[...]).astype(BF16)

    row = lambda i: (i, 0)
    return pl.pallas_call(
        body, name="mixer_out_bwd",
        grid=(t // tt,),
        in_specs=[pl.BlockSpec((tt, D_MODEL), row), pl.BlockSpec((tt, D_MODEL), row), pl.BlockSpec((tt, LRU_W), row),
                  pl.BlockSpec((tt, LRU_W), row), _full((LRU_W, D_MODEL)), _full((HEADS_W, D_MODEL)),
                  _full((1, D_MODEL))],
        out_specs=[pl.BlockSpec((tt, D_MODEL), row), pl.BlockSpec((tt, LRU_W), row), pl.BlockSpec((tt, LRU_W), row),
                   pl.BlockSpec((tt, HEADS_W), row), _full((1, D_MODEL))],
        out_shape=[jax.ShapeDtypeStruct((t, D_MODEL), BF16), jax.ShapeDtypeStruct((t, LRU_W), BF16),
                   jax.ShapeDtypeStruct((t, LRU_W), F32), jax.ShapeDtypeStruct((t, HEADS_W), BF16),
                   jax.ShapeDtypeStruct((1, D_MODEL), F32)],
        compiler_params=_params(),
    )(dh2, m, hl, gate, w_lru, w_mla, g_post)


def _adamw(w, g, m, v):
    m = ADAM_B1 * m + (1.0 - ADAM_B1) * g
    v = ADAM_B2 * v + (1.0 - ADAM_B2) * (g * g)
    m_hat = m / (1.0 - ADAM_B1 ** ADAM_STEP)
    v_hat = v / (1.0 - ADAM_B2 ** ADAM_STEP)
    delta = -ADAM_LR * (m_hat / (jnp.sqrt(v_hat) + ADAM_EPS) + ADAM_WD * w)
    return delta, m, v


def _sum_adamw(terms, w, m, v, name):
    r, c = w.shape
    n_terms, rp = terms.shape[:2]
    cb = 256 if c % 256 == 0 else 128

    def body(t_ref, w_ref, m_ref, v_ref, g_out, d_out, m_out, v_out):
        g = t_ref[0, :r, :].astype(F32)
        for q in range(1, n_terms):
            g = g + t_ref[q, :r, :].astype(F32)
        g_out[...] = g
        d_out[...], m_out[...], v_out[...] = _adamw(w_ref[...], g, m_ref[...], v_ref[...])

    cols = pl.BlockSpec((r, cb), lambda j: (0, j))
    return pl.pallas_call(
        body, name=name, grid=(c // cb,),
        in_specs=[pl.BlockSpec((n_terms, rp, cb), lambda j: (0, 0, j)), cols, cols, cols], out_specs=[cols] * 4,
        out_shape=[jax.ShapeDtypeStruct((r, c), F32)] * 4,
        compiler_params=_params(),
    )(terms, w, m, v)


def _sum_terms(terms):
    def body(t_ref, out_ref):
        g = t_ref[0]
        for q in range(1, N_DEV):
            g = g + t_ref[q]
        out_ref[...] = g

    return pl.pallas_call(
        body, name="small_grad_sum", in_specs=[WHOLE], out_specs=WHOLE,
        out_shape=jax.ShapeDtypeStruct(terms.shape[1:], F32),
        compiler_params=_params(),
    )(terms)


def _adamw_flat(w, g, m, v):
    def body(w_ref, g_ref, m_ref, v_ref, d_out, m_out, v_out):
        d_out[...], m_out[...], v_out[...] = _adamw(w_ref[...], g_ref[...], m_ref[...], v_ref[...])

    return pl.pallas_call(
        body, name="small_adamw", in_specs=[WHOLE] * 4, out_specs=[WHOLE] * 3,
        out_shape=[jax.ShapeDtypeStruct(w.shape, F32)] * 3,
        compiler_params=_params(),
    )(w, g, m, v)


def _pack(arrays):
    flat = jnp.concatenate([a.reshape(-1).astype(F32) for a in arrays])
    rows = -(-flat.shape[0] // 1024) * 8
    return jnp.pad(flat, (0, rows * 128 - flat.shape[0])).reshape(rows, 128)


def _unpack(packed, shapes):
    flat = packed.reshape(-1)
    out, pos = [], 0
    for s in shapes:
        n = math.prod(s)
        out.append(flat[pos:pos + n].reshape(s))
        pos += n
    return out


def _rope_tables(positions):
    t = positions.shape[-1]
    inv_freq = 1.0 / (ROPE_THETA ** (jnp.arange(0, QK_ROPE, 2, dtype=F32) / QK_ROPE))
    ang = positions.reshape(t, 1).astype(F32) * inv_freq
    cos, sin = jnp.cos(ang), jnp.sin(ang)
    z16 = jnp.zeros((t, 16), F32)

    def place(first, second, lead, lead_value):
        parts = [jnp.full((t, lead), lead_value, F32)] if lead else []
        parts += [first, second, jnp.zeros((t, HEAD_PAD - lead - 32), F32)]
        return jnp.concatenate(parts, axis=1)

    q_tabs = [place(cos, cos, QK_NOPE, 1.0) * ATT_SCALE, place(-sin, z16, QK_NOPE, 0.0) * ATT_SCALE,
              place(z16, sin, QK_NOPE, 0.0) * ATT_SCALE]
    k_tabs = [place(cos, cos, 0, 0.0), place(-sin, z16, 0, 0.0), place(z16, sin, 0, 0.0)]
    return q_tabs + k_tabs


def _block_diag(w):
    eye = jnp.eye(w.shape[0], dtype=w.dtype)
    return jnp.einsum("nde,nm->ndme", w, eye).reshape(LRU_W, LRU_W)


def _diag_blocks(g):
    g4 = g.reshape(8, 64, 8, 64)
    return jnp.moveaxis(jnp.diagonal(g4, axis1=0, axis2=2), -1, 0)


def _mixer_layouts(got):
    full = {}
    w_in = got["w_in"][:, :IN_SHARD].reshape(IN_COLS, D_MODEL)
    full["w_in"] = jnp.pad(w_in, ((0, IN_PAD - IN_COLS), (0, 0)))
    full["wq"] = got["w_q_b"].reshape(HEADS_W, Q_LORA)
    kvb = got["w_kv_b"]
    zeros = jnp.zeros((N_HEADS, KV_LORA, HEAD_PAD - QK_NOPE), BF16)
    k_part = jnp.transpose(jnp.concatenate([kvb[:, :, :QK_NOPE], zeros], axis=2), (1, 0, 2)).reshape(KV_LORA, HEADS_W)
    rope_rows = jnp.pad(jnp.eye(QK_ROPE, dtype=BF16), ((0, HEAD_PAD - QK_ROPE), (QK_NOPE, HEAD_PAD - QK_NOPE - QK_ROPE)))
    full["wk"] = jnp.concatenate([k_part, jnp.tile(rope_rows, (1, N_HEADS))], axis=0)
    full["wv"] = jnp.transpose(jnp.concatenate([kvb[:, :, QK_NOPE:], zeros], axis=2), (1, 0, 2)).reshape(KV_LORA, HEADS_W)
    w_out = got["w_out"].reshape(D_MODEL, D_MODEL)
    full["w_out_lru"] = w_out[:LRU_W]
    full["w_out_mla"] = jnp.pad(w_out[LRU_W:].reshape(N_HEADS, V_DIM, D_MODEL),
                                ((0, 0), (0, HEAD_PAD - V_DIM), (0, 0))).reshape(HEADS_W, D_MODEL)
    conv_w = jnp.transpose(got["conv_w"][:, :CONV_W, :LRU_W // N_DEV], (1, 0, 2)).reshape(CONV_W, LRU_W)
    full["conv_w"] = jnp.pad(conv_w, ((0, 8 - CONV_W), (0, 0)))
    return full


_FFN1 = ["w_ffn1_gate", "w_ffn1_up", "w_ffn1_down"]
_MIX = ["w_in", "w_q_b", "w_kv_b", "w_out"]
_FFN2 = ["w_ffn2_gate", "w_ffn2_up", "w_ffn2_down"]
_SHARDED = _FFN1 + _MIX + _FFN2
_TRANSPOSED = ("w_ffn1_gate", "w_ffn1_up", "w_in", "w_q_b", "w_ffn2_gate", "w_ffn2_up")
_SMALL = ["g_ffn1_pre", "g_ffn1_post", "g_mix_pre", "g_mix_post", "conv_b", "w_lru_a", "b_lru_a", "w_lru_x",
          "b_lru_x", "lru_lambda", "q_a_norm", "kv_a_norm", "g_ffn2_pre", "g_ffn2_post", "conv_w"]
_SMALL_EARLY = [n for n in _SMALL if n != "g_ffn1_pre"]
_ORDER = ["g_ffn1_pre", "g_ffn1_post", "w_ffn1_gate", "w_ffn1_up", "w_ffn1_down", "g_mix_pre", "g_mix_post", "w_in",
          "conv_w", "conv_b", "w_lru_a", "b_lru_a", "w_lru_x", "b_lru_x", "lru_lambda", "q_a_norm", "w_q_b",
          "kv_a_norm", "w_kv_b", "w_out", "g_ffn2_pre", "g_ffn2_post", "w_ffn2_gate", "w_ffn2_up", "w_ffn2_down"]
_FF_BLOCK = (FF_PAD, D_MODEL)
_PADS = {"w_ffn1_gate": _FF_BLOCK, "w_ffn1_up": _FF_BLOCK, "w_ffn1_down": _FF_BLOCK,
         "w_in": (IN_SHARD_PAD, D_MODEL), "w_q_b": (HEAD_PAD, Q_LORA), "w_kv_b": (KV_LORA, HEAD_PAD),
         "w_out": (HEAD_PAD, D_MODEL), "conv_w": (8, 128),
         "w_ffn2_gate": _FF_BLOCK, "w_ffn2_up": _FF_BLOCK, "w_ffn2_down": _FF_BLOCK}


def _rows(name, a):
    return a.T if name in _TRANSPOSED else a


def _step(args):
    x = args["x"][0]
    target = args["loss_target"][0]
    positions = args["positions"][0]
    w = {n: args[n][0] for n in _ORDER}
    mom = {n: args["m_" + n][0] for n in _ORDER}
    var = {n: args["v_" + n][0] for n in _ORDER}
    vec = lambda name: w[name].reshape(1, -1)
    t = x.shape[0]
    ff_blocks = lambda a: a.reshape(N_DEV, FF_PAD, D_MODEL)
    ff_rows = lambda a: a.reshape(FF_VIRT, D_MODEL)

    names = _FFN1 + _MIX + ["conv_w"] + _FFN2
    (wg1,), staged = _first_gather([_rows(n, w[n]) for n in names], [_PADS[n] for n in names],
                                   [F32 if n == "conv_w" else BF16 for n in names], 1)
    wg1 = ff_rows(wg1)
    n_mix = 2 + len(_MIX) + 1

    tables = _rope_tables(positions)
    wa = _block_diag(w["w_lru_a"]).astype(BF16)
    wx = _block_diag(w["w_lru_x"]).astype(BF16)

    (n1, gt1), got = _ffn_fwd_gate(x, vec("g_ffn1_pre"), wg1, _Gather(staged[:1]), "ffn_fwd_gate_gather")
    wu1 = ff_rows(got[0])
    (up1,), got = _ffn_fwd_up(n1, wu1, _Gather(staged[1:2]), "ffn_fwd_up_gather")
    wd1 = ff_rows(got[0])
    (h1, f1), got = _ffn_fwd_out(x, gt1, up1, vec("g_ffn1_post"), wd1, _Gather(staged[2:n_mix]),
                                 "ffn_fwd_out_gather")
    mix = _mixer_layouts(dict(zip(_MIX + ["conv_w"], got)))
    nmix, xl, gate, qlat, kvlat, krope = _mixer_in(h1, vec("g_mix_pre"), mix["w_in"])
    xc, a, u = _lru_gates(xl, mix["conv_w"], vec("conv_b"), wa, wx, vec("b_lru_a"), vec("b_lru_x"), vec("lru_lambda"))
    hl = _lru_scan(a, u, reverse=False)
    q, k, v, qn, kva, vt = _mla_proj(qlat, kvlat, krope, tables, vec("q_a_norm"), vec("kv_a_norm"),
                                 mix["wq"], mix["wk"], mix["wv"])
    (o, lse), ffn2 = _attn_fwd(q, k, vt, _Gather(staged[n_mix:]))
    wg2, wu2, wd2 = [ff_rows(a_) for a_ in ffn2]
    h2, m, ylru = _mixer_out(h1, hl, gate, o, mix["w_out_lru"], mix["w_out_mla"], vec("g_mix_post"))
    dy, f2, n2, gt2, up2, loss_tile = _ffn_fwd_loss(h2, target, vec("g_ffn2_pre"), vec("g_ffn2_post"), wg2, wu2, wd2,
                                                    "ffn_fwd_loss")

    (df2, act2, dgt2, dup2, dg2post), _ = _ffn_bwd_hidden(dy, f2, gt2, up2, vec("g_ffn2_post"), wd2, None,
                                                          "ffn_bwd_hidden")
    (dh2, dg2pre), _ = _ffn_bwd_input(dy, h2, dgt2, dup2, vec("g_ffn2_pre"), wg2, wu2, None, "ffn_bwd_input")
    dw_ffn = lambda a_, b_, name, comm=None: _mm_tn(a_, b_, FF_VIRT, FF_CHUNK, D_MODEL, name, comm)
    big2 = [ff_blocks(dw_ffn(dgt2, n2, "dw_ffn_gate")[0]), ff_blocks(dw_ffn(dup2, n2, "dw_ffn_up")[0]),
            ff_blocks(dw_ffn(act2, df2, "dw_ffn_down")[0])]

    dm, dgate, dhl, do, dgmixpost = _mixer_out_bwd(dh2, m, hl, gate, mix["w_out_lru"], mix["w_out_mla"],
                                                   vec("g_mix_post"))
    dw_out_lru, _ = _mm_tn(ylru, dm, LRU_W, LRU_W, D_MODEL, "dw_out_lru")
    dw_out_mla, _ = _mm_tn(o, dm, HEADS_W, HEADS_W, D_MODEL, "dw_out_mla")
    dw_out = jnp.concatenate(
        [dw_out_lru, dw_out_mla.reshape(N_HEADS, HEAD_PAD, D_MODEL)[:, :V_DIM].reshape(N_HEADS * V_DIM, D_MODEL)], axis=0)

    (dq, dk, dv), terms2 = _attn_bwd(q, k, v, o, do, lse, _Exchange(big2))
    dqr, dqlat, dkvlat, dkrope, dgq, dgkv = _mla_proj_bwd(
        dq, dk, dv, qlat, kvlat, tables, vec("q_a_norm"), vec("kv_a_norm"), mix["wq"], mix["wk"], mix["wv"])
    dwq, _ = _mm_tn(dqr, qn, HEADS_W, HEADS_W, Q_LORA, "dw_q_b")
    dwk, _ = _mm_tn(kva, dk, KV_LORA, KV_LORA, HEADS_W, "dw_kv_b_k")
    dwv, _ = _mm_tn(kva, dv, KV_LORA, KV_LORA, HEADS_W, "dw_kv_b_v")
    dw_kvb = jnp.transpose(jnp.concatenate(
        [dwk.reshape(KV_LORA, N_HEADS, HEAD_PAD)[:, :, :QK_NOPE], dwv.reshape(KV_LORA, N_HEADS, HEAD_PAD)[:, :, :V_DIM]],
        axis=2), (1, 0, 2))

    dh = _lru_scan(a, dhl, reverse=True)
    dxc, dwa, dwx, dba, dbx, dlam = _lru_gates_bwd(dh, hl, xc, wa, wx, vec("b_lru_a"), vec("b_lru_x"),
                                                   vec("lru_lambda"))
    dxl, dconv_w8, dconv_b = _conv_bwd(dxc, xl, mix["conv_w"])
    dh1, dproj, dgmixpre = _mixer_in_bwd(dh2, h1, vec("g_mix_pre"), mix["w_in"], [dxl, dgate, dqlat, dkvlat, dkrope])
    dw_in, _ = _mm_tn(dproj, nmix, IN_PAD, IN_PAD // 2, D_MODEL, "dw_in")
    dw_in = jnp.pad(dw_in[:IN_COLS].reshape(N_DEV, IN_SHARD, D_MODEL), ((0, 0), (0, IN_SHARD_PAD - IN_SHARD), (0, 0)))
    big_mix = [dw_in, dwq.reshape(N_DEV, HEAD_PAD, Q_LORA), dw_kvb, dw_out.reshape(N_DEV, HEAD_PAD, D_MODEL)]

    (df1, act1, dgt1, dup1, dg1post), terms_mix = _ffn_bwd_hidden(
        dh1, f1, gt1, up1, vec("g_ffn1_post"), wd1, _Exchange(big_mix), "ffn_bwd_hidden_exchange")
    dwd1 = ff_blocks(dw_ffn(act1, df1, "dw_ffn_down")[0])
    dwg1, (swap_d,) = dw_ffn(dgt1, n1, "dw_ffn_gate_exchange", _PairSwap([dwd1]))
    dwg1 = ff_blocks(dwg1)
    dwu1, (terms_d1, swap_g) = dw_ffn(dup1, n1, "dw_ffn_up_exchange",
                                      _Several([_ChipExchange([_pair_sum(dwd1, swap_d)]), _PairSwap([dwg1])]))
    dwu1 = ff_blocks(dwu1)
    small = {"g_ffn1_post": dg1post, "g_mix_pre": dgmixpre, "g_mix_post": dgmixpost,
             "conv_b": dconv_b, "w_lru_a": _diag_blocks(dwa), "b_lru_a": dba, "w_lru_x": _diag_blocks(dwx),
             "b_lru_x": dbx, "lru_lambda": dlam, "q_a_norm": dgq, "kv_a_norm": dgkv, "g_ffn2_pre": dg2pre,
             "g_ffn2_post": dg2post, "conv_w": dconv_w8[:CONV_W]}
    to_all = lambda packed: jnp.broadcast_to(packed[None], (N_DEV,) + packed.shape)
    early = _pack([small[n] for n in _SMALL_EARLY] + [loss_tile[:1, :1]])
    (dx, dg1pre), (terms_g1, swap_u, early_terms) = _ffn_bwd_input(
        dh1, x, dgt1, dup1, vec("g_ffn1_pre"), wg1, wu1,
        _Several([_ChipExchange([_pair_sum(dwg1, swap_g)]), _PairSwap([dwu1]), _Exchange([to_all(early)])]),
        "ffn_bwd_input_exchange")
    small["g_ffn1_pre"] = dg1pre
    terms_u1, late_terms = _last_exchange(_Several([
        _ChipExchange([_pair_sum(dwu1, swap_u)]), _Exchange([to_all(_pack([dg1pre]))])]))

    terms = dict(zip(_FFN2, terms2))
    terms.update(zip(_MIX, terms_mix))
    terms.update({"w_ffn1_down": terms_d1, "w_ffn1_gate": terms_g1, "w_ffn1_up": terms_u1})

    grads, delta, new_m, new_v = {}, {}, {}, {}
    for n in _SHARDED:
        res = _sum_adamw(terms[n], _rows(n, w[n]), _rows(n, mom[n]), _rows(n, var[n]), "adamw_" + n)
        grads[n], delta[n], new_m[n], new_v[n] = [_rows(n, r) for r in res]

    small_sum = dict(zip(_SMALL_EARLY + ["loss"],
                         _unpack(_sum_terms(early_terms), [small[n].shape for n in _SMALL_EARLY] + [(1, 1)])))
    small_sum["g_ffn1_pre"], = _unpack(_sum_terms(late_terms), [dg1pre.shape])
    loss = small_sum.pop("loss").reshape(())
    me = 4 * lax.axis_index("x") + 2 * lax.axis_index("y") + lax.axis_index("c")
    cshard = LRU_W // N_DEV
    small_sum["conv_w"] = lax.dynamic_slice(small_sum["conv_w"], (0, me * cshard), (CONV_W, cshard))
    for n in _SMALL:
        grads[n] = small_sum[n].reshape(w[n].shape)
    shapes = [w[n].shape for n in _SMALL]
    d_p, m_p, v_p = _adamw_flat(_pack([w[n] for n in _SMALL]), _pack([grads[n] for n in _SMALL]),
                                _pack([mom[n] for n in _SMALL]), _pack([var[n] for n in _SMALL]))
    for n, d_, m_, v_ in zip(_SMALL, _unpack(d_p, shapes), _unpack(m_p, shapes), _unpack(v_p, shapes)):
        delta[n], new_m[n], new_v[n] = d_, m_, v_

    lead = lambda d: [d[n][None] for n in _ORDER]
    return (loss, dx[None], *lead(grads), *lead(delta), *lead(new_m), *lead(new_v))


def kernel(x, positions, g_ffn1_pre, g_ffn1_post, w_ffn1_gate, w_ffn1_up, w_ffn1_down, g_mix_pre, g_mix_post, w_in, conv_w, conv_b, w_lru_a, b_lru_a, w_lru_x, b_lru_x, lru_lambda, q_a_norm, w_q_b, kv_a_norm, w_kv_b, w_out, g_ffn2_pre, g_ffn2_post, w_ffn2_gate, w_ffn2_up, w_ffn2_down, loss_target, m_g_ffn1_pre, m_g_ffn1_post, m_w_ffn1_gate, m_w_ffn1_up, m_w_ffn1_down, m_g_mix_pre, m_g_mix_post, m_w_in, m_conv_w, m_conv_b, m_w_lru_a, m_b_lru_a, m_w_lru_x, m_b_lru_x, m_lru_lambda, m_q_a_norm, m_w_q_b, m_kv_a_norm, m_w_kv_b, m_w_out, m_g_ffn2_pre, m_g_ffn2_post, m_w_ffn2_gate, m_w_ffn2_up, m_w_ffn2_down, v_g_ffn1_pre, v_g_ffn1_post, v_w_ffn1_gate, v_w_ffn1_up, v_w_ffn1_down, v_g_mix_pre, v_g_mix_post, v_w_in, v_conv_w, v_conv_b, v_w_lru_a, v_b_lru_a, v_w_lru_x, v_b_lru_x, v_lru_lambda, v_q_a_norm, v_w_q_b, v_kv_a_norm, v_w_kv_b, v_w_out, v_g_ffn2_pre, v_g_ffn2_post, v_w_ffn2_gate, v_w_ffn2_up, v_w_ffn2_down):
    return _step(dict(locals()))
```
